```python
import jax
import jax.numpy as jnp
from jax import lax
import numpy as np


D_MODEL = 4096
BATCH = 2
SEQ = 4096
DEPTH = 2

N_EVEN = (DEPTH + 1) // 2
N_ODD = DEPTH // 2
MLA_HEADS = D_MODEL // 256
QK_NOPE = 128
QK_ROPE = 64
QK_HEAD = QK_NOPE + QK_ROPE
V_HEAD = 128
Q_LORA = 3 * D_MODEL // 16
KV_LORA = D_MODEL // 8
ROPE_THETA = 10000.0
ATTN_BLOCK = 128
MAX_POS_OFFSET = 1024
RWKV_HEAD = 64
RWKV_HEADS = D_MODEL // 128
RWKV_WIDTH = RWKV_HEADS * RWKV_HEAD
W_LORA = 64
A_LORA = 64
G_LORA = 128
LNX_EPS = 64e-5
MLA_IN = Q_LORA + KV_LORA + QK_ROPE
RWKV_IN = 3 * RWKV_WIDTH + W_LORA + A_LORA + G_LORA
HYB_IN = MLA_IN + RWKV_IN
MIX_WIDTH = MLA_HEADS * V_HEAD + RWKV_WIDTH
CONV_WIDTH = 3
PEER_HEADS = 8
N_KEYS = 128
N_EXPERTS = N_KEYS * N_KEYS
PEER_TOPK = 16
PEER_QDIM = 256
PEER_HALF = PEER_QDIM // 2
PEER_BLOCK = 128
NORM_EPS = 1e-6

kernel_name = 'hybrid_mla_rwkv7_shortconv_peer'


def rms_norm(x, w):
    xf = x.astype(jnp.float32)
    y = xf * lax.rsqrt(jnp.mean(xf * xf, axis=-1, keepdims=True) + NORM_EPS)
    return (y * w.astype(jnp.float32)).astype(x.dtype)


def modulate(h, shift, scale):
    return h * (1.0 + scale[:, None, :]) + shift[:, None, :]


def rope(x, positions):
    half = x.shape[-1] // 2
    freqs = ROPE_THETA ** (-jnp.arange(half, dtype=jnp.float32) / half)
    ang = positions.astype(jnp.float32)[:, :, None, None] * freqs
    cos, sin = jnp.cos(ang), jnp.sin(ang)
    xf = x.astype(jnp.float32)
    x1, x2 = xf[..., :half], xf[..., half:]
    return jnp.concatenate([x1 * cos - x2 * sin, x2 * cos + x1 * sin], axis=-1).astype(x.dtype)


def token_shift(p):
    return jnp.pad(p, ((0, 0), (1, 0), (0, 0)))[:, :-1]


def causal_block_attention(q, k, v):
    b, s, h, dk = q.shape
    dv = v.shape[-1]
    nb = s // ATTN_BLOCK
    scale = dk ** -0.5
    qb = jnp.moveaxis(q.reshape(b, nb, ATTN_BLOCK, h, dk), 1, 0)
    key_pos = jnp.arange(s)

    def one_block(args):
        q_blk, blk = args
        sc = jnp.einsum('bqhd,bkhd->bhqk', q_blk, k).astype(jnp.float32) * scale
        q_pos = blk * ATTN_BLOCK + jnp.arange(ATTN_BLOCK)
        mask = key_pos[None, :] <= q_pos[:, None]
        sc = jnp.where(mask[None, None], sc, -jnp.inf)
        pr = jax.nn.softmax(sc, axis=-1).astype(v.dtype)
        return jnp.einsum('bhqk,bkhd->bqhd', pr, v)

    out = lax.map(one_block, (qb, jnp.arange(nb)))
    return jnp.moveaxis(out, 0, 1).reshape(b, s, h * dv)


def mla_mixer(cq, ckv, kpe, positions, q_norm_w, w_uq, kv_norm_w, w_ukv, qk_q_w, qk_k_w):
    b, s, _ = cq.shape
    q = (rms_norm(cq, q_norm_w) @ w_uq).reshape(b, s, MLA_HEADS, QK_HEAD)
    kv = (rms_norm(ckv, kv_norm_w) @ w_ukv).reshape(b, s, MLA_HEADS, QK_NOPE + V_HEAD)
    k_nope, v = kv[..., :QK_NOPE], kv[..., QK_NOPE:]
    k_pe = jnp.broadcast_to(kpe[:, :, None, :], (b, s, MLA_HEADS, QK_ROPE))
    k = jnp.concatenate([k_nope, k_pe], axis=-1)
    q = rms_norm(q, qk_q_w)
    k = rms_norm(k, qk_k_w)
    q = jnp.concatenate([q[..., :QK_NOPE], rope(q[..., QK_NOPE:], positions)], axis=-1)
    k = jnp.concatenate([k[..., :QK_NOPE], rope(k[..., QK_NOPE:], positions)], axis=-1)
    return causal_block_attention(q, k, v)


def rwkv7_mixer(p, mu, w0, w2, a0, a2, g2, k_k, k_a, r_k, lnx_w, lnx_b):
    b, s, _ = p.shape
    f32 = jnp.float32
    p = p + (token_shift(p) - p) * mu
    o1 = RWKV_WIDTH
    o2 = 2 * RWKV_WIDTH
    o3 = 3 * RWKV_WIDTH
    r, k, v, xw, xa, xg = jnp.split(p, [o1, o2, o3, o3 + W_LORA, o3 + W_LORA + A_LORA], axis=-1)
    w = -jax.nn.softplus(-(w0 + jnp.tanh(xw) @ w2)) - 0.5
    a = jax.nn.sigmoid(a0 + xa @ a2)
    g = jax.nn.sigmoid(xg) @ g2

    def heads(t):
        return t.reshape(b, s, RWKV_HEADS, RWKV_HEAD)

    kk = heads((k * k_k).astype(f32))
    kk = kk / jnp.maximum(jnp.sqrt(jnp.sum(kk * kk, axis=-1, keepdims=True)), 1e-12)
    k = k * (1.0 + (a - 1.0) * k_a)
    r_h, k_h, v_h, a_h = heads(r), heads(k), heads(v), heads(a)
    decay = jnp.exp(-jnp.exp(heads(w).astype(f32)))

    def step(state, inp):
        r_t, w_t, k_t, v_t, kk_t, a_t = inp
        sa = jnp.einsum('bhij,bhj->bhi', state, -kk_t)
        state = (state * w_t[:, :, None, :]
                 + sa[..., None] * (kk_t * a_t)[:, :, None, :]
                 + v_t[..., None] * k_t[:, :, None, :])
        return state, jnp.einsum('bhij,bhj->bhi', state, r_t)

    def time_major(t):
        return jnp.moveaxis(t.astype(f32), 1, 0)

    s0 = jnp.zeros((b, RWKV_HEADS, RWKV_HEAD, RWKV_HEAD), f32)
    xs = tuple(map(time_major, (r_h, decay, k_h, v_h, kk, a_h)))
    _, y = lax.scan(step, s0, xs)
    y = jnp.moveaxis(y, 0, 1)
    mean = jnp.mean(y, axis=-1, keepdims=True)
    var = jnp.mean(jnp.square(y - mean), axis=-1, keepdims=True)
    y = (y - mean) * lax.rsqrt(var + LNX_EPS)
    gn_w = lnx_w.reshape(RWKV_HEADS, RWKV_HEAD).astype(f32)
    gn_b = lnx_b.reshape(RWKV_HEADS, RWKV_HEAD).astype(f32)
    y = y * gn_w + gn_b
    bonus = jnp.sum((r_h * k_h * r_k).astype(f32), axis=-1, keepdims=True) * v_h.astype(f32)
    return (y + bonus).reshape(b, s, RWKV_WIDTH).astype(p.dtype) * g


def short_conv_mixer(h, w_in, conv_w, w_out):
    s = h.shape[1]
    gate_b, gate_c, u = jnp.split(h @ w_in, 3, axis=-1)
    u = gate_c * u
    up = jnp.pad(u, ((0, 0), (CONV_WIDTH - 1, 0), (0, 0)))
    y = sum(conv_w[j] * up[:, j:j + s] for j in range(CONV_WIDTH))
    return (gate_b * y) @ w_out


def peer_ffn(h, w_q, keys, u_tab, v_tab):
    b, s, d = h.shape
    nb = (b * s) // PEER_BLOCK
    hb = h.reshape(nb, PEER_BLOCK, d)

    def one_block(xb):
        q = (xb @ w_q).reshape(PEER_BLOCK, PEER_HEADS, 2, PEER_HALF)
        sc = jnp.einsum('thpd,hpkd->thpk', q, keys)
        sv, si = lax.top_k(sc, PEER_TOPK)
        cand = sv[:, :, 0, :, None] + sv[:, :, 1, None, :]
        cv, ci = lax.top_k(cand.reshape(PEER_BLOCK, PEER_HEADS, PEER_TOPK * PEER_TOPK), PEER_TOPK)
        i1 = jnp.take_along_axis(si[:, :, 0], ci // PEER_TOPK, axis=-1)
        i2 = jnp.take_along_axis(si[:, :, 1], ci % PEER_TOPK, axis=-1)
        e = i1 * N_KEYS + i2
        gw = jax.nn.softmax(cv.astype(jnp.float32), axis=-1).astype(xb.dtype)
        act = jax.nn.gelu(jnp.einsum('thkd,td->thk', u_tab[e], xb), approximate=False)
        return jnp.einsum('thk,thkd->td', gw * act, v_tab[e])

    return lax.map(one_block, hb).reshape(b, s, d)


def setup_inputs(seed: int = 0) -> dict:
    key = jax.random.key(seed)
    ks = iter(jax.random.split(key, 48))
    f32 = jnp.float32
    D = D_MODEL

    def nrm(shape, scale):
        return jax.random.normal(next(ks), shape, f32) * scale

    def gain(shape):
        return 1.0 + nrm(shape, 0.02)

    positions = (jax.random.randint(next(ks), (BATCH, 1), 0, MAX_POS_OFFSET, jnp.int32)
                 + jnp.arange(SEQ, dtype=jnp.int32)[None, :])
    return {
        'x': nrm((BATCH, SEQ, D), 1.0),
        'c': nrm((BATCH, D), 1.0),
        'positions': positions,
        'ada_w': nrm((DEPTH, D, 6 * D), 0.5 * D ** -0.5),
        'ada_b': nrm((DEPTH, 6 * D), 0.02),
        'norm_mix_w': gain((DEPTH, D)),
        'norm_ffn_w': gain((DEPTH, D)),
        'hyb_w_in': nrm((N_EVEN, D, HYB_IN), D ** -0.5),
        'mla_q_norm_w': gain((N_EVEN, Q_LORA)),
        'mla_w_uq': nrm((N_EVEN, Q_LORA, MLA_HEADS * QK_HEAD), Q_LORA ** -0.5),
        'mla_kv_norm_w': gain((N_EVEN, KV_LORA)),
        'mla_w_ukv': nrm((N_EVEN, KV_LORA, MLA_HEADS * (QK_NOPE + V_HEAD)), KV_LORA ** -0.5),
        'mla_qk_q_w': gain((N_EVEN, QK_HEAD)),
        'mla_qk_k_w': gain((N_EVEN, QK_HEAD)),
        'rwkv_mu': jax.random.uniform(next(ks), (N_EVEN, RWKV_IN), f32),
        'rwkv_w0': jax.random.uniform(next(ks), (N_EVEN, RWKV_WIDTH), f32, minval=-6.0, maxval=1.0),
        'rwkv_w2': nrm((N_EVEN, W_LORA, RWKV_WIDTH), 0.1),
        'rwkv_a0': nrm((N_EVEN, RWKV_WIDTH), 0.5),
        'rwkv_a2': nrm((N_EVEN, A_LORA, RWKV_WIDTH), 0.1),
        'rwkv_g2': nrm((N_EVEN, G_LORA, RWKV_WIDTH), G_LORA ** -0.5),
        'rwkv_k_k': 0.85 + nrm((N_EVEN, RWKV_WIDTH), 0.05),
        'rwkv_k_a': 1.0 + nrm((N_EVEN, RWKV_WIDTH), 0.05),
        'rwkv_r_k': nrm((N_EVEN, RWKV_HEADS, RWKV_HEAD), 0.1),
        'rwkv_lnx_w': gain((N_EVEN, RWKV_WIDTH)),
        'rwkv_lnx_b': nrm((N_EVEN, RWKV_WIDTH), 0.02),
        'hyb_w_out': nrm((N_EVEN, MIX_WIDTH, D), MIX_WIDTH ** -0.5),
        'conv_w_in': nrm((N_ODD, D, 3 * D), D ** -0.5),
        'conv_w': nrm((N_ODD, CONV_WIDTH, D), CONV_WIDTH ** -0.5),
        'conv_w_out': nrm((N_ODD, D, D), D ** -0.5),
        'peer_w_q': nrm((DEPTH, D, PEER_HEADS * PEER_QDIM), D ** -0.5),
        'peer_keys': nrm((DEPTH, PEER_HEADS, 2, N_KEYS, PEER_HALF), PEER_HALF ** -0.5),
        'peer_u': nrm((DEPTH, N_EXPERTS, D), D ** -0.5),
        'peer_v': nrm((DEPTH, N_EXPERTS, D), PEER_HEADS ** -0.5),
    }


def reference(x, c, positions, ada_w, ada_b, norm_mix_w, norm_ffn_w, hyb_w_in,
              mla_q_norm_w, mla_w_uq, mla_kv_norm_w, mla_w_ukv, mla_qk_q_w, mla_qk_k_w,
              rwkv_mu, rwkv_w0, rwkv_w2, rwkv_a0, rwkv_a2, rwkv_g2, rwkv_k_k, rwkv_k_a,
              rwkv_r_k, rwkv_lnx_w, rwkv_lnx_b, hyb_w_out, conv_w_in, conv_w, conv_w_out,
              peer_w_q, peer_keys, peer_u, peer_v):
    cond = jax.nn.silu(c)
    for layer in range(DEPTH):
        mod = cond @ ada_w[layer] + ada_b[layer]
        sh_m, sc_m, gt_m, sh_f, sc_f, gt_f = jnp.split(mod, 6, axis=-1)
        h = modulate(rms_norm(x, norm_mix_w[layer]), sh_m, sc_m)
        i = layer // 2
        if layer % 2 == 0:
            p = h @ hyb_w_in[i]
            cq, ckv, kpe, prw = jnp.split(p, [Q_LORA, Q_LORA + KV_LORA, MLA_IN], axis=-1)
            y_mla = mla_mixer(cq, ckv, kpe, positions, mla_q_norm_w[i], mla_w_uq[i],
                              mla_kv_norm_w[i], mla_w_ukv[i], mla_qk_q_w[i], mla_qk_k_w[i])
            y_rwkv = rwkv7_mixer(prw, rwkv_mu[i], rwkv_w0[i], rwkv_w2[i], rwkv_a0[i], rwkv_a2[i],
                                 rwkv_g2[i], rwkv_k_k[i], rwkv_k_a[i], rwkv_r_k[i],
                                 rwkv_lnx_w[i], rwkv_lnx_b[i])
            y = jnp.concatenate([y_mla, y_rwkv], axis=-1) @ hyb_w_out[i]
        else:
            y = short_conv_mixer(h, conv_w_in[i], conv_w[i], conv_w_out[i])
        x = x + gt_m[:, None, :] * y
        h = modulate(rms_norm(x, norm_ffn_w[layer]), sh_f, sc_f)
        x = x + gt_f[:, None, :] * peer_ffn(h, peer_w_q[layer], peer_keys[layer],
                                            peer_u[layer], peer_v[layer])
    return x
```

```python
import functools

import numpy as np
import jax
import jax.numpy as jnp
from jax import lax
from jax.experimental import pallas as pl
from jax.experimental.pallas import tpu as pltpu

F32 = jnp.float32
BF16 = jnp.bfloat16
I32 = jnp.int32

MLA_HEADS = 16
QK_NOPE = 128
QK_ROPE = 64
QK_HEAD = QK_NOPE + QK_ROPE
V_HEAD = 128
Q_LORA = 768
KV_LORA = 512
ROPE_THETA = 10000.0
RWKV_HEAD = 64
RWKV_HEADS = 32
RWKV_WIDTH = RWKV_HEADS * RWKV_HEAD
W_LORA = 64
A_LORA = 64
G_LORA = 128
LNX_EPS = 64e-5
CONV_WIDTH = 3
PEER_HEADS = 8
N_KEYS = 128
PEER_TOPK = 16
PEER_HALF = 128
NORM_EPS = 1e-6

LANES = 128
SUBLANES = 8
VMEM_LIMIT = 56 * 1024 * 1024

P_CKV = 0
P_KPE = 512
P_CQ = 768
P_LORA = 1536
P_R = 2048
P_K = P_R + RWKV_WIDTH
P_V = P_K + RWKV_WIDTH
P_WIDTH = P_V + RWKV_WIDTH
RWKV_CHUNK = 64


def _cparams(sem):
    return pltpu.CompilerParams(dimension_semantics=sem, vmem_limit_bytes=VMEM_LIMIT)


def _dot(a, b):
    return lax.dot_general(a, b, (((1,), (0,)), ((), ())), preferred_element_type=F32)


def _dot_nt(a, b):
    return lax.dot_general(a, b, (((1,), (1,)), ((), ())), preferred_element_type=F32)


def _dot_tn(a, b):
    return lax.dot_general(a, b, (((0,), (0,)), ((), ())), preferred_element_type=F32)


def _split_dot(x, m):
    hi = x.astype(BF16)
    lo = (x - hi.astype(F32)).astype(BF16)
    return _dot(hi, m) + _dot(lo, m)


def _mm_body(*refs, nb, nk, n_extra, trans_b, epilogue):
    a_ref = refs[0]
    b_refs = refs[1:1 + nb]
    extra = refs[1 + nb:1 + nb + n_extra]
    o_refs = refs[1 + nb + n_extra:]
    a = a_ref[...].astype(BF16)
    parts = []
    for b_ref in b_refs:
        b = b_ref[...].astype(BF16)
        parts.append(_dot_nt(a, b) if trans_b else _dot(a, b))

    def finish(accs):
        outs = epilogue(*accs, *[e[...] for e in extra])
        if not isinstance(outs, (tuple, list)):
            outs = (outs,)
        for o_ref, o in zip(o_refs, outs):
            o_ref[...] = o.astype(o_ref.dtype)

    if nk == 1:
        finish(parts)
        return
    n_out = len(o_refs) - nb
    acc_refs = o_refs[n_out:]
    o_refs = o_refs[:n_out]
    k = pl.program_id(2)

    @pl.when(k == 0)
    def _():
        for acc, p in zip(acc_refs, parts):
            acc[...] = p

    @pl.when(k > 0)
    def _():
        for acc, p in zip(acc_refs, parts):
            acc[...] += p

    @pl.when(k == nk - 1)
    def _():
        finish([acc[...] for acc in acc_refs])


def _matmul(a, bs, b_col_blocks, n_out_cols, *, tm, tn, tk, out_dtypes, epilogue=None,
            extras=(), trans_b=False, name="matmul"):
    m, kdim = a.shape
    nb = len(bs)
    nk = kdim // tk
    assert m % tm == 0 and kdim % tk == 0 and n_out_cols % tn == 0
    if epilogue is None:
        epilogue = lambda acc: acc
    grid = (m // tm, n_out_cols // tn, nk)
    in_specs = [pl.BlockSpec((tm, tk), lambda i, j, k: (i, k))]
    for off in b_col_blocks:
        if trans_b:
            in_specs.append(pl.BlockSpec((tn, tk), lambda i, j, k, off=off: (j + off, k)))
        else:
            in_specs.append(pl.BlockSpec((tk, tn), lambda i, j, k, off=off: (k, j + off)))
    extra_arrays = []
    for arr, bshape, imap in extras:
        in_specs.append(pl.BlockSpec(bshape, lambda i, j, k, imap=imap: imap(i, j)))
        extra_arrays.append(arr)
    out_shape = [jax.ShapeDtypeStruct((m, n_out_cols), dt) for dt in out_dtypes]
    out_specs = [pl.BlockSpec((tm, tn), lambda i, j, k: (i, j)) for _ in out_dtypes]
    scratch = [pltpu.VMEM((tm, tn), F32) for _ in range(nb)] if nk > 1 else []
    body = functools.partial(_mm_body, nb=nb, nk=nk, n_extra=len(extras), trans_b=trans_b,
                             epilogue=epilogue)
    outs = pl.pallas_call(
        body, out_shape=out_shape, grid=grid, in_specs=in_specs, out_specs=out_specs,
        scratch_shapes=scratch, name=name,
        compiler_params=_cparams(("parallel", "parallel", "arbitrary")),
    )(a, *bs, *extra_arrays)
    return outs


def _pick(n, prefs):
    for p in prefs:
        if n % p == 0:
            return p
    raise ValueError(f"no tile for {n}")


def _ada_body(c_ref, w_ref, b_ref, o_ref):
    c = c_ref[...]
    cond = (c * (1.0 / (1.0 + jnp.exp(-c)))).astype(BF16)
    o_ref[...] = _dot(cond, w_ref[...].astype(BF16)) + b_ref[...]


def _ada_mod(c, ada_w, ada_b):
    depth, d, n = ada_w.shape
    bsz = c.shape[0]
    tn = 512
    return pl.pallas_call(
        _ada_body,
        out_shape=jax.ShapeDtypeStruct((depth, bsz, n), F32),
        grid=(depth, n // tn),
        in_specs=[
            pl.BlockSpec((bsz, d), lambda l, j: (0, 0)),
            pl.BlockSpec((None, d, tn), lambda l, j: (l, 0, j)),
            pl.BlockSpec((None, 1, tn), lambda l, j: (l, 0, j)),
        ],
        out_specs=pl.BlockSpec((None, bsz, tn), lambda l, j: (l, 0, j)),
        name="ada_mod",
        compiler_params=_cparams(("parallel", "parallel")),
    )(c, ada_w, ada_b.reshape(depth, 1, n))


def _norm_mod_body(x_ref, w_ref, sh_ref, sc_ref, o_ref):
    x = x_ref[...]
    ms = jnp.mean(x * x, axis=-1, keepdims=True)
    y = x * lax.rsqrt(ms + NORM_EPS) * w_ref[...]
    o_ref[...] = (y * (1.0 + sc_ref[...]) + sh_ref[...]).astype(o_ref.dtype)


def _norm_mod(x, w, shift, scale):
    bsz, s, d = x.shape
    ts = _pick(s, (256, 128))
    vec = lambda v: v.reshape(bsz, 1, d)
    return pl.pallas_call(
        _norm_mod_body,
        out_shape=jax.ShapeDtypeStruct((bsz, s, d), BF16),
        grid=(bsz, s // ts),
        in_specs=[
            pl.BlockSpec((None, ts, d), lambda b, i: (b, i, 0)),
            pl.BlockSpec((1, d), lambda b, i: (0, 0)),
            pl.BlockSpec((None, 1, d), lambda b, i: (b, 0, 0)),
            pl.BlockSpec((None, 1, d), lambda b, i: (b, 0, 0)),
        ],
        out_specs=pl.BlockSpec((None, ts, d), lambda b, i: (b, i, 0)),
        name="norm_mod",
        compiler_params=_cparams(("parallel", "parallel")),
    )(x, w.reshape(1, d), vec(shift), vec(scale))


def _rope_tables(pos_ref, freq_ref):
    ang = pos_ref[...].astype(F32) * freq_ref[...]
    lane = lax.broadcasted_iota(I32, ang.shape, 1)
    cos = jnp.where(lane < QK_ROPE, jnp.cos(ang), 0.0)
    sin = jnp.sin(ang)
    sin_signed = jnp.where(lane < QK_ROPE // 2, -sin, jnp.where(lane < QK_ROPE, sin, 0.0))
    return cos, sin_signed


def _mla_q_body(cq_ref, nw_ref, wuq_ref, hw_ref, pos_ref, freq_ref, o_ref, *, scale):
    cq = cq_ref[...]
    ms = jnp.mean(cq * cq, axis=-1, keepdims=True)
    cqn = (cq * lax.rsqrt(ms + NORM_EPS) * nw_ref[...]).astype(BF16)
    cos, sin_signed = _rope_tables(pos_ref, freq_ref)
    w_nope = hw_ref[0:1, :]
    w_rope = hw_ref[1:2, :]
    w_swap = hw_ref[2:3, :]
    for h in range(MLA_HEADS):
        qh = _dot(cqn, wuq_ref[:, h * 3 * LANES:(h + 1) * 3 * LANES])
        nope = qh[:, :LANES]
        rope = qh[:, LANES:2 * LANES]
        swap = qh[:, 2 * LANES:]
        ss = jnp.sum(nope * nope + rope * rope, axis=-1, keepdims=True)
        fac = lax.rsqrt(ss * (1.0 / QK_HEAD) + NORM_EPS) * scale
        o_ref[:, h * 2 * LANES:h * 2 * LANES + LANES] = (nope * fac * w_nope).astype(o_ref.dtype)
        roped = (rope * w_rope * cos + swap * w_swap * sin_signed) * fac
        o_ref[:, h * 2 * LANES + LANES:(h + 1) * 2 * LANES] = roped.astype(o_ref.dtype)


def _mla_kv_body(ckv_ref, kpe_ref, nw_ref, wukv_ref, hw_ref, pos_ref, freq_ref, k_ref, v_ref):
    ckv = ckv_ref[...]
    ms = jnp.mean(ckv * ckv, axis=-1, keepdims=True)
    ckvn = (ckv * lax.rsqrt(ms + NORM_EPS) * nw_ref[...]).astype(BF16)
    cos, sin_signed = _rope_tables(pos_ref, freq_ref)
    kpe = kpe_ref[:, :LANES]
    kpe_swap = kpe_ref[:, LANES:]
    pe_ss = jnp.sum(kpe * kpe, axis=-1, keepdims=True)
    pe_roped = kpe * hw_ref[1:2, :] * cos + kpe_swap * hw_ref[2:3, :] * sin_signed
    w_nope = hw_ref[0:1, :]
    for h in range(MLA_HEADS):
        kn = _dot(ckvn, wukv_ref[:, h * LANES:(h + 1) * LANES])
        ss = jnp.sum(kn * kn, axis=-1, keepdims=True) + pe_ss
        fac = lax.rsqrt(ss * (1.0 / QK_HEAD) + NORM_EPS)
        k_ref[:, h * 2 * LANES:h * 2 * LANES + LANES] = (kn * fac * w_nope).astype(k_ref.dtype)
        k_ref[:, h * 2 * LANES + LANES:(h + 1) * 2 * LANES] = (pe_roped * fac).astype(k_ref.dtype)
    nv = MLA_HEADS * V_HEAD
    v_ref[...] = _dot(ckvn, wukv_ref[:, nv:]).astype(v_ref.dtype)


def _head_norm_rows(w):
    half = QK_ROPE // 2
    zeros = jnp.zeros((LANES - QK_ROPE,), F32)
    rope = jnp.concatenate([w[QK_NOPE:], zeros])
    swap = jnp.concatenate([w[QK_NOPE + half:], w[QK_NOPE:QK_NOPE + half], zeros])
    rows = jnp.stack([w[:QK_NOPE], rope, swap])
    return jnp.concatenate([rows, jnp.zeros((SUBLANES - 3, LANES), F32)], axis=0)


def _rope_freq_row():
    half = QK_ROPE // 2
    f = ROPE_THETA ** (-jnp.arange(half, dtype=F32) / half)
    return jnp.concatenate([f, f, jnp.zeros((LANES - QK_ROPE,), F32)]).reshape(1, LANES)


def _wuq_layout():
    half = QK_ROPE // 2
    idx, mask = [], []
    for h in range(MLA_HEADS):
        base = h * QK_HEAD
        nope = list(range(base, base + QK_NOPE))
        rope = list(range(base + QK_NOPE, base + QK_HEAD))
        swap = rope[half:] + rope[:half]
        pad = [0] * (LANES - QK_ROPE)
        idx += nope + rope + pad + swap + pad
        mask += [1] * QK_NOPE + [1] * QK_ROPE + [0] * len(pad) + [1] * QK_ROPE + [0] * len(pad)
    return np.asarray(idx, np.int32), np.asarray(mask, np.float32)


def _wukv_layout():
    per = QK_NOPE + V_HEAD
    k_idx = [h * per + i for h in range(MLA_HEADS) for i in range(QK_NOPE)]
    v_idx = [h * per + QK_NOPE + i for h in range(MLA_HEADS) for i in range(V_HEAD)]
    return np.asarray(k_idx + v_idx, np.int32)


def _mla_qkv(p, positions, q_norm_w, w_uq, kv_norm_w, w_ukv, qk_q_w, qk_k_w):
    t = p.shape[0]
    tm = _pick(t, (256, 128))
    idx, mask = _wuq_layout()
    wuq = (w_uq[:, idx] * mask[None, :]).astype(BF16)
    wukv = w_ukv[:, _wukv_layout()].astype(BF16)
    pos = positions.reshape(t, 1)
    freq = _rope_freq_row()
    hq = _head_norm_rows(qk_q_w)
    hk = _head_norm_rows(qk_k_w)
    qw = MLA_HEADS * 2 * LANES
    row = lambda i: (i, 0)
    fixed = lambda i: (0, 0)
    q = pl.pallas_call(
        functools.partial(_mla_q_body, scale=QK_HEAD ** -0.5),
        out_shape=jax.ShapeDtypeStruct((t, qw), BF16),
        grid=(t // tm,),
        in_specs=[
            pl.BlockSpec((tm, Q_LORA), lambda i: (i, P_CQ // Q_LORA)),
            pl.BlockSpec((1, Q_LORA), fixed),
            pl.BlockSpec(wuq.shape, fixed),
            pl.BlockSpec((SUBLANES, LANES), fixed),
            pl.BlockSpec((tm, 1), row),
            pl.BlockSpec((1, LANES), fixed),
        ],
        out_specs=pl.BlockSpec((tm, qw), row),
        name="mla_q",
        compiler_params=_cparams(("parallel",)),
    )(p, q_norm_w.reshape(1, Q_LORA), wuq, hq, pos, freq)
    k, v = pl.pallas_call(
        _mla_kv_body,
        out_shape=[jax.ShapeDtypeStruct((t, qw), BF16),
                   jax.ShapeDtypeStruct((t, MLA_HEADS * V_HEAD), BF16)],
        grid=(t // tm,),
        in_specs=[
            pl.BlockSpec((tm, KV_LORA), lambda i: (i, P_CKV // KV_LORA)),
            pl.BlockSpec((tm, 2 * LANES), lambda i: (i, P_KPE // (2 * LANES))),
            pl.BlockSpec((1, KV_LORA), fixed),
            pl.BlockSpec(wukv.shape, fixed),
            pl.BlockSpec((SUBLANES, LANES), fixed),
            pl.BlockSpec((tm, 1), row),
            pl.BlockSpec((1, LANES), fixed),
        ],
        out_specs=[pl.BlockSpec((tm, qw), row), pl.BlockSpec((tm, MLA_HEADS * V_HEAD), row)],
        name="mla_kv",
        compiler_params=_cparams(("parallel",)),
    )(p, p, kv_norm_w.reshape(1, KV_LORA), wukv, hk, pos, freq)
    return q, k, v


def _attn_body(q_ref, k_ref, v_ref, o_ref, m_ref, l_ref, acc_ref, *, tq):
    qi = pl.program_id(2)
    ki = pl.program_id(3)

    @pl.when(ki == 0)
    def _():
        m_ref[...] = jnp.full(m_ref.shape, -jnp.inf, F32)
        l_ref[...] = jnp.zeros(l_ref.shape, F32)
        acc_ref[...] = jnp.zeros(acc_ref.shape, F32)

    @pl.when(ki <= qi)
    def _():
        s = _dot_nt(q_ref[...], k_ref[...])
        row = lax.broadcasted_iota(I32, s.shape, 0)
        col = lax.broadcasted_iota(I32, s.shape, 1)
        s = jnp.where(col + ki * tq <= row + qi * tq, s, -jnp.inf)
        m_prev = m_ref[...]
        m_new = jnp.maximum(m_prev, jnp.max(s, axis=-1, keepdims=True))
        alpha = jnp.exp(m_prev - m_new)
        pr = jnp.exp(s - m_new)
        l_ref[...] = alpha * l_ref[...] + jnp.sum(pr, axis=-1, keepdims=True)
        acc_ref[...] = alpha * acc_ref[...] + _dot(pr.astype(BF16), v_ref[...])
        m_ref[...] = m_new

    @pl.when(ki == qi)
    def _():
        o_ref[...] = (acc_ref[...] / l_ref[...]).astype(o_ref.dtype)


def _attention(q, k, v, bsz, s):
    tq = _pick(s, (512, 256, 128))
    n = s // tq
    return pl.pallas_call(
        functools.partial(_attn_body, tq=tq),
        out_shape=jax.ShapeDtypeStruct((bsz, s, MLA_HEADS * V_HEAD), BF16),
        grid=(bsz, MLA_HEADS, n, n),
        in_specs=[
            pl.BlockSpec((None, tq, 2 * LANES), lambda b, h, i, j: (b, i, h)),
            pl.BlockSpec((None, tq, 2 * LANES), lambda b, h, i, j: (b, jnp.minimum(i, j), h)),
            pl.BlockSpec((None, tq, V_HEAD), lambda b, h, i, j: (b, jnp.minimum(i, j), h)),
        ],
        out_specs=pl.BlockSpec((None, tq, V_HEAD), lambda b, h, i, j: (b, i, h)),
        scratch_shapes=[pltpu.VMEM((tq, 1), F32), pltpu.VMEM((tq, 1), F32),
                        pltpu.VMEM((tq, V_HEAD), F32)],
        name="mla_attention",
        compiler_params=_cparams(("parallel", "parallel", "parallel", "arbitrary")),
    )(q, k, v)


def _rwkv_body(r_ref, k_ref, v_ref, xl_ref, mu_r_ref, mu_k_ref, mu_v_ref, mu_l_ref, vec_ref,
               w2a2_ref, g2_ref, o_ref, state_ref, pr_ref, pk_ref, pv_ref, pl_ref, *, tb):
    c = RWKV_CHUNK
    t_idx = pl.program_id(2)

    @pl.when(t_idx == 0)
    def _():
        state_ref[...] = jnp.zeros(state_ref.shape, F32)
        pr_ref[...] = jnp.zeros(pr_ref.shape, F32)
        pk_ref[...] = jnp.zeros(pk_ref.shape, F32)
        pv_ref[...] = jnp.zeros(pv_ref.shape, F32)
        pl_ref[...] = jnp.zeros(pl_ref.shape, F32)

    def mix(x_ref, prev_ref, mu_ref):
        x = x_ref[...]
        row = lax.broadcasted_iota(I32, x.shape, 0)
        shifted = jnp.where(row == 0, prev_ref[...], pltpu.roll(x, 1, axis=0))
        prev_ref[...] = x[tb - 1:tb, :]
        return x + (shifted - x) * mu_ref[...]

    r = mix(r_ref, pr_ref, mu_r_ref)
    k = mix(k_ref, pk_ref, mu_k_ref)
    v = mix(v_ref, pv_ref, mu_v_ref)
    xl = mix(xl_ref, pl_ref, mu_l_ref)

    w0 = vec_ref[0:1, :]
    a0 = vec_ref[1:2, :]
    k_k = vec_ref[2:3, :]
    k_a = vec_ref[3:4, :]
    r_k = vec_ref[4:5, :]
    lnx_w = vec_ref[5:6, :]
    lnx_b = vec_ref[6:7, :]

    lane = lax.broadcasted_iota(I32, (tb, LANES), 1)
    first = lane < RWKV_HEAD
    x0 = xl[:, :LANES]
    w2a2 = w2a2_ref[...]
    lw = _dot(jnp.where(first, jnp.tanh(x0), 0.0).astype(BF16), w2a2)
    la = _dot(jnp.where(first, 0.0, x0).astype(BF16), w2a2)
    xg = xl[:, LANES:]
    g = _dot((1.0 / (1.0 + jnp.exp(-xg))).astype(BF16), g2_ref[...])

    zw = -(w0 + lw)
    softplus = jnp.maximum(zw, 0.0) + jnp.log(1.0 + jnp.exp(-jnp.abs(zw)))
    w = -softplus - 0.5
    a = 1.0 / (1.0 + jnp.exp(-(a0 + la)))
    logd = -jnp.exp(w)

    ri = lax.broadcasted_iota(I32, (LANES, LANES), 0)
    ci = lax.broadcasted_iota(I32, (LANES, LANES), 1)
    same = (ri // RWKV_HEAD) == (ci // RWKV_HEAD)
    ones_bd = jnp.where(same, 1.0, 0.0).astype(BF16)

    kk = k * k_k
    nrm = jnp.sqrt(_split_dot(kk * kk, ones_bd))
    kk = kk / jnp.maximum(nrm, 1e-12)
    kp = k * (1.0 + (a - 1.0) * k_a)
    bvec = kk * a
    avec = -kk
    bonus = _split_dot(r * kp * r_k, ones_bd) * v

    tri_r = lax.broadcasted_iota(I32, (c, c), 0)
    tri_c = lax.broadcasted_iota(I32, (c, c), 1)
    tri = jnp.where(tri_c <= tri_r, 1.0, 0.0).astype(BF16)
    same_chunk = (ri // c) == (ci // c)
    strict = same_chunk & ((ci % c) < (ri % c))
    incl = same_chunk & ((ci % c) <= (ri % c))
    eye = jnp.where(ri == ci, 1.0, 0.0)
    lane_c = lax.broadcasted_iota(I32, (c, LANES), 1)
    first_c = lane_c < RWKV_HEAD

    def stack(z):
        return jnp.concatenate([jnp.where(first_c, z, 0.0), jnp.where(first_c, 0.0, z)],
                               axis=0).astype(BF16)

    ys = []
    for ch in range(tb // c):
        sl = slice(ch * c, (ch + 1) * c)
        ld = logd[sl]
        lc = _split_dot_left(tri, ld)
        lprev = lc - ld
        ltot = lc[c - 1:c, :]
        e_neg = jnp.exp(-lc)
        e_rem = jnp.exp(ltot - lc)
        xa = stack(avec[sl] * jnp.exp(lprev))
        xr = stack(r[sl] * jnp.exp(lc))
        yb = stack(bvec[sl] * e_neg)
        yk = stack(kp[sl] * e_neg)
        vs = stack(v[sl])
        bh = stack(bvec[sl] * e_rem)
        kh = stack(kp[sl] * e_rem)

        lab = jnp.where(strict, _dot_nt(xa, yb), 0.0)
        lak = jnp.where(strict, _dot_nt(xa, yk), 0.0).astype(BF16)
        mrb = jnp.where(incl, _dot_nt(xr, yb), 0.0).astype(BF16)
        mrk = jnp.where(incl, _dot_nt(xr, yk), 0.0).astype(BF16)

        pw = lab
        tinv = eye + lab
        n_sq = 1
        while n_sq * 2 < c:
            pwb = pw.astype(BF16)
            pw = _dot(pwb, pwb)
            tinv = tinv + _dot(pw.astype(BF16), tinv.astype(BF16))
            n_sq *= 2

        s0 = state_ref[...]
        s0b = s0.astype(BF16)
        rhs = _dot_nt(xa, s0b) + _dot(lak, vs)
        u = _dot(tinv.astype(BF16), rhs.astype(BF16))
        ub = u.astype(BF16)
        ystack = _dot_nt(xr, s0b) + _dot(mrb, ub) + _dot(mrk, vs)
        ys.append(ystack[:c] + ystack[c:])
        state_ref[...] = s0 * jnp.exp(ltot) + _dot_tn(ub, bh) + _dot_tn(vs, kh)

    y = jnp.concatenate(ys, axis=0)
    inv_n = 1.0 / RWKV_HEAD
    mean = _split_dot(y, ones_bd) * inv_n
    d = y - mean
    var = _split_dot(d * d, ones_bd) * inv_n
    yn = d * lax.rsqrt(var + LNX_EPS) * lnx_w + lnx_b
    o_ref[...] = ((yn + bonus) * g).astype(o_ref.dtype)


def _split_dot_left(m, x):
    hi = x.astype(BF16)
    lo = (x - hi.astype(F32)).astype(BF16)
    return _dot(m, hi) + _dot(m, lo)


def _rwkv(p3, mu_pad, vecs, w2a2, g2):
    bsz, s, _ = p3.shape
    tb = _pick(s, (256, 128, 64))
    pairs = RWKV_HEADS // 2
    colblk = lambda off: (lambda b, hp, t: (b, t, off // LANES + hp))
    mublk = lambda off: (lambda b, hp, t: (0, off // LANES + hp))
    return pl.pallas_call(
        functools.partial(_rwkv_body, tb=tb),
        out_shape=jax.ShapeDtypeStruct((bsz, s, RWKV_WIDTH), BF16),
        grid=(bsz, pairs, s // tb),
        in_specs=[
            pl.BlockSpec((None, tb, LANES), colblk(P_R)),
            pl.BlockSpec((None, tb, LANES), colblk(P_K)),
            pl.BlockSpec((None, tb, LANES), colblk(P_V)),
            pl.BlockSpec((None, tb, 2 * LANES), lambda b, hp, t: (b, t, P_LORA // (2 * LANES))),
            pl.BlockSpec((1, LANES), mublk(P_R)),
            pl.BlockSpec((1, LANES), mublk(P_K)),
            pl.BlockSpec((1, LANES), mublk(P_V)),
            pl.BlockSpec((1, 2 * LANES), lambda b, hp, t: (0, P_LORA // (2 * LANES))),
            pl.BlockSpec((SUBLANES, LANES), lambda b, hp, t: (0, hp)),
            pl.BlockSpec((LANES, LANES), lambda b, hp, t: (0, hp)),
            pl.BlockSpec((LANES, LANES), lambda b, hp, t: (0, hp)),
        ],
        out_specs=pl.BlockSpec((None, tb, LANES), lambda b, hp, t: (b, t, hp)),
        scratch_shapes=[pltpu.VMEM((LANES, LANES), F32),
                        pltpu.VMEM((1, LANES), F32), pltpu.VMEM((1, LANES), F32),
                        pltpu.VMEM((1, LANES), F32), pltpu.VMEM((1, 2 * LANES), F32)],
        name="rwkv7",
        compiler_params=_cparams(("parallel", "parallel", "arbitrary")),
    )(p3, p3, p3, p3, mu_pad, mu_pad, mu_pad, mu_pad, vecs, w2a2, g2)


def _conv_body(gb_ref, uu_ref, halo_ref, cw_ref, o_ref):
    i = pl.program_id(1)
    uu = uu_ref[...]
    halo = jnp.where(i == 0, 0.0, halo_ref[...])
    row = lax.broadcasted_iota(I32, uu.shape, 0)
    h1 = halo[SUBLANES - 1:SUBLANES, :]
    h2 = halo[SUBLANES - 2:SUBLANES - 1, :]
    s1 = jnp.where(row == 0, h1, pltpu.roll(uu, 1, axis=0))
    s2 = jnp.where(row == 0, h2, jnp.where(row == 1, h1, pltpu.roll(uu, 2, axis=0)))
    y = cw_ref[0:1, :] * s2 + cw_ref[1:2, :] * s1 + cw_ref[2:3, :] * uu
    o_ref[...] = (gb_ref[...] * y).astype(o_ref.dtype)


def _conv_gate(gate_b, uu, conv_w):
    bsz, s, d = uu.shape
    ts = _pick(s, (512, 256, 128))
    td = 1024
    cw = jnp.concatenate([conv_w, jnp.zeros((SUBLANES - CONV_WIDTH, d), F32)], axis=0)
    blk = lambda b, i, j: (b, i, j)
    return pl.pallas_call(
        _conv_body,
        out_shape=jax.ShapeDtypeStruct((bsz, s, d), BF16),
        grid=(bsz, s // ts, d // td),
        in_specs=[
            pl.BlockSpec((None, ts, td), blk),
            pl.BlockSpec((None, ts, td), blk),
            pl.BlockSpec((None, SUBLANES, td),
                         lambda b, i, j: (b, jnp.maximum(i * (ts // SUBLANES) - 1, 0), j)),
            pl.BlockSpec((SUBLANES, td), lambda b, i, j: (0, j)),
        ],
        out_specs=pl.BlockSpec((None, ts, td), blk),
        name="conv_gate",
        compiler_params=_cparams(("parallel", "parallel", "parallel")),
    )(gate_b, uu, uu, cw)


def _extract_top(x, iota, n, emit):
    width = float(x.shape[-1])
    for kk in range(n):
        m = jnp.max(x, axis=-1, keepdims=True)
        idx = jnp.min(jnp.where(x == m, iota, width), axis=-1, keepdims=True)
        emit(kk, m, idx)
        x = jnp.where(iota == idx, -jnp.inf, x)


def _exact_dot(x, m):
    hi = x.astype(BF16)
    r1 = x - hi.astype(F32)
    mid = r1.astype(BF16)
    lo = (r1 - mid.astype(F32)).astype(BF16)
    return (_dot(hi, m) + _dot(mid, m)) + _dot(lo, m)


def _route_body(q_ref, keys_ref, e_ref, gw_ref):
    tb = q_ref.shape[0]
    kt = PEER_TOPK
    lane = lax.broadcasted_iota(I32, (tb, LANES), 1)
    lane_f = lane.astype(F32)
    lane2_f = lax.broadcasted_iota(I32, (tb, 2 * LANES), 1).astype(F32)
    er = lax.broadcasted_iota(I32, (LANES, 2 * LANES), 0)
    ec = lax.broadcasted_iota(I32, (LANES, 2 * LANES), 1)
    expand_hi = jnp.where((er < kt) & (ec // kt == er), 1.0, 0.0).astype(BF16)
    expand_lo = jnp.where((er < kt) & (ec % kt == er), 1.0, 0.0).astype(BF16)

    e_out = jnp.zeros((tb, LANES), F32)
    cv_out = jnp.zeros((tb, LANES), F32)
    mx_out = jnp.zeros((tb, LANES), F32)
    for h in range(PEER_HEADS):
        svs, sis = [], []
        for half in range(2):
            hp = h * 2 + half
            qh = q_ref[:, hp * PEER_HALF:(hp + 1) * PEER_HALF].astype(BF16)
            sc = _dot_nt(qh, keys_ref[hp])
            acc = {"v": jnp.zeros((tb, LANES), F32), "i": jnp.zeros((tb, LANES), F32)}

            def emit1(kk, m, idx, acc=acc):
                acc["v"] = jnp.where(lane == kk, m, acc["v"])
                acc["i"] = jnp.where(lane == kk, idx, acc["i"])

            _extract_top(sc, lane_f, kt, emit1)
            svs.append(acc["v"])
            sis.append(acc["i"])
        cand = _exact_dot(svs[0], expand_hi) + _exact_dot(svs[1], expand_lo)
        eids = (_dot(sis[0].astype(BF16), expand_hi) * float(N_KEYS)
                + _dot(sis[1].astype(BF16), expand_lo))
        out = {"e": e_out, "cv": cv_out, "mx": mx_out}

        def emit2(kk, m, idx, out=out, h=h, eids=eids):
            sel = lane == (h * kt + kk)
            e_k = jnp.max(jnp.where(lane2_f == idx, eids, -1.0), axis=-1, keepdims=True)
            out["e"] = jnp.where(sel, e_k, out["e"])
            out["cv"] = jnp.where(sel, m, out["cv"])
            if kk == 0:
                out["mx"] = jnp.where(lane // kt == h, m, out["mx"])

        _extract_top(cand, lane2_f, kt, emit2)
        e_out, cv_out, mx_out = out["e"], out["cv"], out["mx"]

    gr = lax.broadcasted_iota(I32, (LANES, LANES), 0)
    gc = lax.broadcasted_iota(I32, (LANES, LANES), 1)
    group = jnp.where(gr // kt == gc // kt, 1.0, 0.0).astype(BF16)
    ex = jnp.exp(cv_out - mx_out)
    den = _split_dot(ex, group)
    gw_ref[...] = ex / den
    e_ref[...] = e_out.astype(I32)


def _peer_route(q, keys):
    t = q.shape[0]
    tb = _pick(t, (256, 128))
    return pl.pallas_call(
        _route_body,
        out_shape=[jax.ShapeDtypeStruct((t, LANES), I32), jax.ShapeDtypeStruct((t, LANES), F32)],
        grid=(t // tb,),
        in_specs=[pl.BlockSpec((tb, q.shape[1]), lambda i: (i, 0)),
                  pl.BlockSpec(keys.shape, lambda i: (0, 0, 0))],
        out_specs=[pl.BlockSpec((tb, LANES), lambda i: (i, 0)),
                   pl.BlockSpec((tb, LANES), lambda i: (i, 0))],
        name="peer_route",
        compiler_params=_cparams(("parallel",)),
    )(q, keys)


def _gate_body(e_ref, gw_ref, o_ref):
    tb = e_ref.shape[0]
    sub = lax.broadcasted_iota(I32, (N_KEYS, LANES), 0)

    def one(t, carry):
        e = e_ref[pl.ds(t, 1), :]
        gw = gw_ref[pl.ds(t, 1), :]
        a = jnp.where((e >> 7) == sub, gw, 0.0).astype(BF16)
        b = jnp.where((e & (N_KEYS - 1)) == sub, 1.0, 0.0).astype(BF16)
        o_ref[t] = _dot_nt(a, b).astype(o_ref.dtype)
        return carry

    lax.fori_loop(0, tb, one, 0)


def _peer_gates(e, gw):
    t = e.shape[0]
    tb = _pick(t, (128,))
    return pl.pallas_call(
        _gate_body,
        out_shape=jax.ShapeDtypeStruct((t, N_KEYS, N_KEYS), BF16),
        grid=(t // tb,),
        in_specs=[pl.BlockSpec((tb, LANES), lambda i: (i, 0)),
                  pl.BlockSpec((tb, LANES), lambda i: (i, 0))],
        out_specs=pl.BlockSpec((tb, N_KEYS, N_KEYS), lambda i: (i, 0, 0)),
        name="peer_gates",
        compiler_params=_cparams(("parallel",)),
    )(e, gw)


def _gelu_gate(acc, g):
    act = 0.5 * acc * (1.0 + lax.erf(acc * (2.0 ** -0.5)))
    return act * g.astype(F32)


def _residual(acc, x, gate):
    return x + gate * acc


def _gated_residual_matmul(a, w, x2, gate, s, *, tk, name):
    t, d = x2.shape
    bsz = gate.shape[0]
    tm = _pick(s, (1024, 512, 256, 128))
    tn = 1024
    (out,) = _matmul(
        a, [w], [0], d, tm=tm, tn=tn, tk=tk, out_dtypes=[F32], epilogue=_residual,
        extras=[(x2, (tm, tn), lambda i, j: (i, j)),
                (gate.reshape(bsz, 1, d), (None, 1, tn), lambda i, j: (i * tm // s, 0, j))],
        name=name)
    return out


def _peer_ffn(x, w_norm, shift, scale, gate, w_q, keys, u_tab, v_tab):
    bsz, s, d = x.shape
    t = bsz * s
    h = _norm_mod(x, w_norm, shift, scale).reshape(t, d)
    tm = _pick(s, (1024, 512, 256, 128))
    (q,) = _matmul(h, [w_q.astype(BF16)], [0], w_q.shape[1], tm=tm, tn=1024, tk=d,
                   out_dtypes=[F32], name="peer_query")
    keys_b = keys.reshape(PEER_HEADS * 2, N_KEYS, PEER_HALF).astype(BF16)
    e, gw = _peer_route(q, keys_b)
    n_exp = N_KEYS * N_KEYS
    g = _peer_gates(e, gw).reshape(t, n_exp)
    (pmat,) = _matmul(h, [u_tab.astype(BF16)], [0], n_exp, tm=tm, tn=1024, tk=d,
                      out_dtypes=[BF16], epilogue=_gelu_gate, trans_b=True,
                      extras=[(g, (tm, 1024), lambda i, j: (i, j))], name="peer_up")
    out = _gated_residual_matmul(pmat, v_tab.astype(BF16), x.reshape(t, d), gate, s,
                                 tk=2048, name="peer_down")
    return out.reshape(bsz, s, d)


def _pad_in_proj(w):
    o_kv = Q_LORA
    o_pe = Q_LORA + KV_LORA
    o_rw = o_pe + QK_ROPE
    half = QK_ROPE // 2
    rows = w.shape[0]
    z = lambda n: jnp.zeros((rows, n), w.dtype)
    cq = w[:, :Q_LORA]
    ckv = w[:, o_kv:o_pe]
    kpe = w[:, o_pe:o_rw]
    kpe_swap = jnp.concatenate([kpe[:, half:], kpe[:, :half]], axis=1)
    r = w[:, o_rw:o_rw + RWKV_WIDTH]
    k = w[:, o_rw + RWKV_WIDTH:o_rw + 2 * RWKV_WIDTH]
    v = w[:, o_rw + 2 * RWKV_WIDTH:o_rw + 3 * RWKV_WIDTH]
    lora = w[:, o_rw + 3 * RWKV_WIDTH:]
    pad = LANES - QK_ROPE
    out = jnp.concatenate([ckv, kpe, z(pad), kpe_swap, z(pad), cq, lora,
                           z(P_R - P_LORA - lora.shape[1]), r, k, v], axis=1)
    assert out.shape[1] == P_WIDTH
    return out


def _hybrid_mixer(h, positions, w_in, q_norm_w, w_uq, kv_norm_w, w_ukv, qk_q_w, qk_k_w,
                  mu, w0, w2, a0, a2, g2, k_k, k_a, r_k, lnx_w, lnx_b):
    bsz, s, d = h.shape
    t = bsz * s
    tm = _pick(s, (1024, 512, 256, 128))
    w_pad = _pad_in_proj(w_in).astype(BF16)
    (p,) = _matmul(h.reshape(t, d), [w_pad], [0], P_WIDTH, tm=tm, tn=1024, tk=d,
                   out_dtypes=[F32], name="hyb_in_proj")
    q, k, v = _mla_qkv(p, positions, q_norm_w, w_uq, kv_norm_w, w_ukv, qk_q_w, qk_k_w)
    y_mla = _attention(q.reshape(bsz, s, -1), k.reshape(bsz, s, -1), v.reshape(bsz, s, -1), bsz, s)

    mu_full = jnp.concatenate([jnp.zeros((Q_LORA + KV_LORA + QK_ROPE,), F32), mu]).reshape(1, -1)
    mu_pad = _pad_in_proj(mu_full)
    zrow = jnp.zeros((RWKV_WIDTH,), F32)
    vecs = jnp.stack([w0, a0, k_k, k_a, r_k.reshape(-1), lnx_w, lnx_b, zrow])
    w2a2 = jnp.concatenate([w2, a2], axis=0).astype(BF16)
    y_rwkv = _rwkv(p.reshape(bsz, s, P_WIDTH), mu_pad, vecs, w2a2, g2.astype(BF16))
    return jnp.concatenate([y_mla, y_rwkv], axis=-1).reshape(t, -1)


def _conv_mixer(h, w_in, conv_w, s):
    t, d = h.shape
    bsz = t // s
    tm = _pick(s, (512, 256, 128))
    tn = 512
    nblk = d // tn
    w = w_in.astype(BF16)
    gate_b, uu = _matmul(h, [w, w, w], [0, nblk, 2 * nblk], d, tm=tm, tn=tn, tk=d,
                         out_dtypes=[F32, F32], epilogue=lambda gb, gc, u: (gb, gc * u),
                         name="conv_in_proj")
    z = _conv_gate(gate_b.reshape(bsz, s, d), uu.reshape(bsz, s, d), conv_w)
    return z.reshape(t, d)


def kernel(x, c, positions, ada_w, ada_b, norm_mix_w, norm_ffn_w, hyb_w_in, mla_q_norm_w, mla_w_uq, mla_kv_norm_w, mla_w_ukv, mla_qk_q_w, mla_qk_k_w, rwkv_mu, rwkv_w0, rwkv_w2, rwkv_a0, rwkv_a2, rwkv_g2, rwkv_k_k, rwkv_k_a, rwkv_r_k, rwkv_lnx_w, rwkv_lnx_b, hyb_w_out, conv_w_in, conv_w, conv_w_out, peer_w_q, peer_keys, peer_u, peer_v):
    bsz, s, d = x.shape
    t = bsz * s
    depth = ada_w.shape[0]
    mod = _ada_mod(c, ada_w, ada_b)
    for layer in range(depth):
        sh_m, sc_m, gt_m, sh_f, sc_f, gt_f = [mod[layer, :, j * d:(j + 1) * d] for j in range(6)]
        h = _norm_mod(x, norm_mix_w[layer], sh_m, sc_m)
        i = layer // 2
        if layer % 2 == 0:
            y = _hybrid_mixer(h, positions, hyb_w_in[i], mla_q_norm_w[i], mla_w_uq[i],
                              mla_kv_norm_w[i], mla_w_ukv[i], mla_qk_q_w[i], mla_qk_k_w[i],
                              rwkv_mu[i], rwkv_w0[i], rwkv_w2[i], rwkv_a0[i], rwkv_a2[i],
                              rwkv_g2[i], rwkv_k_k[i], rwkv_k_a[i], rwkv_r_k[i],
                              rwkv_lnx_w[i], rwkv_lnx_b[i])
            w_out = hyb_w_out[i].astype(BF16)
        else:
            y = _conv_mixer(h.reshape(t, d), conv_w_in[i], conv_w[i], s)
            w_out = conv_w_out[i].astype(BF16)
        x = _gated_residual_matmul(y, w_out, x.reshape(t, d), gt_m, s, tk=d,
                                   name="mixer_out_proj").reshape(bsz, s, d)
        x = _peer_ffn(x, norm_ffn_w[layer], sh_f, sc_f, gt_f, peer_w_q[layer], peer_keys[layer],
                      peer_u[layer], peer_v[layer])
    return x
```

```python
import functools

import numpy as np
import jax
import jax.numpy as jnp
from jax import lax
from jax.experimental import pallas as pl
from jax.experimental.pallas import tpu as pltpu

F32 = jnp.float32
BF16 = jnp.bfloat16
I32 = jnp.int32

MLA_HEADS = 16
QK_NOPE = 128
QK_ROPE = 64
QK_HEAD = QK_NOPE + QK_ROPE
V_HEAD = 128
Q_LORA = 768
KV_LORA = 512
ROPE_THETA = 10000.0
RWKV_HEAD = 64
RWKV_HEADS = 32
RWKV_WIDTH = RWKV_HEADS * RWKV_HEAD
W_LORA = 64
A_LORA = 64
G_LORA = 128
LNX_EPS = 64e-5
CONV_WIDTH = 3
PEER_HEADS = 8
N_KEYS = 128
PEER_TOPK = 16
PEER_HALF = 128
NORM_EPS = 1e-6

LANES = 128
SUBLANES = 8
VMEM_LIMIT = 56 * 1024 * 1024

P_CKV = 0
P_KPE = 512
P_CQ = 768
P_LORA = 1536
P_R = 2048
P_K = P_R + RWKV_WIDTH
P_V = P_K + RWKV_WIDTH
P_WIDTH = P_V + RWKV_WIDTH
RWKV_CHUNK = 64


def _cparams(sem):
    return pltpu.CompilerParams(dimension_semantics=sem, vmem_limit_bytes=VMEM_LIMIT)


def _dot(a, b):
    return lax.dot_general(a, b, (((1,), (0,)), ((), ())), preferred_element_type=F32)


def _dot_nt(a, b):
    return lax.dot_general(a, b, (((1,), (1,)), ((), ())), preferred_element_type=F32)


def _dot_tn(a, b):
    return lax.dot_general(a, b, (((0,), (0,)), ((), ())), preferred_element_type=F32)


def _split_dot(x, m):
    hi = x.astype(BF16)
    lo = (x - hi.astype(F32)).astype(BF16)
    return _dot(hi, m) + _dot(lo, m)


def _mm_body(*refs, nb, nk, n_extra, trans_b, epilogue):
    a_ref = refs[0]
    b_refs = refs[1:1 + nb]
    extra = refs[1 + nb:1 + nb + n_extra]
    o_refs = refs[1 + nb + n_extra:]
    a = a_ref[...].astype(BF16)
    parts = []
    for b_ref in b_refs:
        b = b_ref[...].astype(BF16)
        parts.append(_dot_nt(a, b) if trans_b else _dot(a, b))

    def finish(accs):
        outs = epilogue(*accs, *[e[...] for e in extra])
        if not isinstance(outs, (tuple, list)):
            outs = (outs,)
        for o_ref, o in zip(o_refs, outs):
            o_ref[...] = o.astype(o_ref.dtype)

    if nk == 1:
        finish(parts)
        return
    n_out = len(o_refs) - nb
    acc_refs = o_refs[n_out:]
    o_refs = o_refs[:n_out]
    k = pl.program_id(2)

    @pl.when(k == 0)
    def _():
        for acc, p in zip(acc_refs, parts):
            acc[...] = p

    @pl.when(k > 0)
    def _():
        for acc, p in zip(acc_refs, parts):
            acc[...] += p

    @pl.when(k == nk - 1)
    def _():
        finish([acc[...] for acc in acc_refs])


def _matmul(a, bs, b_col_blocks, n_out_cols, *, tm, tn, tk, out_dtypes, epilogue=None,
            extras=(), trans_b=False, b_lead=None, name="matmul"):
    m, kdim = a.shape
    nb = len(bs)
    nk = kdim // tk
    assert m % tm == 0 and kdim % tk == 0 and n_out_cols % tn == 0
    if epilogue is None:
        epilogue = lambda acc: acc
    grid = (m // tm, n_out_cols // tn, nk)
    in_specs = [pl.BlockSpec((tm, tk), lambda i, j, k: (i, k))]
    for off in b_col_blocks:
        if b_lead is not None:
            assert not trans_b
            in_specs.append(pl.BlockSpec((None, tk, tn),
                                         lambda i, j, k, off=off: (b_lead, k, j + off)))
        elif trans_b:
            in_specs.append(pl.BlockSpec((tn, tk), lambda i, j, k, off=off: (j + off, k)))
        else:
            in_specs.append(pl.BlockSpec((tk, tn), lambda i, j, k, off=off: (k, j + off)))
    extra_arrays = []
    for arr, bshape, imap in extras:
        in_specs.append(pl.BlockSpec(bshape, lambda i, j, k, imap=imap: imap(i, j)))
        extra_arrays.append(arr)
    out_shape = [jax.ShapeDtypeStruct((m, n_out_cols), dt) for dt in out_dtypes]
    out_specs = [pl.BlockSpec((tm, tn), lambda i, j, k: (i, j)) for _ in out_dtypes]
    scratch = [pltpu.VMEM((tm, tn), F32) for _ in range(nb)] if nk > 1 else []
    body = functools.partial(_mm_body, nb=nb, nk=nk, n_extra=len(extras), trans_b=trans_b,
                             epilogue=epilogue)
    outs = pl.pallas_call(
        body, out_shape=out_shape, grid=grid, in_specs=in_specs, out_specs=out_specs,
        scratch_shapes=scratch, name=name,
        compiler_params=_cparams(("parallel", "parallel", "arbitrary")),
    )(a, *bs, *extra_arrays)
    return outs


def _pick(n, prefs):
    for p in prefs:
        if n % p == 0:
            return p
    raise ValueError(f"no tile for {n}")


def _ada_body(c_ref, w_ref, b_ref, o_ref):
    c = c_ref[...]
    cond = (c * (1.0 / (1.0 + jnp.exp(-c)))).astype(BF16)
    o_ref[...] = _dot(cond, w_ref[...].astype(BF16)) + b_ref[...]


def _ada_mod(c, ada_w, ada_b):
    depth, d, n = ada_w.shape
    bsz = c.shape[0]
    tn = 512
    return pl.pallas_call(
        _ada_body,
        out_shape=jax.ShapeDtypeStruct((depth, bsz, n), F32),
        grid=(depth, n // tn),
        in_specs=[
            pl.BlockSpec((bsz, d), lambda l, j: (0, 0)),
            pl.BlockSpec((None, d, tn), lambda l, j: (l, 0, j)),
            pl.BlockSpec((None, 1, tn), lambda l, j: (l, 0, j)),
        ],
        out_specs=pl.BlockSpec((None, bsz, tn), lambda l, j: (l, 0, j)),
        name="ada_mod",
        compiler_params=_cparams(("parallel", "parallel")),
    )(c, ada_w, ada_b.reshape(depth, 1, n))


def _norm_mod_body(x_ref, w_ref, sh_ref, sc_ref, o_ref):
    x = x_ref[...]
    ms = jnp.mean(x * x, axis=-1, keepdims=True)
    y = x * lax.rsqrt(ms + NORM_EPS) * w_ref[...]
    o_ref[...] = (y * (1.0 + sc_ref[...]) + sh_ref[...]).astype(o_ref.dtype)


def _norm_mod(x, w, shift, scale):
    bsz, s, d = x.shape
    ts = _pick(s, (256, 128))
    vec = lambda v: v.reshape(bsz, 1, d)
    return pl.pallas_call(
        _norm_mod_body,
        out_shape=jax.ShapeDtypeStruct((bsz, s, d), BF16),
        grid=(bsz, s // ts),
        in_specs=[
            pl.BlockSpec((None, ts, d), lambda b, i: (b, i, 0)),
            pl.BlockSpec((1, d), lambda b, i: (0, 0)),
            pl.BlockSpec((None, 1, d), lambda b, i: (b, 0, 0)),
            pl.BlockSpec((None, 1, d), lambda b, i: (b, 0, 0)),
        ],
        out_specs=pl.BlockSpec((None, ts, d), lambda b, i: (b, i, 0)),
        name="norm_mod",
        compiler_params=_cparams(("parallel", "parallel")),
    )(x, w.reshape(1, d), vec(shift), vec(scale))


def _rope_tables(pos_ref, freq_ref):
    ang = pos_ref[...].astype(F32) * freq_ref[...]
    lane = lax.broadcasted_iota(I32, ang.shape, 1)
    cos = jnp.where(lane < QK_ROPE, jnp.cos(ang), 0.0)
    sin = jnp.sin(ang)
    sin_signed = jnp.where(lane < QK_ROPE // 2, -sin, jnp.where(lane < QK_ROPE, sin, 0.0))
    return cos, sin_signed


def _mla_q_body(cq_ref, nw_ref, wuq_ref, hw_ref, pos_ref, freq_ref, o_ref, *, scale):
    cq = cq_ref[...]
    ms = jnp.mean(cq * cq, axis=-1, keepdims=True)
    cqn = (cq * lax.rsqrt(ms + NORM_EPS) * nw_ref[...]).astype(BF16)
    cos, sin_signed = _rope_tables(pos_ref, freq_ref)
    w_nope = hw_ref[0:1, :]
    w_rope = hw_ref[1:2, :]
    w_swap = hw_ref[2:3, :]
    for h in range(MLA_HEADS):
        qh = _dot(cqn, wuq_ref[:, h * 3 * LANES:(h + 1) * 3 * LANES])
        nope = qh[:, :LANES]
        rope = qh[:, LANES:2 * LANES]
        swap = qh[:, 2 * LANES:]
        ss = jnp.sum(nope * nope + rope * rope, axis=-1, keepdims=True)
        fac = lax.rsqrt(ss * (1.0 / QK_HEAD) + NORM_EPS) * scale
        o_ref[:, h * 2 * LANES:h * 2 * LANES + LANES] = (nope * fac * w_nope).astype(o_ref.dtype)
        roped = (rope * w_rope * cos + swap * w_swap * sin_signed) * fac
        o_ref[:, h * 2 * LANES + LANES:(h + 1) * 2 * LANES] = roped.astype(o_ref.dtype)


def _mla_kv_body(ckv_ref, kpe_ref, nw_ref, wukv_ref, hw_ref, pos_ref, freq_ref, k_ref, v_ref):
    ckv = ckv_ref[...]
    ms = jnp.mean(ckv * ckv, axis=-1, keepdims=True)
    ckvn = (ckv * lax.rsqrt(ms + NORM_EPS) * nw_ref[...]).astype(BF16)
    cos, sin_signed = _rope_tables(pos_ref, freq_ref)
    kpe = kpe_ref[:, :LANES]
    kpe_swap = kpe_ref[:, LANES:]
    pe_ss = jnp.sum(kpe * kpe, axis=-1, keepdims=True)
    pe_roped = kpe * hw_ref[1:2, :] * cos + kpe_swap * hw_ref[2:3, :] * sin_signed
    w_nope = hw_ref[0:1, :]
    for h in range(MLA_HEADS):
        kn = _dot(ckvn, wukv_ref[:, h * LANES:(h + 1) * LANES])
        ss = jnp.sum(kn * kn, axis=-1, keepdims=True) + pe_ss
        fac = lax.rsqrt(ss * (1.0 / QK_HEAD) + NORM_EPS)
        k_ref[:, h * 2 * LANES:h * 2 * LANES + LANES] = (kn * fac * w_nope).astype(k_ref.dtype)
        k_ref[:, h * 2 * LANES + LANES:(h + 1) * 2 * LANES] = (pe_roped * fac).astype(k_ref.dtype)
    nv = MLA_HEADS * V_HEAD
    v_ref[...] = _dot(ckvn, wukv_ref[:, nv:]).astype(v_ref.dtype)


def _head_norm_rows(w):
    half = QK_ROPE // 2
    zeros = jnp.zeros((LANES - QK_ROPE,), F32)
    rope = jnp.concatenate([w[QK_NOPE:], zeros])
    swap = jnp.concatenate([w[QK_NOPE + half:], w[QK_NOPE:QK_NOPE + half], zeros])
    rows = jnp.stack([w[:QK_NOPE], rope, swap])
    return jnp.concatenate([rows, jnp.zeros((SUBLANES - 3, LANES), F32)], axis=0)


def _rope_freq_row():
    half = QK_ROPE // 2
    f = ROPE_THETA ** (-jnp.arange(half, dtype=F32) / half)
    return jnp.concatenate([f, f, jnp.zeros((LANES - QK_ROPE,), F32)]).reshape(1, LANES)


def _wuq_layout():
    half = QK_ROPE // 2
    idx, mask = [], []
    for h in range(MLA_HEADS):
        base = h * QK_HEAD
        nope = list(range(base, base + QK_NOPE))
        rope = list(range(base + QK_NOPE, base + QK_HEAD))
        swap = rope[half:] + rope[:half]
        pad = [0] * (LANES - QK_ROPE)
        idx += nope + rope + pad + swap + pad
        mask += [1] * QK_NOPE + [1] * QK_ROPE + [0] * len(pad) + [1] * QK_ROPE + [0] * len(pad)
    return np.asarray(idx, np.int32), np.asarray(mask, np.float32)


def _wukv_layout():
    per = QK_NOPE + V_HEAD
    k_idx = [h * per + i for h in range(MLA_HEADS) for i in range(QK_NOPE)]
    v_idx = [h * per + QK_NOPE + i for h in range(MLA_HEADS) for i in range(V_HEAD)]
    return np.asarray(k_idx + v_idx, np.int32)


def _mla_qkv(p, positions, q_norm_w, w_uq, kv_norm_w, w_ukv, qk_q_w, qk_k_w):
    t = p.shape[0]
    tm = _pick(t, (256, 128))
    idx, mask = _wuq_layout()
    wuq = (w_uq[:, idx] * mask[None, :]).astype(BF16)
    wukv = w_ukv[:, _wukv_layout()].astype(BF16)
    pos = positions.reshape(t, 1)
    freq = _rope_freq_row()
    hq = _head_norm_rows(qk_q_w)
    hk = _head_norm_rows(qk_k_w)
    qw = MLA_HEADS * 2 * LANES
    row = lambda i: (i, 0)
    fixed = lambda i: (0, 0)
    q = pl.pallas_call(
        functools.partial(_mla_q_body, scale=QK_HEAD ** -0.5),
        out_shape=jax.ShapeDtypeStruct((t, qw), BF16),
        grid=(t // tm,),
        in_specs=[
            pl.BlockSpec((tm, Q_LORA), lambda i: (i, P_CQ // Q_LORA)),
            pl.BlockSpec((1, Q_LORA), fixed),
            pl.BlockSpec(wuq.shape, fixed),
            pl.BlockSpec((SUBLANES, LANES), fixed),
            pl.BlockSpec((tm, 1), row),
            pl.BlockSpec((1, LANES), fixed),
        ],
        out_specs=pl.BlockSpec((tm, qw), row),
        name="mla_q",
        compiler_params=_cparams(("parallel",)),
    )(p, q_norm_w.reshape(1, Q_LORA), wuq, hq, pos, freq)
    k, v = pl.pallas_call(
        _mla_kv_body,
        out_shape=[jax.ShapeDtypeStruct((t, qw), BF16),
                   jax.ShapeDtypeStruct((t, MLA_HEADS * V_HEAD), BF16)],
        grid=(t // tm,),
        in_specs=[
            pl.BlockSpec((tm, KV_LORA), lambda i: (i, P_CKV // KV_LORA)),
            pl.BlockSpec((tm, 2 * LANES), lambda i: (i, P_KPE // (2 * LANES))),
            pl.BlockSpec((1, KV_LORA), fixed),
            pl.BlockSpec(wukv.shape, fixed),
            pl.BlockSpec((SUBLANES, LANES), fixed),
            pl.BlockSpec((tm, 1), row),
            pl.BlockSpec((1, LANES), fixed),
        ],
        out_specs=[pl.BlockSpec((tm, qw), row), pl.BlockSpec((tm, MLA_HEADS * V_HEAD), row)],
        name="mla_kv",
        compiler_params=_cparams(("parallel",)),
    )(p, p, kv_norm_w.reshape(1, KV_LORA), wukv, hk, pos, freq)
    return q, k, v


def _attn_body(q_ref, k_ref, v_ref, o_ref, m_ref, l_ref, acc_ref, *, tq):
    qi = pl.program_id(2)
    ki = pl.program_id(3)

    @pl.when(ki == 0)
    def _():
        m_ref[...] = jnp.full(m_ref.shape, -jnp.inf, F32)
        l_ref[...] = jnp.zeros(l_ref.shape, F32)
        acc_ref[...] = jnp.zeros(acc_ref.shape, F32)

    def step(masked):
        s = _dot_nt(q_ref[...], k_ref[...])
        if masked:
            row = lax.broadcasted_iota(I32, s.shape, 0)
            col = lax.broadcasted_iota(I32, s.shape, 1)
            s = jnp.where(col <= row, s, -jnp.inf)
        reps = tq // LANES
        m_prev = m_ref[...]
        m_new = jnp.maximum(m_prev, jnp.max(s, axis=-1, keepdims=True))
        alpha = jnp.exp(m_prev - m_new)
        pr = jnp.exp(s - jnp.tile(m_new, (1, reps)))
        l_ref[...] = alpha * l_ref[...] + jnp.sum(pr, axis=-1, keepdims=True)
        acc_ref[...] = alpha * acc_ref[...] + _dot(pr.astype(BF16), v_ref[...])
        m_ref[...] = m_new

    @pl.when(ki < qi)
    def _():
        step(False)

    @pl.when(ki == qi)
    def _():
        step(True)
        o_ref[...] = (acc_ref[...] / l_ref[...]).astype(o_ref.dtype)


def _attention(q, k, v, bsz, s):
    tq = _pick(s, (512, 256, 128))
    n = s // tq
    return pl.pallas_call(
        functools.partial(_attn_body, tq=tq),
        out_shape=jax.ShapeDtypeStruct((bsz, s, MLA_HEADS * V_HEAD), BF16),
        grid=(bsz, MLA_HEADS, n, n),
        in_specs=[
            pl.BlockSpec((None, tq, 2 * LANES), lambda b, h, i, j: (b, i, h)),
            pl.BlockSpec((None, tq, 2 * LANES), lambda b, h, i, j: (b, jnp.minimum(i, j), h)),
            pl.BlockSpec((None, tq, V_HEAD), lambda b, h, i, j: (b, jnp.minimum(i, j), h)),
        ],
        out_specs=pl.BlockSpec((None, tq, V_HEAD), lambda b, h, i, j: (b, i, h)),
        scratch_shapes=[pltpu.VMEM((tq, LANES), F32), pltpu.VMEM((tq, LANES), F32),
                        pltpu.VMEM((tq, V_HEAD), F32)],
        name="mla_attention",
        compiler_params=_cparams(("parallel", "parallel", "parallel", "arbitrary")),
    )(q, k, v)


def _rwkv_body(r_ref, k_ref, v_ref, xl_ref, mu_r_ref, mu_k_ref, mu_v_ref, mu_l_ref, vec_ref,
               w2a2_ref, g2_ref, o_ref, state_ref, pr_ref, pk_ref, pv_ref, pl_ref, *, tb):
    c = RWKV_CHUNK
    t_idx = pl.program_id(2)

    @pl.when(t_idx == 0)
    def _():
        state_ref[...] = jnp.zeros(state_ref.shape, F32)
        pr_ref[...] = jnp.zeros(pr_ref.shape, F32)
        pk_ref[...] = jnp.zeros(pk_ref.shape, F32)
        pv_ref[...] = jnp.zeros(pv_ref.shape, F32)
        pl_ref[...] = jnp.zeros(pl_ref.shape, F32)

    def mix(x_ref, prev_ref, mu_ref):
        x = x_ref[...]
        row = lax.broadcasted_iota(I32, x.shape, 0)
        shifted = jnp.where(row == 0, prev_ref[...], pltpu.roll(x, 1, axis=0))
        prev_ref[...] = x[tb - 1:tb, :]
        return x + (shifted - x) * mu_ref[...]

    r = mix(r_ref, pr_ref, mu_r_ref)
    k = mix(k_ref, pk_ref, mu_k_ref)
    v = mix(v_ref, pv_ref, mu_v_ref)
    xl = mix(xl_ref, pl_ref, mu_l_ref)

    w0 = vec_ref[0:1, :]
    a0 = vec_ref[1:2, :]
    k_k = vec_ref[2:3, :]
    k_a = vec_ref[3:4, :]
    r_k = vec_ref[4:5, :]
    lnx_w = vec_ref[5:6, :]
    lnx_b = vec_ref[6:7, :]

    lane = lax.broadcasted_iota(I32, (tb, LANES), 1)
    first = lane < RWKV_HEAD
    x0 = xl[:, :LANES]
    w2a2 = w2a2_ref[...]
    lw = _dot(jnp.where(first, jnp.tanh(x0), 0.0).astype(BF16), w2a2)
    la = _dot(jnp.where(first, 0.0, x0).astype(BF16), w2a2)
    xg = xl[:, LANES:]
    g = _dot((1.0 / (1.0 + jnp.exp(-xg))).astype(BF16), g2_ref[...])

    zw = -(w0 + lw)
    softplus = jnp.maximum(zw, 0.0) + jnp.log(1.0 + jnp.exp(-jnp.abs(zw)))
    w = -softplus - 0.5
    a = 1.0 / (1.0 + jnp.exp(-(a0 + la)))
    logd = -jnp.exp(w)

    ri = lax.broadcasted_iota(I32, (LANES, LANES), 0)
    ci = lax.broadcasted_iota(I32, (LANES, LANES), 1)
    same = (ri // RWKV_HEAD) == (ci // RWKV_HEAD)
    ones_bd = jnp.where(same, 1.0, 0.0).astype(BF16)

    kk = k * k_k
    nrm = jnp.sqrt(_split_dot(kk * kk, ones_bd))
    kk = kk / jnp.maximum(nrm, 1e-12)
    kp = k * (1.0 + (a - 1.0) * k_a)
    bvec = kk * a
    avec = -kk
    bonus = _split_dot(r * kp * r_k, ones_bd) * v

    tri_r = lax.broadcasted_iota(I32, (c, c), 0)
    tri_c = lax.broadcasted_iota(I32, (c, c), 1)
    tri = jnp.where(tri_c <= tri_r, 1.0, 0.0).astype(BF16)
    same_chunk = (ri // c) == (ci // c)
    strict = same_chunk & ((ci % c) < (ri % c))
    incl = same_chunk & ((ci % c) <= (ri % c))
    eye = jnp.where(ri == ci, 1.0, 0.0)
    lane_c = lax.broadcasted_iota(I32, (c, LANES), 1)
    first_c = lane_c < RWKV_HEAD

    def stack(z):
        return jnp.concatenate([jnp.where(first_c, z, 0.0), jnp.where(first_c, 0.0, z)],
                               axis=0).astype(BF16)

    ys = []
    for ch in range(tb // c):
        sl = slice(ch * c, (ch + 1) * c)
        ld = logd[sl]
        lc = _split_dot_left(tri, ld)
        lprev = lc - ld
        ltot = lc[c - 1:c, :]
        e_neg = jnp.exp(-lc)
        e_rem = jnp.exp(ltot - lc)
        xa = stack(avec[sl] * jnp.exp(lprev))
        xr = stack(r[sl] * jnp.exp(lc))
        yb = stack(bvec[sl] * e_neg)
        yk = stack(kp[sl] * e_neg)
        vs = stack(v[sl])
        bh = stack(bvec[sl] * e_rem)
        kh = stack(kp[sl] * e_rem)

        lab = jnp.where(strict, _dot_nt(xa, yb), 0.0)
        lak = jnp.where(strict, _dot_nt(xa, yk), 0.0).astype(BF16)
        mrb = jnp.where(incl, _dot_nt(xr, yb), 0.0).astype(BF16)
        mrk = jnp.where(incl, _dot_nt(xr, yk), 0.0).astype(BF16)

        pw = lab
        tinv = eye + lab
        n_sq = 1
        while n_sq * 2 < c:
            pwb = pw.astype(BF16)
            pw = _dot(pwb, pwb)
            tinv = tinv + _dot(pw.astype(BF16), tinv.astype(BF16))
            n_sq *= 2

        s0 = state_ref[...]
        s0b = s0.astype(BF16)
        rhs = _dot_nt(xa, s0b) + _dot(lak, vs)
        u = _dot(tinv.astype(BF16), rhs.astype(BF16))
        ub = u.astype(BF16)
        ystack = _dot_nt(xr, s0b) + _dot(mrb, ub) + _dot(mrk, vs)
        ys.append(ystack[:c] + ystack[c:])
        state_ref[...] = s0 * jnp.exp(ltot) + _dot_tn(ub, bh) + _dot_tn(vs, kh)

    y = jnp.concatenate(ys, axis=0)
    inv_n = 1.0 / RWKV_HEAD
    mean = _split_dot(y, ones_bd) * inv_n
    d = y - mean
    var = _split_dot(d * d, ones_bd) * inv_n
    yn = d * lax.rsqrt(var + LNX_EPS) * lnx_w + lnx_b
    o_ref[...] = ((yn + bonus) * g).astype(o_ref.dtype)


def _split_dot_left(m, x):
    hi = x.astype(BF16)
    lo = (x - hi.astype(F32)).astype(BF16)
    return _dot(m, hi) + _dot(m, lo)


def _rwkv(p3, mu_pad, vecs, w2a2, g2):
    bsz, s, _ = p3.shape
    tb = _pick(s, (256, 128, 64))
    pairs = RWKV_HEADS // 2
    colblk = lambda off: (lambda b, hp, t: (b, t, off // LANES + hp))
    mublk = lambda off: (lambda b, hp, t: (0, off // LANES + hp))
    return pl.pallas_call(
        functools.partial(_rwkv_body, tb=tb),
        out_shape=jax.ShapeDtypeStruct((bsz, s, RWKV_WIDTH), BF16),
        grid=(bsz, pairs, s // tb),
        in_specs=[
            pl.BlockSpec((None, tb, LANES), colblk(P_R)),
            pl.BlockSpec((None, tb, LANES), colblk(P_K)),
            pl.BlockSpec((None, tb, LANES), colblk(P_V)),
            pl.BlockSpec((None, tb, 2 * LANES), lambda b, hp, t: (b, t, P_LORA // (2 * LANES))),
            pl.BlockSpec((1, LANES), mublk(P_R)),
            pl.BlockSpec((1, LANES), mublk(P_K)),
            pl.BlockSpec((1, LANES), mublk(P_V)),
            pl.BlockSpec((1, 2 * LANES), lambda b, hp, t: (0, P_LORA // (2 * LANES))),
            pl.BlockSpec((SUBLANES, LANES), lambda b, hp, t: (0, hp)),
            pl.BlockSpec((LANES, LANES), lambda b, hp, t: (0, hp)),
            pl.BlockSpec((LANES, LANES), lambda b, hp, t: (0, hp)),
        ],
        out_specs=pl.BlockSpec((None, tb, LANES), lambda b, hp, t: (b, t, hp)),
        scratch_shapes=[pltpu.VMEM((LANES, LANES), F32),
                        pltpu.VMEM((1, LANES), F32), pltpu.VMEM((1, LANES), F32),
                        pltpu.VMEM((1, LANES), F32), pltpu.VMEM((1, 2 * LANES), F32)],
        name="rwkv7",
        compiler_params=_cparams(("parallel", "parallel", "arbitrary")),
    )(p3, p3, p3, p3, mu_pad, mu_pad, mu_pad, mu_pad, vecs, w2a2, g2)


def _conv_body(gb_ref, uu_ref, halo_ref, cw_ref, o_ref):
    i = pl.program_id(1)
    uu = uu_ref[...]
    halo = jnp.where(i == 0, 0.0, halo_ref[...])
    row = lax.broadcasted_iota(I32, uu.shape, 0)
    h1 = halo[SUBLANES - 1:SUBLANES, :]
    h2 = halo[SUBLANES - 2:SUBLANES - 1, :]
    s1 = jnp.where(row == 0, h1, pltpu.roll(uu, 1, axis=0))
    s2 = jnp.where(row == 0, h2, jnp.where(row == 1, h1, pltpu.roll(uu, 2, axis=0)))
    y = cw_ref[0:1, :] * s2 + cw_ref[1:2, :] * s1 + cw_ref[2:3, :] * uu
    o_ref[...] = (gb_ref[...] * y).astype(o_ref.dtype)


def _conv_gate(gate_b, uu, conv_w):
    bsz, s, d = uu.shape
    ts = _pick(s, (512, 256, 128))
    td = 1024
    cw = jnp.concatenate([conv_w, jnp.zeros((SUBLANES - CONV_WIDTH, d), F32)], axis=0)
    blk = lambda b, i, j: (b, i, j)
    return pl.pallas_call(
        _conv_body,
        out_shape=jax.ShapeDtypeStruct((bsz, s, d), BF16),
        grid=(bsz, s // ts, d // td),
        in_specs=[
            pl.BlockSpec((None, ts, td), blk),
            pl.BlockSpec((None, ts, td), blk),
            pl.BlockSpec((None, SUBLANES, td),
                         lambda b, i, j: (b, jnp.maximum(i * (ts // SUBLANES) - 1, 0), j)),
            pl.BlockSpec((SUBLANES, td), lambda b, i, j: (0, j)),
        ],
        out_specs=pl.BlockSpec((None, ts, td), blk),
        name="conv_gate",
        compiler_params=_cparams(("parallel", "parallel", "parallel")),
    )(gate_b, uu, uu, cw)


def _rank_pairs():
    kt = PEER_TOPK
    return [(a, b) for a in range(kt) for b in range(kt) if (a + 1) * (b + 1) <= kt]


PAIR_ROWS = -(-len(_rank_pairs()) // SUBLANES) * SUBLANES


def _pair_matrices():
    m1 = np.zeros((PAIR_ROWS, PEER_TOPK), np.float32)
    m2 = np.zeros((PAIR_ROWS, PEER_TOPK), np.float32)
    for p, (a, b) in enumerate(_rank_pairs()):
        m1[p, a] = 1.0
        m2[p, b] = 1.0
    return m1, m2


def _exact_dot_left(m, x):
    hi = x.astype(BF16)
    r1 = x - hi.astype(F32)
    mid = r1.astype(BF16)
    lo = (r1 - mid.astype(F32)).astype(BF16)
    return (_dot(m, hi) + _dot(m, mid)) + _dot(m, lo)


def _extract_top_rows(x, row_iota, n, rank_iota, extra=None):
    sentinel = float(x.shape[0])
    vals = jnp.zeros((n, x.shape[1]), F32)
    idxs = jnp.zeros((n, x.shape[1]), F32)
    exts = jnp.zeros((n, x.shape[1]), F32)
    for kk in range(n):
        m = jnp.max(x, axis=0, keepdims=True)
        idx = jnp.min(jnp.where(x == m, row_iota, sentinel), axis=0, keepdims=True)
        hit = row_iota == idx
        if extra is not None:
            ext = jnp.max(jnp.where(hit, extra, -1.0), axis=0, keepdims=True)
            exts = jnp.where(rank_iota == kk, ext, exts)
        x = jnp.where(hit, -jnp.inf, x)
        vals = jnp.where(rank_iota == kk, m, vals)
        idxs = jnp.where(rank_iota == kk, idx, idxs)
    return vals, idxs, exts


def _route_body(q_ref, keys_ref, m1_ref, m2_ref, e1_ref, e2_ref, gw_ref):
    kt = PEER_TOPK
    tb = q_ref.shape[0]
    key_iota = lax.broadcasted_iota(I32, (N_KEYS, tb), 0).astype(F32)
    rank_iota = lax.broadcasted_iota(I32, (kt, tb), 0)
    svs, sis = [], []
    for half in range(2):
        qh = q_ref[:, half * PEER_HALF:(half + 1) * PEER_HALF].astype(BF16)
        sc = _dot_nt(keys_ref[half], qh)
        sv, si, _ = _extract_top_rows(sc, key_iota, kt, rank_iota)
        svs.append(sv)
        sis.append(si)
    m1 = m1_ref[...]
    m2 = m2_ref[...]
    pair_iota = lax.broadcasted_iota(I32, (PAIR_ROWS, tb), 0).astype(F32)
    cand = _exact_dot_left(m1, svs[0]) + _exact_dot_left(m2, svs[1])
    cand = jnp.where(pair_iota < float(len(_rank_pairs())), cand, -jnp.inf)
    eids = _dot(m1, sis[0].astype(BF16)) * float(N_KEYS) + _dot(m2, sis[1].astype(BF16))
    cv, _, ev = _extract_top_rows(cand, pair_iota, kt, rank_iota, extra=eids)
    ex = jnp.exp(cv - cv[0:1, :])
    gw_ref[...] = ex / jnp.sum(ex, axis=0, keepdims=True)
    ei = ev.astype(I32)
    e1_ref[...] = (ei >> 7).astype(F32)
    e2_ref[...] = (ei & (N_KEYS - 1)).astype(F32)


def _peer_route(q, keys):
    t = q.shape[0]
    tb = LANES
    m1, m2 = _pair_matrices()
    slots = PEER_HEADS * PEER_TOPK
    out = jax.ShapeDtypeStruct((slots, t), F32)
    pair_spec = pl.BlockSpec((PAIR_ROWS, PEER_TOPK), lambda i, h: (0, 0))
    out_spec = pl.BlockSpec((PEER_TOPK, tb), lambda i, h: (h, i))
    return pl.pallas_call(
        _route_body,
        out_shape=[out, out, out],
        grid=(t // tb, PEER_HEADS),
        in_specs=[pl.BlockSpec((tb, 2 * PEER_HALF), lambda i, h: (i, h)),
                  pl.BlockSpec((2, N_KEYS, PEER_HALF), lambda i, h: (h, 0, 0)),
                  pair_spec, pair_spec],
        out_specs=[out_spec, out_spec, out_spec],
        name="peer_route",
        compiler_params=_cparams(("parallel", "parallel")),
    )(q, keys, jnp.asarray(m1, BF16), jnp.asarray(m2, BF16))


GATE_UNROLL = 4


def _gate_body(e1t_ref, e2t_ref, gwt_ref, o_ref, e1_s, e2_s, gw_s):
    tb = o_ref.shape[0]
    e1_s[...] = e1t_ref[...].T
    e2_s[...] = e2t_ref[...].T
    gw_s[...] = gwt_ref[...].T
    sub = lax.broadcasted_iota(I32, (N_KEYS, LANES), 0).astype(F32)

    def one(t, carry):
        e1 = e1_s[pl.ds(t, 1), :]
        e2 = e2_s[pl.ds(t, 1), :]
        gw = gw_s[pl.ds(t, 1), :]
        a = jnp.where(e1 == sub, gw, 0.0).astype(BF16)
        b = jnp.where(e2 == sub, 1.0, 0.0).astype(BF16)
        o_ref[t] = _dot_nt(a, b)
        return carry

    lax.fori_loop(0, tb, one, 0, unroll=GATE_UNROLL)


def _peer_gates(e1t, e2t, gwt):
    t = e1t.shape[1]
    tb = LANES
    in_spec = pl.BlockSpec((LANES, tb), lambda i: (0, i))
    return pl.pallas_call(
        _gate_body,
        out_shape=jax.ShapeDtypeStruct((t, N_KEYS, N_KEYS), F32),
        grid=(t // tb,),
        in_specs=[in_spec, in_spec, in_spec],
        out_specs=pl.BlockSpec((tb, N_KEYS, N_KEYS), lambda i: (i, 0, 0)),
        scratch_shapes=[pltpu.VMEM((tb, LANES), F32)] * 3,
        name="peer_gates",
        compiler_params=_cparams(("parallel",)),
    )(e1t, e2t, gwt)


def _peer_up_body(h_ref, u_ref, g_ref, o_ref):
    s = _dot_nt(h_ref[...], u_ref[...])
    act = 0.5 * s * (1.0 + lax.erf(s * (2.0 ** -0.5)))
    for j in range(o_ref.shape[1] // N_KEYS):
        cols = slice(j * N_KEYS, (j + 1) * N_KEYS)
        o_ref[:, cols] = (act[:, cols] * g_ref[:, j, :]).astype(o_ref.dtype)


def _peer_up(h, u_all, layer, g3):
    t, d = h.shape
    n_exp = u_all.shape[1]
    tm = _pick(t, (512, 256, 128))
    tn = SUBLANES * N_KEYS
    return pl.pallas_call(
        _peer_up_body,
        out_shape=jax.ShapeDtypeStruct((t, n_exp), BF16),
        grid=(t // tm, n_exp // tn),
        in_specs=[pl.BlockSpec((tm, d), lambda i, j: (i, 0)),
                  pl.BlockSpec((None, tn, d), lambda i, j: (layer, j, 0)),
                  pl.BlockSpec((tm, SUBLANES, N_KEYS), lambda i, j: (i, j, 0))],
        out_specs=pl.BlockSpec((tm, tn), lambda i, j: (i, j)),
        name="peer_up",
        compiler_params=_cparams(("parallel", "parallel")),
    )(h, u_all, g3)


def _residual(acc, x, gate):
    return x + gate * acc


def _gated_residual_matmul(a, w_all, layer, x2, gate, s, *, tk, name):
    t, d = x2.shape
    bsz = gate.shape[0]
    tm = _pick(s, (1024, 512, 256, 128))
    tn = 1024
    (out,) = _matmul(
        a, [w_all], [0], d, tm=tm, tn=tn, tk=tk, out_dtypes=[F32], epilogue=_residual,
        extras=[(x2, (tm, tn), lambda i, j: (i, j)),
                (gate.reshape(bsz, 1, d), (None, 1, tn), lambda i, j: (i * tm // s, 0, j))],
        b_lead=layer, name=name)
    return out


def _peer_ffn(x, w_norm, shift, scale, gate, layer, w_q_all, keys, u_all, v_all):
    bsz, s, d = x.shape
    t = bsz * s
    h = _norm_mod(x, w_norm, shift, scale).reshape(t, d)
    tm = _pick(s, (1024, 512, 256, 128))
    (q,) = _matmul(h, [w_q_all], [0], w_q_all.shape[2], tm=tm, tn=1024, tk=d,
                   out_dtypes=[F32], b_lead=layer, name="peer_query")
    keys_b = keys.reshape(PEER_HEADS * 2, N_KEYS, PEER_HALF).astype(BF16)
    e1t, e2t, gwt = _peer_route(q, keys_b)
    g3 = _peer_gates(e1t, e2t, gwt)
    pmat = _peer_up(h, u_all, layer, g3)
    out = _gated_residual_matmul(pmat, v_all, layer, x.reshape(t, d), gate, s,
                                 tk=2048, name="peer_down")
    return out.reshape(bsz, s, d)


def _pad_in_proj(w):
    o_kv = Q_LORA
    o_pe = Q_LORA + KV_LORA
    o_rw = o_pe + QK_ROPE
    half = QK_ROPE // 2
    rows = w.shape[0]
    z = lambda n: jnp.zeros((rows, n), w.dtype)
    cq = w[:, :Q_LORA]
    ckv = w[:, o_kv:o_pe]
    kpe = w[:, o_pe:o_rw]
    kpe_swap = jnp.concatenate([kpe[:, half:], kpe[:, :half]], axis=1)
    r = w[:, o_rw:o_rw + RWKV_WIDTH]
    k = w[:, o_rw + RWKV_WIDTH:o_rw + 2 * RWKV_WIDTH]
    v = w[:, o_rw + 2 * RWKV_WIDTH:o_rw + 3 * RWKV_WIDTH]
    lora = w[:, o_rw + 3 * RWKV_WIDTH:]
    pad = LANES - QK_ROPE
    out = jnp.concatenate([ckv, kpe, z(pad), kpe_swap, z(pad), cq, lora,
                           z(P_R - P_LORA - lora.shape[1]), r, k, v], axis=1)
    assert out.shape[1] == P_WIDTH
    return out


def _hybrid_mixer(h, positions, w_in, q_norm_w, w_uq, kv_norm_w, w_ukv, qk_q_w, qk_k_w,
                  mu, w0, w2, a0, a2, g2, k_k, k_a, r_k, lnx_w, lnx_b):
    bsz, s, d = h.shape
    t = bsz * s
    tm = _pick(s, (1024, 512, 256, 128))
    w_pad = _pad_in_proj(w_in).astype(BF16)
    (p,) = _matmul(h.reshape(t, d), [w_pad], [0], P_WIDTH, tm=tm, tn=1024, tk=d,
                   out_dtypes=[F32], name="hyb_in_proj")
    q, k, v = _mla_qkv(p, positions, q_norm_w, w_uq, kv_norm_w, w_ukv, qk_q_w, qk_k_w)
    y_mla = _attention(q.reshape(bsz, s, -1), k.reshape(bsz, s, -1), v.reshape(bsz, s, -1), bsz, s)

    mu_full = jnp.concatenate([jnp.zeros((Q_LORA + KV_LORA + QK_ROPE,), F32), mu]).reshape(1, -1)
    mu_pad = _pad_in_proj(mu_full)
    zrow = jnp.zeros((RWKV_WIDTH,), F32)
    vecs = jnp.stack([w0, a0, k_k, k_a, r_k.reshape(-1), lnx_w, lnx_b, zrow])
    w2a2 = jnp.concatenate([w2, a2], axis=0).astype(BF16)
    y_rwkv = _rwkv(p.reshape(bsz, s, P_WIDTH), mu_pad, vecs, w2a2, g2.astype(BF16))
    return jnp.concatenate([y_mla, y_rwkv], axis=-1).reshape(t, -1)


def _conv_mixer(h, w_in, conv_w, s):
    t, d = h.shape
    bsz = t // s
    tm = _pick(s, (512, 256, 128))
    tn = 512
    nblk = d // tn
    w = w_in.astype(BF16)
    gate_b, uu = _matmul(h, [w, w, w], [0, nblk, 2 * nblk], d, tm=tm, tn=tn, tk=d,
                         out_dtypes=[F32, F32], epilogue=lambda gb, gc, u: (gb, gc * u),
                         name="conv_in_proj")
    z = _conv_gate(gate_b.reshape(bsz, s, d), uu.reshape(bsz, s, d), conv_w)
    return z.reshape(t, d)


def kernel(x, c, positions, ada_w, ada_b, norm_mix_w, norm_ffn_w, hyb_w_in, mla_q_norm_w, mla_w_uq, mla_kv_norm_w, mla_w_ukv, mla_qk_q_w, mla_qk_k_w, rwkv_mu, rwkv_w0, rwkv_w2, rwkv_a0, rwkv_a2, rwkv_g2, rwkv_k_k, rwkv_k_a, rwkv_r_k, rwkv_lnx_w, rwkv_lnx_b, hyb_w_out, conv_w_in, conv_w, conv_w_out, peer_w_q, peer_keys, peer_u, peer_v):
    bsz, s, d = x.shape
    t = bsz * s
    depth = ada_w.shape[0]
    mod = _ada_mod(c, ada_w, ada_b)
    w_q_all = peer_w_q.astype(BF16)
    u_all = peer_u.astype(BF16)
    v_all = peer_v.astype(BF16)
    for layer in range(depth):
        sh_m, sc_m, gt_m, sh_f, sc_f, gt_f = [mod[layer, :, j * d:(j + 1) * d] for j in range(6)]
        h = _norm_mod(x, norm_mix_w[layer], sh_m, sc_m)
        i = layer // 2
        if layer % 2 == 0:
            y = _hybrid_mixer(h, positions, hyb_w_in[i], mla_q_norm_w[i], mla_w_uq[i],
                              mla_kv_norm_w[i], mla_w_ukv[i], mla_qk_q_w[i], mla_qk_k_w[i],
                              rwkv_mu[i], rwkv_w0[i], rwkv_w2[i], rwkv_a0[i], rwkv_a2[i],
                              rwkv_g2[i], rwkv_k_k[i], rwkv_k_a[i], rwkv_r_k[i],
                              rwkv_lnx_w[i], rwkv_lnx_b[i])
            w_out = hyb_w_out.astype(BF16)
        else:
            y = _conv_mixer(h.reshape(t, d), conv_w_in[i], conv_w[i], s)
            w_out = conv_w_out.astype(BF16)
        x = _gated_residual_matmul(y, w_out, i, x.reshape(t, d), gt_m, s, tk=d,
                                   name="mixer_out_proj").reshape(bsz, s, d)
        x = _peer_ffn(x, norm_ffn_w[layer], sh_f, sc_f, gt_f, layer, w_q_all, peer_keys[layer],
                      u_all, v_all)
    return x
```

```python
import functools

import numpy as np
import jax
import jax.numpy as jnp
from jax import lax
from jax.experimental import pallas as pl
from jax.experimental.pallas import tpu as pltpu

F32 = jnp.float32
BF16 = jnp.bfloat16
I32 = jnp.int32

MLA_HEADS = 16
QK_NOPE = 128
QK_ROPE = 64
QK_HEAD = QK_NOPE + QK_ROPE
V_HEAD = 128
Q_LORA = 768
KV_LORA = 512
ROPE_THETA = 10000.0
RWKV_HEAD = 64
RWKV_HEADS = 32
RWKV_WIDTH = RWKV_HEADS * RWKV_HEAD
W_LORA = 64
A_LORA = 64
G_LORA = 128
LNX_EPS = 64e-5
CONV_WIDTH = 3
PEER_HEADS = 8
N_KEYS = 128
PEER_TOPK = 16
PEER_HALF = 128
NORM_EPS = 1e-6

LANES = 128
SUBLANES = 8
VMEM_LIMIT = 56 * 1024 * 1024

P_CKV = 0
P_KPE = 512
P_CQ = 768
P_LORA = 1536
P_R = 2048
P_K = P_R + RWKV_WIDTH
P_V = P_K + RWKV_WIDTH
P_WIDTH = P_V + RWKV_WIDTH
RWKV_CHUNK = 64
RWKV_PAIRS_PER_STEP = 4


def _cparams(sem):
    return pltpu.CompilerParams(dimension_semantics=sem, vmem_limit_bytes=VMEM_LIMIT)


def _dot(a, b):
    return lax.dot_general(a, b, (((1,), (0,)), ((), ())), preferred_element_type=F32)


def _dot_nt(a, b):
    return lax.dot_general(a, b, (((1,), (1,)), ((), ())), preferred_element_type=F32)


def _dot_tn(a, b):
    return lax.dot_general(a, b, (((0,), (0,)), ((), ())), preferred_element_type=F32)


def _split_dot(x, m):
    hi = x.astype(BF16)
    lo = (x - hi.astype(F32)).astype(BF16)
    return _dot(hi, m) + _dot(lo, m)


def _mm_body(*refs, nb, nk, n_extra, trans_b, epilogue):
    a_ref = refs[0]
    b_refs = refs[1:1 + nb]
    extra = refs[1 + nb:1 + nb + n_extra]
    o_refs = refs[1 + nb + n_extra:]
    a = a_ref[...].astype(BF16)
    parts = []
    for b_ref in b_refs:
        b = b_ref[...].astype(BF16)
        parts.append(_dot_nt(a, b) if trans_b else _dot(a, b))

    def finish(accs):
        outs = epilogue(*accs, *[e[...] for e in extra])
        if not isinstance(outs, (tuple, list)):
            outs = (outs,)
        for o_ref, o in zip(o_refs, outs):
            o_ref[...] = o.astype(o_ref.dtype)

    if nk == 1:
        finish(parts)
        return
    n_out = len(o_refs) - nb
    acc_refs = o_refs[n_out:]
    o_refs = o_refs[:n_out]
    k = pl.program_id(2)

    @pl.when(k == 0)
    def _():
        for acc, p in zip(acc_refs, parts):
            acc[...] = p

    @pl.when(k > 0)
    def _():
        for acc, p in zip(acc_refs, parts):
            acc[...] += p

    @pl.when(k == nk - 1)
    def _():
        finish([acc[...] for acc in acc_refs])


def _matmul(a, bs, b_col_blocks, n_out_cols, *, tm, tn, tk, out_dtypes, epilogue=None,
            extras=(), trans_b=False, b_lead=None, name="matmul"):
    m, kdim = a.shape
    nb = len(bs)
    nk = kdim // tk
    assert m % tm == 0 and kdim % tk == 0 and n_out_cols % tn == 0
    if epilogue is None:
        epilogue = lambda acc: acc
    grid = (m // tm, n_out_cols // tn, nk)
    in_specs = [pl.BlockSpec((tm, tk), lambda i, j, k: (i, k))]
    for off in b_col_blocks:
        if b_lead is not None:
            assert not trans_b
            in_specs.append(pl.BlockSpec((None, tk, tn),
                                         lambda i, j, k, off=off: (b_lead, k, j + off)))
        elif trans_b:
            in_specs.append(pl.BlockSpec((tn, tk), lambda i, j, k, off=off: (j + off, k)))
        else:
            in_specs.append(pl.BlockSpec((tk, tn), lambda i, j, k, off=off: (k, j + off)))
    extra_arrays = []
    for arr, bshape, imap in extras:
        in_specs.append(pl.BlockSpec(bshape, lambda i, j, k, imap=imap: imap(i, j)))
        extra_arrays.append(arr)
    out_shape = [jax.ShapeDtypeStruct((m, n_out_cols), dt) for dt in out_dtypes]
    out_specs = [pl.BlockSpec((tm, tn), lambda i, j, k: (i, j)) for _ in out_dtypes]
    scratch = [pltpu.VMEM((tm, tn), F32) for _ in range(nb)] if nk > 1 else []
    body = functools.partial(_mm_body, nb=nb, nk=nk, n_extra=len(extras), trans_b=trans_b,
                             epilogue=epilogue)
    outs = pl.pallas_call(
        body, out_shape=out_shape, grid=grid, in_specs=in_specs, out_specs=out_specs,
        scratch_shapes=scratch, name=name,
        compiler_params=_cparams(("parallel", "parallel", "arbitrary")),
    )(a, *bs, *extra_arrays)
    return outs


def _pick(n, prefs):
    for p in prefs:
        if n % p == 0:
            return p
    raise ValueError(f"no tile for {n}")


def _ada_body(c_ref, w_ref, b_ref, o_ref):
    c = c_ref[...]
    cond = (c * (1.0 / (1.0 + jnp.exp(-c)))).astype(BF16)
    o_ref[...] = _dot(cond, w_ref[...].astype(BF16)) + b_ref[...]


def _ada_mod(c, ada_w, ada_b):
    depth, d, n = ada_w.shape
    bsz = c.shape[0]
    tn = 512
    return pl.pallas_call(
        _ada_body,
        out_shape=jax.ShapeDtypeStruct((depth, bsz, n), F32),
        grid=(depth, n // tn),
        in_specs=[
            pl.BlockSpec((bsz, d), lambda l, j: (0, 0)),
            pl.BlockSpec((None, d, tn), lambda l, j: (l, 0, j)),
            pl.BlockSpec((None, 1, tn), lambda l, j: (l, 0, j)),
        ],
        out_specs=pl.BlockSpec((None, bsz, tn), lambda l, j: (l, 0, j)),
        name="ada_mod",
        compiler_params=_cparams(("parallel", "parallel")),
    )(c, ada_w, ada_b.reshape(depth, 1, n))


def _norm_mod_body(x_ref, w_ref, sh_ref, sc_ref, o_ref):
    x = x_ref[...]
    ms = jnp.mean(x * x, axis=-1, keepdims=True)
    y = x * lax.rsqrt(ms + NORM_EPS) * w_ref[...]
    o_ref[...] = (y * (1.0 + sc_ref[...]) + sh_ref[...]).astype(o_ref.dtype)


def _norm_mod(x, w, shift, scale):
    bsz, s, d = x.shape
    ts = _pick(s, (256, 128))
    vec = lambda v: v.reshape(bsz, 1, d)
    return pl.pallas_call(
        _norm_mod_body,
        out_shape=jax.ShapeDtypeStruct((bsz, s, d), BF16),
        grid=(bsz, s // ts),
        in_specs=[
            pl.BlockSpec((None, ts, d), lambda b, i: (b, i, 0)),
            pl.BlockSpec((1, d), lambda b, i: (0, 0)),
            pl.BlockSpec((None, 1, d), lambda b, i: (b, 0, 0)),
            pl.BlockSpec((None, 1, d), lambda b, i: (b, 0, 0)),
        ],
        out_specs=pl.BlockSpec((None, ts, d), lambda b, i: (b, i, 0)),
        name="norm_mod",
        compiler_params=_cparams(("parallel", "parallel")),
    )(x, w.reshape(1, d), vec(shift), vec(scale))


def _rope_tables(pos_ref, freq_ref):
    ang = pos_ref[...].astype(F32) * freq_ref[...]
    lane = lax.broadcasted_iota(I32, ang.shape, 1)
    cos = jnp.where(lane < QK_ROPE, jnp.cos(ang), 0.0)
    sin = jnp.sin(ang)
    sin_signed = jnp.where(lane < QK_ROPE // 2, -sin, jnp.where(lane < QK_ROPE, sin, 0.0))
    return cos, sin_signed


def _mla_q_body(cq_ref, nw_ref, wuq_ref, hw_ref, pos_ref, freq_ref, o_ref, *, scale):
    cq = cq_ref[...]
    ms = jnp.mean(cq * cq, axis=-1, keepdims=True)
    cqn = (cq * lax.rsqrt(ms + NORM_EPS) * nw_ref[...]).astype(BF16)
    cos, sin_signed = _rope_tables(pos_ref, freq_ref)
    w_nope = hw_ref[0:1, :]
    w_rope = hw_ref[1:2, :]
    w_swap = hw_ref[2:3, :]
    for h in range(MLA_HEADS):
        qh = _dot(cqn, wuq_ref[:, h * 3 * LANES:(h + 1) * 3 * LANES])
        nope = qh[:, :LANES]
        rope = qh[:, LANES:2 * LANES]
        swap = qh[:, 2 * LANES:]
        ss = jnp.sum(nope * nope + rope * rope, axis=-1, keepdims=True)
        fac = lax.rsqrt(ss * (1.0 / QK_HEAD) + NORM_EPS) * scale
        o_ref[:, h * 2 * LANES:h * 2 * LANES + LANES] = (nope * fac * w_nope).astype(o_ref.dtype)
        roped = (rope * w_rope * cos + swap * w_swap * sin_signed) * fac
        o_ref[:, h * 2 * LANES + LANES:(h + 1) * 2 * LANES] = roped.astype(o_ref.dtype)


def _mla_kv_body(ckv_ref, kpe_ref, nw_ref, wukv_ref, hw_ref, pos_ref, freq_ref, k_ref, v_ref):
    ckv = ckv_ref[...]
    ms = jnp.mean(ckv * ckv, axis=-1, keepdims=True)
    ckvn = (ckv * lax.rsqrt(ms + NORM_EPS) * nw_ref[...]).astype(BF16)
    cos, sin_signed = _rope_tables(pos_ref, freq_ref)
    kpe = kpe_ref[:, :LANES]
    kpe_swap = kpe_ref[:, LANES:]
    pe_ss = jnp.sum(kpe * kpe, axis=-1, keepdims=True)
    pe_roped = kpe * hw_ref[1:2, :] * cos + kpe_swap * hw_ref[2:3, :] * sin_signed
    w_nope = hw_ref[0:1, :]
    for h in range(MLA_HEADS):
        kn = _dot(ckvn, wukv_ref[:, h * LANES:(h + 1) * LANES])
        ss = jnp.sum(kn * kn, axis=-1, keepdims=True) + pe_ss
        fac = lax.rsqrt(ss * (1.0 / QK_HEAD) + NORM_EPS)
        k_ref[:, h * 2 * LANES:h * 2 * LANES + LANES] = (kn * fac * w_nope).astype(k_ref.dtype)
        k_ref[:, h * 2 * LANES + LANES:(h + 1) * 2 * LANES] = (pe_roped * fac).astype(k_ref.dtype)
    nv = MLA_HEADS * V_HEAD
    v_ref[...] = _dot(ckvn, wukv_ref[:, nv:]).astype(v_ref.dtype)


def _head_norm_rows(w):
    half = QK_ROPE // 2
    zeros = jnp.zeros((LANES - QK_ROPE,), F32)
    rope = jnp.concatenate([w[QK_NOPE:], zeros])
    swap = jnp.concatenate([w[QK_NOPE + half:], w[QK_NOPE:QK_NOPE + half], zeros])
    rows = jnp.stack([w[:QK_NOPE], rope, swap])
    return jnp.concatenate([rows, jnp.zeros((SUBLANES - 3, LANES), F32)], axis=0)


def _rope_freq_row():
    half = QK_ROPE // 2
    f = ROPE_THETA ** (-jnp.arange(half, dtype=F32) / half)
    return jnp.concatenate([f, f, jnp.zeros((LANES - QK_ROPE,), F32)]).reshape(1, LANES)


def _wuq_layout():
    half = QK_ROPE // 2
    idx, mask = [], []
    for h in range(MLA_HEADS):
        base = h * QK_HEAD
        nope = list(range(base, base + QK_NOPE))
        rope = list(range(base + QK_NOPE, base + QK_HEAD))
        swap = rope[half:] + rope[:half]
        pad = [0] * (LANES - QK_ROPE)
        idx += nope + rope + pad + swap + pad
        mask += [1] * QK_NOPE + [1] * QK_ROPE + [0] * len(pad) + [1] * QK_ROPE + [0] * len(pad)
    return np.asarray(idx, np.int32), np.asarray(mask, np.float32)


def _wukv_layout():
    per = QK_NOPE + V_HEAD
    k_idx = [h * per + i for h in range(MLA_HEADS) for i in range(QK_NOPE)]
    v_idx = [h * per + QK_NOPE + i for h in range(MLA_HEADS) for i in range(V_HEAD)]
    return np.asarray(k_idx + v_idx, np.int32)


def _mla_qkv(p, positions, q_norm_w, w_uq, kv_norm_w, w_ukv, qk_q_w, qk_k_w):
    t = p.shape[0]
    tm = _pick(t, (256, 128))
    idx, mask = _wuq_layout()
    wuq = (w_uq[:, idx] * mask[None, :]).astype(BF16)
    wukv = w_ukv[:, _wukv_layout()].astype(BF16)
    pos = positions.reshape(t, 1)
    freq = _rope_freq_row()
    hq = _head_norm_rows(qk_q_w)
    hk = _head_norm_rows(qk_k_w)
    qw = MLA_HEADS * 2 * LANES
    row = lambda i: (i, 0)
    fixed = lambda i: (0, 0)
    q = pl.pallas_call(
        functools.partial(_mla_q_body, scale=QK_HEAD ** -0.5),
        out_shape=jax.ShapeDtypeStruct((t, qw), BF16),
        grid=(t // tm,),
        in_specs=[
            pl.BlockSpec((tm, Q_LORA), lambda i: (i, P_CQ // Q_LORA)),
            pl.BlockSpec((1, Q_LORA), fixed),
            pl.BlockSpec(wuq.shape, fixed),
            pl.BlockSpec((SUBLANES, LANES), fixed),
            pl.BlockSpec((tm, 1), row),
            pl.BlockSpec((1, LANES), fixed),
        ],
        out_specs=pl.BlockSpec((tm, qw), row),
        name="mla_q",
        compiler_params=_cparams(("parallel",)),
    )(p, q_norm_w.reshape(1, Q_LORA), wuq, hq, pos, freq)
    k, v = pl.pallas_call(
        _mla_kv_body,
        out_shape=[jax.ShapeDtypeStruct((t, qw), BF16),
                   jax.ShapeDtypeStruct((t, MLA_HEADS * V_HEAD), BF16)],
        grid=(t // tm,),
        in_specs=[
            pl.BlockSpec((tm, KV_LORA), lambda i: (i, P_CKV // KV_LORA)),
            pl.BlockSpec((tm, 2 * LANES), lambda i: (i, P_KPE // (2 * LANES))),
            pl.BlockSpec((1, KV_LORA), fixed),
            pl.BlockSpec(wukv.shape, fixed),
            pl.BlockSpec((SUBLANES, LANES), fixed),
            pl.BlockSpec((tm, 1), row),
            pl.BlockSpec((1, LANES), fixed),
        ],
        out_specs=[pl.BlockSpec((tm, qw), row), pl.BlockSpec((tm, MLA_HEADS * V_HEAD), row)],
        name="mla_kv",
        compiler_params=_cparams(("parallel",)),
    )(p, p, kv_norm_w.reshape(1, KV_LORA), wukv, hk, pos, freq)
    return q, k, v


def _attn_body(qi_ref, ki_ref, q_ref, k_ref, v_ref, o_ref, m_ref, l_ref, acc_ref, *, tq):
    step_id = pl.program_id(2)
    qi = qi_ref[step_id]
    ki = ki_ref[step_id]

    @pl.when(ki == 0)
    def _():
        m_ref[...] = jnp.full(m_ref.shape, -jnp.inf, F32)
        l_ref[...] = jnp.zeros(l_ref.shape, F32)
        acc_ref[...] = jnp.zeros(acc_ref.shape, F32)

    def step(masked):
        s = _dot_nt(q_ref[...], k_ref[...])
        if masked:
            row = lax.broadcasted_iota(I32, s.shape, 0)
            col = lax.broadcasted_iota(I32, s.shape, 1)
            s = jnp.where(col <= row, s, -jnp.inf)
        reps = tq // LANES
        m_prev = m_ref[...]
        m_new = jnp.maximum(m_prev, jnp.max(s, axis=-1, keepdims=True))
        alpha = jnp.exp(m_prev - m_new)
        pr = jnp.exp(s - jnp.tile(m_new, (1, reps)))
        l_ref[...] = alpha * l_ref[...] + jnp.sum(pr, axis=-1, keepdims=True)
        acc_ref[...] = alpha * acc_ref[...] + _dot(pr.astype(BF16), v_ref[...])
        m_ref[...] = m_new

    @pl.when(ki < qi)
    def _():
        step(False)

    @pl.when(ki == qi)
    def _():
        step(True)
        o_ref[...] = (acc_ref[...] / l_ref[...]).astype(o_ref.dtype)


def _attention(q, k, v, bsz, s):
    tq = _pick(s, (512, 256, 128))
    n = s // tq
    pairs = [(i, j) for i in range(n) for j in range(i + 1)]
    qi_tab = jnp.asarray([i for i, _ in pairs], I32)
    ki_tab = jnp.asarray([j for _, j in pairs], I32)
    grid_spec = pltpu.PrefetchScalarGridSpec(
        num_scalar_prefetch=2,
        grid=(bsz, MLA_HEADS, len(pairs)),
        in_specs=[
            pl.BlockSpec((None, tq, 2 * LANES), lambda b, h, p, qi, ki: (b, qi[p], h)),
            pl.BlockSpec((None, tq, 2 * LANES), lambda b, h, p, qi, ki: (b, ki[p], h)),
            pl.BlockSpec((None, tq, V_HEAD), lambda b, h, p, qi, ki: (b, ki[p], h)),
        ],
        out_specs=pl.BlockSpec((None, tq, V_HEAD), lambda b, h, p, qi, ki: (b, qi[p], h)),
        scratch_shapes=[pltpu.VMEM((tq, LANES), F32), pltpu.VMEM((tq, LANES), F32),
                        pltpu.VMEM((tq, V_HEAD), F32)],
    )
    return pl.pallas_call(
        functools.partial(_attn_body, tq=tq),
        out_shape=jax.ShapeDtypeStruct((bsz, s, MLA_HEADS * V_HEAD), BF16),
        grid_spec=grid_spec,
        name="mla_attention",
        compiler_params=_cparams(("parallel", "parallel", "arbitrary")),
    )(qi_tab, ki_tab, q, k, v)


def _rwkv_body(r_ref, k_ref, v_ref, xl_ref, mu_r_ref, mu_k_ref, mu_v_ref, mu_l_ref, vec_ref,
               w2a2_ref, g2_ref, o_ref, state_ref, pr_ref, pk_ref, pv_ref, pl_ref, *, tb, npairs):
    t_idx = pl.program_id(2)

    @pl.when(t_idx == 0)
    def _():
        state_ref[...] = jnp.zeros(state_ref.shape, F32)
        pr_ref[...] = jnp.zeros(pr_ref.shape, F32)
        pk_ref[...] = jnp.zeros(pk_ref.shape, F32)
        pv_ref[...] = jnp.zeros(pv_ref.shape, F32)
        pl_ref[...] = jnp.zeros(pl_ref.shape, F32)

    def mix(x_ref, prev_ref, mu_ref):
        x = x_ref[...]
        row = lax.broadcasted_iota(I32, x.shape, 0)
        shifted = jnp.where(row == 0, prev_ref[...], pltpu.roll(x, 1, axis=0))
        prev_ref[...] = x[tb - 1:tb, :]
        return x + (shifted - x) * mu_ref[...]

    r = mix(r_ref, pr_ref, mu_r_ref)
    k = mix(k_ref, pk_ref, mu_k_ref)
    v = mix(v_ref, pv_ref, mu_v_ref)
    xl = mix(xl_ref, pl_ref, mu_l_ref)

    lane = lax.broadcasted_iota(I32, (tb, LANES), 1)
    first = lane < RWKV_HEAD
    x0 = xl[:, :LANES]
    lhs_w = jnp.where(first, jnp.tanh(x0), 0.0).astype(BF16)
    lhs_a = jnp.where(first, 0.0, x0).astype(BF16)
    xg = xl[:, LANES:]
    lhs_g = (1.0 / (1.0 + jnp.exp(-xg))).astype(BF16)
    cols = [slice(p * LANES, (p + 1) * LANES) for p in range(npairs)]
    outs = _rwkv_pairs([r[:, c] for c in cols], [k[:, c] for c in cols], [v[:, c] for c in cols],
                       lhs_w, lhs_a, lhs_g, [vec_ref[:, c] for c in cols],
                       [w2a2_ref[:, c] for c in cols], [g2_ref[:, c] for c in cols],
                       state_ref, tb)
    for c, out in zip(cols, outs):
        o_ref[:, c] = out.astype(o_ref.dtype)


def _rwkv_pairs(rs, ks, vs_in, lhs_w, lhs_a, lhs_g, vecs, w2a2s, g2s, state_ref, tb):
    c = RWKV_CHUNK
    pairs = range(len(rs))
    chunks = range(tb // c)
    each = lambda f, *lists: [f(*args) for args in zip(*lists)]

    ri = lax.broadcasted_iota(I32, (LANES, LANES), 0)
    ci = lax.broadcasted_iota(I32, (LANES, LANES), 1)
    same = (ri // RWKV_HEAD) == (ci // RWKV_HEAD)
    ones_bd = jnp.where(same, 1.0, 0.0).astype(BF16)
    tri_r = lax.broadcasted_iota(I32, (c, c), 0)
    tri_c = lax.broadcasted_iota(I32, (c, c), 1)
    tri = jnp.where(tri_c <= tri_r, 1.0, 0.0).astype(BF16)
    same_chunk = (ri // c) == (ci // c)
    strict = same_chunk & ((ci % c) < (ri % c))
    incl = same_chunk & ((ci % c) <= (ri % c))
    eye = jnp.where(ri == ci, 1.0, 0.0)
    first_c = lax.broadcasted_iota(I32, (c, LANES), 1) < RWKV_HEAD

    def stack(z):
        return jnp.concatenate([jnp.where(first_c, z, 0.0), jnp.where(first_c, 0.0, z)],
                               axis=0).astype(BF16)

    def split(x):
        hi = x.astype(BF16)
        return hi, (x - hi.astype(F32)).astype(BF16)

    def group_sum(xs):
        parts = each(split, xs)
        his = [_dot(hi, ones_bd) for hi, _ in parts]
        los = [_dot(lo, ones_bd) for _, lo in parts]
        return each(lambda a, b: a + b, his, los)

    lws = [_dot(lhs_w, w) for w in w2a2s]
    las = [_dot(lhs_a, w) for w in w2a2s]
    gs = [_dot(lhs_g, g2) for g2 in g2s]
    kks = each(lambda k, vec: k * vec[2:3, :], ks, vecs)
    nrms = group_sum(each(lambda kk: kk * kk, kks))
    kks = each(lambda kk, n: kk / jnp.maximum(jnp.sqrt(n), 1e-12), kks, nrms)

    def decay_and_gate(lw, la, vec):
        zw = -(vec[0:1, :] + lw)
        softplus = jnp.maximum(zw, 0.0) + jnp.log(1.0 + jnp.exp(-jnp.abs(zw)))
        a = 1.0 / (1.0 + jnp.exp(-(vec[1:2, :] + la)))
        return -jnp.exp(-softplus - 0.5), a

    dg = each(decay_and_gate, lws, las, vecs)
    logds = [d for d, _ in dg]
    a_gates = [a for _, a in dg]
    kps = each(lambda k, a, vec: k * (1.0 + (a - 1.0) * vec[3:4, :]), ks, a_gates, vecs)
    bvecs = each(lambda kk, a: kk * a, kks, a_gates)
    bonus_sums = group_sum(each(lambda r, kp, vec: r * kp * vec[4:5, :], rs, kps, vecs))

    inst = [(p, ch) for p in pairs for ch in chunks]
    rows = lambda x, ch: x[ch * c:(ch + 1) * c]
    lds = [rows(logds[p], ch) for p, ch in inst]
    ld_parts = each(split, lds)
    lc_hi = [_dot(tri, hi) for hi, _ in ld_parts]
    lc_lo = [_dot(tri, lo) for _, lo in ld_parts]
    lcs = each(lambda a, b: a + b, lc_hi, lc_lo)
    ltots = [lc[c - 1:c, :] for lc in lcs]
    e_negs = [jnp.exp(-lc) for lc in lcs]
    e_rems = each(lambda lt, lc: jnp.exp(lt - lc), ltots, lcs)
    xas = [stack(-rows(kks[p], ch) * jnp.exp(lc - ld)) for (p, ch), lc, ld in zip(inst, lcs, lds)]
    xrs = [stack(rows(rs[p], ch) * jnp.exp(lc)) for (p, ch), lc in zip(inst, lcs)]
    ybs = [stack(rows(bvecs[p], ch) * e) for (p, ch), e in zip(inst, e_negs)]
    yks = [stack(rows(kps[p], ch) * e) for (p, ch), e in zip(inst, e_negs)]
    vss = [stack(rows(vs_in[p], ch)) for p, ch in inst]
    bhs = [stack(rows(bvecs[p], ch) * e) for (p, ch), e in zip(inst, e_rems)]
    khs = [stack(rows(kps[p], ch) * e) for (p, ch), e in zip(inst, e_rems)]

    labs = each(lambda a, b: jnp.where(strict, _dot_nt(a, b), 0.0), xas, ybs)
    laks = each(lambda a, b: jnp.where(strict, _dot_nt(a, b), 0.0).astype(BF16), xas, yks)
    mrbs = each(lambda a, b: jnp.where(incl, _dot_nt(a, b), 0.0).astype(BF16), xrs, ybs)
    mrks = each(lambda a, b: jnp.where(incl, _dot_nt(a, b), 0.0).astype(BF16), xrs, yks)

    pws = labs
    tinvs = [eye + lab for lab in labs]
    n_sq = 1
    while n_sq * 2 < c:
        pwbs = [pw.astype(BF16) for pw in pws]
        pws = [_dot(pwb, pwb) for pwb in pwbs]
        tinvs = each(lambda pw, ti: ti + _dot(pw.astype(BF16), ti.astype(BF16)), pws, tinvs)
        n_sq *= 2
    tinvs = [ti.astype(BF16) for ti in tinvs]
    lakvs = each(_dot, laks, vss)
    mrkvs = each(_dot, mrks, vss)
    kvs = each(_dot_tn, vss, khs)

    states = [state_ref[p] for p in pairs]
    ys = [[] for _ in pairs]
    n_ch = len(chunks)
    for ch in chunks:
        ids = [p * n_ch + ch for p in pairs]
        sbs = [s.astype(BF16) for s in states]
        rhss = [_dot_nt(xas[i], sb) + lakvs[i] for i, sb in zip(ids, sbs)]
        y0s = [_dot_nt(xrs[i], sb) + mrkvs[i] for i, sb in zip(ids, sbs)]
        ubs = [_dot(tinvs[i], rhs.astype(BF16)).astype(BF16) for i, rhs in zip(ids, rhss)]
        states = [s * jnp.exp(ltots[i]) + _dot_tn(ub, bhs[i]) + kvs[i]
                  for i, s, ub in zip(ids, states, ubs)]
        for p, i, y0, ub in zip(pairs, ids, y0s, ubs):
            ystack = y0 + _dot(mrbs[i], ub)
            ys[p].append(ystack[:c] + ystack[c:])
    for p, s in zip(pairs, states):
        state_ref[p] = s

    ycat = [jnp.concatenate(y, axis=0) for y in ys]
    inv_n = 1.0 / RWKV_HEAD
    means = [m * inv_n for m in group_sum(ycat)]
    ds = each(lambda y, m: y - m, ycat, means)
    variances = [v * inv_n for v in group_sum(each(lambda d: d * d, ds))]
    outs = []
    for d, var, vec, bs, v, g in zip(ds, variances, vecs, bonus_sums, vs_in, gs):
        yn = d * lax.rsqrt(var + LNX_EPS) * vec[5:6, :] + vec[6:7, :]
        outs.append((yn + bs * v) * g)
    return outs


def _rwkv(p3, mu_pad, vecs, w2a2, g2):
    bsz, s, _ = p3.shape
    tb = _pick(s, (256, 128, 64))
    npairs = RWKV_PAIRS_PER_STEP
    wd = npairs * LANES
    groups = RWKV_HEADS // 2 // npairs
    colblk = lambda off: (lambda b, hp, t: (b, t, off // wd + hp))
    mublk = lambda off: (lambda b, hp, t: (0, off // wd + hp))
    return pl.pallas_call(
        functools.partial(_rwkv_body, tb=tb, npairs=npairs),
        out_shape=jax.ShapeDtypeStruct((bsz, s, RWKV_WIDTH), BF16),
        grid=(bsz, groups, s // tb),
        in_specs=[
            pl.BlockSpec((None, tb, wd), colblk(P_R)),
            pl.BlockSpec((None, tb, wd), colblk(P_K)),
            pl.BlockSpec((None, tb, wd), colblk(P_V)),
            pl.BlockSpec((None, tb, 2 * LANES), lambda b, hp, t: (b, t, P_LORA // (2 * LANES))),
            pl.BlockSpec((1, wd), mublk(P_R)),
            pl.BlockSpec((1, wd), mublk(P_K)),
            pl.BlockSpec((1, wd), mublk(P_V)),
            pl.BlockSpec((1, 2 * LANES), lambda b, hp, t: (0, P_LORA // (2 * LANES))),
            pl.BlockSpec((SUBLANES, wd), lambda b, hp, t: (0, hp)),
            pl.BlockSpec((LANES, wd), lambda b, hp, t: (0, hp)),
            pl.BlockSpec((LANES, wd), lambda b, hp, t: (0, hp)),
        ],
        out_specs=pl.BlockSpec((None, tb, wd), lambda b, hp, t: (b, t, hp)),
        scratch_shapes=[pltpu.VMEM((npairs, LANES, LANES), F32),
                        pltpu.VMEM((1, wd), F32), pltpu.VMEM((1, wd), F32),
                        pltpu.VMEM((1, wd), F32), pltpu.VMEM((1, 2 * LANES), F32)],
        name="rwkv7",
        compiler_params=_cparams(("parallel", "parallel", "arbitrary")),
    )(p3, p3, p3, p3, mu_pad, mu_pad, mu_pad, mu_pad, vecs, w2a2, g2)


def _conv_body(gb_ref, uu_ref, halo_ref, cw_ref, o_ref):
    i = pl.program_id(1)
    uu = uu_ref[...]
    halo = jnp.where(i == 0, 0.0, halo_ref[...])
    row = lax.broadcasted_iota(I32, uu.shape, 0)
    h1 = halo[SUBLANES - 1:SUBLANES, :]
    h2 = halo[SUBLANES - 2:SUBLANES - 1, :]
    s1 = jnp.where(row == 0, h1, pltpu.roll(uu, 1, axis=0))
    s2 = jnp.where(row == 0, h2, jnp.where(row == 1, h1, pltpu.roll(uu, 2, axis=0)))
    y = cw_ref[0:1, :] * s2 + cw_ref[1:2, :] * s1 + cw_ref[2:3, :] * uu
    o_ref[...] = (gb_ref[...] * y).astype(o_ref.dtype)


def _conv_gate(gate_b, uu, conv_w):
    bsz, s, d = uu.shape
    ts = _pick(s, (512, 256, 128))
    td = 1024
    cw = jnp.concatenate([conv_w, jnp.zeros((SUBLANES - CONV_WIDTH, d), F32)], axis=0)
    blk = lambda b, i, j: (b, i, j)
    return pl.pallas_call(
        _conv_body,
        out_shape=jax.ShapeDtypeStruct((bsz, s, d), BF16),
        grid=(bsz, s // ts, d // td),
        in_specs=[
            pl.BlockSpec((None, ts, td), blk),
            pl.BlockSpec((None, ts, td), blk),
            pl.BlockSpec((None, SUBLANES, td),
                         lambda b, i, j: (b, jnp.maximum(i * (ts // SUBLANES) - 1, 0), j)),
            pl.BlockSpec((SUBLANES, td), lambda b, i, j: (0, j)),
        ],
        out_specs=pl.BlockSpec((None, ts, td), blk),
        name="conv_gate",
        compiler_params=_cparams(("parallel", "parallel", "parallel")),
    )(gate_b, uu, uu, cw)


def _rank_pairs():
    kt = PEER_TOPK
    return [(a, b) for a in range(kt) for b in range(kt) if (a + 1) * (b + 1) <= kt]


PAIR_ROWS = -(-len(_rank_pairs()) // SUBLANES) * SUBLANES


def _pair_matrices():
    m1 = np.zeros((PAIR_ROWS, PEER_TOPK), np.float32)
    m2 = np.zeros((PAIR_ROWS, PEER_TOPK), np.float32)
    for p, (a, b) in enumerate(_rank_pairs()):
        m1[p, a] = 1.0
        m2[p, b] = 1.0
    return m1, m2


def _exact_dot_left(m, x):
    hi = x.astype(BF16)
    r1 = x - hi.astype(F32)
    mid = r1.astype(BF16)
    lo = (r1 - mid.astype(F32)).astype(BF16)
    return (_dot(m, hi) + _dot(m, mid)) + _dot(m, lo)


def _extract_top_rows(x, row_iota, n, rank_iota, extra=None):
    sentinel = float(x.shape[0])
    vals = jnp.zeros((n, x.shape[1]), F32)
    idxs = jnp.zeros((n, x.shape[1]), F32)
    exts = jnp.zeros((n, x.shape[1]), F32)
    for kk in range(n):
        m = jnp.max(x, axis=0, keepdims=True)
        idx = jnp.min(jnp.where(x == m, row_iota, sentinel), axis=0, keepdims=True)
        hit = row_iota == idx
        if extra is not None:
            ext = jnp.max(jnp.where(hit, extra, -1.0), axis=0, keepdims=True)
            exts = jnp.where(rank_iota == kk, ext, exts)
        x = jnp.where(hit, -jnp.inf, x)
        vals = jnp.where(rank_iota == kk, m, vals)
        idxs = jnp.where(rank_iota == kk, idx, idxs)
    return vals, idxs, exts


def _route_body(q_ref, keys_ref, m1_ref, m2_ref, e1_ref, e2_ref, gw_ref):
    kt = PEER_TOPK
    tb = q_ref.shape[0]
    key_iota = lax.broadcasted_iota(I32, (N_KEYS, tb), 0).astype(F32)
    rank_iota = lax.broadcasted_iota(I32, (kt, tb), 0)
    svs, sis = [], []
    for half in range(2):
        qh = q_ref[:, half * PEER_HALF:(half + 1) * PEER_HALF].astype(BF16)
        sc = _dot_nt(keys_ref[half], qh)
        sv, si, _ = _extract_top_rows(sc, key_iota, kt, rank_iota)
        svs.append(sv)
        sis.append(si)
    m1 = m1_ref[...]
    m2 = m2_ref[...]
    pair_iota = lax.broadcasted_iota(I32, (PAIR_ROWS, tb), 0).astype(F32)
    cand = _exact_dot_left(m1, svs[0]) + _exact_dot_left(m2, svs[1])
    cand = jnp.where(pair_iota < float(len(_rank_pairs())), cand, -jnp.inf)
    eids = _dot(m1, sis[0].astype(BF16)) * float(N_KEYS) + _dot(m2, sis[1].astype(BF16))
    cv, _, ev = _extract_top_rows(cand, pair_iota, kt, rank_iota, extra=eids)
    ex = jnp.exp(cv - cv[0:1, :])
    gw_ref[...] = ex / jnp.sum(ex, axis=0, keepdims=True)
    ei = ev.astype(I32)
    e1_ref[...] = (ei >> 7).astype(F32)
    e2_ref[...] = (ei & (N_KEYS - 1)).astype(F32)


def _peer_route(q, keys):
    t = q.shape[0]
    tb = LANES
    m1, m2 = _pair_matrices()
    slots = PEER_HEADS * PEER_TOPK
    out = jax.ShapeDtypeStruct((slots, t), F32)
    pair_spec = pl.BlockSpec((PAIR_ROWS, PEER_TOPK), lambda i, h: (0, 0))
    out_spec = pl.BlockSpec((PEER_TOPK, tb), lambda i, h: (h, i))
    return pl.pallas_call(
        _route_body,
        out_shape=[out, out, out],
        grid=(t // tb, PEER_HEADS),
        in_specs=[pl.BlockSpec((tb, 2 * PEER_HALF), lambda i, h: (i, h)),
                  pl.BlockSpec((2, N_KEYS, PEER_HALF), lambda i, h: (h, 0, 0)),
                  pair_spec, pair_spec],
        out_specs=[out_spec, out_spec, out_spec],
        name="peer_route",
        compiler_params=_cparams(("parallel", "parallel")),
    )(q, keys, jnp.asarray(m1, BF16), jnp.asarray(m2, BF16))


GATE_UNROLL = 8


def _gate_body(e1t_ref, e2t_ref, gwt_ref, o_ref, e1_s, e2_s, gw_s):
    tb = o_ref.shape[0]
    e1_s[...] = e1t_ref[...].T
    e2_s[...] = e2t_ref[...].T
    gw_s[...] = gwt_ref[...].T
    sub = lax.broadcasted_iota(I32, (N_KEYS, LANES), 0).astype(F32)

    def one(t, carry):
        e1 = e1_s[pl.ds(t, 1), :]
        e2 = e2_s[pl.ds(t, 1), :]
        gw = gw_s[pl.ds(t, 1), :]
        a = jnp.where(e1 == sub, gw, 0.0).astype(BF16)
        b = jnp.where(e2 == sub, 1.0, 0.0).astype(BF16)
        o_ref[t] = _dot_nt(a, b)
        return carry

    lax.fori_loop(0, tb, one, 0, unroll=GATE_UNROLL)


def _peer_gates(e1t, e2t, gwt):
    t = e1t.shape[1]
    tb = LANES
    in_spec = pl.BlockSpec((LANES, tb), lambda i: (0, i))
    return pl.pallas_call(
        _gate_body,
        out_shape=jax.ShapeDtypeStruct((t, N_KEYS, N_KEYS), F32),
        grid=(t // tb,),
        in_specs=[in_spec, in_spec, in_spec],
        out_specs=pl.BlockSpec((tb, N_KEYS, N_KEYS), lambda i: (i, 0, 0)),
        scratch_shapes=[pltpu.VMEM((tb, LANES), F32)] * 3,
        name="peer_gates",
        compiler_params=_cparams(("parallel",)),
    )(e1t, e2t, gwt)


def _peer_up_body(h_ref, u_ref, g_ref, o_ref):
    s = _dot_nt(h_ref[...], u_ref[...])
    act = 0.5 * s * (1.0 + lax.erf(s * (2.0 ** -0.5)))
    for j in range(o_ref.shape[1] // N_KEYS):
        cols = slice(j * N_KEYS, (j + 1) * N_KEYS)
        o_ref[:, cols] = (act[:, cols] * g_ref[:, j, :]).astype(o_ref.dtype)


def _peer_up(h, u_all, layer, g3):
    t, d = h.shape
    n_exp = u_all.shape[1]
    tm = _pick(t, (1024, 512, 256, 128))
    tn = SUBLANES * N_KEYS
    return pl.pallas_call(
        _peer_up_body,
        out_shape=jax.ShapeDtypeStruct((t, n_exp), BF16),
        grid=(t // tm, n_exp // tn),
        in_specs=[pl.BlockSpec((tm, d), lambda i, j: (i, 0), pipeline_mode=pl.Buffered(1)),
                  pl.BlockSpec((None, tn, d), lambda i, j: (layer, j, 0)),
                  pl.BlockSpec((tm, SUBLANES, N_KEYS), lambda i, j: (i, j, 0))],
        out_specs=pl.BlockSpec((tm, tn), lambda i, j: (i, j)),
        name="peer_up",
        compiler_params=_cparams(("parallel", "parallel")),
    )(h, u_all, g3)


def _residual(acc, x, gate):
    return x + gate * acc


def _gated_residual_matmul(a, w_all, layer, x2, gate, s, *, tk, name):
    t, d = x2.shape
    bsz = gate.shape[0]
    tm = _pick(s, (1024, 512, 256, 128))
    tn = 1024
    (out,) = _matmul(
        a, [w_all], [0], d, tm=tm, tn=tn, tk=tk, out_dtypes=[F32], epilogue=_residual,
        extras=[(x2, (tm, tn), lambda i, j: (i, j)),
                (gate.reshape(bsz, 1, d), (None, 1, tn), lambda i, j: (i * tm // s, 0, j))],
        b_lead=layer, name=name)
    return out


def _peer_ffn(x, w_norm, shift, scale, gate, layer, w_q_all, keys, u_all, v_all):
    bsz, s, d = x.shape
    t = bsz * s
    h = _norm_mod(x, w_norm, shift, scale).reshape(t, d)
    tm = _pick(s, (1024, 512, 256, 128))
    (q,) = _matmul(h, [w_q_all], [0], w_q_all.shape[2], tm=tm, tn=1024, tk=d,
                   out_dtypes=[F32], b_lead=layer, name="peer_query")
    keys_b = keys.reshape(PEER_HEADS * 2, N_KEYS, PEER_HALF).astype(BF16)
    e1t, e2t, gwt = _peer_route(q, keys_b)
    g3 = _peer_gates(e1t, e2t, gwt)
    pmat = _peer_up(h, u_all, layer, g3)
    out = _gated_residual_matmul(pmat, v_all, layer, x.reshape(t, d), gate, s,
                                 tk=2048, name="peer_down")
    return out.reshape(bsz, s, d)


def _pad_in_proj(w):
    o_kv = Q_LORA
    o_pe = Q_LORA + KV_LORA
    o_rw = o_pe + QK_ROPE
    half = QK_ROPE // 2
    rows = w.shape[0]
    z = lambda n: jnp.zeros((rows, n), w.dtype)
    cq = w[:, :Q_LORA]
    ckv = w[:, o_kv:o_pe]
    kpe = w[:, o_pe:o_rw]
    kpe_swap = jnp.concatenate([kpe[:, half:], kpe[:, :half]], axis=1)
    r = w[:, o_rw:o_rw + RWKV_WIDTH]
    k = w[:, o_rw + RWKV_WIDTH:o_rw + 2 * RWKV_WIDTH]
    v = w[:, o_rw + 2 * RWKV_WIDTH:o_rw + 3 * RWKV_WIDTH]
    lora = w[:, o_rw + 3 * RWKV_WIDTH:]
    pad = LANES - QK_ROPE
    out = jnp.concatenate([ckv, kpe, z(pad), kpe_swap, z(pad), cq, lora,
                           z(P_R - P_LORA - lora.shape[1]), r, k, v], axis=1)
    assert out.shape[1] == P_WIDTH
    return out


def _hybrid_mixer(h, positions, w_in, q_norm_w, w_uq, kv_norm_w, w_ukv, qk_q_w, qk_k_w,
                  mu, w0, w2, a0, a2, g2, k_k, k_a, r_k, lnx_w, lnx_b):
    bsz, s, d = h.shape
    t = bsz * s
    tm = _pick(s, (1024, 512, 256, 128))
    w_pad = _pad_in_proj(w_in).astype(BF16)
    (p,) = _matmul(h.reshape(t, d), [w_pad], [0], P_WIDTH, tm=tm, tn=1024, tk=d,
                   out_dtypes=[F32], name="hyb_in_proj")
    q, k, v = _mla_qkv(p, positions, q_norm_w, w_uq, kv_norm_w, w_ukv, qk_q_w, qk_k_w)
    y_mla = _attention(q.reshape(bsz, s, -1), k.reshape(bsz, s, -1), v.reshape(bsz, s, -1), bsz, s)

    mu_full = jnp.concatenate([jnp.zeros((Q_LORA + KV_LORA + QK_ROPE,), F32), mu]).reshape(1, -1)
    mu_pad = _pad_in_proj(mu_full)
    zrow = jnp.zeros((RWKV_WIDTH,), F32)
    vecs = jnp.stack([w0, a0, k_k, k_a, r_k.reshape(-1), lnx_w, lnx_b, zrow])
    w2a2 = jnp.concatenate([w2, a2], axis=0).astype(BF16)
    y_rwkv = _rwkv(p.reshape(bsz, s, P_WIDTH), mu_pad, vecs, w2a2, g2.astype(BF16))
    return jnp.concatenate([y_mla, y_rwkv], axis=-1).reshape(t, -1)


def _conv_mixer(h, w_in, conv_w, s):
    t, d = h.shape
    bsz = t // s
    tm = _pick(s, (512, 256, 128))
    tn = 512
    nblk = d // tn
    w = w_in.astype(BF16)
    gate_b, uu = _matmul(h, [w, w, w], [0, nblk, 2 * nblk], d, tm=tm, tn=tn, tk=d,
                         out_dtypes=[F32, F32], epilogue=lambda gb, gc, u: (gb, gc * u),
                         name="conv_in_proj")
    z = _conv_gate(gate_b.reshape(bsz, s, d), uu.reshape(bsz, s, d), conv_w)
    return z.reshape(t, d)


def kernel(x, c, positions, ada_w, ada_b, norm_mix_w, norm_ffn_w, hyb_w_in, mla_q_norm_w, mla_w_uq, mla_kv_norm_w, mla_w_ukv, mla_qk_q_w, mla_qk_k_w, rwkv_mu, rwkv_w0, rwkv_w2, rwkv_a0, rwkv_a2, rwkv_g2, rwkv_k_k, rwkv_k_a, rwkv_r_k, rwkv_lnx_w, rwkv_lnx_b, hyb_w_out, conv_w_in, conv_w, conv_w_out, peer_w_q, peer_keys, peer_u, peer_v):
    bsz, s, d = x.shape
    t = bsz * s
    depth = ada_w.shape[0]
    mod = _ada_mod(c, ada_w, ada_b)
    w_q_all = peer_w_q.astype(BF16)
    u_all = peer_u.astype(BF16)
    v_all = peer_v.astype(BF16)
    for layer in range(depth):
        sh_m, sc_m, gt_m, sh_f, sc_f, gt_f = [mod[layer, :, j * d:(j + 1) * d] for j in range(6)]
        h = _norm_mod(x, norm_mix_w[layer], sh_m, sc_m)
        i = layer // 2
        if layer % 2 == 0:
            y = _hybrid_mixer(h, positions, hyb_w_in[i], mla_q_norm_w[i], mla_w_uq[i],
                              mla_kv_norm_w[i], mla_w_ukv[i], mla_qk_q_w[i], mla_qk_k_w[i],
                              rwkv_mu[i], rwkv_w0[i], rwkv_w2[i], rwkv_a0[i], rwkv_a2[i],
                              rwkv_g2[i], rwkv_k_k[i], rwkv_k_a[i], rwkv_r_k[i],
                              rwkv_lnx_w[i], rwkv_lnx_b[i])
            w_out = hyb_w_out.astype(BF16)
        else:
            y = _conv_mixer(h.reshape(t, d), conv_w_in[i], conv_w[i], s)
            w_out = conv_w_out.astype(BF16)
        x = _gated_residual_matmul(y, w_out, i, x.reshape(t, d), gt_m, s, tk=d,
                                   name="mixer_out_proj").reshape(bsz, s, d)
        x = _peer_ffn(x, norm_ffn_w[layer], sh_f, sc_f, gt_f, layer, w_q_all, peer_keys[layer],
                      u_all, v_all)
    return x
```

```python
import functools

import numpy as np
import jax
import jax.numpy as jnp
from jax import lax
from jax.experimental import pallas as pl
from jax.experimental.pallas import tpu as pltpu

F32 = jnp.float32
BF16 = jnp.bfloat16
I32 = jnp.int32

MLA_HEADS = 16
QK_NOPE = 128
QK_ROPE = 64
QK_HEAD = QK_NOPE + QK_ROPE
V_HEAD = 128
Q_LORA = 768
KV_LORA = 512
ROPE_THETA = 10000.0
RWKV_HEAD = 64
RWKV_HEADS = 32
RWKV_WIDTH = RWKV_HEADS * RWKV_HEAD
W_LORA = 64
A_LORA = 64
G_LORA = 128
LNX_EPS = 64e-5
CONV_WIDTH = 3
PEER_HEADS = 8
N_KEYS = 128
PEER_TOPK = 16
PEER_HALF = 128
NORM_EPS = 1e-6

LANES = 128
SUBLANES = 8
VMEM_LIMIT = 56 * 1024 * 1024

P_CKV = 0
P_KPE = 512
P_CQ = 768
P_LORA = 1536
P_R = 2048
P_K = P_R + RWKV_WIDTH
P_V = P_K + RWKV_WIDTH
P_WIDTH = P_V + RWKV_WIDTH
RWKV_CHUNK = 64
RWKV_PAIRS_PER_STEP = 4
ATTN_HEADS_PER_STEP = 2
ROUTE_HEADS_PER_STEP = 2


def _cparams(sem):
    return pltpu.CompilerParams(dimension_semantics=sem, vmem_limit_bytes=VMEM_LIMIT)


def _dot(a, b):
    return lax.dot_general(a, b, (((1,), (0,)), ((), ())), preferred_element_type=F32)


def _dot_nt(a, b):
    return lax.dot_general(a, b, (((1,), (1,)), ((), ())), preferred_element_type=F32)


def _dot_tn(a, b):
    return lax.dot_general(a, b, (((0,), (0,)), ((), ())), preferred_element_type=F32)


def _split_dot(x, m):
    hi = x.astype(BF16)
    lo = (x - hi.astype(F32)).astype(BF16)
    return _dot(hi, m) + _dot(lo, m)


def _mm_body(*refs, nb, nk, n_extra, trans_b, epilogue):
    a_ref = refs[0]
    b_refs = refs[1:1 + nb]
    extra = refs[1 + nb:1 + nb + n_extra]
    o_refs = refs[1 + nb + n_extra:]
    a = a_ref[...].astype(BF16)
    parts = []
    for b_ref in b_refs:
        b = b_ref[...].astype(BF16)
        parts.append(_dot_nt(a, b) if trans_b else _dot(a, b))

    def finish(accs):
        outs = epilogue(*accs, *[e[...] for e in extra])
        if not isinstance(outs, (tuple, list)):
            outs = (outs,)
        for o_ref, o in zip(o_refs, outs):
            o_ref[...] = o.astype(o_ref.dtype)

    if nk == 1:
        finish(parts)
        return
    n_out = len(o_refs) - nb
    acc_refs = o_refs[n_out:]
    o_refs = o_refs[:n_out]
    k = pl.program_id(2)

    @pl.when(k == 0)
    def _():
        for acc, p in zip(acc_refs, parts):
            acc[...] = p

    @pl.when(k > 0)
    def _():
        for acc, p in zip(acc_refs, parts):
            acc[...] += p

    @pl.when(k == nk - 1)
    def _():
        finish([acc[...] for acc in acc_refs])


def _matmul(a, bs, b_col_blocks, n_out_cols, *, tm, tn, tk, out_dtypes, epilogue=None,
            extras=(), trans_b=False, b_lead=None, name="matmul"):
    m, kdim = a.shape
    nb = len(bs)
    nk = kdim // tk
    assert m % tm == 0 and kdim % tk == 0 and n_out_cols % tn == 0
    if epilogue is None:
        epilogue = lambda acc: acc
    grid = (m // tm, n_out_cols // tn, nk)
    in_specs = [pl.BlockSpec((tm, tk), lambda i, j, k: (i, k))]
    for off in b_col_blocks:
        if b_lead is not None:
            assert not trans_b
            in_specs.append(pl.BlockSpec((None, tk, tn),
                                         lambda i, j, k, off=off: (b_lead, k, j + off)))
        elif trans_b:
            in_specs.append(pl.BlockSpec((tn, tk), lambda i, j, k, off=off: (j + off, k)))
        else:
            in_specs.append(pl.BlockSpec((tk, tn), lambda i, j, k, off=off: (k, j + off)))
    extra_arrays = []
    for arr, bshape, imap in extras:
        in_specs.append(pl.BlockSpec(bshape, lambda i, j, k, imap=imap: imap(i, j)))
        extra_arrays.append(arr)
    out_shape = [jax.ShapeDtypeStruct((m, n_out_cols), dt) for dt in out_dtypes]
    out_specs = [pl.BlockSpec((tm, tn), lambda i, j, k: (i, j)) for _ in out_dtypes]
    scratch = [pltpu.VMEM((tm, tn), F32) for _ in range(nb)] if nk > 1 else []
    body = functools.partial(_mm_body, nb=nb, nk=nk, n_extra=len(extras), trans_b=trans_b,
                             epilogue=epilogue)
    outs = pl.pallas_call(
        body, out_shape=out_shape, grid=grid, in_specs=in_specs, out_specs=out_specs,
        scratch_shapes=scratch, name=name,
        compiler_params=_cparams(("parallel", "parallel", "arbitrary")),
    )(a, *bs, *extra_arrays)
    return outs


def _pick(n, prefs):
    for p in prefs:
        if n % p == 0:
            return p
    raise ValueError(f"no tile for {n}")


def _ada_body(c_ref, w_ref, b_ref, o_ref):
    c = c_ref[...]
    cond = (c * (1.0 / (1.0 + jnp.exp(-c)))).astype(BF16)
    o_ref[...] = _dot(cond, w_ref[...].astype(BF16)) + b_ref[...]


def _ada_mod(c, ada_w, ada_b):
    depth, d, n = ada_w.shape
    bsz = c.shape[0]
    tn = 512
    return pl.pallas_call(
        _ada_body,
        out_shape=jax.ShapeDtypeStruct((depth, bsz, n), F32),
        grid=(depth, n // tn),
        in_specs=[
            pl.BlockSpec((bsz, d), lambda l, j: (0, 0)),
            pl.BlockSpec((None, d, tn), lambda l, j: (l, 0, j)),
            pl.BlockSpec((None, 1, tn), lambda l, j: (l, 0, j)),
        ],
        out_specs=pl.BlockSpec((None, bsz, tn), lambda l, j: (l, 0, j)),
        name="ada_mod",
        compiler_params=_cparams(("parallel", "parallel")),
    )(c, ada_w, ada_b.reshape(depth, 1, n))


def _norm_mod_body(x_ref, w_ref, sh_ref, sc_ref, o_ref):
    x = x_ref[...]
    ms = jnp.mean(x * x, axis=-1, keepdims=True)
    y = x * lax.rsqrt(ms + NORM_EPS) * w_ref[...]
    o_ref[...] = (y * (1.0 + sc_ref[...]) + sh_ref[...]).astype(o_ref.dtype)


def _norm_mod(x, w, shift, scale):
    bsz, s, d = x.shape
    ts = _pick(s, (256, 128))
    vec = lambda v: v.reshape(bsz, 1, d)
    return pl.pallas_call(
        _norm_mod_body,
        out_shape=jax.ShapeDtypeStruct((bsz, s, d), BF16),
        grid=(bsz, s // ts),
        in_specs=[
            pl.BlockSpec((None, ts, d), lambda b, i: (b, i, 0)),
            pl.BlockSpec((1, d), lambda b, i: (0, 0)),
            pl.BlockSpec((None, 1, d), lambda b, i: (b, 0, 0)),
            pl.BlockSpec((None, 1, d), lambda b, i: (b, 0, 0)),
        ],
        out_specs=pl.BlockSpec((None, ts, d), lambda b, i: (b, i, 0)),
        name="norm_mod",
        compiler_params=_cparams(("parallel", "parallel")),
    )(x, w.reshape(1, d), vec(shift), vec(scale))


def _rope_tables(pos_ref, freq_ref):
    ang = pos_ref[...].astype(F32) * freq_ref[...]
    lane = lax.broadcasted_iota(I32, ang.shape, 1)
    cos = jnp.where(lane < QK_ROPE, jnp.cos(ang), 0.0)
    sin = jnp.sin(ang)
    sin_signed = jnp.where(lane < QK_ROPE // 2, -sin, jnp.where(lane < QK_ROPE, sin, 0.0))
    return cos, sin_signed


def _mla_q_body(cq_ref, nw_ref, wuq_ref, hw_ref, pos_ref, freq_ref, o_ref, *, scale):
    cq = cq_ref[...]
    ms = jnp.mean(cq * cq, axis=-1, keepdims=True)
    cqn = (cq * lax.rsqrt(ms + NORM_EPS) * nw_ref[...]).astype(BF16)
    cos, sin_signed = _rope_tables(pos_ref, freq_ref)
    w_nope = hw_ref[0:1, :]
    w_rope = hw_ref[1:2, :]
    w_swap = hw_ref[2:3, :]
    for h in range(MLA_HEADS):
        qh = _dot(cqn, wuq_ref[:, h * 3 * LANES:(h + 1) * 3 * LANES])
        nope = qh[:, :LANES]
        rope = qh[:, LANES:2 * LANES]
        swap = qh[:, 2 * LANES:]
        ss = jnp.sum(nope * nope + rope * rope, axis=-1, keepdims=True)
        fac = lax.rsqrt(ss * (1.0 / QK_HEAD) + NORM_EPS) * scale
        o_ref[:, h * 2 * LANES:h * 2 * LANES + LANES] = (nope * fac * w_nope).astype(o_ref.dtype)
        roped = (rope * w_rope * cos + swap * w_swap * sin_signed) * fac
        o_ref[:, h * 2 * LANES + LANES:(h + 1) * 2 * LANES] = roped.astype(o_ref.dtype)


def _mla_kv_body(ckv_ref, kpe_ref, nw_ref, wukv_ref, hw_ref, pos_ref, freq_ref, k_ref, v_ref):
    ckv = ckv_ref[...]
    ms = jnp.mean(ckv * ckv, axis=-1, keepdims=True)
    ckvn = (ckv * lax.rsqrt(ms + NORM_EPS) * nw_ref[...]).astype(BF16)
    cos, sin_signed = _rope_tables(pos_ref, freq_ref)
    kpe = kpe_ref[:, :LANES]
    kpe_swap = kpe_ref[:, LANES:]
    pe_ss = jnp.sum(kpe * kpe, axis=-1, keepdims=True)
    pe_roped = kpe * hw_ref[1:2, :] * cos + kpe_swap * hw_ref[2:3, :] * sin_signed
    w_nope = hw_ref[0:1, :]
    for h in range(MLA_HEADS):
        kn = _dot(ckvn, wukv_ref[:, h * LANES:(h + 1) * LANES])
        ss = jnp.sum(kn * kn, axis=-1, keepdims=True) + pe_ss
        fac = lax.rsqrt(ss * (1.0 / QK_HEAD) + NORM_EPS)
        k_ref[:, h * 2 * LANES:h * 2 * LANES + LANES] = (kn * fac * w_nope).astype(k_ref.dtype)
        k_ref[:, h * 2 * LANES + LANES:(h + 1) * 2 * LANES] = (pe_roped * fac).astype(k_ref.dtype)
    nv = MLA_HEADS * V_HEAD
    v_ref[...] = _dot(ckvn, wukv_ref[:, nv:]).astype(v_ref.dtype)


def _head_norm_rows(w):
    half = QK_ROPE // 2
    zeros = jnp.zeros((LANES - QK_ROPE,), F32)
    rope = jnp.concatenate([w[QK_NOPE:], zeros])
    swap = jnp.concatenate([w[QK_NOPE + half:], w[QK_NOPE:QK_NOPE + half], zeros])
    rows = jnp.stack([w[:QK_NOPE], rope, swap])
    return jnp.concatenate([rows, jnp.zeros((SUBLANES - 3, LANES), F32)], axis=0)


def _rope_freq_row():
    half = QK_ROPE // 2
    f = ROPE_THETA ** (-jnp.arange(half, dtype=F32) / half)
    return jnp.concatenate([f, f, jnp.zeros((LANES - QK_ROPE,), F32)]).reshape(1, LANES)


def _wuq_layout():
    half = QK_ROPE // 2
    idx, mask = [], []
    for h in range(MLA_HEADS):
        base = h * QK_HEAD
        nope = list(range(base, base + QK_NOPE))
        rope = list(range(base + QK_NOPE, base + QK_HEAD))
        swap = rope[half:] + rope[:half]
        pad = [0] * (LANES - QK_ROPE)
        idx += nope + rope + pad + swap + pad
        mask += [1] * QK_NOPE + [1] * QK_ROPE + [0] * len(pad) + [1] * QK_ROPE + [0] * len(pad)
    return np.asarray(idx, np.int32), np.asarray(mask, np.float32)


def _wukv_layout():
    per = QK_NOPE + V_HEAD
    k_idx = [h * per + i for h in range(MLA_HEADS) for i in range(QK_NOPE)]
    v_idx = [h * per + QK_NOPE + i for h in range(MLA_HEADS) for i in range(V_HEAD)]
    return np.asarray(k_idx + v_idx, np.int32)


def _mla_qkv(p, positions, q_norm_w, w_uq, kv_norm_w, w_ukv, qk_q_w, qk_k_w):
    t = p.shape[0]
    tm = _pick(t, (256, 128))
    idx, mask = _wuq_layout()
    wuq = (w_uq[:, idx] * mask[None, :]).astype(BF16)
    wukv = w_ukv[:, _wukv_layout()].astype(BF16)
    pos = positions.reshape(t, 1)
    freq = _rope_freq_row()
    hq = _head_norm_rows(qk_q_w)
    hk = _head_norm_rows(qk_k_w)
    qw = MLA_HEADS * 2 * LANES
    row = lambda i: (i, 0)
    fixed = lambda i: (0, 0)
    q = pl.pallas_call(
        functools.partial(_mla_q_body, scale=QK_HEAD ** -0.5),
        out_shape=jax.ShapeDtypeStruct((t, qw), BF16),
        grid=(t // tm,),
        in_specs=[
            pl.BlockSpec((tm, Q_LORA), lambda i: (i, P_CQ // Q_LORA)),
            pl.BlockSpec((1, Q_LORA), fixed),
            pl.BlockSpec(wuq.shape, fixed),
            pl.BlockSpec((SUBLANES, LANES), fixed),
            pl.BlockSpec((tm, 1), row),
            pl.BlockSpec((1, LANES), fixed),
        ],
        out_specs=pl.BlockSpec((tm, qw), row),
        name="mla_q",
        compiler_params=_cparams(("parallel",)),
    )(p, q_norm_w.reshape(1, Q_LORA), wuq, hq, pos, freq)
    k, v = pl.pallas_call(
        _mla_kv_body,
        out_shape=[jax.ShapeDtypeStruct((t, qw), BF16),
                   jax.ShapeDtypeStruct((t, MLA_HEADS * V_HEAD), BF16)],
        grid=(t // tm,),
        in_specs=[
            pl.BlockSpec((tm, KV_LORA), lambda i: (i, P_CKV // KV_LORA)),
            pl.BlockSpec((tm, 2 * LANES), lambda i: (i, P_KPE // (2 * LANES))),
            pl.BlockSpec((1, KV_LORA), fixed),
            pl.BlockSpec(wukv.shape, fixed),
            pl.BlockSpec((SUBLANES, LANES), fixed),
            pl.BlockSpec((tm, 1), row),
            pl.BlockSpec((1, LANES), fixed),
        ],
        out_specs=[pl.BlockSpec((tm, qw), row), pl.BlockSpec((tm, MLA_HEADS * V_HEAD), row)],
        name="mla_kv",
        compiler_params=_cparams(("parallel",)),
    )(p, p, kv_norm_w.reshape(1, KV_LORA), wukv, hk, pos, freq)
    return q, k, v


def _attn_body(qi_ref, ki_ref, q_ref, k_ref, v_ref, o_ref, m_ref, l_ref, acc_ref, *, tq):
    step_id = pl.program_id(2)
    qi = qi_ref[step_id]
    ki = ki_ref[step_id]

    @pl.when(ki == 0)
    def _():
        m_ref[...] = jnp.full(m_ref.shape, -jnp.inf, F32)
        l_ref[...] = jnp.zeros(l_ref.shape, F32)
        acc_ref[...] = jnp.zeros(acc_ref.shape, F32)

    heads = range(ATTN_HEADS_PER_STEP)
    qk = lambda h: slice(h * 2 * LANES, (h + 1) * 2 * LANES)
    vo = lambda h: slice(h * V_HEAD, (h + 1) * V_HEAD)

    def step(masked):
        ss = [_dot_nt(q_ref[:, qk(h)], k_ref[:, qk(h)]) for h in heads]
        if masked:
            row = lax.broadcasted_iota(I32, ss[0].shape, 0)
            col = lax.broadcasted_iota(I32, ss[0].shape, 1)
            ss = [jnp.where(col <= row, s, -jnp.inf) for s in ss]
        reps = tq // LANES
        prs, alphas = [], []
        for h, s in zip(heads, ss):
            m_prev = m_ref[h]
            m_new = jnp.maximum(m_prev, jnp.max(s, axis=-1, keepdims=True))
            alpha = jnp.exp(m_prev - m_new)
            pr = jnp.exp(s - jnp.tile(m_new, (1, reps)))
            l_ref[h] = alpha * l_ref[h] + jnp.sum(pr, axis=-1, keepdims=True)
            m_ref[h] = m_new
            prs.append(pr.astype(BF16))
            alphas.append(alpha)
        pvs = [_dot(pr, v_ref[:, vo(h)]) for h, pr in zip(heads, prs)]
        for h, alpha, pv in zip(heads, alphas, pvs):
            acc_ref[h] = alpha * acc_ref[h] + pv

    @pl.when(ki < qi)
    def _():
        step(False)

    @pl.when(ki == qi)
    def _():
        step(True)
        for h in heads:
            o_ref[:, vo(h)] = (acc_ref[h] / l_ref[h]).astype(o_ref.dtype)


def _attention(q, k, v, bsz, s):
    tq = _pick(s, (512, 256, 128))
    n = s // tq
    pairs = [(i, j) for i in range(n) for j in range(i + 1)]
    qi_tab = jnp.asarray([i for i, _ in pairs], I32)
    ki_tab = jnp.asarray([j for _, j in pairs], I32)
    nh = ATTN_HEADS_PER_STEP
    grid_spec = pltpu.PrefetchScalarGridSpec(
        num_scalar_prefetch=2,
        grid=(bsz, MLA_HEADS // nh, len(pairs)),
        in_specs=[
            pl.BlockSpec((None, tq, nh * 2 * LANES), lambda b, h, p, qi, ki: (b, qi[p], h)),
            pl.BlockSpec((None, tq, nh * 2 * LANES), lambda b, h, p, qi, ki: (b, ki[p], h)),
            pl.BlockSpec((None, tq, nh * V_HEAD), lambda b, h, p, qi, ki: (b, ki[p], h)),
        ],
        out_specs=pl.BlockSpec((None, tq, nh * V_HEAD), lambda b, h, p, qi, ki: (b, qi[p], h)),
        scratch_shapes=[pltpu.VMEM((nh, tq, LANES), F32), pltpu.VMEM((nh, tq, LANES), F32),
                        pltpu.VMEM((nh, tq, V_HEAD), F32)],
    )
    return pl.pallas_call(
        functools.partial(_attn_body, tq=tq),
        out_shape=jax.ShapeDtypeStruct((bsz, s, MLA_HEADS * V_HEAD), BF16),
        grid_spec=grid_spec,
        name="mla_attention",
        compiler_params=_cparams(("parallel", "parallel", "arbitrary")),
    )(qi_tab, ki_tab, q, k, v)


def _rwkv_body(r_ref, k_ref, v_ref, xl_ref, mu_r_ref, mu_k_ref, mu_v_ref, mu_l_ref, vec_ref,
               w2a2_ref, g2_ref, o_ref, state_ref, pr_ref, pk_ref, pv_ref, pl_ref, *, tb, npairs):
    t_idx = pl.program_id(2)

    @pl.when(t_idx == 0)
    def _():
        state_ref[...] = jnp.zeros(state_ref.shape, F32)
        pr_ref[...] = jnp.zeros(pr_ref.shape, F32)
        pk_ref[...] = jnp.zeros(pk_ref.shape, F32)
        pv_ref[...] = jnp.zeros(pv_ref.shape, F32)
        pl_ref[...] = jnp.zeros(pl_ref.shape, F32)

    def mix(x_ref, prev_ref, mu_ref):
        x = x_ref[...]
        row = lax.broadcasted_iota(I32, x.shape, 0)
        shifted = jnp.where(row == 0, prev_ref[...], pltpu.roll(x, 1, axis=0))
        prev_ref[...] = x[tb - 1:tb, :]
        return x + (shifted - x) * mu_ref[...]

    r = mix(r_ref, pr_ref, mu_r_ref)
    k = mix(k_ref, pk_ref, mu_k_ref)
    v = mix(v_ref, pv_ref, mu_v_ref)
    xl = mix(xl_ref, pl_ref, mu_l_ref)

    lane = lax.broadcasted_iota(I32, (tb, LANES), 1)
    first = lane < RWKV_HEAD
    x0 = xl[:, :LANES]
    lhs_w = jnp.where(first, jnp.tanh(x0), 0.0).astype(BF16)
    lhs_a = jnp.where(first, 0.0, x0).astype(BF16)
    xg = xl[:, LANES:]
    lhs_g = (1.0 / (1.0 + jnp.exp(-xg))).astype(BF16)
    cols = [slice(p * LANES, (p + 1) * LANES) for p in range(npairs)]
    outs = _rwkv_pairs([r[:, c] for c in cols], [k[:, c] for c in cols], [v[:, c] for c in cols],
                       lhs_w, lhs_a, lhs_g, [vec_ref[:, c] for c in cols],
                       [w2a2_ref[:, c] for c in cols], [g2_ref[:, c] for c in cols],
                       state_ref, tb)
    for c, out in zip(cols, outs):
        o_ref[:, c] = out.astype(o_ref.dtype)


def _rwkv_pairs(rs, ks, vs_in, lhs_w, lhs_a, lhs_g, vecs, w2a2s, g2s, state_ref, tb):
    c = RWKV_CHUNK
    pairs = range(len(rs))
    chunks = range(tb // c)
    each = lambda f, *lists: [f(*args) for args in zip(*lists)]

    ri = lax.broadcasted_iota(I32, (LANES, LANES), 0)
    ci = lax.broadcasted_iota(I32, (LANES, LANES), 1)
    same = (ri // RWKV_HEAD) == (ci // RWKV_HEAD)
    ones_bd = jnp.where(same, 1.0, 0.0).astype(BF16)
    tri_r = lax.broadcasted_iota(I32, (c, c), 0)
    tri_c = lax.broadcasted_iota(I32, (c, c), 1)
    tri = jnp.where(tri_c <= tri_r, 1.0, 0.0).astype(BF16)
    same_chunk = (ri // c) == (ci // c)
    strict = same_chunk & ((ci % c) < (ri % c))
    incl = same_chunk & ((ci % c) <= (ri % c))
    eye = jnp.where(ri == ci, 1.0, 0.0)
    first_c = lax.broadcasted_iota(I32, (c, LANES), 1) < RWKV_HEAD

    def stack(z):
        return jnp.concatenate([jnp.where(first_c, z, 0.0), jnp.where(first_c, 0.0, z)],
                               axis=0).astype(BF16)

    def split(x):
        hi = x.astype(BF16)
        return hi, (x - hi.astype(F32)).astype(BF16)

    def group_sum(xs):
        parts = each(split, xs)
        his = [_dot(hi, ones_bd) for hi, _ in parts]
        los = [_dot(lo, ones_bd) for _, lo in parts]
        return each(lambda a, b: a + b, his, los)

    lws = [_dot(lhs_w, w) for w in w2a2s]
    las = [_dot(lhs_a, w) for w in w2a2s]
    gs = [_dot(lhs_g, g2) for g2 in g2s]
    kks = each(lambda k, vec: k * vec[2:3, :], ks, vecs)
    nrms = group_sum(each(lambda kk: kk * kk, kks))
    kks = each(lambda kk, n: kk / jnp.maximum(jnp.sqrt(n), 1e-12), kks, nrms)

    def decay_and_gate(lw, la, vec):
        zw = -(vec[0:1, :] + lw)
        softplus = jnp.maximum(zw, 0.0) + jnp.log(1.0 + jnp.exp(-jnp.abs(zw)))
        a = 1.0 / (1.0 + jnp.exp(-(vec[1:2, :] + la)))
        return -jnp.exp(-softplus - 0.5), a

    dg = each(decay_and_gate, lws, las, vecs)
    logds = [d for d, _ in dg]
    a_gates = [a for _, a in dg]
    kps = each(lambda k, a, vec: k * (1.0 + (a - 1.0) * vec[3:4, :]), ks, a_gates, vecs)
    bvecs = each(lambda kk, a: kk * a, kks, a_gates)
    bonus_sums = group_sum(each(lambda r, kp, vec: r * kp * vec[4:5, :], rs, kps, vecs))

    inst = [(p, ch) for p in pairs for ch in chunks]
    rows = lambda x, ch: x[ch * c:(ch + 1) * c]
    lds = [rows(logds[p], ch) for p, ch in inst]
    ld_parts = each(split, lds)
    lc_hi = [_dot(tri, hi) for hi, _ in ld_parts]
    lc_lo = [_dot(tri, lo) for _, lo in ld_parts]
    lcs = each(lambda a, b: a + b, lc_hi, lc_lo)
    ltots = [lc[c - 1:c, :] for lc in lcs]
    e_negs = [jnp.exp(-lc) for lc in lcs]
    e_rems = each(lambda lt, lc: jnp.exp(lt - lc), ltots, lcs)
    xas = [stack(-rows(kks[p], ch) * jnp.exp(lc - ld)) for (p, ch), lc, ld in zip(inst, lcs, lds)]
    xrs = [stack(rows(rs[p], ch) * jnp.exp(lc)) for (p, ch), lc in zip(inst, lcs)]
    ybs = [stack(rows(bvecs[p], ch) * e) for (p, ch), e in zip(inst, e_negs)]
    yks = [stack(rows(kps[p], ch) * e) for (p, ch), e in zip(inst, e_negs)]
    vss = [stack(rows(vs_in[p], ch)) for p, ch in inst]
    bhs = [stack(rows(bvecs[p], ch) * e) for (p, ch), e in zip(inst, e_rems)]
    khs = [stack(rows(kps[p], ch) * e) for (p, ch), e in zip(inst, e_rems)]

    labs = each(lambda a, b: jnp.where(strict, _dot_nt(a, b), 0.0), xas, ybs)
    laks = each(lambda a, b: jnp.where(strict, _dot_nt(a, b), 0.0).astype(BF16), xas, yks)
    mrbs = each(lambda a, b: jnp.where(incl, _dot_nt(a, b), 0.0).astype(BF16), xrs, ybs)
    mrks = each(lambda a, b: jnp.where(incl, _dot_nt(a, b), 0.0).astype(BF16), xrs, yks)

    pws = labs
    tinvs = [eye + lab for lab in labs]
    n_sq = 1
    while n_sq * 2 < c:
        pwbs = [pw.astype(BF16) for pw in pws]
        pws = [_dot(pwb, pwb) for pwb in pwbs]
        tinvs = each(lambda pw, ti: ti + _dot(pw.astype(BF16), ti.astype(BF16)), pws, tinvs)
        n_sq *= 2
    tinvs = [ti.astype(BF16) for ti in tinvs]
    lakvs = each(_dot, laks, vss)
    mrkvs = each(_dot, mrks, vss)
    kvs = each(_dot_tn, vss, khs)

    states = [state_ref[p] for p in pairs]
    ys = [[] for _ in pairs]
    n_ch = len(chunks)
    for ch in chunks:
        ids = [p * n_ch + ch for p in pairs]
        sbs = [s.astype(BF16) for s in states]
        rhss = [_dot_nt(xas[i], sb) + lakvs[i] for i, sb in zip(ids, sbs)]
        y0s = [_dot_nt(xrs[i], sb) + mrkvs[i] for i, sb in zip(ids, sbs)]
        ubs = [_dot(tinvs[i], rhs.astype(BF16)).astype(BF16) for i, rhs in zip(ids, rhss)]
        states = [s * jnp.exp(ltots[i]) + _dot_tn(ub, bhs[i]) + kvs[i]
                  for i, s, ub in zip(ids, states, ubs)]
        for p, i, y0, ub in zip(pairs, ids, y0s, ubs):
            ystack = y0 + _dot(mrbs[i], ub)
            ys[p].append(ystack[:c] + ystack[c:])
    for p, s in zip(pairs, states):
        state_ref[p] = s

    ycat = [jnp.concatenate(y, axis=0) for y in ys]
    inv_n = 1.0 / RWKV_HEAD
    means = [m * inv_n for m in group_sum(ycat)]
    ds = each(lambda y, m: y - m, ycat, means)
    variances = [v * inv_n for v in group_sum(each(lambda d: d * d, ds))]
    outs = []
    for d, var, vec, bs, v, g in zip(ds, variances, vecs, bonus_sums, vs_in, gs):
        yn = d * lax.rsqrt(var + LNX_EPS) * vec[5:6, :] + vec[6:7, :]
        outs.append((yn + bs * v) * g)
    return outs


def _rwkv(p3, mu_pad, vecs, w2a2, g2):
    bsz, s, _ = p3.shape
    tb = _pick(s, (256, 128, 64))
    npairs = RWKV_PAIRS_PER_STEP
    wd = npairs * LANES
    groups = RWKV_HEADS // 2 // npairs
    colblk = lambda off: (lambda b, hp, t: (b, t, off // wd + hp))
    mublk = lambda off: (lambda b, hp, t: (0, off // wd + hp))
    return pl.pallas_call(
        functools.partial(_rwkv_body, tb=tb, npairs=npairs),
        out_shape=jax.ShapeDtypeStruct((bsz, s, RWKV_WIDTH), BF16),
        grid=(bsz, groups, s // tb),
        in_specs=[
            pl.BlockSpec((None, tb, wd), colblk(P_R)),
            pl.BlockSpec((None, tb, wd), colblk(P_K)),
            pl.BlockSpec((None, tb, wd), colblk(P_V)),
            pl.BlockSpec((None, tb, 2 * LANES), lambda b, hp, t: (b, t, P_LORA // (2 * LANES))),
            pl.BlockSpec((1, wd), mublk(P_R)),
            pl.BlockSpec((1, wd), mublk(P_K)),
            pl.BlockSpec((1, wd), mublk(P_V)),
            pl.BlockSpec((1, 2 * LANES), lambda b, hp, t: (0, P_LORA // (2 * LANES))),
            pl.BlockSpec((SUBLANES, wd), lambda b, hp, t: (0, hp)),
            pl.BlockSpec((LANES, wd), lambda b, hp, t: (0, hp)),
            pl.BlockSpec((LANES, wd), lambda b, hp, t: (0, hp)),
        ],
        out_specs=pl.BlockSpec((None, tb, wd), lambda b, hp, t: (b, t, hp)),
        scratch_shapes=[pltpu.VMEM((npairs, LANES, LANES), F32),
                        pltpu.VMEM((1, wd), F32), pltpu.VMEM((1, wd), F32),
                        pltpu.VMEM((1, wd), F32), pltpu.VMEM((1, 2 * LANES), F32)],
        name="rwkv7",
        compiler_params=_cparams(("parallel", "parallel", "arbitrary")),
    )(p3, p3, p3, p3, mu_pad, mu_pad, mu_pad, mu_pad, vecs, w2a2, g2)


def _conv_body(gb_ref, uu_ref, halo_ref, cw_ref, o_ref):
    i = pl.program_id(1)
    uu = uu_ref[...]
    halo = jnp.where(i == 0, 0.0, halo_ref[...])
    row = lax.broadcasted_iota(I32, uu.shape, 0)
    h1 = halo[SUBLANES - 1:SUBLANES, :]
    h2 = halo[SUBLANES - 2:SUBLANES - 1, :]
    s1 = jnp.where(row == 0, h1, pltpu.roll(uu, 1, axis=0))
    s2 = jnp.where(row == 0, h2, jnp.where(row == 1, h1, pltpu.roll(uu, 2, axis=0)))
    y = cw_ref[0:1, :] * s2 + cw_ref[1:2, :] * s1 + cw_ref[2:3, :] * uu
    o_ref[...] = (gb_ref[...] * y).astype(o_ref.dtype)


def _conv_gate(gate_b, uu, conv_w):
    bsz, s, d = uu.shape
    ts = _pick(s, (512, 256, 128))
    td = 1024
    cw = jnp.concatenate([conv_w, jnp.zeros((SUBLANES - CONV_WIDTH, d), F32)], axis=0)
    blk = lambda b, i, j: (b, i, j)
    return pl.pallas_call(
        _conv_body,
        out_shape=jax.ShapeDtypeStruct((bsz, s, d), BF16),
        grid=(bsz, s // ts, d // td),
        in_specs=[
            pl.BlockSpec((None, ts, td), blk),
            pl.BlockSpec((None, ts, td), blk),
            pl.BlockSpec((None, SUBLANES, td),
                         lambda b, i, j: (b, jnp.maximum(i * (ts // SUBLANES) - 1, 0), j)),
            pl.BlockSpec((SUBLANES, td), lambda b, i, j: (0, j)),
        ],
        out_specs=pl.BlockSpec((None, ts, td), blk),
        name="conv_gate",
        compiler_params=_cparams(("parallel", "parallel", "parallel")),
    )(gate_b, uu, uu, cw)


def _rank_pairs():
    kt = PEER_TOPK
    return [(a, b) for a in range(kt) for b in range(kt) if (a + 1) * (b + 1) <= kt]


PAIR_ROWS = -(-len(_rank_pairs()) // SUBLANES) * SUBLANES


def _pair_matrices():
    m1 = np.zeros((PAIR_ROWS, PEER_TOPK), np.float32)
    m2 = np.zeros((PAIR_ROWS, PEER_TOPK), np.float32)
    for p, (a, b) in enumerate(_rank_pairs()):
        m1[p, a] = 1.0
        m2[p, b] = 1.0
    return m1, m2


def _exact_dot_left(m, x):
    hi = x.astype(BF16)
    r1 = x - hi.astype(F32)
    mid = r1.astype(BF16)
    lo = (r1 - mid.astype(F32)).astype(BF16)
    return (_dot(m, hi) + _dot(m, mid)) + _dot(m, lo)


def _extract_top_rows(xs, row_iota, n, rank_iota, extras=None):
    xs = list(xs)
    sentinel = float(xs[0].shape[0])
    zeros = lambda: [jnp.zeros((n, x.shape[1]), F32) for x in xs]
    vals, idxs, exts = zeros(), zeros(), zeros()
    for kk in range(n):
        here = rank_iota == kk
        for j, x in enumerate(xs):
            m = jnp.max(x, axis=0, keepdims=True)
            idx = jnp.min(jnp.where(x == m, row_iota, sentinel), axis=0, keepdims=True)
            hit = row_iota == idx
            if extras is not None:
                ext = jnp.max(jnp.where(hit, extras[j], -1.0), axis=0, keepdims=True)
                exts[j] = jnp.where(here, ext, exts[j])
            xs[j] = jnp.where(hit, -jnp.inf, x)
            vals[j] = jnp.where(here, m, vals[j])
            idxs[j] = jnp.where(here, idx, idxs[j])
    return list(zip(vals, idxs, exts))


def _route_body(q_ref, keys_ref, m1_ref, m2_ref, e1_ref, e2_ref, gw_ref):
    kt = PEER_TOPK
    tb = q_ref.shape[0]
    nh = ROUTE_HEADS_PER_STEP
    key_iota = lax.broadcasted_iota(I32, (N_KEYS, tb), 0).astype(F32)
    rank_iota = lax.broadcasted_iota(I32, (kt, tb), 0)
    scs = []
    for hp in range(2 * nh):
        qh = q_ref[:, hp * PEER_HALF:(hp + 1) * PEER_HALF].astype(BF16)
        scs.append(_dot_nt(keys_ref[hp], qh))
    tops = _extract_top_rows(scs, key_iota, kt, rank_iota)
    m1 = m1_ref[...]
    m2 = m2_ref[...]
    pair_iota = lax.broadcasted_iota(I32, (PAIR_ROWS, tb), 0).astype(F32)
    cands, eids = [], []
    for h in range(nh):
        (sv1, si1, _), (sv2, si2, _) = tops[2 * h], tops[2 * h + 1]
        cand = _exact_dot_left(m1, sv1) + _exact_dot_left(m2, sv2)
        cands.append(jnp.where(pair_iota < float(len(_rank_pairs())), cand, -jnp.inf))
        eids.append(_dot(m1, si1.astype(BF16)) * float(N_KEYS) + _dot(m2, si2.astype(BF16)))
    picks = _extract_top_rows(cands, pair_iota, kt, rank_iota, extras=eids)
    for h, (cv, _, ev) in enumerate(picks):
        rows = slice(h * kt, (h + 1) * kt)
        ex = jnp.exp(cv - cv[0:1, :])
        gw_ref[rows, :] = ex / jnp.sum(ex, axis=0, keepdims=True)
        ei = ev.astype(I32)
        e1_ref[rows, :] = (ei >> 7).astype(F32)
        e2_ref[rows, :] = (ei & (N_KEYS - 1)).astype(F32)


def _peer_route(q, keys):
    t = q.shape[0]
    tb = LANES
    m1, m2 = _pair_matrices()
    slots = PEER_HEADS * PEER_TOPK
    out = jax.ShapeDtypeStruct((slots, t), F32)
    nh = ROUTE_HEADS_PER_STEP
    pair_spec = pl.BlockSpec((PAIR_ROWS, PEER_TOPK), lambda i, h: (0, 0))
    out_spec = pl.BlockSpec((nh * PEER_TOPK, tb), lambda i, h: (h, i))
    return pl.pallas_call(
        _route_body,
        out_shape=[out, out, out],
        grid=(t // tb, PEER_HEADS // nh),
        in_specs=[pl.BlockSpec((tb, nh * 2 * PEER_HALF), lambda i, h: (i, h)),
                  pl.BlockSpec((nh * 2, N_KEYS, PEER_HALF), lambda i, h: (h, 0, 0)),
                  pair_spec, pair_spec],
        out_specs=[out_spec, out_spec, out_spec],
        name="peer_route",
        compiler_params=_cparams(("parallel", "parallel")),
    )(q, keys, jnp.asarray(m1, BF16), jnp.asarray(m2, BF16))


GATE_UNROLL = 8


def _gate_body(e1t_ref, e2t_ref, gwt_ref, o_ref, e1_s, e2_s, gw_s):
    tb = o_ref.shape[0]
    e1_s[...] = e1t_ref[...].T
    e2_s[...] = e2t_ref[...].T
    gw_s[...] = gwt_ref[...].T
    sub = lax.broadcasted_iota(I32, (N_KEYS, LANES), 0).astype(F32)

    def one(t, carry):
        e1 = e1_s[pl.ds(t, 1), :]
        e2 = e2_s[pl.ds(t, 1), :]
        gw = gw_s[pl.ds(t, 1), :]
        a = jnp.where(e1 == sub, gw, 0.0).astype(BF16)
        b = jnp.where(e2 == sub, 1.0, 0.0).astype(BF16)
        o_ref[t] = _dot_nt(a, b)
        return carry

    lax.fori_loop(0, tb, one, 0, unroll=GATE_UNROLL)


def _peer_gates(e1t, e2t, gwt):
    t = e1t.shape[1]
    tb = LANES
    in_spec = pl.BlockSpec((LANES, tb), lambda i: (0, i))
    return pl.pallas_call(
        _gate_body,
        out_shape=jax.ShapeDtypeStruct((t, N_KEYS, N_KEYS), F32),
        grid=(t // tb,),
        in_specs=[in_spec, in_spec, in_spec],
        out_specs=pl.BlockSpec((tb, N_KEYS, N_KEYS), lambda i: (i, 0, 0)),
        scratch_shapes=[pltpu.VMEM((tb, LANES), F32)] * 3,
        name="peer_gates",
        compiler_params=_cparams(("parallel",)),
    )(e1t, e2t, gwt)


def _peer_up_body(h_ref, u_ref, g_ref, o_ref):
    s = _dot_nt(h_ref[...], u_ref[...])
    act = 0.5 * s * (1.0 + lax.erf(s * (2.0 ** -0.5)))
    for j in range(o_ref.shape[1] // N_KEYS):
        cols = slice(j * N_KEYS, (j + 1) * N_KEYS)
        o_ref[:, cols] = (act[:, cols] * g_ref[:, j, :]).astype(o_ref.dtype)


def _peer_up(h, u_all, layer, g3):
    t, d = h.shape
    n_exp = u_all.shape[1]
    tm = _pick(t, (1024, 512, 256, 128))
    tn = SUBLANES * N_KEYS
    return pl.pallas_call(
        _peer_up_body,
        out_shape=jax.ShapeDtypeStruct((t, n_exp), BF16),
        grid=(t // tm, n_exp // tn),
        in_specs=[pl.BlockSpec((tm, d), lambda i, j: (i, 0), pipeline_mode=pl.Buffered(1)),
                  pl.BlockSpec((None, tn, d), lambda i, j: (layer, j, 0)),
                  pl.BlockSpec((tm, SUBLANES, N_KEYS), lambda i, j: (i, j, 0))],
        out_specs=pl.BlockSpec((tm, tn), lambda i, j: (i, j)),
        name="peer_up",
        compiler_params=_cparams(("parallel", "parallel")),
    )(h, u_all, g3)


def _residual(acc, x, gate):
    return x + gate * acc


def _gated_residual_matmul(a, w_all, layer, x2, gate, s, *, tk, name):
    t, d = x2.shape
    bsz = gate.shape[0]
    tm = _pick(s, (1024, 512, 256, 128))
    tn = 1024
    (out,) = _matmul(
        a, [w_all], [0], d, tm=tm, tn=tn, tk=tk, out_dtypes=[F32], epilogue=_residual,
        extras=[(x2, (tm, tn), lambda i, j: (i, j)),
                (gate.reshape(bsz, 1, d), (None, 1, tn), lambda i, j: (i * tm // s, 0, j))],
        b_lead=layer, name=name)
    return out


def _peer_ffn(x, w_norm, shift, scale, gate, layer, w_q_all, keys, u_all, v_all):
    bsz, s, d = x.shape
    t = bsz * s
    h = _norm_mod(x, w_norm, shift, scale).reshape(t, d)
    tm = _pick(s, (1024, 512, 256, 128))
    (q,) = _matmul(h, [w_q_all], [0], w_q_all.shape[2], tm=tm, tn=1024, tk=d,
                   out_dtypes=[F32], b_lead=layer, name="peer_query")
    keys_b = keys.reshape(PEER_HEADS * 2, N_KEYS, PEER_HALF).astype(BF16)
    e1t, e2t, gwt = _peer_route(q, keys_b)
    g3 = _peer_gates(e1t, e2t, gwt)
    pmat = _peer_up(h, u_all, layer, g3)
    out = _gated_residual_matmul(pmat, v_all, layer, x.reshape(t, d), gate, s,
                                 tk=2048, name="peer_down")
    return out.reshape(bsz, s, d)


def _pad_in_proj(w):
    o_kv = Q_LORA
    o_pe = Q_LORA + KV_LORA
    o_rw = o_pe + QK_ROPE
    half = QK_ROPE // 2
    rows = w.shape[0]
    z = lambda n: jnp.zeros((rows, n), w.dtype)
    cq = w[:, :Q_LORA]
    ckv = w[:, o_kv:o_pe]
    kpe = w[:, o_pe:o_rw]
    kpe_swap = jnp.concatenate([kpe[:, half:], kpe[:, :half]], axis=1)
    r = w[:, o_rw:o_rw + RWKV_WIDTH]
    k = w[:, o_rw + RWKV_WIDTH:o_rw + 2 * RWKV_WIDTH]
    v = w[:, o_rw + 2 * RWKV_WIDTH:o_rw + 3 * RWKV_WIDTH]
    lora = w[:, o_rw + 3 * RWKV_WIDTH:]
    pad = LANES - QK_ROPE
    out = jnp.concatenate([ckv, kpe, z(pad), kpe_swap, z(pad), cq, lora,
                           z(P_R - P_LORA - lora.shape[1]), r, k, v], axis=1)
    assert out.shape[1] == P_WIDTH
    return out


def _hybrid_mixer(h, positions, w_in, q_norm_w, w_uq, kv_norm_w, w_ukv, qk_q_w, qk_k_w,
                  mu, w0, w2, a0, a2, g2, k_k, k_a, r_k, lnx_w, lnx_b):
    bsz, s, d = h.shape
    t = bsz * s
    tm = _pick(s, (1024, 512, 256, 128))
    w_pad = _pad_in_proj(w_in).astype(BF16)
    (p,) = _matmul(h.reshape(t, d), [w_pad], [0], P_WIDTH, tm=tm, tn=1024, tk=d,
                   out_dtypes=[F32], name="hyb_in_proj")
    q, k, v = _mla_qkv(p, positions, q_norm_w, w_uq, kv_norm_w, w_ukv, qk_q_w, qk_k_w)
    y_mla = _attention(q.reshape(bsz, s, -1), k.reshape(bsz, s, -1), v.reshape(bsz, s, -1), bsz, s)

    mu_full = jnp.concatenate([jnp.zeros((Q_LORA + KV_LORA + QK_ROPE,), F32), mu]).reshape(1, -1)
    mu_pad = _pad_in_proj(mu_full)
    zrow = jnp.zeros((RWKV_WIDTH,), F32)
    vecs = jnp.stack([w0, a0, k_k, k_a, r_k.reshape(-1), lnx_w, lnx_b, zrow])
    w2a2 = jnp.concatenate([w2, a2], axis=0).astype(BF16)
    y_rwkv = _rwkv(p.reshape(bsz, s, P_WIDTH), mu_pad, vecs, w2a2, g2.astype(BF16))
    return jnp.concatenate([y_mla, y_rwkv], axis=-1).reshape(t, -1)


def _conv_mixer(h, w_in, conv_w, s):
    t, d = h.shape
    bsz = t // s
    tm = _pick(s, (512, 256, 128))
    tn = 512
    nblk = d // tn
    w = w_in.astype(BF16)
    gate_b, uu = _matmul(h, [w, w, w], [0, nblk, 2 * nblk], d, tm=tm, tn=tn, tk=d,
                         out_dtypes=[F32, F32], epilogue=lambda gb, gc, u: (gb, gc * u),
                         name="conv_in_proj")
    z = _conv_gate(gate_b.reshape(bsz, s, d), uu.reshape(bsz, s, d), conv_w)
    return z.reshape(t, d)


def kernel(x, c, positions, ada_w, ada_b, norm_mix_w, norm_ffn_w, hyb_w_in, mla_q_norm_w, mla_w_uq, mla_kv_norm_w, mla_w_ukv, mla_qk_q_w, mla_qk_k_w, rwkv_mu, rwkv_w0, rwkv_w2, rwkv_a0, rwkv_a2, rwkv_g2, rwkv_k_k, rwkv_k_a, rwkv_r_k, rwkv_lnx_w, rwkv_lnx_b, hyb_w_out, conv_w_in, conv_w, conv_w_out, peer_w_q, peer_keys, peer_u, peer_v):
    bsz, s, d = x.shape
    t = bsz * s
    depth = ada_w.shape[0]
    mod = _ada_mod(c, ada_w, ada_b)
    w_q_all = peer_w_q.astype(BF16)
    u_all = peer_u.astype(BF16)
    v_all = peer_v
    for layer in range(depth):
        sh_m, sc_m, gt_m, sh_f, sc_f, gt_f = [mod[layer, :, j * d:(j + 1) * d] for j in range(6)]
        h = _norm_mod(x, norm_mix_w[layer], sh_m, sc_m)
        i = layer // 2
        if layer % 2 == 0:
            y = _hybrid_mixer(h, positions, hyb_w_in[i], mla_q_norm_w[i], mla_w_uq[i],
                              mla_kv_norm_w[i], mla_w_ukv[i], mla_qk_q_w[i], mla_qk_k_w[i],
                              rwkv_mu[i], rwkv_w0[i], rwkv_w2[i], rwkv_a0[i], rwkv_a2[i],
                              rwkv_g2[i], rwkv_k_k[i], rwkv_k_a[i], rwkv_r_k[i],
                              rwkv_lnx_w[i], rwkv_lnx_b[i])
            w_out = hyb_w_out.astype(BF16)
        else:
            y = _conv_mixer(h.reshape(t, d), conv_w_in[i], conv_w[i], s)
            w_out = conv_w_out.astype(BF16)
        x = _gated_residual_matmul(y, w_out, i, x.reshape(t, d), gt_m, s, tk=d,
                                   name="mixer_out_proj").reshape(bsz, s, d)
        x = _peer_ffn(x, norm_ffn_w[layer], sh_f, sc_f, gt_f, layer, w_q_all, peer_keys[layer],
                      u_all, v_all)
    return x
```

```python
import functools

import numpy as np
import jax
import jax.numpy as jnp
from jax import lax
from jax.experimental import pallas as pl
from jax.experimental.pallas import tpu as pltpu

F32 = jnp.float32
BF16 = jnp.bfloat16
I32 = jnp.int32

MLA_HEADS = 16
QK_NOPE = 128
QK_ROPE = 64
QK_HEAD = QK_NOPE + QK_ROPE
V_HEAD = 128
Q_LORA = 768
KV_LORA = 512
ROPE_THETA = 10000.0
RWKV_HEAD = 64
RWKV_HEADS = 32
RWKV_WIDTH = RWKV_HEADS * RWKV_HEAD
W_LORA = 64
A_LORA = 64
G_LORA = 128
LNX_EPS = 64e-5
CONV_WIDTH = 3
PEER_HEADS = 8
N_KEYS = 128
PEER_TOPK = 16
PEER_HALF = 128
NORM_EPS = 1e-6

LANES = 128
SUBLANES = 8
VMEM_LIMIT = 56 * 1024 * 1024

P_CKV = 0
P_KPE = 512
P_CQ = 768
P_LORA = 1536
P_R = 2048
P_K = P_R + RWKV_WIDTH
P_V = P_K + RWKV_WIDTH
P_WIDTH = P_V + RWKV_WIDTH
RWKV_CHUNK = 64
RWKV_PAIRS_PER_STEP = 8
ATTN_HEADS_PER_STEP = 2
ROUTE_HEADS_PER_STEP = 2


def _cparams(sem):
    return pltpu.CompilerParams(dimension_semantics=sem, vmem_limit_bytes=VMEM_LIMIT)


def _dot(a, b):
    return lax.dot_general(a, b, (((1,), (0,)), ((), ())), preferred_element_type=F32)


def _dot_nt(a, b):
    return lax.dot_general(a, b, (((1,), (1,)), ((), ())), preferred_element_type=F32)


def _dot_tn(a, b):
    return lax.dot_general(a, b, (((0,), (0,)), ((), ())), preferred_element_type=F32)


def _split_dot(x, m):
    hi = x.astype(BF16)
    lo = (x - hi.astype(F32)).astype(BF16)
    return _dot(hi, m) + _dot(lo, m)


def _mm_body(*refs, nb, nk, n_extra, trans_b, epilogue):
    a_ref = refs[0]
    b_refs = refs[1:1 + nb]
    extra = refs[1 + nb:1 + nb + n_extra]
    o_refs = refs[1 + nb + n_extra:]
    a = a_ref[...].astype(BF16)
    parts = []
    for b_ref in b_refs:
        b = b_ref[...].astype(BF16)
        parts.append(_dot_nt(a, b) if trans_b else _dot(a, b))

    def finish(accs):
        outs = epilogue(*accs, *[e[...] for e in extra])
        if not isinstance(outs, (tuple, list)):
            outs = (outs,)
        for o_ref, o in zip(o_refs, outs):
            o_ref[...] = o.astype(o_ref.dtype)

    if nk == 1:
        finish(parts)
        return
    n_out = len(o_refs) - nb
    acc_refs = o_refs[n_out:]
    o_refs = o_refs[:n_out]
    k = pl.program_id(2)

    @pl.when(k == 0)
    def _():
        for acc, p in zip(acc_refs, parts):
            acc[...] = p

    @pl.when(k > 0)
    def _():
        for acc, p in zip(acc_refs, parts):
            acc[...] += p

    @pl.when(k == nk - 1)
    def _():
        finish([acc[...] for acc in acc_refs])


def _matmul(a, bs, b_col_blocks, n_out_cols, *, tm, tn, tk, out_dtypes, epilogue=None,
            extras=(), trans_b=False, b_lead=None, name="matmul"):
    m, kdim = a.shape
    nb = len(bs)
    nk = kdim // tk
    assert m % tm == 0 and kdim % tk == 0 and n_out_cols % tn == 0
    if epilogue is None:
        epilogue = lambda acc: acc
    grid = (m // tm, n_out_cols // tn, nk)
    in_specs = [pl.BlockSpec((tm, tk), lambda i, j, k: (i, k))]
    for off in b_col_blocks:
        if b_lead is not None:
            assert not trans_b
            in_specs.append(pl.BlockSpec((None, tk, tn),
                                         lambda i, j, k, off=off: (b_lead, k, j + off)))
        elif trans_b:
            in_specs.append(pl.BlockSpec((tn, tk), lambda i, j, k, off=off: (j + off, k)))
        else:
            in_specs.append(pl.BlockSpec((tk, tn), lambda i, j, k, off=off: (k, j + off)))
    extra_arrays = []
    for arr, bshape, imap in extras:
        in_specs.append(pl.BlockSpec(bshape, lambda i, j, k, imap=imap: imap(i, j)))
        extra_arrays.append(arr)
    out_shape = [jax.ShapeDtypeStruct((m, n_out_cols), dt) for dt in out_dtypes]
    out_specs = [pl.BlockSpec((tm, tn), lambda i, j, k: (i, j)) for _ in out_dtypes]
    scratch = [pltpu.VMEM((tm, tn), F32) for _ in range(nb)] if nk > 1 else []
    body = functools.partial(_mm_body, nb=nb, nk=nk, n_extra=len(extras), trans_b=trans_b,
                             epilogue=epilogue)
    outs = pl.pallas_call(
        body, out_shape=out_shape, grid=grid, in_specs=in_specs, out_specs=out_specs,
        scratch_shapes=scratch, name=name,
        compiler_params=_cparams(("parallel", "parallel", "arbitrary")),
    )(a, *bs, *extra_arrays)
    return outs


def _pick(n, prefs):
    for p in prefs:
        if n % p == 0:
            return p
    raise ValueError(f"no tile for {n}")


def _ada_body(c_ref, w_ref, b_ref, o_ref):
    c = c_ref[...]
    cond = (c * (1.0 / (1.0 + jnp.exp(-c)))).astype(BF16)
    o_ref[...] = _dot(cond, w_ref[...].astype(BF16)) + b_ref[...]


def _ada_mod(c, ada_w, ada_b):
    depth, d, n = ada_w.shape
    bsz = c.shape[0]
    tn = 512
    return pl.pallas_call(
        _ada_body,
        out_shape=jax.ShapeDtypeStruct((depth, bsz, n), F32),
        grid=(depth, n // tn),
        in_specs=[
            pl.BlockSpec((bsz, d), lambda l, j: (0, 0)),
            pl.BlockSpec((None, d, tn), lambda l, j: (l, 0, j)),
            pl.BlockSpec((None, 1, tn), lambda l, j: (l, 0, j)),
        ],
        out_specs=pl.BlockSpec((None, bsz, tn), lambda l, j: (l, 0, j)),
        name="ada_mod",
        compiler_params=_cparams(("parallel", "parallel")),
    )(c, ada_w, ada_b.reshape(depth, 1, n))


def _norm_mod_body(x_ref, w_ref, sh_ref, sc_ref, o_ref):
    x = x_ref[...]
    ms = jnp.mean(x * x, axis=-1, keepdims=True)
    y = x * lax.rsqrt(ms + NORM_EPS) * w_ref[...]
    o_ref[...] = (y * (1.0 + sc_ref[...]) + sh_ref[...]).astype(o_ref.dtype)


def _norm_mod(x, w, shift, scale):
    bsz, s, d = x.shape
    ts = _pick(s, (256, 128))
    vec = lambda v: v.reshape(bsz, 1, d)
    return pl.pallas_call(
        _norm_mod_body,
        out_shape=jax.ShapeDtypeStruct((bsz, s, d), BF16),
        grid=(bsz, s // ts),
        in_specs=[
            pl.BlockSpec((None, ts, d), lambda b, i: (b, i, 0)),
            pl.BlockSpec((1, d), lambda b, i: (0, 0)),
            pl.BlockSpec((None, 1, d), lambda b, i: (b, 0, 0)),
            pl.BlockSpec((None, 1, d), lambda b, i: (b, 0, 0)),
        ],
        out_specs=pl.BlockSpec((None, ts, d), lambda b, i: (b, i, 0)),
        name="norm_mod",
        compiler_params=_cparams(("parallel", "parallel")),
    )(x, w.reshape(1, d), vec(shift), vec(scale))


def _rope_tables(pos_ref, freq_ref):
    ang = pos_ref[...].astype(F32) * freq_ref[...]
    lane = lax.broadcasted_iota(I32, ang.shape, 1)
    cos = jnp.where(lane < QK_ROPE, jnp.cos(ang), 0.0)
    sin = jnp.sin(ang)
    sin_signed = jnp.where(lane < QK_ROPE // 2, -sin, jnp.where(lane < QK_ROPE, sin, 0.0))
    return cos, sin_signed


def _mla_q_body(cq_ref, nw_ref, wuq_ref, hw_ref, pos_ref, freq_ref, o_ref, *, scale):
    cq = cq_ref[...]
    ms = jnp.mean(cq * cq, axis=-1, keepdims=True)
    cqn = (cq * lax.rsqrt(ms + NORM_EPS) * nw_ref[...]).astype(BF16)
    cos, sin_signed = _rope_tables(pos_ref, freq_ref)
    w_nope = hw_ref[0:1, :]
    w_rope = hw_ref[1:2, :]
    w_swap = hw_ref[2:3, :]
    for h in range(MLA_HEADS):
        qh = _dot(cqn, wuq_ref[:, h * 3 * LANES:(h + 1) * 3 * LANES])
        nope = qh[:, :LANES]
        rope = qh[:, LANES:2 * LANES]
        swap = qh[:, 2 * LANES:]
        ss = jnp.sum(nope * nope + rope * rope, axis=-1, keepdims=True)
        fac = lax.rsqrt(ss * (1.0 / QK_HEAD) + NORM_EPS) * scale
        o_ref[:, h * 2 * LANES:h * 2 * LANES + LANES] = (nope * fac * w_nope).astype(o_ref.dtype)
        roped = (rope * w_rope * cos + swap * w_swap * sin_signed) * fac
        o_ref[:, h * 2 * LANES + LANES:(h + 1) * 2 * LANES] = roped.astype(o_ref.dtype)


def _mla_kv_body(ckv_ref, kpe_ref, nw_ref, wukv_ref, hw_ref, pos_ref, freq_ref, k_ref, v_ref):
    ckv = ckv_ref[...]
    ms = jnp.mean(ckv * ckv, axis=-1, keepdims=True)
    ckvn = (ckv * lax.rsqrt(ms + NORM_EPS) * nw_ref[...]).astype(BF16)
    cos, sin_signed = _rope_tables(pos_ref, freq_ref)
    kpe = kpe_ref[:, :LANES]
    kpe_swap = kpe_ref[:, LANES:]
    pe_ss = jnp.sum(kpe * kpe, axis=-1, keepdims=True)
    pe_roped = kpe * hw_ref[1:2, :] * cos + kpe_swap * hw_ref[2:3, :] * sin_signed
    w_nope = hw_ref[0:1, :]
    for h in range(MLA_HEADS):
        kn = _dot(ckvn, wukv_ref[:, h * LANES:(h + 1) * LANES])
        ss = jnp.sum(kn * kn, axis=-1, keepdims=True) + pe_ss
        fac = lax.rsqrt(ss * (1.0 / QK_HEAD) + NORM_EPS)
        k_ref[:, h * 2 * LANES:h * 2 * LANES + LANES] = (kn * fac * w_nope).astype(k_ref.dtype)
        k_ref[:, h * 2 * LANES + LANES:(h + 1) * 2 * LANES] = (pe_roped * fac).astype(k_ref.dtype)
    nv = MLA_HEADS * V_HEAD
    v_ref[...] = _dot(ckvn, wukv_ref[:, nv:]).astype(v_ref.dtype)


def _head_norm_rows(w):
    half = QK_ROPE // 2
    zeros = jnp.zeros((LANES - QK_ROPE,), F32)
    rope = jnp.concatenate([w[QK_NOPE:], zeros])
    swap = jnp.concatenate([w[QK_NOPE + half:], w[QK_NOPE:QK_NOPE + half], zeros])
    rows = jnp.stack([w[:QK_NOPE], rope, swap])
    return jnp.concatenate([rows, jnp.zeros((SUBLANES - 3, LANES), F32)], axis=0)


def _rope_freq_row():
    half = QK_ROPE // 2
    f = ROPE_THETA ** (-jnp.arange(half, dtype=F32) / half)
    return jnp.concatenate([f, f, jnp.zeros((LANES - QK_ROPE,), F32)]).reshape(1, LANES)


def _wuq_layout():
    half = QK_ROPE // 2
    idx, mask = [], []
    for h in range(MLA_HEADS):
        base = h * QK_HEAD
        nope = list(range(base, base + QK_NOPE))
        rope = list(range(base + QK_NOPE, base + QK_HEAD))
        swap = rope[half:] + rope[:half]
        pad = [0] * (LANES - QK_ROPE)
        idx += nope + rope + pad + swap + pad
        mask += [1] * QK_NOPE + [1] * QK_ROPE + [0] * len(pad) + [1] * QK_ROPE + [0] * len(pad)
    return np.asarray(idx, np.int32), np.asarray(mask, np.float32)


def _wukv_layout():
    per = QK_NOPE + V_HEAD
    k_idx = [h * per + i for h in range(MLA_HEADS) for i in range(QK_NOPE)]
    v_idx = [h * per + QK_NOPE + i for h in range(MLA_HEADS) for i in range(V_HEAD)]
    return np.asarray(k_idx + v_idx, np.int32)


def _mla_qkv(p, positions, q_norm_w, w_uq, kv_norm_w, w_ukv, qk_q_w, qk_k_w):
    t = p.shape[0]
    tm = _pick(t, (256, 128))
    idx, mask = _wuq_layout()
    wuq = (w_uq[:, idx] * mask[None, :]).astype(BF16)
    wukv = w_ukv[:, _wukv_layout()].astype(BF16)
    pos = positions.reshape(t, 1)
    freq = _rope_freq_row()
    hq = _head_norm_rows(qk_q_w)
    hk = _head_norm_rows(qk_k_w)
    qw = MLA_HEADS * 2 * LANES
    row = lambda i: (i, 0)
    fixed = lambda i: (0, 0)
    q = pl.pallas_call(
        functools.partial(_mla_q_body, scale=QK_HEAD ** -0.5),
        out_shape=jax.ShapeDtypeStruct((t, qw), BF16),
        grid=(t // tm,),
        in_specs=[
            pl.BlockSpec((tm, Q_LORA), lambda i: (i, P_CQ // Q_LORA)),
            pl.BlockSpec((1, Q_LORA), fixed),
            pl.BlockSpec(wuq.shape, fixed),
            pl.BlockSpec((SUBLANES, LANES), fixed),
            pl.BlockSpec((tm, 1), row),
            pl.BlockSpec((1, LANES), fixed),
        ],
        out_specs=pl.BlockSpec((tm, qw), row),
        name="mla_q",
        compiler_params=_cparams(("parallel",)),
    )(p, q_norm_w.reshape(1, Q_LORA), wuq, hq, pos, freq)
    k, v = pl.pallas_call(
        _mla_kv_body,
        out_shape=[jax.ShapeDtypeStruct((t, qw), BF16),
                   jax.ShapeDtypeStruct((t, MLA_HEADS * V_HEAD), BF16)],
        grid=(t // tm,),
        in_specs=[
            pl.BlockSpec((tm, KV_LORA), lambda i: (i, P_CKV // KV_LORA)),
            pl.BlockSpec((tm, 2 * LANES), lambda i: (i, P_KPE // (2 * LANES))),
            pl.BlockSpec((1, KV_LORA), fixed),
            pl.BlockSpec(wukv.shape, fixed),
            pl.BlockSpec((SUBLANES, LANES), fixed),
            pl.BlockSpec((tm, 1), row),
            pl.BlockSpec((1, LANES), fixed),
        ],
        out_specs=[pl.BlockSpec((tm, qw), row), pl.BlockSpec((tm, MLA_HEADS * V_HEAD), row)],
        name="mla_kv",
        compiler_params=_cparams(("parallel",)),
    )(p, p, kv_norm_w.reshape(1, KV_LORA), wukv, hk, pos, freq)
    return q, k, v


def _attn_body(qi_ref, ki_ref, q_ref, k_ref, v_ref, o_ref, m_ref, l_ref, acc_ref, *, tq):
    step_id = pl.program_id(2)
    qi = qi_ref[step_id]
    ki = ki_ref[step_id]

    @pl.when(ki == 0)
    def _():
        m_ref[...] = jnp.full(m_ref.shape, -jnp.inf, F32)
        l_ref[...] = jnp.zeros(l_ref.shape, F32)
        acc_ref[...] = jnp.zeros(acc_ref.shape, F32)

    heads = range(ATTN_HEADS_PER_STEP)
    qk = lambda h: slice(h * 2 * LANES, (h + 1) * 2 * LANES)
    vo = lambda h: slice(h * V_HEAD, (h + 1) * V_HEAD)

    def step(masked):
        ss = [_dot_nt(q_ref[:, qk(h)], k_ref[:, qk(h)]) for h in heads]
        if masked:
            row = lax.broadcasted_iota(I32, ss[0].shape, 0)
            col = lax.broadcasted_iota(I32, ss[0].shape, 1)
            ss = [jnp.where(col <= row, s, -jnp.inf) for s in ss]
        reps = tq // LANES
        prs, alphas = [], []
        for h, s in zip(heads, ss):
            m_prev = m_ref[h]
            m_new = jnp.maximum(m_prev, jnp.max(s, axis=-1, keepdims=True))
            alpha = jnp.exp(m_prev - m_new)
            pr = jnp.exp(s - jnp.tile(m_new, (1, reps)))
            l_ref[h] = alpha * l_ref[h] + jnp.sum(pr, axis=-1, keepdims=True)
            m_ref[h] = m_new
            prs.append(pr.astype(BF16))
            alphas.append(alpha)
        pvs = [_dot(pr, v_ref[:, vo(h)]) for h, pr in zip(heads, prs)]
        for h, alpha, pv in zip(heads, alphas, pvs):
            acc_ref[h] = alpha * acc_ref[h] + pv

    @pl.when(ki < qi)
    def _():
        step(False)

    @pl.when(ki == qi)
    def _():
        step(True)
        for h in heads:
            o_ref[:, vo(h)] = (acc_ref[h] / l_ref[h]).astype(o_ref.dtype)


def _attention(q, k, v, bsz, s):
    tq = _pick(s, (512, 256, 128))
    n = s // tq
    pairs = [(i, j) for i in range(n) for j in range(i + 1)]
    qi_tab = jnp.asarray([i for i, _ in pairs], I32)
    ki_tab = jnp.asarray([j for _, j in pairs], I32)
    nh = ATTN_HEADS_PER_STEP
    grid_spec = pltpu.PrefetchScalarGridSpec(
        num_scalar_prefetch=2,
        grid=(bsz, MLA_HEADS // nh, len(pairs)),
        in_specs=[
            pl.BlockSpec((None, tq, nh * 2 * LANES), lambda b, h, p, qi, ki: (b, qi[p], h)),
            pl.BlockSpec((None, tq, nh * 2 * LANES), lambda b, h, p, qi, ki: (b, ki[p], h)),
            pl.BlockSpec((None, tq, nh * V_HEAD), lambda b, h, p, qi, ki: (b, ki[p], h)),
        ],
        out_specs=pl.BlockSpec((None, tq, nh * V_HEAD), lambda b, h, p, qi, ki: (b, qi[p], h)),
        scratch_shapes=[pltpu.VMEM((nh, tq, LANES), F32), pltpu.VMEM((nh, tq, LANES), F32),
                        pltpu.VMEM((nh, tq, V_HEAD), F32)],
    )
    return pl.pallas_call(
        functools.partial(_attn_body, tq=tq),
        out_shape=jax.ShapeDtypeStruct((bsz, s, MLA_HEADS * V_HEAD), BF16),
        grid_spec=grid_spec,
        name="mla_attention",
        compiler_params=_cparams(("parallel", "parallel", "arbitrary")),
    )(qi_tab, ki_tab, q, k, v)


def _rwkv_body(r_ref, k_ref, v_ref, xl_ref, mu_r_ref, mu_k_ref, mu_v_ref, mu_l_ref, vec_ref,
               w2a2_ref, g2_ref, o_ref, state_ref, pr_ref, pk_ref, pv_ref, pl_ref, *, tb, npairs):
    t_idx = pl.program_id(2)

    @pl.when(t_idx == 0)
    def _():
        state_ref[...] = jnp.zeros(state_ref.shape, F32)
        pr_ref[...] = jnp.zeros(pr_ref.shape, F32)
        pk_ref[...] = jnp.zeros(pk_ref.shape, F32)
        pv_ref[...] = jnp.zeros(pv_ref.shape, F32)
        pl_ref[...] = jnp.zeros(pl_ref.shape, F32)

    def mix(x_ref, prev_ref, mu_ref):
        x = x_ref[...]
        row = lax.broadcasted_iota(I32, x.shape, 0)
        shifted = jnp.where(row == 0, prev_ref[...], pltpu.roll(x, 1, axis=0))
        prev_ref[...] = x[tb - 1:tb, :]
        return x + (shifted - x) * mu_ref[...]

    r = mix(r_ref, pr_ref, mu_r_ref)
    k = mix(k_ref, pk_ref, mu_k_ref)
    v = mix(v_ref, pv_ref, mu_v_ref)
    xl = mix(xl_ref, pl_ref, mu_l_ref)

    lane = lax.broadcasted_iota(I32, (tb, LANES), 1)
    first = lane < RWKV_HEAD
    x0 = xl[:, :LANES]
    lhs_w = jnp.where(first, jnp.tanh(x0), 0.0).astype(BF16)
    lhs_a = jnp.where(first, 0.0, x0).astype(BF16)
    xg = xl[:, LANES:]
    lhs_g = (1.0 / (1.0 + jnp.exp(-xg))).astype(BF16)
    cols = [slice(p * LANES, (p + 1) * LANES) for p in range(npairs)]
    outs = _rwkv_pairs([r[:, c] for c in cols], [k[:, c] for c in cols], [v[:, c] for c in cols],
                       lhs_w, lhs_a, lhs_g, [vec_ref[:, c] for c in cols],
                       [w2a2_ref[:, c] for c in cols], [g2_ref[:, c] for c in cols],
                       state_ref, tb)
    for c, out in zip(cols, outs):
        o_ref[:, c] = out.astype(o_ref.dtype)


def _rwkv_pairs(rs, ks, vs_in, lhs_w, lhs_a, lhs_g, vecs, w2a2s, g2s, state_ref, tb):
    c = RWKV_CHUNK
    pairs = range(len(rs))
    chunks = range(tb // c)
    each = lambda f, *lists: [f(*args) for args in zip(*lists)]

    ri = lax.broadcasted_iota(I32, (LANES, LANES), 0)
    ci = lax.broadcasted_iota(I32, (LANES, LANES), 1)
    same = (ri // RWKV_HEAD) == (ci // RWKV_HEAD)
    ones_bd = jnp.where(same, 1.0, 0.0).astype(BF16)
    tri_r = lax.broadcasted_iota(I32, (c, c), 0)
    tri_c = lax.broadcasted_iota(I32, (c, c), 1)
    tri = jnp.where(tri_c <= tri_r, 1.0, 0.0).astype(BF16)
    same_chunk = (ri // c) == (ci // c)
    strict = same_chunk & ((ci % c) < (ri % c))
    incl = same_chunk & ((ci % c) <= (ri % c))
    eye = jnp.where(ri == ci, 1.0, 0.0)
    first_c = lax.broadcasted_iota(I32, (c, LANES), 1) < RWKV_HEAD

    def stack(z):
        return jnp.concatenate([jnp.where(first_c, z, 0.0), jnp.where(first_c, 0.0, z)],
                               axis=0).astype(BF16)

    def split(x):
        hi = x.astype(BF16)
        return hi, (x - hi.astype(F32)).astype(BF16)

    def group_sum(xs):
        parts = each(split, xs)
        his = [_dot(hi, ones_bd) for hi, _ in parts]
        los = [_dot(lo, ones_bd) for _, lo in parts]
        return each(lambda a, b: a + b, his, los)

    lws = [_dot(lhs_w, w) for w in w2a2s]
    las = [_dot(lhs_a, w) for w in w2a2s]
    gs = [_dot(lhs_g, g2) for g2 in g2s]
    kks = each(lambda k, vec: k * vec[2:3, :], ks, vecs)
    nrms = group_sum(each(lambda kk: kk * kk, kks))
    kks = each(lambda kk, n: kk / jnp.maximum(jnp.sqrt(n), 1e-12), kks, nrms)

    def decay_and_gate(lw, la, vec):
        zw = -(vec[0:1, :] + lw)
        softplus = jnp.maximum(zw, 0.0) + jnp.log(1.0 + jnp.exp(-jnp.abs(zw)))
        a = 1.0 / (1.0 + jnp.exp(-(vec[1:2, :] + la)))
        return -jnp.exp(-softplus - 0.5), a

    dg = each(decay_and_gate, lws, las, vecs)
    logds = [d for d, _ in dg]
    a_gates = [a for _, a in dg]
    kps = each(lambda k, a, vec: k * (1.0 + (a - 1.0) * vec[3:4, :]), ks, a_gates, vecs)
    bvecs = each(lambda kk, a: kk * a, kks, a_gates)
    bonus_sums = group_sum(each(lambda r, kp, vec: r * kp * vec[4:5, :], rs, kps, vecs))

    inst = [(p, ch) for p in pairs for ch in chunks]
    rows = lambda x, ch: x[ch * c:(ch + 1) * c]
    lds = [rows(logds[p], ch) for p, ch in inst]
    ld_parts = each(split, lds)
    lc_hi = [_dot(tri, hi) for hi, _ in ld_parts]
    lc_lo = [_dot(tri, lo) for _, lo in ld_parts]
    lcs = each(lambda a, b: a + b, lc_hi, lc_lo)
    ltots = [lc[c - 1:c, :] for lc in lcs]
    e_negs = [jnp.exp(-lc) for lc in lcs]
    e_rems = each(lambda lt, lc: jnp.exp(lt - lc), ltots, lcs)
    xas = [stack(-rows(kks[p], ch) * jnp.exp(lc - ld)) for (p, ch), lc, ld in zip(inst, lcs, lds)]
    xrs = [stack(rows(rs[p], ch) * jnp.exp(lc)) for (p, ch), lc in zip(inst, lcs)]
    ybs = [stack(rows(bvecs[p], ch) * e) for (p, ch), e in zip(inst, e_negs)]
    yks = [stack(rows(kps[p], ch) * e) for (p, ch), e in zip(inst, e_negs)]
    vss = [stack(rows(vs_in[p], ch)) for p, ch in inst]
    bhs = [stack(rows(bvecs[p], ch) * e) for (p, ch), e in zip(inst, e_rems)]
    khs = [stack(rows(kps[p], ch) * e) for (p, ch), e in zip(inst, e_rems)]

    labs = each(lambda a, b: jnp.where(strict, _dot_nt(a, b), 0.0), xas, ybs)
    laks = each(lambda a, b: jnp.where(strict, _dot_nt(a, b), 0.0).astype(BF16), xas, yks)
    mrbs = each(lambda a, b: jnp.where(incl, _dot_nt(a, b), 0.0).astype(BF16), xrs, ybs)
    mrks = each(lambda a, b: jnp.where(incl, _dot_nt(a, b), 0.0).astype(BF16), xrs, yks)

    pws = labs
    tinvs = [eye + lab for lab in labs]
    n_sq = 1
    while n_sq * 2 < c:
        pwbs = [pw.astype(BF16) for pw in pws]
        pws = [_dot(pwb, pwb) for pwb in pwbs]
        tinvs = each(lambda pw, ti: ti + _dot(pw.astype(BF16), ti.astype(BF16)), pws, tinvs)
        n_sq *= 2
    tinvs = [ti.astype(BF16) for ti in tinvs]
    lakvs = each(_dot, laks, vss)
    mrkvs = each(_dot, mrks, vss)
    kvs = each(_dot_tn, vss, khs)

    states = [state_ref[p] for p in pairs]
    ys = [[] for _ in pairs]
    n_ch = len(chunks)
    for ch in chunks:
        ids = [p * n_ch + ch for p in pairs]
        sbs = [s.astype(BF16) for s in states]
        rhss = [_dot_nt(xas[i], sb) + lakvs[i] for i, sb in zip(ids, sbs)]
        y0s = [_dot_nt(xrs[i], sb) + mrkvs[i] for i, sb in zip(ids, sbs)]
        ubs = [_dot(tinvs[i], rhs.astype(BF16)).astype(BF16) for i, rhs in zip(ids, rhss)]
        states = [s * jnp.exp(ltots[i]) + _dot_tn(ub, bhs[i]) + kvs[i]
                  for i, s, ub in zip(ids, states, ubs)]
        for p, i, y0, ub in zip(pairs, ids, y0s, ubs):
            ystack = y0 + _dot(mrbs[i], ub)
            ys[p].append(ystack[:c] + ystack[c:])
    for p, s in zip(pairs, states):
        state_ref[p] = s

    ycat = [jnp.concatenate(y, axis=0) for y in ys]
    inv_n = 1.0 / RWKV_HEAD
    means = [m * inv_n for m in group_sum(ycat)]
    ds = each(lambda y, m: y - m, ycat, means)
    variances = [v * inv_n for v in group_sum(each(lambda d: d * d, ds))]
    outs = []
    for d, var, vec, bs, v, g in zip(ds, variances, vecs, bonus_sums, vs_in, gs):
        yn = d * lax.rsqrt(var + LNX_EPS) * vec[5:6, :] + vec[6:7, :]
        outs.append((yn + bs * v) * g)
    return outs


def _rwkv(p3, mu_pad, vecs, w2a2, g2):
    bsz, s, _ = p3.shape
    tb = _pick(s, (256, 128, 64))
    npairs = RWKV_PAIRS_PER_STEP
    wd = npairs * LANES
    groups = RWKV_HEADS // 2 // npairs
    colblk = lambda off: (lambda b, hp, t: (b, t, off // wd + hp))
    mublk = lambda off: (lambda b, hp, t: (0, off // wd + hp))
    return pl.pallas_call(
        functools.partial(_rwkv_body, tb=tb, npairs=npairs),
        out_shape=jax.ShapeDtypeStruct((bsz, s, RWKV_WIDTH), BF16),
        grid=(bsz, groups, s // tb),
        in_specs=[
            pl.BlockSpec((None, tb, wd), colblk(P_R)),
            pl.BlockSpec((None, tb, wd), colblk(P_K)),
            pl.BlockSpec((None, tb, wd), colblk(P_V)),
            pl.BlockSpec((None, tb, 2 * LANES), lambda b, hp, t: (b, t, P_LORA // (2 * LANES))),
            pl.BlockSpec((1, wd), mublk(P_R)),
            pl.BlockSpec((1, wd), mublk(P_K)),
            pl.BlockSpec((1, wd), mublk(P_V)),
            pl.BlockSpec((1, 2 * LANES), lambda b, hp, t: (0, P_LORA // (2 * LANES))),
            pl.BlockSpec((SUBLANES, wd), lambda b, hp, t: (0, hp)),
            pl.BlockSpec((LANES, wd), lambda b, hp, t: (0, hp)),
            pl.BlockSpec((LANES, wd), lambda b, hp, t: (0, hp)),
        ],
        out_specs=pl.BlockSpec((None, tb, wd), lambda b, hp, t: (b, t, hp)),
        scratch_shapes=[pltpu.VMEM((npairs, LANES, LANES), F32),
                        pltpu.VMEM((1, wd), F32), pltpu.VMEM((1, wd), F32),
                        pltpu.VMEM((1, wd), F32), pltpu.VMEM((1, 2 * LANES), F32)],
        name="rwkv7",
        compiler_params=_cparams(("parallel", "parallel", "arbitrary")),
    )(p3, p3, p3, p3, mu_pad, mu_pad, mu_pad, mu_pad, vecs, w2a2, g2)


def _conv_body(gb_ref, uu_ref, halo_ref, cw_ref, o_ref):
    i = pl.program_id(1)
    uu = uu_ref[...]
    halo = jnp.where(i == 0, 0.0, halo_ref[...])
    row = lax.broadcasted_iota(I32, uu.shape, 0)
    h1 = halo[SUBLANES - 1:SUBLANES, :]
    h2 = halo[SUBLANES - 2:SUBLANES - 1, :]
    s1 = jnp.where(row == 0, h1, pltpu.roll(uu, 1, axis=0))
    s2 = jnp.where(row == 0, h2, jnp.where(row == 1, h1, pltpu.roll(uu, 2, axis=0)))
    y = cw_ref[0:1, :] * s2 + cw_ref[1:2, :] * s1 + cw_ref[2:3, :] * uu
    o_ref[...] = (gb_ref[...] * y).astype(o_ref.dtype)


def _conv_gate(gate_b, uu, conv_w):
    bsz, s, d = uu.shape
    ts = _pick(s, (512, 256, 128))
    td = 1024
    cw = jnp.concatenate([conv_w, jnp.zeros((SUBLANES - CONV_WIDTH, d), F32)], axis=0)
    blk = lambda b, i, j: (b, i, j)
    return pl.pallas_call(
        _conv_body,
        out_shape=jax.ShapeDtypeStruct((bsz, s, d), BF16),
        grid=(bsz, s // ts, d // td),
        in_specs=[
            pl.BlockSpec((None, ts, td), blk),
            pl.BlockSpec((None, ts, td), blk),
            pl.BlockSpec((None, SUBLANES, td),
                         lambda b, i, j: (b, jnp.maximum(i * (ts // SUBLANES) - 1, 0), j)),
            pl.BlockSpec((SUBLANES, td), lambda b, i, j: (0, j)),
        ],
        out_specs=pl.BlockSpec((None, ts, td), blk),
        name="conv_gate",
        compiler_params=_cparams(("parallel", "parallel", "parallel")),
    )(gate_b, uu, uu, cw)


def _rank_pairs():
    kt = PEER_TOPK
    return [(a, b) for a in range(kt) for b in range(kt) if (a + 1) * (b + 1) <= kt]


PAIR_ROWS = -(-len(_rank_pairs()) // SUBLANES) * SUBLANES


def _pair_matrices():
    m1 = np.zeros((PAIR_ROWS, PEER_TOPK), np.float32)
    m2 = np.zeros((PAIR_ROWS, PEER_TOPK), np.float32)
    for p, (a, b) in enumerate(_rank_pairs()):
        m1[p, a] = 1.0
        m2[p, b] = 1.0
    return m1, m2


def _exact_dot_left(m, x):
    hi = x.astype(BF16)
    r1 = x - hi.astype(F32)
    mid = r1.astype(BF16)
    lo = (r1 - mid.astype(F32)).astype(BF16)
    return (_dot(m, hi) + _dot(m, mid)) + _dot(m, lo)


def _extract_top_rows(xs, row_iota, n, rank_iota, extras=None):
    xs = list(xs)
    sentinel = float(xs[0].shape[0])
    zeros = lambda: [jnp.zeros((n, x.shape[1]), F32) for x in xs]
    vals, idxs, exts = zeros(), zeros(), zeros()
    for kk in range(n):
        here = rank_iota == kk
        for j, x in enumerate(xs):
            m = jnp.max(x, axis=0, keepdims=True)
            idx = jnp.min(jnp.where(x == m, row_iota, sentinel), axis=0, keepdims=True)
            hit = row_iota == idx
            if extras is not None:
                ext = jnp.max(jnp.where(hit, extras[j], -1.0), axis=0, keepdims=True)
                exts[j] = jnp.where(here, ext, exts[j])
            xs[j] = jnp.where(hit, -jnp.inf, x)
            vals[j] = jnp.where(here, m, vals[j])
            idxs[j] = jnp.where(here, idx, idxs[j])
    return list(zip(vals, idxs, exts))


def _route_body(q_ref, keys_ref, m1_ref, m2_ref, e1_ref, e2_ref, gw_ref):
    kt = PEER_TOPK
    tb = q_ref.shape[0]
    nh = ROUTE_HEADS_PER_STEP
    key_iota = lax.broadcasted_iota(I32, (N_KEYS, tb), 0).astype(F32)
    rank_iota = lax.broadcasted_iota(I32, (kt, tb), 0)
    scs = []
    for hp in range(2 * nh):
        qh = q_ref[:, hp * PEER_HALF:(hp + 1) * PEER_HALF].astype(BF16)
        scs.append(_dot_nt(keys_ref[hp], qh))
    tops = _extract_top_rows(scs, key_iota, kt, rank_iota)
    m1 = m1_ref[...]
    m2 = m2_ref[...]
    pair_iota = lax.broadcasted_iota(I32, (PAIR_ROWS, tb), 0).astype(F32)
    cands, eids = [], []
    for h in range(nh):
        (sv1, si1, _), (sv2, si2, _) = tops[2 * h], tops[2 * h + 1]
        cand = _exact_dot_left(m1, sv1) + _exact_dot_left(m2, sv2)
        cands.append(jnp.where(pair_iota < float(len(_rank_pairs())), cand, -jnp.inf))
        eids.append(_dot(m1, si1.astype(BF16)) * float(N_KEYS) + _dot(m2, si2.astype(BF16)))
    picks = _extract_top_rows(cands, pair_iota, kt, rank_iota, extras=eids)
    for h, (cv, _, ev) in enumerate(picks):
        rows = slice(h * kt, (h + 1) * kt)
        ex = jnp.exp(cv - cv[0:1, :])
        gw_ref[rows, :] = ex / jnp.sum(ex, axis=0, keepdims=True)
        ei = ev.astype(I32)
        e1_ref[rows, :] = (ei >> 7).astype(F32)
        e2_ref[rows, :] = (ei & (N_KEYS - 1)).astype(F32)


def _peer_route(q, keys):
    t = q.shape[0]
    tb = LANES
    m1, m2 = _pair_matrices()
    slots = PEER_HEADS * PEER_TOPK
    out = jax.ShapeDtypeStruct((slots, t), F32)
    nh = ROUTE_HEADS_PER_STEP
    pair_spec = pl.BlockSpec((PAIR_ROWS, PEER_TOPK), lambda i, h: (0, 0))
    out_spec = pl.BlockSpec((nh * PEER_TOPK, tb), lambda i, h: (h, i))
    return pl.pallas_call(
        _route_body,
        out_shape=[out, out, out],
        grid=(t // tb, PEER_HEADS // nh),
        in_specs=[pl.BlockSpec((tb, nh * 2 * PEER_HALF), lambda i, h: (i, h)),
                  pl.BlockSpec((nh * 2, N_KEYS, PEER_HALF), lambda i, h: (h, 0, 0)),
                  pair_spec, pair_spec],
        out_specs=[out_spec, out_spec, out_spec],
        name="peer_route",
        compiler_params=_cparams(("parallel", "parallel")),
    )(q, keys, jnp.asarray(m1, BF16), jnp.asarray(m2, BF16))


def _swap_token_and_row(xs):
    xs = list(xs)
    row = lax.broadcasted_iota(I32, xs[0].shape, 1)
    d = SUBLANES // 2
    while d >= 1:
        low = (row % (2 * d)) < d
        for s in range(SUBLANES):
            if (s // d) % 2 == 0:
                a, b = xs[s], xs[s + d]
                xs[s] = jnp.where(low, a, pltpu.roll(b, d, axis=1))
                xs[s + d] = jnp.where(low, pltpu.roll(a, SUBLANES - d, axis=1), b)
        d //= 2
    return xs


def _gate_body(e1t_ref, e2t_ref, gwt_ref, o_ref, e1_s, e2_s, gw_s):
    tb = o_ref.shape[2]
    e1_s[...] = e1t_ref[...].T
    e2_s[...] = e2t_ref[...].T
    gw_s[...] = gwt_ref[...].T
    sub = lax.broadcasted_iota(I32, (N_KEYS, LANES), 0).astype(F32)
    nblk = N_KEYS // SUBLANES

    def eight_tokens(g, carry):
        t0 = pl.multiple_of(g * SUBLANES, SUBLANES)
        lhs, rhs = [], []
        for s in range(SUBLANES):
            e1 = e1_s[pl.ds(t0 + s, 1), :]
            e2 = e2_s[pl.ds(t0 + s, 1), :]
            gw = gw_s[pl.ds(t0 + s, 1), :]
            lhs.append(jnp.where(e1 == sub, gw, 0.0).astype(BF16))
            rhs.append(jnp.where(e2 == sub, 1.0, 0.0).astype(BF16))
        per_token = [_dot_nt(a, b).reshape(nblk, SUBLANES, N_KEYS) for a, b in zip(lhs, rhs)]
        for r, y in enumerate(_swap_token_and_row(per_token)):
            o_ref[:, r, pl.ds(t0, SUBLANES), :] = y
        return carry

    lax.fori_loop(0, tb // SUBLANES, eight_tokens, 0, unroll=4)


def _peer_gates(e1t, e2t, gwt):
    t = e1t.shape[1]
    tb = LANES
    nblk = N_KEYS // SUBLANES
    in_spec = pl.BlockSpec((LANES, tb), lambda i: (0, i))
    out = pl.pallas_call(
        _gate_body,
        out_shape=jax.ShapeDtypeStruct((nblk, SUBLANES, t, N_KEYS), F32),
        grid=(t // tb,),
        in_specs=[in_spec, in_spec, in_spec],
        out_specs=pl.BlockSpec((nblk, SUBLANES, tb, N_KEYS), lambda i: (0, 0, i, 0)),
        scratch_shapes=[pltpu.VMEM((tb, LANES), F32)] * 3,
        name="peer_gates",
        compiler_params=_cparams(("parallel",)),
    )(e1t, e2t, gwt)
    return out.reshape(N_KEYS, t, N_KEYS)


def _peer_up_body(h_ref, u_ref, g_ref, o_ref):
    s = _dot_nt(h_ref[...], u_ref[...])
    act = 0.5 * s * (1.0 + lax.erf(s * (2.0 ** -0.5)))
    for j in range(o_ref.shape[1] // N_KEYS):
        cols = slice(j * N_KEYS, (j + 1) * N_KEYS)
        o_ref[:, cols] = (act[:, cols] * g_ref[j]).astype(o_ref.dtype)


def _peer_up(h, u_all, layer, g3):
    t, d = h.shape
    n_exp = u_all.shape[1]
    tm = _pick(t, (1024, 512, 256, 128))
    tn = SUBLANES * N_KEYS
    return pl.pallas_call(
        _peer_up_body,
        out_shape=jax.ShapeDtypeStruct((t, n_exp), BF16),
        grid=(t // tm, n_exp // tn),
        in_specs=[pl.BlockSpec((tm, d), lambda i, j: (i, 0), pipeline_mode=pl.Buffered(1)),
                  pl.BlockSpec((None, tn, d), lambda i, j: (layer, j, 0)),
                  pl.BlockSpec((SUBLANES, tm, N_KEYS), lambda i, j: (j, i, 0))],
        out_specs=pl.BlockSpec((tm, tn), lambda i, j: (i, j)),
        name="peer_up",
        compiler_params=_cparams(("parallel", "parallel")),
    )(h, u_all, g3)


def _residual(acc, x, gate):
    return x + gate * acc


def _gated_residual_matmul(a, w_all, layer, x2, gate, s, *, tk, name):
    t, d = x2.shape
    bsz = gate.shape[0]
    tm = _pick(s, (1024, 512, 256, 128))
    tn = 1024
    (out,) = _matmul(
        a, [w_all], [0], d, tm=tm, tn=tn, tk=tk, out_dtypes=[F32], epilogue=_residual,
        extras=[(x2, (tm, tn), lambda i, j: (i, j)),
                (gate.reshape(bsz, 1, d), (None, 1, tn), lambda i, j: (i * tm // s, 0, j))],
        b_lead=layer, name=name)
    return out


def _peer_ffn(x, w_norm, shift, scale, gate, layer, w_q_all, keys, u_all, v_all):
    bsz, s, d = x.shape
    t = bsz * s
    h = _norm_mod(x, w_norm, shift, scale).reshape(t, d)
    tm = _pick(s, (1024, 512, 256, 128))
    (q,) = _matmul(h, [w_q_all], [0], w_q_all.shape[2], tm=tm, tn=1024, tk=d,
                   out_dtypes=[F32], b_lead=layer, name="peer_query")
    keys_b = keys.reshape(PEER_HEADS * 2, N_KEYS, PEER_HALF).astype(BF16)
    e1t, e2t, gwt = _peer_route(q, keys_b)
    g3 = _peer_gates(e1t, e2t, gwt)
    pmat = _peer_up(h, u_all, layer, g3)
    out = _gated_residual_matmul(pmat, v_all, layer, x.reshape(t, d), gate, s,
                                 tk=2048, name="peer_down")
    return out.reshape(bsz, s, d)


def _pad_in_proj(w):
    o_kv = Q_LORA
    o_pe = Q_LORA + KV_LORA
    o_rw = o_pe + QK_ROPE
    half = QK_ROPE // 2
    rows = w.shape[0]
    z = lambda n: jnp.zeros((rows, n), w.dtype)
    cq = w[:, :Q_LORA]
    ckv = w[:, o_kv:o_pe]
    kpe = w[:, o_pe:o_rw]
    kpe_swap = jnp.concatenate([kpe[:, half:], kpe[:, :half]], axis=1)
    r = w[:, o_rw:o_rw + RWKV_WIDTH]
    k = w[:, o_rw + RWKV_WIDTH:o_rw + 2 * RWKV_WIDTH]
    v = w[:, o_rw + 2 * RWKV_WIDTH:o_rw + 3 * RWKV_WIDTH]
    lora = w[:, o_rw + 3 * RWKV_WIDTH:]
    pad = LANES - QK_ROPE
    out = jnp.concatenate([ckv, kpe, z(pad), kpe_swap, z(pad), cq, lora,
                           z(P_R - P_LORA - lora.shape[1]), r, k, v], axis=1)
    assert out.shape[1] == P_WIDTH
    return out


def _hybrid_mixer(h, positions, w_in, q_norm_w, w_uq, kv_norm_w, w_ukv, qk_q_w, qk_k_w,
                  mu, w0, w2, a0, a2, g2, k_k, k_a, r_k, lnx_w, lnx_b):
    bsz, s, d = h.shape
    t = bsz * s
    tm = _pick(s, (1024, 512, 256, 128))
    w_pad = _pad_in_proj(w_in).astype(BF16)
    (p,) = _matmul(h.reshape(t, d), [w_pad], [0], P_WIDTH, tm=tm, tn=1024, tk=d,
                   out_dtypes=[F32], name="hyb_in_proj")
    q, k, v = _mla_qkv(p, positions, q_norm_w, w_uq, kv_norm_w, w_ukv, qk_q_w, qk_k_w)
    y_mla = _attention(q.reshape(bsz, s, -1), k.reshape(bsz, s, -1), v.reshape(bsz, s, -1), bsz, s)

    mu_full = jnp.concatenate([jnp.zeros((Q_LORA + KV_LORA + QK_ROPE,), F32), mu]).reshape(1, -1)
    mu_pad = _pad_in_proj(mu_full)
    zrow = jnp.zeros((RWKV_WIDTH,), F32)
    vecs = jnp.stack([w0, a0, k_k, k_a, r_k.reshape(-1), lnx_w, lnx_b, zrow])
    w2a2 = jnp.concatenate([w2, a2], axis=0).astype(BF16)
    y_rwkv = _rwkv(p.reshape(bsz, s, P_WIDTH), mu_pad, vecs, w2a2, g2.astype(BF16))
    return jnp.concatenate([y_mla, y_rwkv], axis=-1).reshape(t, -1)


def _conv_mixer(h, w_in, conv_w, s):
    t, d = h.shape
    bsz = t // s
    tm = _pick(s, (512, 256, 128))
    tn = 512
    nblk = d // tn
    w = w_in.astype(BF16)
    gate_b, uu = _matmul(h, [w, w, w], [0, nblk, 2 * nblk], d, tm=tm, tn=tn, tk=d,
                         out_dtypes=[F32, F32], epilogue=lambda gb, gc, u: (gb, gc * u),
                         name="conv_in_proj")
    z = _conv_gate(gate_b.reshape(bsz, s, d), uu.reshape(bsz, s, d), conv_w)
    return z.reshape(t, d)


def kernel(x, c, positions, ada_w, ada_b, norm_mix_w, norm_ffn_w, hyb_w_in, mla_q_norm_w, mla_w_uq, mla_kv_norm_w, mla_w_ukv, mla_qk_q_w, mla_qk_k_w, rwkv_mu, rwkv_w0, rwkv_w2, rwkv_a0, rwkv_a2, rwkv_g2, rwkv_k_k, rwkv_k_a, rwkv_r_k, rwkv_lnx_w, rwkv_lnx_b, hyb_w_out, conv_w_in, conv_w, conv_w_out, peer_w_q, peer_keys, peer_u, peer_v):
    bsz, s, d = x.shape
    t = bsz * s
    depth = ada_w.shape[0]
    mod = _ada_mod(c, ada_w, ada_b)
    w_q_all = peer_w_q.astype(BF16)
    u_all = peer_u.astype(BF16)
    v_all = peer_v
    for layer in range(depth):
        sh_m, sc_m, gt_m, sh_f, sc_f, gt_f = [mod[layer, :, j * d:(j + 1) * d] for j in range(6)]
        h = _norm_mod(x, norm_mix_w[layer], sh_m, sc_m)
        i = layer // 2
        if layer % 2 == 0:
            y = _hybrid_mixer(h, positions, hyb_w_in[i], mla_q_norm_w[i], mla_w_uq[i],
                              mla_kv_norm_w[i], mla_w_ukv[i], mla_qk_q_w[i], mla_qk_k_w[i],
                              rwkv_mu[i], rwkv_w0[i], rwkv_w2[i], rwkv_a0[i], rwkv_a2[i],
                              rwkv_g2[i], rwkv_k_k[i], rwkv_k_a[i], rwkv_r_k[i],
                              rwkv_lnx_w[i], rwkv_lnx_b[i])
            w_out = hyb_w_out.astype(BF16)
        else:
            y = _conv_mixer(h.reshape(t, d), conv_w_in[i], conv_w[i], s)
            w_out = conv_w_out.astype(BF16)
        x = _gated_residual_matmul(y, w_out, i, x.reshape(t, d), gt_m, s, tk=d,
                                   name="mixer_out_proj").reshape(bsz, s, d)
        x = _peer_ffn(x, norm_ffn_w[layer], sh_f, sc_f, gt_f, layer, w_q_all, peer_keys[layer],
                      u_all, v_all)
    return x
```

```python
import functools

import numpy as np
import jax
import jax.numpy as jnp
from jax import lax
from jax.experimental import pallas as pl
from jax.experimental.pallas import tpu as pltpu

F32 = jnp.float32
BF16 = jnp.bfloat16
I32 = jnp.int32

MLA_HEADS = 16
QK_NOPE = 128
QK_ROPE = 64
QK_HEAD = QK_NOPE + QK_ROPE
V_HEAD = 128
Q_LORA = 768
KV_LORA = 512
ROPE_THETA = 10000.0
RWKV_HEAD = 64
RWKV_HEADS = 32
RWKV_WIDTH = RWKV_HEADS * RWKV_HEAD
W_LORA = 64
A_LORA = 64
G_LORA = 128
LNX_EPS = 64e-5
CONV_WIDTH = 3
PEER_HEADS = 8
N_KEYS = 128
PEER_TOPK = 16
PEER_HALF = 128
NORM_EPS = 1e-6

LANES = 128
SUBLANES = 8
VMEM_LIMIT = 56 * 1024 * 1024

P_CKV = 0
P_KPE = 512
P_CQ = 768
P_LORA = 1536
P_R = 2048
P_K = P_R + RWKV_WIDTH
P_V = P_K + RWKV_WIDTH
P_WIDTH = P_V + RWKV_WIDTH
RWKV_CHUNK = 64
RWKV_PAIRS_PER_STEP = 8
ATTN_HEADS_PER_STEP = 4
ROUTE_HEADS_PER_STEP = 2


def _cparams(sem):
    return pltpu.CompilerParams(dimension_semantics=sem, vmem_limit_bytes=VMEM_LIMIT)


def _dot(a, b):
    return lax.dot_general(a, b, (((1,), (0,)), ((), ())), preferred_element_type=F32)


def _dot_nt(a, b):
    return lax.dot_general(a, b, (((1,), (1,)), ((), ())), preferred_element_type=F32)


def _dot_tn(a, b):
    return lax.dot_general(a, b, (((0,), (0,)), ((), ())), preferred_element_type=F32)


def _split_dot(x, m):
    hi = x.astype(BF16)
    lo = (x - hi.astype(F32)).astype(BF16)
    return _dot(hi, m) + _dot(lo, m)


def _mm_body(*refs, nb, nk, n_extra, trans_b, epilogue):
    a_ref = refs[0]
    b_refs = refs[1:1 + nb]
    extra = refs[1 + nb:1 + nb + n_extra]
    o_refs = refs[1 + nb + n_extra:]
    a = a_ref[...].astype(BF16)
    parts = []
    for b_ref in b_refs:
        b = b_ref[...].astype(BF16)
        parts.append(_dot_nt(a, b) if trans_b else _dot(a, b))

    def finish(accs):
        outs = epilogue(*accs, *[e[...] for e in extra])
        if not isinstance(outs, (tuple, list)):
            outs = (outs,)
        for o_ref, o in zip(o_refs, outs):
            o_ref[...] = o.astype(o_ref.dtype)

    if nk == 1:
        finish(parts)
        return
    n_out = len(o_refs) - nb
    acc_refs = o_refs[n_out:]
    o_refs = o_refs[:n_out]
    k = pl.program_id(2)

    @pl.when(k == 0)
    def _():
        for acc, p in zip(acc_refs, parts):
            acc[...] = p

    @pl.when(k > 0)
    def _():
        for acc, p in zip(acc_refs, parts):
            acc[...] += p

    @pl.when(k == nk - 1)
    def _():
        finish([acc[...] for acc in acc_refs])


def _matmul(a, bs, b_col_blocks, n_out_cols, *, tm, tn, tk, out_dtypes, epilogue=None,
            extras=(), trans_b=False, b_lead=None, name="matmul"):
    m, kdim = a.shape
    nb = len(bs)
    nk = kdim // tk
    assert m % tm == 0 and kdim % tk == 0 and n_out_cols % tn == 0
    if epilogue is None:
        epilogue = lambda acc: acc
    grid = (m // tm, n_out_cols // tn, nk)
    in_specs = [pl.BlockSpec((tm, tk), lambda i, j, k: (i, k))]
    for off in b_col_blocks:
        if b_lead is not None:
            assert not trans_b
            in_specs.append(pl.BlockSpec((None, tk, tn),
                                         lambda i, j, k, off=off: (b_lead, k, j + off)))
        elif trans_b:
            in_specs.append(pl.BlockSpec((tn, tk), lambda i, j, k, off=off: (j + off, k)))
        else:
            in_specs.append(pl.BlockSpec((tk, tn), lambda i, j, k, off=off: (k, j + off)))
    extra_arrays = []
    for arr, bshape, imap in extras:
        in_specs.append(pl.BlockSpec(bshape, lambda i, j, k, imap=imap: imap(i, j)))
        extra_arrays.append(arr)
    out_shape = [jax.ShapeDtypeStruct((m, n_out_cols), dt) for dt in out_dtypes]
    out_specs = [pl.BlockSpec((tm, tn), lambda i, j, k: (i, j)) for _ in out_dtypes]
    scratch = [pltpu.VMEM((tm, tn), F32) for _ in range(nb)] if nk > 1 else []
    body = functools.partial(_mm_body, nb=nb, nk=nk, n_extra=len(extras), trans_b=trans_b,
                             epilogue=epilogue)
    outs = pl.pallas_call(
        body, out_shape=out_shape, grid=grid, in_specs=in_specs, out_specs=out_specs,
        scratch_shapes=scratch, name=name,
        compiler_params=_cparams(("parallel", "parallel", "arbitrary")),
    )(a, *bs, *extra_arrays)
    return outs


def _pick(n, prefs):
    for p in prefs:
        if n % p == 0:
            return p
    raise ValueError(f"no tile for {n}")


def _ada_body(c_ref, w_ref, b_ref, o_ref):
    c = c_ref[...]
    cond = (c * (1.0 / (1.0 + jnp.exp(-c)))).astype(BF16)
    o_ref[...] = _dot(cond, w_ref[...].astype(BF16)) + b_ref[...]


def _ada_mod(c, ada_w, ada_b):
    depth, d, n = ada_w.shape
    bsz = c.shape[0]
    tn = 512
    return pl.pallas_call(
        _ada_body,
        out_shape=jax.ShapeDtypeStruct((depth, bsz, n), F32),
        grid=(depth, n // tn),
        in_specs=[
            pl.BlockSpec((bsz, d), lambda l, j: (0, 0)),
            pl.BlockSpec((None, d, tn), lambda l, j: (l, 0, j)),
            pl.BlockSpec((None, 1, tn), lambda l, j: (l, 0, j)),
        ],
        out_specs=pl.BlockSpec((None, bsz, tn), lambda l, j: (l, 0, j)),
        name="ada_mod",
        compiler_params=_cparams(("parallel", "parallel")),
    )(c, ada_w, ada_b.reshape(depth, 1, n))


def _norm_mod_body(x_ref, w_ref, sh_ref, sc_ref, o_ref):
    x = x_ref[...]
    ms = jnp.mean(x * x, axis=-1, keepdims=True)
    y = x * lax.rsqrt(ms + NORM_EPS) * w_ref[...]
    o_ref[...] = (y * (1.0 + sc_ref[...]) + sh_ref[...]).astype(o_ref.dtype)


def _norm_mod(x, w, shift, scale):
    bsz, s, d = x.shape
    ts = _pick(s, (256, 128))
    vec = lambda v: v.reshape(bsz, 1, d)
    return pl.pallas_call(
        _norm_mod_body,
        out_shape=jax.ShapeDtypeStruct((bsz, s, d), BF16),
        grid=(bsz, s // ts),
        in_specs=[
            pl.BlockSpec((None, ts, d), lambda b, i: (b, i, 0)),
            pl.BlockSpec((1, d), lambda b, i: (0, 0)),
            pl.BlockSpec((None, 1, d), lambda b, i: (b, 0, 0)),
            pl.BlockSpec((None, 1, d), lambda b, i: (b, 0, 0)),
        ],
        out_specs=pl.BlockSpec((None, ts, d), lambda b, i: (b, i, 0)),
        name="norm_mod",
        compiler_params=_cparams(("parallel", "parallel")),
    )(x, w.reshape(1, d), vec(shift), vec(scale))


def _rope_tables(pos_ref, freq_ref):
    ang = pos_ref[...].astype(F32) * freq_ref[...]
    lane = lax.broadcasted_iota(I32, ang.shape, 1)
    cos = jnp.where(lane < QK_ROPE, jnp.cos(ang), 0.0)
    sin = jnp.sin(ang)
    sin_signed = jnp.where(lane < QK_ROPE // 2, -sin, jnp.where(lane < QK_ROPE, sin, 0.0))
    return cos, sin_signed


def _mla_q_body(cq_ref, nw_ref, wuq_ref, hw_ref, pos_ref, freq_ref, o_ref, *, scale):
    cq = cq_ref[...]
    ms = jnp.mean(cq * cq, axis=-1, keepdims=True)
    cqn = (cq * lax.rsqrt(ms + NORM_EPS) * nw_ref[...]).astype(BF16)
    cos, sin_signed = _rope_tables(pos_ref, freq_ref)
    w_nope = hw_ref[0:1, :]
    w_rope = hw_ref[1:2, :]
    w_swap = hw_ref[2:3, :]
    for h in range(MLA_HEADS):
        qh = _dot(cqn, wuq_ref[:, h * 3 * LANES:(h + 1) * 3 * LANES])
        nope = qh[:, :LANES]
        rope = qh[:, LANES:2 * LANES]
        swap = qh[:, 2 * LANES:]
        ss = jnp.sum(nope * nope + rope * rope, axis=-1, keepdims=True)
        fac = lax.rsqrt(ss * (1.0 / QK_HEAD) + NORM_EPS) * scale
        o_ref[:, h * 2 * LANES:h * 2 * LANES + LANES] = (nope * fac * w_nope).astype(o_ref.dtype)
        roped = (rope * w_rope * cos + swap * w_swap * sin_signed) * fac
        o_ref[:, h * 2 * LANES + LANES:(h + 1) * 2 * LANES] = roped.astype(o_ref.dtype)


def _mla_kv_body(ckv_ref, kpe_ref, nw_ref, wukv_ref, hw_ref, pos_ref, freq_ref, k_ref, v_ref):
    ckv = ckv_ref[...]
    ms = jnp.mean(ckv * ckv, axis=-1, keepdims=True)
    ckvn = (ckv * lax.rsqrt(ms + NORM_EPS) * nw_ref[...]).astype(BF16)
    cos, sin_signed = _rope_tables(pos_ref, freq_ref)
    kpe = kpe_ref[:, :LANES]
    kpe_swap = kpe_ref[:, LANES:]
    pe_ss = jnp.sum(kpe * kpe, axis=-1, keepdims=True)
    pe_roped = kpe * hw_ref[1:2, :] * cos + kpe_swap * hw_ref[2:3, :] * sin_signed
    w_nope = hw_ref[0:1, :]
    for h in range(MLA_HEADS):
        kn = _dot(ckvn, wukv_ref[:, h * LANES:(h + 1) * LANES])
        ss = jnp.sum(kn * kn, axis=-1, keepdims=True) + pe_ss
        fac = lax.rsqrt(ss * (1.0 / QK_HEAD) + NORM_EPS)
        k_ref[:, h * 2 * LANES:h * 2 * LANES + LANES] = (kn * fac * w_nope).astype(k_ref.dtype)
        k_ref[:, h * 2 * LANES + LANES:(h + 1) * 2 * LANES] = (pe_roped * fac).astype(k_ref.dtype)
    nv = MLA_HEADS * V_HEAD
    v_ref[...] = _dot(ckvn, wukv_ref[:, nv:]).astype(v_ref.dtype)


def _head_norm_rows(w):
    half = QK_ROPE // 2
    zeros = jnp.zeros((LANES - QK_ROPE,), F32)
    rope = jnp.concatenate([w[QK_NOPE:], zeros])
    swap = jnp.concatenate([w[QK_NOPE + half:], w[QK_NOPE:QK_NOPE + half], zeros])
    rows = jnp.stack([w[:QK_NOPE], rope, swap])
    return jnp.concatenate([rows, jnp.zeros((SUBLANES - 3, LANES), F32)], axis=0)


def _rope_freq_row():
    half = QK_ROPE // 2
    f = ROPE_THETA ** (-jnp.arange(half, dtype=F32) / half)
    return jnp.concatenate([f, f, jnp.zeros((LANES - QK_ROPE,), F32)]).reshape(1, LANES)


def _wuq_layout():
    half = QK_ROPE // 2
    idx, mask = [], []
    for h in range(MLA_HEADS):
        base = h * QK_HEAD
        nope = list(range(base, base + QK_NOPE))
        rope = list(range(base + QK_NOPE, base + QK_HEAD))
        swap = rope[half:] + rope[:half]
        pad = [0] * (LANES - QK_ROPE)
        idx += nope + rope + pad + swap + pad
        mask += [1] * QK_NOPE + [1] * QK_ROPE + [0] * len(pad) + [1] * QK_ROPE + [0] * len(pad)
    return np.asarray(idx, np.int32), np.asarray(mask, np.float32)


def _wukv_layout():
    per = QK_NOPE + V_HEAD
    k_idx = [h * per + i for h in range(MLA_HEADS) for i in range(QK_NOPE)]
    v_idx = [h * per + QK_NOPE + i for h in range(MLA_HEADS) for i in range(V_HEAD)]
    return np.asarray(k_idx + v_idx, np.int32)


def _mla_qkv(p, positions, q_norm_w, w_uq, kv_norm_w, w_ukv, qk_q_w, qk_k_w):
    t = p.shape[0]
    tm = _pick(t, (256, 128))
    idx, mask = _wuq_layout()
    wuq = (w_uq[:, idx] * mask[None, :]).astype(BF16)
    wukv = w_ukv[:, _wukv_layout()].astype(BF16)
    pos = positions.reshape(t, 1)
    freq = _rope_freq_row()
    hq = _head_norm_rows(qk_q_w)
    hk = _head_norm_rows(qk_k_w)
    qw = MLA_HEADS * 2 * LANES
    row = lambda i: (i, 0)
    fixed = lambda i: (0, 0)
    q = pl.pallas_call(
        functools.partial(_mla_q_body, scale=QK_HEAD ** -0.5),
        out_shape=jax.ShapeDtypeStruct((t, qw), BF16),
        grid=(t // tm,),
        in_specs=[
            pl.BlockSpec((tm, Q_LORA), lambda i: (i, P_CQ // Q_LORA)),
            pl.BlockSpec((1, Q_LORA), fixed),
            pl.BlockSpec(wuq.shape, fixed),
            pl.BlockSpec((SUBLANES, LANES), fixed),
            pl.BlockSpec((tm, 1), row),
            pl.BlockSpec((1, LANES), fixed),
        ],
        out_specs=pl.BlockSpec((tm, qw), row),
        name="mla_q",
        compiler_params=_cparams(("parallel",)),
    )(p, q_norm_w.reshape(1, Q_LORA), wuq, hq, pos, freq)
    k, v = pl.pallas_call(
        _mla_kv_body,
        out_shape=[jax.ShapeDtypeStruct((t, qw), BF16),
                   jax.ShapeDtypeStruct((t, MLA_HEADS * V_HEAD), BF16)],
        grid=(t // tm,),
        in_specs=[
            pl.BlockSpec((tm, KV_LORA), lambda i: (i, P_CKV // KV_LORA)),
            pl.BlockSpec((tm, 2 * LANES), lambda i: (i, P_KPE // (2 * LANES))),
            pl.BlockSpec((1, KV_LORA), fixed),
            pl.BlockSpec(wukv.shape, fixed),
            pl.BlockSpec((SUBLANES, LANES), fixed),
            pl.BlockSpec((tm, 1), row),
            pl.BlockSpec((1, LANES), fixed),
        ],
        out_specs=[pl.BlockSpec((tm, qw), row), pl.BlockSpec((tm, MLA_HEADS * V_HEAD), row)],
        name="mla_kv",
        compiler_params=_cparams(("parallel",)),
    )(p, p, kv_norm_w.reshape(1, KV_LORA), wukv, hk, pos, freq)
    return q, k, v


def _attn_body(qi_ref, ki_ref, q_ref, k_ref, v_ref, o_ref, m_ref, l_ref, acc_ref, *, tq):
    step_id = pl.program_id(2)
    qi = qi_ref[step_id]
    ki = ki_ref[step_id]

    @pl.when(ki == 0)
    def _():
        m_ref[...] = jnp.full(m_ref.shape, -jnp.inf, F32)
        l_ref[...] = jnp.zeros(l_ref.shape, F32)
        acc_ref[...] = jnp.zeros(acc_ref.shape, F32)

    heads = range(ATTN_HEADS_PER_STEP)
    qk = lambda h: slice(h * 2 * LANES, (h + 1) * 2 * LANES)
    vo = lambda h: slice(h * V_HEAD, (h + 1) * V_HEAD)

    def step(masked):
        ss = [_dot_nt(q_ref[:, qk(h)], k_ref[:, qk(h)]) for h in heads]
        if masked:
            row = lax.broadcasted_iota(I32, ss[0].shape, 0)
            col = lax.broadcasted_iota(I32, ss[0].shape, 1)
            ss = [jnp.where(col <= row, s, -jnp.inf) for s in ss]
        reps = tq // LANES
        prs, alphas = [], []
        for h, s in zip(heads, ss):
            m_prev = m_ref[h]
            m_new = jnp.maximum(m_prev, jnp.max(s, axis=-1, keepdims=True))
            alpha = jnp.exp(m_prev - m_new)
            pr = jnp.exp(s - jnp.tile(m_new, (1, reps)))
            l_ref[h] = alpha * l_ref[h] + jnp.sum(pr, axis=-1, keepdims=True)
            m_ref[h] = m_new
            prs.append(pr.astype(BF16))
            alphas.append(alpha)
        pvs = [_dot(pr, v_ref[:, vo(h)]) for h, pr in zip(heads, prs)]
        for h, alpha, pv in zip(heads, alphas, pvs):
            acc_ref[h] = alpha * acc_ref[h] + pv

    @pl.when(ki < qi)
    def _():
        step(False)

    @pl.when(ki == qi)
    def _():
        step(True)
        for h in heads:
            o_ref[:, vo(h)] = (acc_ref[h] / l_ref[h]).astype(o_ref.dtype)


def _attention(q, k, v, bsz, s):
    tq = _pick(s, (512, 256, 128))
    n = s // tq
    pairs = [(i, j) for i in range(n) for j in range(i + 1)]
    qi_tab = jnp.asarray([i for i, _ in pairs], I32)
    ki_tab = jnp.asarray([j for _, j in pairs], I32)
    nh = ATTN_HEADS_PER_STEP
    grid_spec = pltpu.PrefetchScalarGridSpec(
        num_scalar_prefetch=2,
        grid=(bsz, MLA_HEADS // nh, len(pairs)),
        in_specs=[
            pl.BlockSpec((None, tq, nh * 2 * LANES), lambda b, h, p, qi, ki: (b, qi[p], h)),
            pl.BlockSpec((None, tq, nh * 2 * LANES), lambda b, h, p, qi, ki: (b, ki[p], h)),
            pl.BlockSpec((None, tq, nh * V_HEAD), lambda b, h, p, qi, ki: (b, ki[p], h)),
        ],
        out_specs=pl.BlockSpec((None, tq, nh * V_HEAD), lambda b, h, p, qi, ki: (b, qi[p], h)),
        scratch_shapes=[pltpu.VMEM((nh, tq, LANES), F32), pltpu.VMEM((nh, tq, LANES), F32),
                        pltpu.VMEM((nh, tq, V_HEAD), F32)],
    )
    return pl.pallas_call(
        functools.partial(_attn_body, tq=tq),
        out_shape=jax.ShapeDtypeStruct((bsz, s, MLA_HEADS * V_HEAD), BF16),
        grid_spec=grid_spec,
        name="mla_attention",
        compiler_params=_cparams(("parallel", "parallel", "arbitrary")),
    )(qi_tab, ki_tab, q, k, v)


def _rwkv_body(r_ref, k_ref, v_ref, xl_ref, mu_r_ref, mu_k_ref, mu_v_ref, mu_l_ref, vec_ref,
               w2a2_ref, g2_ref, o_ref, state_ref, pr_ref, pk_ref, pv_ref, pl_ref, *, tb, npairs):
    t_idx = pl.program_id(2)

    @pl.when(t_idx == 0)
    def _():
        state_ref[...] = jnp.zeros(state_ref.shape, F32)
        pr_ref[...] = jnp.zeros(pr_ref.shape, F32)
        pk_ref[...] = jnp.zeros(pk_ref.shape, F32)
        pv_ref[...] = jnp.zeros(pv_ref.shape, F32)
        pl_ref[...] = jnp.zeros(pl_ref.shape, F32)

    def mix(x_ref, prev_ref, mu_ref):
        x = x_ref[...]
        row = lax.broadcasted_iota(I32, x.shape, 0)
        shifted = jnp.where(row == 0, prev_ref[...], pltpu.roll(x, 1, axis=0))
        prev_ref[...] = x[tb - 1:tb, :]
        return x + (shifted - x) * mu_ref[...]

    r = mix(r_ref, pr_ref, mu_r_ref)
    k = mix(k_ref, pk_ref, mu_k_ref)
    v = mix(v_ref, pv_ref, mu_v_ref)
    xl = mix(xl_ref, pl_ref, mu_l_ref)

    lane = lax.broadcasted_iota(I32, (tb, LANES), 1)
    first = lane < RWKV_HEAD
    x0 = xl[:, :LANES]
    lhs_w = jnp.where(first, jnp.tanh(x0), 0.0).astype(BF16)
    lhs_a = jnp.where(first, 0.0, x0).astype(BF16)
    xg = xl[:, LANES:]
    lhs_g = (1.0 / (1.0 + jnp.exp(-xg))).astype(BF16)
    cols = [slice(p * LANES, (p + 1) * LANES) for p in range(npairs)]
    outs = _rwkv_pairs([r[:, c] for c in cols], [k[:, c] for c in cols], [v[:, c] for c in cols],
                       lhs_w, lhs_a, lhs_g, [vec_ref[:, c] for c in cols],
                       [w2a2_ref[:, c] for c in cols], [g2_ref[:, c] for c in cols],
                       state_ref, tb)
    for c, out in zip(cols, outs):
        o_ref[:, c] = out.astype(o_ref.dtype)


def _rwkv_pairs(rs, ks, vs_in, lhs_w, lhs_a, lhs_g, vecs, w2a2s, g2s, state_ref, tb):
    c = RWKV_CHUNK
    pairs = range(len(rs))
    chunks = range(tb // c)
    each = lambda f, *lists: [f(*args) for args in zip(*lists)]

    ri = lax.broadcasted_iota(I32, (LANES, LANES), 0)
    ci = lax.broadcasted_iota(I32, (LANES, LANES), 1)
    same = (ri // RWKV_HEAD) == (ci // RWKV_HEAD)
    ones_bd = jnp.where(same, 1.0, 0.0).astype(BF16)
    tri_r = lax.broadcasted_iota(I32, (c, c), 0)
    tri_c = lax.broadcasted_iota(I32, (c, c), 1)
    tri = jnp.where(tri_c <= tri_r, 1.0, 0.0).astype(BF16)
    eye = jnp.where(ri == ci, 1.0, 0.0)
    first_c = lax.broadcasted_iota(I32, (c, LANES), 1) < RWKV_HEAD

    def stack(z):
        return jnp.concatenate([jnp.where(first_c, z, 0.0), jnp.where(first_c, 0.0, z)],
                               axis=0).astype(BF16)

    def split(x):
        hi = x.astype(BF16)
        return hi, (x - hi.astype(F32)).astype(BF16)

    def group_sum(xs):
        parts = each(split, xs)
        his = [_dot(hi, ones_bd) for hi, _ in parts]
        los = [_dot(lo, ones_bd) for _, lo in parts]
        return each(lambda a, b: a + b, his, los)

    lws = [_dot(lhs_w, w) for w in w2a2s]
    las = [_dot(lhs_a, w) for w in w2a2s]
    gs = [_dot(lhs_g, g2) for g2 in g2s]
    kks = each(lambda k, vec: k * vec[2:3, :], ks, vecs)
    nrms = group_sum(each(lambda kk: kk * kk, kks))
    kks = each(lambda kk, n: kk / jnp.maximum(jnp.sqrt(n), 1e-12), kks, nrms)

    def decay_and_gate(lw, la, vec):
        zw = -(vec[0:1, :] + lw)
        softplus = jnp.maximum(zw, 0.0) + jnp.log(1.0 + jnp.exp(-jnp.abs(zw)))
        a = 1.0 / (1.0 + jnp.exp(-(vec[1:2, :] + la)))
        return -jnp.exp(-softplus - 0.5), a

    dg = each(decay_and_gate, lws, las, vecs)
    logds = [d for d, _ in dg]
    a_gates = [a for _, a in dg]
    kps = each(lambda k, a, vec: k * (1.0 + (a - 1.0) * vec[3:4, :]), ks, a_gates, vecs)
    bvecs = each(lambda kk, a: kk * a, kks, a_gates)
    bonus_sums = group_sum(each(lambda r, kp, vec: r * kp * vec[4:5, :], rs, kps, vecs))

    inst = [(p, ch) for p in pairs for ch in chunks]
    rows = lambda x, ch: x[ch * c:(ch + 1) * c]
    lds = [rows(logds[p], ch) for p, ch in inst]
    ld_parts = each(split, lds)
    lc_hi = [_dot(tri, hi) for hi, _ in ld_parts]
    lc_lo = [_dot(tri, lo) for _, lo in ld_parts]
    lcs = each(lambda a, b: a + b, lc_hi, lc_lo)
    ltots = [lc[c - 1:c, :] for lc in lcs]
    e_negs = [jnp.exp(-lc) for lc in lcs]
    e_rems = each(lambda lt, lc: jnp.exp(lt - lc), ltots, lcs)
    xas = [stack(-rows(kks[p], ch) * jnp.exp(lc - ld)) for (p, ch), lc, ld in zip(inst, lcs, lds)]
    xrs = [stack(rows(rs[p], ch) * jnp.exp(lc)) for (p, ch), lc in zip(inst, lcs)]
    ybs = [stack(rows(bvecs[p], ch) * e) for (p, ch), e in zip(inst, e_negs)]
    yks = [stack(rows(kps[p], ch) * e) for (p, ch), e in zip(inst, e_negs)]
    vss = [stack(rows(vs_in[p], ch)) for p, ch in inst]
    bhs = [stack(rows(bvecs[p], ch) * e) for (p, ch), e in zip(inst, e_rems)]
    khs = [stack(rows(kps[p], ch) * e) for (p, ch), e in zip(inst, e_rems)]

    n2 = 2 * c
    xars = each(lambda a, b: jnp.concatenate([a, b], axis=0), xas, xrs)
    rr = lax.broadcasted_iota(I32, (2 * n2, n2), 0)
    cc = lax.broadcasted_iota(I32, (2 * n2, n2), 1)
    rr_in = rr % n2
    tri_mask = ((rr_in // c) == (cc // c)) & ((cc % c) < (rr_in % c) + rr // n2)
    mbs = each(lambda a, b: jnp.where(tri_mask, _dot_nt(a, b), 0.0), xars, ybs)
    mks = each(lambda a, b: jnp.where(tri_mask, _dot_nt(a, b), 0.0).astype(BF16), xars, yks)
    labs = [mb[:n2] for mb in mbs]
    mrbs = [mb[n2:].astype(BF16) for mb in mbs]

    pbs = [lab.astype(BF16) for lab in labs]
    tinvs = [eye + lab for lab in labs]
    pbs = [_dot(pb, pb).astype(BF16) for pb in pbs]
    n_sq = 2
    while n_sq * 2 < c:
        both = each(lambda pb, ti: _dot(pb, jnp.concatenate([ti.astype(BF16), pb], axis=1)),
                    pbs, tinvs)
        tinvs = each(lambda ti, bo: ti + bo[:, :n2], tinvs, both)
        pbs = [bo[:, n2:].astype(BF16) for bo in both]
        n_sq *= 2
    tinvs = each(lambda ti, pb: (ti + _dot(pb, ti.astype(BF16))).astype(BF16), tinvs, pbs)
    mkvs = each(_dot, mks, vss)
    kvs = each(_dot_tn, vss, khs)

    states = [state_ref[p] for p in pairs]
    ys = [[] for _ in pairs]
    n_ch = len(chunks)
    for ch in chunks:
        ids = [p * n_ch + ch for p in pairs]
        sbs = [s.astype(BF16) for s in states]
        withs = [_dot_nt(xars[i], sb) + mkvs[i] for i, sb in zip(ids, sbs)]
        ubs = [_dot(tinvs[i], w[:n2].astype(BF16)).astype(BF16) for i, w in zip(ids, withs)]
        states = [s * jnp.exp(ltots[i]) + _dot_tn(ub, bhs[i]) + kvs[i]
                  for i, s, ub in zip(ids, states, ubs)]
        for p, i, w, ub in zip(pairs, ids, withs, ubs):
            ystack = w[n2:] + _dot(mrbs[i], ub)
            ys[p].append(ystack[:c] + ystack[c:])
    for p, s in zip(pairs, states):
        state_ref[p] = s

    ycat = [jnp.concatenate(y, axis=0) for y in ys]
    inv_n = 1.0 / RWKV_HEAD
    means = [m * inv_n for m in group_sum(ycat)]
    ds = each(lambda y, m: y - m, ycat, means)
    variances = [v * inv_n for v in group_sum(each(lambda d: d * d, ds))]
    outs = []
    for d, var, vec, bs, v, g in zip(ds, variances, vecs, bonus_sums, vs_in, gs):
        yn = d * lax.rsqrt(var + LNX_EPS) * vec[5:6, :] + vec[6:7, :]
        outs.append((yn + bs * v) * g)
    return outs


def _rwkv(p3, mu_pad, vecs, w2a2, g2):
    bsz, s, _ = p3.shape
    tb = _pick(s, (256, 128, 64))
    npairs = RWKV_PAIRS_PER_STEP
    wd = npairs * LANES
    groups = RWKV_HEADS // 2 // npairs
    colblk = lambda off: (lambda b, hp, t: (b, t, off // wd + hp))
    mublk = lambda off: (lambda b, hp, t: (0, off // wd + hp))
    return pl.pallas_call(
        functools.partial(_rwkv_body, tb=tb, npairs=npairs),
        out_shape=jax.ShapeDtypeStruct((bsz, s, RWKV_WIDTH), BF16),
        grid=(bsz, groups, s // tb),
        in_specs=[
            pl.BlockSpec((None, tb, wd), colblk(P_R)),
            pl.BlockSpec((None, tb, wd), colblk(P_K)),
            pl.BlockSpec((None, tb, wd), colblk(P_V)),
            pl.BlockSpec((None, tb, 2 * LANES), lambda b, hp, t: (b, t, P_LORA // (2 * LANES))),
            pl.BlockSpec((1, wd), mublk(P_R)),
            pl.BlockSpec((1, wd), mublk(P_K)),
            pl.BlockSpec((1, wd), mublk(P_V)),
            pl.BlockSpec((1, 2 * LANES), lambda b, hp, t: (0, P_LORA // (2 * LANES))),
            pl.BlockSpec((SUBLANES, wd), lambda b, hp, t: (0, hp)),
            pl.BlockSpec((LANES, wd), lambda b, hp, t: (0, hp)),
            pl.BlockSpec((LANES, wd), lambda b, hp, t: (0, hp)),
        ],
        out_specs=pl.BlockSpec((None, tb, wd), lambda b, hp, t: (b, t, hp)),
        scratch_shapes=[pltpu.VMEM((npairs, LANES, LANES), F32),
                        pltpu.VMEM((1, wd), F32), pltpu.VMEM((1, wd), F32),
                        pltpu.VMEM((1, wd), F32), pltpu.VMEM((1, 2 * LANES), F32)],
        name="rwkv7",
        compiler_params=_cparams(("parallel", "parallel", "arbitrary")),
    )(p3, p3, p3, p3, mu_pad, mu_pad, mu_pad, mu_pad, vecs, w2a2, g2)


def _conv_body(h_ref, wb_ref, wc_ref, wu_ref, cw_ref, o_ref, tail_ref, *, tm, seq):
    i = pl.program_id(1)
    a = h_ref[...]
    gate_b = _dot(a, wb_ref[...])
    uu = _dot(a, wc_ref[...]) * _dot(a, wu_ref[...])

    @pl.when((i * tm) % seq == 0)
    def _():
        tail_ref[...] = jnp.zeros(tail_ref.shape, F32)

    tail = tail_ref[...]
    row = lax.broadcasted_iota(I32, uu.shape, 0)
    h1 = tail[SUBLANES - 1:SUBLANES, :]
    h2 = tail[SUBLANES - 2:SUBLANES - 1, :]
    s1 = jnp.where(row == 0, h1, pltpu.roll(uu, 1, axis=0))
    s2 = jnp.where(row == 0, h2, jnp.where(row == 1, h1, pltpu.roll(uu, 2, axis=0)))
    y = cw_ref[0:1, :] * s2 + cw_ref[1:2, :] * s1 + cw_ref[2:3, :] * uu
    o_ref[...] = (gate_b * y).astype(o_ref.dtype)
    tail_ref[...] = uu[tm - SUBLANES:tm, :]


def _conv_mixer(h, w_in, conv_w, s):
    t, d = h.shape
    tm = _pick(s, (512, 256, 128))
    tn = 512
    nblk = d // tn
    w = w_in.astype(BF16)
    cw = jnp.concatenate([conv_w, jnp.zeros((SUBLANES - CONV_WIDTH, d), F32)], axis=0)
    wspec = lambda off: pl.BlockSpec((d, tn), lambda j, i: (0, j + off))
    return pl.pallas_call(
        functools.partial(_conv_body, tm=tm, seq=s),
        out_shape=jax.ShapeDtypeStruct((t, d), BF16),
        grid=(nblk, t // tm),
        in_specs=[pl.BlockSpec((tm, d), lambda j, i: (i, 0)),
                  wspec(0), wspec(nblk), wspec(2 * nblk),
                  pl.BlockSpec((SUBLANES, tn), lambda j, i: (0, j))],
        out_specs=pl.BlockSpec((tm, tn), lambda j, i: (i, j)),
        scratch_shapes=[pltpu.VMEM((SUBLANES, tn), F32)],
        name="conv_mixer",
        compiler_params=_cparams(("parallel", "arbitrary")),
    )(h, w, w, w, cw)


def _rank_pairs():
    kt = PEER_TOPK
    return [(a, b) for a in range(kt) for b in range(kt) if (a + 1) * (b + 1) <= kt]


PAIR_ROWS = -(-len(_rank_pairs()) // SUBLANES) * SUBLANES


def _pair_matrices():
    m1 = np.zeros((PAIR_ROWS, PEER_TOPK), np.float32)
    m2 = np.zeros((PAIR_ROWS, PEER_TOPK), np.float32)
    for p, (a, b) in enumerate(_rank_pairs()):
        m1[p, a] = 1.0
        m2[p, b] = 1.0
    return m1, m2


def _exact_dot_left(m, x):
    hi = x.astype(BF16)
    r1 = x - hi.astype(F32)
    mid = r1.astype(BF16)
    lo = (r1 - mid.astype(F32)).astype(BF16)
    return (_dot(m, hi) + _dot(m, mid)) + _dot(m, lo)


def _extract_top_rows(xs, row_iota, n, rank_iota, extras=None):
    xs = list(xs)
    sentinel = float(xs[0].shape[0])
    zeros = lambda: [jnp.zeros((n, x.shape[1]), F32) for x in xs]
    vals, idxs, exts = zeros(), zeros(), zeros()
    for kk in range(n):
        here = rank_iota == kk
        for j, x in enumerate(xs):
            m = jnp.max(x, axis=0, keepdims=True)
            idx = jnp.min(jnp.where(x == m, row_iota, sentinel), axis=0, keepdims=True)
            hit = row_iota == idx
            if extras is not None:
                ext = jnp.max(jnp.where(hit, extras[j], -1.0), axis=0, keepdims=True)
                exts[j] = jnp.where(here, ext, exts[j])
            xs[j] = jnp.where(hit, -jnp.inf, x)
            vals[j] = jnp.where(here, m, vals[j])
            idxs[j] = jnp.where(here, idx, idxs[j])
    return list(zip(vals, idxs, exts))


def _route_body(q_ref, keys_ref, m1_ref, m2_ref, e1_ref, e2_ref, gw_ref):
    kt = PEER_TOPK
    tb = q_ref.shape[0]
    nh = ROUTE_HEADS_PER_STEP
    key_iota = lax.broadcasted_iota(I32, (N_KEYS, tb), 0).astype(F32)
    rank_iota = lax.broadcasted_iota(I32, (kt, tb), 0)
    scs = []
    for hp in range(2 * nh):
        qh = q_ref[:, hp * PEER_HALF:(hp + 1) * PEER_HALF].astype(BF16)
        scs.append(_dot_nt(keys_ref[hp], qh))
    tops = _extract_top_rows(scs, key_iota, kt, rank_iota)
    m1 = m1_ref[...]
    m2 = m2_ref[...]
    pair_iota = lax.broadcasted_iota(I32, (PAIR_ROWS, tb), 0).astype(F32)
    cands, eids = [], []
    for h in range(nh):
        (sv1, si1, _), (sv2, si2, _) = tops[2 * h], tops[2 * h + 1]
        cand = _exact_dot_left(m1, sv1) + _exact_dot_left(m2, sv2)
        cands.append(jnp.where(pair_iota < float(len(_rank_pairs())), cand, -jnp.inf))
        eids.append(_dot(m1, si1.astype(BF16)) * float(N_KEYS) + _dot(m2, si2.astype(BF16)))
    picks = _extract_top_rows(cands, pair_iota, kt, rank_iota, extras=eids)
    for h, (cv, _, ev) in enumerate(picks):
        rows = slice(h * kt, (h + 1) * kt)
        ex = jnp.exp(cv - cv[0:1, :])
        gw_ref[rows, :] = ex / jnp.sum(ex, axis=0, keepdims=True)
        ei = ev.astype(I32)
        e1_ref[rows, :] = (ei >> 7).astype(F32)
        e2_ref[rows, :] = (ei & (N_KEYS - 1)).astype(F32)


def _peer_route(q, keys):
    t = q.shape[0]
    tb = LANES
    m1, m2 = _pair_matrices()
    slots = PEER_HEADS * PEER_TOPK
    out = jax.ShapeDtypeStruct((slots, t), F32)
    nh = ROUTE_HEADS_PER_STEP
    pair_spec = pl.BlockSpec((PAIR_ROWS, PEER_TOPK), lambda i, h: (0, 0))
    out_spec = pl.BlockSpec((nh * PEER_TOPK, tb), lambda i, h: (h, i))
    return pl.pallas_call(
        _route_body,
        out_shape=[out, out, out],
        grid=(t // tb, PEER_HEADS // nh),
        in_specs=[pl.BlockSpec((tb, nh * 2 * PEER_HALF), lambda i, h: (i, h)),
                  pl.BlockSpec((nh * 2, N_KEYS, PEER_HALF), lambda i, h: (h, 0, 0)),
                  pair_spec, pair_spec],
        out_specs=[out_spec, out_spec, out_spec],
        name="peer_route",
        compiler_params=_cparams(("parallel", "parallel")),
    )(q, keys, jnp.asarray(m1, BF16), jnp.asarray(m2, BF16))


def _swap_token_and_row(xs):
    xs = list(xs)
    row = lax.broadcasted_iota(I32, xs[0].shape, 1)
    d = SUBLANES // 2
    while d >= 1:
        low = (row % (2 * d)) < d
        for s in range(SUBLANES):
            if (s // d) % 2 == 0:
                a, b = xs[s], xs[s + d]
                xs[s] = jnp.where(low, a, pltpu.roll(b, d, axis=1))
                xs[s + d] = jnp.where(low, pltpu.roll(a, SUBLANES - d, axis=1), b)
        d //= 2
    return xs


def _gate_body(e1t_ref, e2t_ref, gwt_ref, o_ref, e1_s, e2_s, gw_s):
    tb = o_ref.shape[2]
    e1_s[...] = e1t_ref[...].T
    e2_s[...] = e2t_ref[...].T
    gw_s[...] = gwt_ref[...].T
    sub = lax.broadcasted_iota(I32, (N_KEYS, LANES), 0).astype(F32)
    nblk = N_KEYS // SUBLANES

    def eight_tokens(g, carry):
        t0 = pl.multiple_of(g * SUBLANES, SUBLANES)
        lhs, rhs = [], []
        for s in range(SUBLANES):
            e1 = e1_s[pl.ds(t0 + s, 1), :]
            e2 = e2_s[pl.ds(t0 + s, 1), :]
            gw = gw_s[pl.ds(t0 + s, 1), :]
            lhs.append(jnp.where(e1 == sub, gw, 0.0).astype(BF16))
            rhs.append(jnp.where(e2 == sub, 1.0, 0.0).astype(BF16))
        per_token = [_dot_nt(a, b).reshape(nblk, SUBLANES, N_KEYS) for a, b in zip(lhs, rhs)]
        for r, y in enumerate(_swap_token_and_row(per_token)):
            o_ref[:, r, pl.ds(t0, SUBLANES), :] = y
        return carry

    lax.fori_loop(0, tb // SUBLANES, eight_tokens, 0, unroll=4)


def _peer_gates(e1t, e2t, gwt):
    t = e1t.shape[1]
    tb = LANES
    nblk = N_KEYS // SUBLANES
    in_spec = pl.BlockSpec((LANES, tb), lambda i: (0, i))
    out = pl.pallas_call(
        _gate_body,
        out_shape=jax.ShapeDtypeStruct((nblk, SUBLANES, t, N_KEYS), F32),
        grid=(t // tb,),
        in_specs=[in_spec, in_spec, in_spec],
        out_specs=pl.BlockSpec((nblk, SUBLANES, tb, N_KEYS), lambda i: (0, 0, i, 0)),
        scratch_shapes=[pltpu.VMEM((tb, LANES), F32)] * 3,
        name="peer_gates",
        compiler_params=_cparams(("parallel",)),
    )(e1t, e2t, gwt)
    return out.reshape(N_KEYS, t, N_KEYS)


def _peer_up_body(h_ref, u_ref, g_ref, o_ref):
    s = _dot_nt(h_ref[...], u_ref[...])
    act = 0.5 * s * (1.0 + lax.erf(s * (2.0 ** -0.5)))
    for j in range(o_ref.shape[1] // N_KEYS):
        cols = slice(j * N_KEYS, (j + 1) * N_KEYS)
        o_ref[:, cols] = (act[:, cols] * g_ref[j]).astype(o_ref.dtype)


def _peer_up(h, u_all, layer, g3):
    t, d = h.shape
    n_exp = u_all.shape[1]
    tm = _pick(t, (1024, 512, 256, 128))
    tn = SUBLANES * N_KEYS
    return pl.pallas_call(
        _peer_up_body,
        out_shape=jax.ShapeDtypeStruct((t, n_exp), BF16),
        grid=(t // tm, n_exp // tn),
        in_specs=[pl.BlockSpec((tm, d), lambda i, j: (i, 0), pipeline_mode=pl.Buffered(1)),
                  pl.BlockSpec((None, tn, d), lambda i, j: (layer, j, 0)),
                  pl.BlockSpec((SUBLANES, tm, N_KEYS), lambda i, j: (j, i, 0))],
        out_specs=pl.BlockSpec((tm, tn), lambda i, j: (i, j)),
        name="peer_up",
        compiler_params=_cparams(("parallel", "parallel")),
    )(h, u_all, g3)


def _residual(acc, x, gate):
    return x + gate * acc


def _gated_residual_matmul(a, w_all, layer, x2, gate, s, *, tk, name):
    t, d = x2.shape
    bsz = gate.shape[0]
    tm = _pick(s, (1024, 512, 256, 128))
    tn = 1024
    (out,) = _matmul(
        a, [w_all], [0], d, tm=tm, tn=tn, tk=tk, out_dtypes=[F32], epilogue=_residual,
        extras=[(x2, (tm, tn), lambda i, j: (i, j)),
                (gate.reshape(bsz, 1, d), (None, 1, tn), lambda i, j: (i * tm // s, 0, j))],
        b_lead=layer, name=name)
    return out


def _peer_ffn(x, w_norm, shift, scale, gate, layer, w_q_all, keys, u_all, v_all):
    bsz, s, d = x.shape
    t = bsz * s
    h = _norm_mod(x, w_norm, shift, scale).reshape(t, d)
    tm = _pick(s, (1024, 512, 256, 128))
    (q,) = _matmul(h, [w_q_all], [0], w_q_all.shape[2], tm=tm, tn=1024, tk=d,
                   out_dtypes=[F32], b_lead=layer, name="peer_query")
    keys_b = keys.reshape(PEER_HEADS * 2, N_KEYS, PEER_HALF).astype(BF16)
    e1t, e2t, gwt = _peer_route(q, keys_b)
    g3 = _peer_gates(e1t, e2t, gwt)
    pmat = _peer_up(h, u_all, layer, g3)
    out = _gated_residual_matmul(pmat, v_all, layer, x.reshape(t, d), gate, s,
                                 tk=2048, name="peer_down")
    return out.reshape(bsz, s, d)


def _pad_in_proj(w):
    o_kv = Q_LORA
    o_pe = Q_LORA + KV_LORA
    o_rw = o_pe + QK_ROPE
    half = QK_ROPE // 2
    rows = w.shape[0]
    z = lambda n: jnp.zeros((rows, n), w.dtype)
    cq = w[:, :Q_LORA]
    ckv = w[:, o_kv:o_pe]
    kpe = w[:, o_pe:o_rw]
    kpe_swap = jnp.concatenate([kpe[:, half:], kpe[:, :half]], axis=1)
    r = w[:, o_rw:o_rw + RWKV_WIDTH]
    k = w[:, o_rw + RWKV_WIDTH:o_rw + 2 * RWKV_WIDTH]
    v = w[:, o_rw + 2 * RWKV_WIDTH:o_rw + 3 * RWKV_WIDTH]
    lora = w[:, o_rw + 3 * RWKV_WIDTH:]
    pad = LANES - QK_ROPE
    out = jnp.concatenate([ckv, kpe, z(pad), kpe_swap, z(pad), cq, lora,
                           z(P_R - P_LORA - lora.shape[1]), r, k, v], axis=1)
    assert out.shape[1] == P_WIDTH
    return out


def _hybrid_mixer(h, positions, w_in, q_norm_w, w_uq, kv_norm_w, w_ukv, qk_q_w, qk_k_w,
                  mu, w0, w2, a0, a2, g2, k_k, k_a, r_k, lnx_w, lnx_b):
    bsz, s, d = h.shape
    t = bsz * s
    tm = _pick(s, (1024, 512, 256, 128))
    w_pad = _pad_in_proj(w_in).astype(BF16)
    (p,) = _matmul(h.reshape(t, d), [w_pad], [0], P_WIDTH, tm=tm, tn=1024, tk=d,
                   out_dtypes=[F32], name="hyb_in_proj")
    q, k, v = _mla_qkv(p, positions, q_norm_w, w_uq, kv_norm_w, w_ukv, qk_q_w, qk_k_w)
    y_mla = _attention(q.reshape(bsz, s, -1), k.reshape(bsz, s, -1), v.reshape(bsz, s, -1), bsz, s)

    mu_full = jnp.concatenate([jnp.zeros((Q_LORA + KV_LORA + QK_ROPE,), F32), mu]).reshape(1, -1)
    mu_pad = _pad_in_proj(mu_full)
    zrow = jnp.zeros((RWKV_WIDTH,), F32)
    vecs = jnp.stack([w0, a0, k_k, k_a, r_k.reshape(-1), lnx_w, lnx_b, zrow])
    w2a2 = jnp.concatenate([w2, a2], axis=0).astype(BF16)
    y_rwkv = _rwkv(p.reshape(bsz, s, P_WIDTH), mu_pad, vecs, w2a2, g2.astype(BF16))
    return jnp.concatenate([y_mla, y_rwkv], axis=-1).reshape(t, -1)


def kernel(x, c, positions, ada_w, ada_b, norm_mix_w, norm_ffn_w, hyb_w_in, mla_q_norm_w, mla_w_uq, mla_kv_norm_w, mla_w_ukv, mla_qk_q_w, mla_qk_k_w, rwkv_mu, rwkv_w0, rwkv_w2, rwkv_a0, rwkv_a2, rwkv_g2, rwkv_k_k, rwkv_k_a, rwkv_r_k, rwkv_lnx_w, rwkv_lnx_b, hyb_w_out, conv_w_in, conv_w, conv_w_out, peer_w_q, peer_keys, peer_u, peer_v):
    bsz, s, d = x.shape
    t = bsz * s
    depth = ada_w.shape[0]
    mod = _ada_mod(c, ada_w, ada_b)
    w_q_all = peer_w_q.astype(BF16)
    u_all = peer_u.astype(BF16)
    v_all = peer_v
    for layer in range(depth):
        sh_m, sc_m, gt_m, sh_f, sc_f, gt_f = [mod[layer, :, j * d:(j + 1) * d] for j in range(6)]
        h = _norm_mod(x, norm_mix_w[layer], sh_m, sc_m)
        i = layer // 2
        if layer % 2 == 0:
            y = _hybrid_mixer(h, positions, hyb_w_in[i], mla_q_norm_w[i], mla_w_uq[i],
                              mla_kv_norm_w[i], mla_w_ukv[i], mla_qk_q_w[i], mla_qk_k_w[i],
                              rwkv_mu[i], rwkv_w0[i], rwkv_w2[i], rwkv_a0[i], rwkv_a2[i],
                              rwkv_g2[i], rwkv_k_k[i], rwkv_k_a[i], rwkv_r_k[i],
                              rwkv_lnx_w[i], rwkv_lnx_b[i])
            w_out = hyb_w_out.astype(BF16)
        else:
            y = _conv_mixer(h.reshape(t, d), conv_w_in[i], conv_w[i], s)
            w_out = conv_w_out.astype(BF16)
        x = _gated_residual_matmul(y, w_out, i, x.reshape(t, d), gt_m, s, tk=d,
                                   name="mixer_out_proj").reshape(bsz, s, d)
        x = _peer_ffn(x, norm_ffn_w[layer], sh_f, sc_f, gt_f, layer, w_q_all, peer_keys[layer],
                      u_all, v_all)
    return x
```

```python
import functools

import numpy as np
import jax
import jax.numpy as jnp
from jax import lax
from jax.experimental import pallas as pl
from jax.experimental.pallas import tpu as pltpu

F32 = jnp.float32
BF16 = jnp.bfloat16
I32 = jnp.int32

MLA_HEADS = 16
QK_NOPE = 128
QK_ROPE = 64
QK_HEAD = QK_NOPE + QK_ROPE
V_HEAD = 128
Q_LORA = 768
KV_LORA = 512
ROPE_THETA = 10000.0
RWKV_HEAD = 64
RWKV_HEADS = 32
RWKV_WIDTH = RWKV_HEADS * RWKV_HEAD
W_LORA = 64
A_LORA = 64
G_LORA = 128
LNX_EPS = 64e-5
CONV_WIDTH = 3
PEER_HEADS = 8
N_KEYS = 128
PEER_TOPK = 16
PEER_HALF = 128
NORM_EPS = 1e-6

LANES = 128
SUBLANES = 8
VMEM_LIMIT = 56 * 1024 * 1024

P_CKV = 0
P_KPE = 512
P_CQ = 768
P_LORA = 1536
P_R = 2048
P_K = P_R + RWKV_WIDTH
P_V = P_K + RWKV_WIDTH
P_WIDTH = P_V + RWKV_WIDTH
RWKV_CHUNK = 64
RWKV_PAIRS_PER_STEP = 8
ATTN_HEADS_PER_STEP = 4
ROUTE_HEADS_PER_STEP = 4


def _cparams(sem):
    return pltpu.CompilerParams(dimension_semantics=sem, vmem_limit_bytes=VMEM_LIMIT)


def _dot(a, b):
    return lax.dot_general(a, b, (((1,), (0,)), ((), ())), preferred_element_type=F32)


def _dot_nt(a, b):
    return lax.dot_general(a, b, (((1,), (1,)), ((), ())), preferred_element_type=F32)


def _dot_tn(a, b):
    return lax.dot_general(a, b, (((0,), (0,)), ((), ())), preferred_element_type=F32)


def _mm_body(*refs, nb, nk, n_extra, trans_b, epilogue):
    a_ref = refs[0]
    b_refs = refs[1:1 + nb]
    extra = refs[1 + nb:1 + nb + n_extra]
    o_refs = refs[1 + nb + n_extra:]
    a = a_ref[...].astype(BF16)
    parts = []
    for b_ref in b_refs:
        b = b_ref[...].astype(BF16)
        parts.append(_dot_nt(a, b) if trans_b else _dot(a, b))

    def finish(accs):
        outs = epilogue(*accs, *[e[...] for e in extra])
        if not isinstance(outs, (tuple, list)):
            outs = (outs,)
        for o_ref, o in zip(o_refs, outs):
            o_ref[...] = o.astype(o_ref.dtype)

    if nk == 1:
        finish(parts)
        return
    n_out = len(o_refs) - nb
    acc_refs = o_refs[n_out:]
    o_refs = o_refs[:n_out]
    k = pl.program_id(2)

    @pl.when(k == 0)
    def _():
        for acc, p in zip(acc_refs, parts):
            acc[...] = p

    @pl.when(k > 0)
    def _():
        for acc, p in zip(acc_refs, parts):
            acc[...] += p

    @pl.when(k == nk - 1)
    def _():
        finish([acc[...] for acc in acc_refs])


def _matmul(a, bs, b_col_blocks, n_out_cols, *, tm, tn, tk, out_dtypes, epilogue=None,
            extras=(), trans_b=False, b_lead=None, name="matmul"):
    m, kdim = a.shape
    nb = len(bs)
    nk = kdim // tk
    assert m % tm == 0 and kdim % tk == 0 and n_out_cols % tn == 0
    if epilogue is None:
        epilogue = lambda acc: acc
    grid = (m // tm, n_out_cols // tn, nk)
    in_specs = [pl.BlockSpec((tm, tk), lambda i, j, k: (i, k))]
    for off in b_col_blocks:
        if b_lead is not None:
            assert not trans_b
            in_specs.append(pl.BlockSpec((None, tk, tn),
                                         lambda i, j, k, off=off: (b_lead, k, j + off)))
        elif trans_b:
            in_specs.append(pl.BlockSpec((tn, tk), lambda i, j, k, off=off: (j + off, k)))
        else:
            in_specs.append(pl.BlockSpec((tk, tn), lambda i, j, k, off=off: (k, j + off)))
    extra_arrays = []
    for arr, bshape, imap in extras:
        in_specs.append(pl.BlockSpec(bshape, lambda i, j, k, imap=imap: imap(i, j)))
        extra_arrays.append(arr)
    out_shape = [jax.ShapeDtypeStruct((m, n_out_cols), dt) for dt in out_dtypes]
    out_specs = [pl.BlockSpec((tm, tn), lambda i, j, k: (i, j)) for _ in out_dtypes]
    scratch = [pltpu.VMEM((tm, tn), F32) for _ in range(nb)] if nk > 1 else []
    body = functools.partial(_mm_body, nb=nb, nk=nk, n_extra=len(extras), trans_b=trans_b,
                             epilogue=epilogue)
    outs = pl.pallas_call(
        body, out_shape=out_shape, grid=grid, in_specs=in_specs, out_specs=out_specs,
        scratch_shapes=scratch, name=name,
        compiler_params=_cparams(("parallel", "parallel", "arbitrary")),
    )(a, *bs, *extra_arrays)
    return outs


def _pick(n, prefs):
    for p in prefs:
        if n % p == 0:
            return p
    raise ValueError(f"no tile for {n}")


def _ada_body(c_ref, w_ref, b_ref, o_ref):
    c = c_ref[...]
    cond = (c * (1.0 / (1.0 + jnp.exp(-c)))).astype(BF16)
    o_ref[...] = _dot(cond, w_ref[...].astype(BF16)) + b_ref[...]


def _ada_mod(c, ada_w, ada_b):
    depth, d, n = ada_w.shape
    bsz = c.shape[0]
    tn = 512
    return pl.pallas_call(
        _ada_body,
        out_shape=jax.ShapeDtypeStruct((depth, bsz, n), F32),
        grid=(depth, n // tn),
        in_specs=[
            pl.BlockSpec((bsz, d), lambda l, j: (0, 0)),
            pl.BlockSpec((None, d, tn), lambda l, j: (l, 0, j)),
            pl.BlockSpec((None, 1, tn), lambda l, j: (l, 0, j)),
        ],
        out_specs=pl.BlockSpec((None, bsz, tn), lambda l, j: (l, 0, j)),
        name="ada_mod",
        compiler_params=_cparams(("parallel", "parallel")),
    )(c, ada_w, ada_b.reshape(depth, 1, n))


def _norm_mod_body(x_ref, w_ref, sh_ref, sc_ref, o_ref):
    x = x_ref[...]
    ms = jnp.mean(x * x, axis=-1, keepdims=True)
    y = x * lax.rsqrt(ms + NORM_EPS) * w_ref[...]
    o_ref[...] = (y * (1.0 + sc_ref[...]) + sh_ref[...]).astype(o_ref.dtype)


def _norm_mod(x, w, shift, scale):
    bsz, s, d = x.shape
    ts = _pick(s, (256, 128))
    vec = lambda v: v.reshape(bsz, 1, d)
    return pl.pallas_call(
        _norm_mod_body,
        out_shape=jax.ShapeDtypeStruct((bsz, s, d), BF16),
        grid=(bsz, s // ts),
        in_specs=[
            pl.BlockSpec((None, ts, d), lambda b, i: (b, i, 0)),
            pl.BlockSpec((1, d), lambda b, i: (0, 0)),
            pl.BlockSpec((None, 1, d), lambda b, i: (b, 0, 0)),
            pl.BlockSpec((None, 1, d), lambda b, i: (b, 0, 0)),
        ],
        out_specs=pl.BlockSpec((None, ts, d), lambda b, i: (b, i, 0)),
        name="norm_mod",
        compiler_params=_cparams(("parallel", "parallel")),
    )(x, w.reshape(1, d), vec(shift), vec(scale))


def _rope_tables(pos_ref, freq_ref):
    ang = pos_ref[...].astype(F32) * freq_ref[...]
    lane = lax.broadcasted_iota(I32, ang.shape, 1)
    cos = jnp.where(lane < QK_ROPE, jnp.cos(ang), 0.0)
    sin = jnp.sin(ang)
    sin_signed = jnp.where(lane < QK_ROPE // 2, -sin, jnp.where(lane < QK_ROPE, sin, 0.0))
    return cos, sin_signed


def _mla_q_body(cq_ref, nw_ref, wuq_ref, hw_ref, pos_ref, freq_ref, o_ref, *, scale):
    cq = cq_ref[...]
    ms = jnp.mean(cq * cq, axis=-1, keepdims=True)
    cqn = (cq * lax.rsqrt(ms + NORM_EPS) * nw_ref[...]).astype(BF16)
    cos, sin_signed = _rope_tables(pos_ref, freq_ref)
    w_nope = hw_ref[0:1, :]
    w_rope = hw_ref[1:2, :]
    w_swap = hw_ref[2:3, :]
    for h in range(MLA_HEADS):
        qh = _dot(cqn, wuq_ref[:, h * 3 * LANES:(h + 1) * 3 * LANES])
        nope = qh[:, :LANES]
        rope = qh[:, LANES:2 * LANES]
        swap = qh[:, 2 * LANES:]
        ss = jnp.sum(nope * nope + rope * rope, axis=-1, keepdims=True)
        fac = lax.rsqrt(ss * (1.0 / QK_HEAD) + NORM_EPS) * scale
        o_ref[:, h * 2 * LANES:h * 2 * LANES + LANES] = (nope * fac * w_nope).astype(o_ref.dtype)
        roped = (rope * w_rope * cos + swap * w_swap * sin_signed) * fac
        o_ref[:, h * 2 * LANES + LANES:(h + 1) * 2 * LANES] = roped.astype(o_ref.dtype)


def _mla_kv_body(ckv_ref, kpe_ref, nw_ref, wukv_ref, hw_ref, pos_ref, freq_ref, k_ref, v_ref):
    ckv = ckv_ref[...]
    ms = jnp.mean(ckv * ckv, axis=-1, keepdims=True)
    ckvn = (ckv * lax.rsqrt(ms + NORM_EPS) * nw_ref[...]).astype(BF16)
    cos, sin_signed = _rope_tables(pos_ref, freq_ref)
    kpe = kpe_ref[:, :LANES]
    kpe_swap = kpe_ref[:, LANES:]
    pe_ss = jnp.sum(kpe * kpe, axis=-1, keepdims=True)
    pe_roped = kpe * hw_ref[1:2, :] * cos + kpe_swap * hw_ref[2:3, :] * sin_signed
    w_nope = hw_ref[0:1, :]
    for h in range(MLA_HEADS):
        kn = _dot(ckvn, wukv_ref[:, h * LANES:(h + 1) * LANES])
        ss = jnp.sum(kn * kn, axis=-1, keepdims=True) + pe_ss
        fac = lax.rsqrt(ss * (1.0 / QK_HEAD) + NORM_EPS)
        k_ref[:, h * 2 * LANES:h * 2 * LANES + LANES] = (kn * fac * w_nope).astype(k_ref.dtype)
        k_ref[:, h * 2 * LANES + LANES:(h + 1) * 2 * LANES] = (pe_roped * fac).astype(k_ref.dtype)
    nv = MLA_HEADS * V_HEAD
    v_ref[...] = _dot(ckvn, wukv_ref[:, nv:]).astype(v_ref.dtype)


def _head_norm_rows(w):
    half = QK_ROPE // 2
    zeros = jnp.zeros((LANES - QK_ROPE,), F32)
    rope = jnp.concatenate([w[QK_NOPE:], zeros])
    swap = jnp.concatenate([w[QK_NOPE + half:], w[QK_NOPE:QK_NOPE + half], zeros])
    rows = jnp.stack([w[:QK_NOPE], rope, swap])
    return jnp.concatenate([rows, jnp.zeros((SUBLANES - 3, LANES), F32)], axis=0)


def _rope_freq_row():
    half = QK_ROPE // 2
    f = ROPE_THETA ** (-jnp.arange(half, dtype=F32) / half)
    return jnp.concatenate([f, f, jnp.zeros((LANES - QK_ROPE,), F32)]).reshape(1, LANES)


def _wuq_layout():
    half = QK_ROPE // 2
    idx, mask = [], []
    for h in range(MLA_HEADS):
        base = h * QK_HEAD
        nope = list(range(base, base + QK_NOPE))
        rope = list(range(base + QK_NOPE, base + QK_HEAD))
        swap = rope[half:] + rope[:half]
        pad = [0] * (LANES - QK_ROPE)
        idx += nope + rope + pad + swap + pad
        mask += [1] * QK_NOPE + [1] * QK_ROPE + [0] * len(pad) + [1] * QK_ROPE + [0] * len(pad)
    return np.asarray(idx, np.int32), np.asarray(mask, np.float32)


def _wukv_layout():
    per = QK_NOPE + V_HEAD
    k_idx = [h * per + i for h in range(MLA_HEADS) for i in range(QK_NOPE)]
    v_idx = [h * per + QK_NOPE + i for h in range(MLA_HEADS) for i in range(V_HEAD)]
    return np.asarray(k_idx + v_idx, np.int32)


def _mla_qkv(p, positions, q_norm_w, w_uq, kv_norm_w, w_ukv, qk_q_w, qk_k_w):
    t = p.shape[0]
    tm = _pick(t, (256, 128))
    idx, mask = _wuq_layout()
    wuq = (w_uq[:, idx] * mask[None, :]).astype(BF16)
    wukv = w_ukv[:, _wukv_layout()].astype(BF16)
    pos = positions.reshape(t, 1)
    freq = _rope_freq_row()
    hq = _head_norm_rows(qk_q_w)
    hk = _head_norm_rows(qk_k_w)
    qw = MLA_HEADS * 2 * LANES
    row = lambda i: (i, 0)
    fixed = lambda i: (0, 0)
    q = pl.pallas_call(
        functools.partial(_mla_q_body, scale=QK_HEAD ** -0.5),
        out_shape=jax.ShapeDtypeStruct((t, qw), BF16),
        grid=(t // tm,),
        in_specs=[
            pl.BlockSpec((tm, Q_LORA), lambda i: (i, P_CQ // Q_LORA)),
            pl.BlockSpec((1, Q_LORA), fixed),
            pl.BlockSpec(wuq.shape, fixed),
            pl.BlockSpec((SUBLANES, LANES), fixed),
            pl.BlockSpec((tm, 1), row),
            pl.BlockSpec((1, LANES), fixed),
        ],
        out_specs=pl.BlockSpec((tm, qw), row),
        name="mla_q",
        compiler_params=_cparams(("parallel",)),
    )(p, q_norm_w.reshape(1, Q_LORA), wuq, hq, pos, freq)
    k, v = pl.pallas_call(
        _mla_kv_body,
        out_shape=[jax.ShapeDtypeStruct((t, qw), BF16),
                   jax.ShapeDtypeStruct((t, MLA_HEADS * V_HEAD), BF16)],
        grid=(t // tm,),
        in_specs=[
            pl.BlockSpec((tm, KV_LORA), lambda i: (i, P_CKV // KV_LORA)),
            pl.BlockSpec((tm, 2 * LANES), lambda i: (i, P_KPE // (2 * LANES))),
            pl.BlockSpec((1, KV_LORA), fixed),
            pl.BlockSpec(wukv.shape, fixed),
            pl.BlockSpec((SUBLANES, LANES), fixed),
            pl.BlockSpec((tm, 1), row),
            pl.BlockSpec((1, LANES), fixed),
        ],
        out_specs=[pl.BlockSpec((tm, qw), row), pl.BlockSpec((tm, MLA_HEADS * V_HEAD), row)],
        name="mla_kv",
        compiler_params=_cparams(("parallel",)),
    )(p, p, kv_norm_w.reshape(1, KV_LORA), wukv, hk, pos, freq)
    return q, k, v


def _attn_body(qi_ref, ki_ref, q_ref, k_ref, v_ref, o_ref, m_ref, l_ref, acc_ref, *, tq):
    step_id = pl.program_id(2)
    qi = qi_ref[step_id]
    ki = ki_ref[step_id]

    @pl.when(ki == 0)
    def _():
        m_ref[...] = jnp.full(m_ref.shape, -jnp.inf, F32)
        l_ref[...] = jnp.zeros(l_ref.shape, F32)
        acc_ref[...] = jnp.zeros(acc_ref.shape, F32)

    heads = range(ATTN_HEADS_PER_STEP)
    qk = lambda h: slice(h * 2 * LANES, (h + 1) * 2 * LANES)
    vo = lambda h: slice(h * V_HEAD, (h + 1) * V_HEAD)

    def step(masked):
        ss = [_dot_nt(q_ref[:, qk(h)], k_ref[:, qk(h)]) for h in heads]
        if masked:
            row = lax.broadcasted_iota(I32, ss[0].shape, 0)
            col = lax.broadcasted_iota(I32, ss[0].shape, 1)
            ss = [jnp.where(col <= row, s, -jnp.inf) for s in ss]
        reps = tq // LANES
        prs, alphas = [], []
        for h, s in zip(heads, ss):
            m_prev = m_ref[h]
            m_new = jnp.maximum(m_prev, jnp.max(s, axis=-1, keepdims=True))
            alpha = jnp.exp(m_prev - m_new)
            pr = jnp.exp(s - jnp.tile(m_new, (1, reps)))
            l_ref[h] = alpha * l_ref[h] + jnp.sum(pr, axis=-1, keepdims=True)
            m_ref[h] = m_new
            prs.append(pr.astype(BF16))
            alphas.append(alpha)
        pvs = [_dot(pr, v_ref[:, vo(h)]) for h, pr in zip(heads, prs)]
        for h, alpha, pv in zip(heads, alphas, pvs):
            acc_ref[h] = alpha * acc_ref[h] + pv

    @pl.when(ki < qi)
    def _():
        step(False)

    @pl.when(ki == qi)
    def _():
        step(True)
        for h in heads:
            o_ref[:, vo(h)] = (acc_ref[h] / l_ref[h]).astype(o_ref.dtype)


def _attention(q, k, v, bsz, s):
    tq = _pick(s, (512, 256, 128))
    n = s // tq
    pairs = [(i, j) for i in range(n) for j in range(i + 1)]
    qi_tab = jnp.asarray([i for i, _ in pairs], I32)
    ki_tab = jnp.asarray([j for _, j in pairs], I32)
    nh = ATTN_HEADS_PER_STEP
    grid_spec = pltpu.PrefetchScalarGridSpec(
        num_scalar_prefetch=2,
        grid=(bsz, MLA_HEADS // nh, len(pairs)),
        in_specs=[
            pl.BlockSpec((None, tq, nh * 2 * LANES), lambda b, h, p, qi, ki: (b, qi[p], h)),
            pl.BlockSpec((None, tq, nh * 2 * LANES), lambda b, h, p, qi, ki: (b, ki[p], h)),
            pl.BlockSpec((None, tq, nh * V_HEAD), lambda b, h, p, qi, ki: (b, ki[p], h)),
        ],
        out_specs=pl.BlockSpec((None, tq, nh * V_HEAD), lambda b, h, p, qi, ki: (b, qi[p], h)),
        scratch_shapes=[pltpu.VMEM((nh, tq, LANES), F32), pltpu.VMEM((nh, tq, LANES), F32),
                        pltpu.VMEM((nh, tq, V_HEAD), F32)],
    )
    return pl.pallas_call(
        functools.partial(_attn_body, tq=tq),
        out_shape=jax.ShapeDtypeStruct((bsz, s, MLA_HEADS * V_HEAD), BF16),
        grid_spec=grid_spec,
        name="mla_attention",
        compiler_params=_cparams(("parallel", "parallel", "arbitrary")),
    )(qi_tab, ki_tab, q, k, v)


def _rwkv_body(r_ref, k_ref, v_ref, xl_ref, mu_r_ref, mu_k_ref, mu_v_ref, mu_l_ref, vec_ref,
               w2a2_ref, g2_ref, o_ref, state_ref, pr_ref, pk_ref, pv_ref, pl_ref, *, tb, npairs):
    t_idx = pl.program_id(2)

    @pl.when(t_idx == 0)
    def _():
        state_ref[...] = jnp.zeros(state_ref.shape, F32)
        pr_ref[...] = jnp.zeros(pr_ref.shape, F32)
        pk_ref[...] = jnp.zeros(pk_ref.shape, F32)
        pv_ref[...] = jnp.zeros(pv_ref.shape, F32)
        pl_ref[...] = jnp.zeros(pl_ref.shape, F32)

    def mix(x_ref, prev_ref, mu_ref):
        x = x_ref[...]
        row = lax.broadcasted_iota(I32, x.shape, 0)
        shifted = jnp.where(row == 0, prev_ref[...], pltpu.roll(x, 1, axis=0))
        prev_ref[...] = x[tb - 1:tb, :]
        return x + (shifted - x) * mu_ref[...]

    r = mix(r_ref, pr_ref, mu_r_ref)
    k = mix(k_ref, pk_ref, mu_k_ref)
    v = mix(v_ref, pv_ref, mu_v_ref)
    xl = mix(xl_ref, pl_ref, mu_l_ref)

    lane = lax.broadcasted_iota(I32, (tb, LANES), 1)
    first = lane < RWKV_HEAD
    x0 = xl[:, :LANES]
    lhs_w = jnp.where(first, jnp.tanh(x0), 0.0).astype(BF16)
    lhs_a = jnp.where(first, 0.0, x0).astype(BF16)
    xg = xl[:, LANES:]
    lhs_g = (1.0 / (1.0 + jnp.exp(-xg))).astype(BF16)
    cols = [slice(p * LANES, (p + 1) * LANES) for p in range(npairs)]
    outs = _rwkv_pairs([r[:, c] for c in cols], [k[:, c] for c in cols], [v[:, c] for c in cols],
                       lhs_w, lhs_a, lhs_g, [vec_ref[:, c] for c in cols],
                       [w2a2_ref[:, c] for c in cols], [g2_ref[:, c] for c in cols],
                       state_ref, tb)
    for c, out in zip(cols, outs):
        o_ref[:, c] = out.astype(o_ref.dtype)


def _rwkv_pairs(rs, ks, vs_in, lhs_w, lhs_a, lhs_g, vecs, w2a2s, g2s, state_ref, tb):
    c = RWKV_CHUNK
    pairs = range(len(rs))
    chunks = range(tb // c)
    each = lambda f, *lists: [f(*args) for args in zip(*lists)]

    ri = lax.broadcasted_iota(I32, (LANES, LANES), 0)
    ci = lax.broadcasted_iota(I32, (LANES, LANES), 1)
    same = (ri // RWKV_HEAD) == (ci // RWKV_HEAD)
    ones_bd = jnp.where(same, 1.0, 0.0).astype(BF16)
    tri_r = lax.broadcasted_iota(I32, (c, c), 0)
    tri_c = lax.broadcasted_iota(I32, (c, c), 1)
    tri = jnp.where(tri_c <= tri_r, 1.0, 0.0).astype(BF16)
    eye = jnp.where(ri == ci, 1.0, 0.0)
    first_c = lax.broadcasted_iota(I32, (c, LANES), 1) < RWKV_HEAD

    def stack(z):
        return jnp.concatenate([jnp.where(first_c, z, 0.0), jnp.where(first_c, 0.0, z)],
                               axis=0).astype(BF16)

    def split(x):
        hi = x.astype(BF16)
        return hi, (x - hi.astype(F32)).astype(BF16)

    def group_sum(xs):
        parts = each(split, xs)
        his = [_dot(hi, ones_bd) for hi, _ in parts]
        los = [_dot(lo, ones_bd) for _, lo in parts]
        return each(lambda a, b: a + b, his, los)

    lws = [_dot(lhs_w, w) for w in w2a2s]
    las = [_dot(lhs_a, w) for w in w2a2s]
    gs = [_dot(lhs_g, g2) for g2 in g2s]
    kks = each(lambda k, vec: k * vec[2:3, :], ks, vecs)
    nrms = group_sum(each(lambda kk: kk * kk, kks))
    kks = each(lambda kk, n: kk / jnp.maximum(jnp.sqrt(n), 1e-12), kks, nrms)

    def decay_and_gate(lw, la, vec):
        zw = -(vec[0:1, :] + lw)
        softplus = jnp.maximum(zw, 0.0) + jnp.log(1.0 + jnp.exp(-jnp.abs(zw)))
        a = 1.0 / (1.0 + jnp.exp(-(vec[1:2, :] + la)))
        return -jnp.exp(-softplus - 0.5), a

    dg = each(decay_and_gate, lws, las, vecs)
    logds = [d for d, _ in dg]
    a_gates = [a for _, a in dg]
    kps = each(lambda k, a, vec: k * (1.0 + (a - 1.0) * vec[3:4, :]), ks, a_gates, vecs)
    bvecs = each(lambda kk, a: kk * a, kks, a_gates)
    bonus_sums = group_sum(each(lambda r, kp, vec: r * kp * vec[4:5, :], rs, kps, vecs))

    inst = [(p, ch) for p in pairs for ch in chunks]
    rows = lambda x, ch: x[ch * c:(ch + 1) * c]
    lds = [rows(logds[p], ch) for p, ch in inst]
    ld_parts = each(split, lds)
    lc_hi = [_dot(tri, hi) for hi, _ in ld_parts]
    lc_lo = [_dot(tri, lo) for _, lo in ld_parts]
    lcs = each(lambda a, b: a + b, lc_hi, lc_lo)
    ltots = [lc[c - 1:c, :] for lc in lcs]
    e_negs = [jnp.exp(-lc) for lc in lcs]
    e_rems = each(lambda lt, lc: jnp.exp(lt - lc), ltots, lcs)
    xas = [stack(-rows(kks[p], ch) * jnp.exp(lc - ld)) for (p, ch), lc, ld in zip(inst, lcs, lds)]
    xrs = [stack(rows(rs[p], ch) * jnp.exp(lc)) for (p, ch), lc in zip(inst, lcs)]
    ybs = [stack(rows(bvecs[p], ch) * e) for (p, ch), e in zip(inst, e_negs)]
    yks = [stack(rows(kps[p], ch) * e) for (p, ch), e in zip(inst, e_negs)]
    vss = [stack(rows(vs_in[p], ch)) for p, ch in inst]
    bhs = [stack(rows(bvecs[p], ch) * e) for (p, ch), e in zip(inst, e_rems)]
    khs = [stack(rows(kps[p], ch) * e) for (p, ch), e in zip(inst, e_rems)]

    n2 = 2 * c
    xars = each(lambda a, b: jnp.concatenate([a, b], axis=0), xas, xrs)
    rr = lax.broadcasted_iota(I32, (2 * n2, n2), 0)
    cc = lax.broadcasted_iota(I32, (2 * n2, n2), 1)
    rr_in = rr % n2
    tri_mask = ((rr_in // c) == (cc // c)) & ((cc % c) < (rr_in % c) + rr // n2)
    mbs = each(lambda a, b: jnp.where(tri_mask, _dot_nt(a, b), 0.0), xars, ybs)
    mks = each(lambda a, b: jnp.where(tri_mask, _dot_nt(a, b), 0.0).astype(BF16), xars, yks)
    labs = [mb[:n2] for mb in mbs]
    mrbs = [mb[n2:].astype(BF16) for mb in mbs]

    pbs = [lab.astype(BF16) for lab in labs]
    tinvs = [eye + lab for lab in labs]
    pbs = [_dot(pb, pb).astype(BF16) for pb in pbs]
    n_sq = 2
    while n_sq * 2 < c:
        both = each(lambda pb, ti: _dot(pb, jnp.concatenate([ti.astype(BF16), pb], axis=1)),
                    pbs, tinvs)
        tinvs = each(lambda ti, bo: ti + bo[:, :n2], tinvs, both)
        pbs = [bo[:, n2:].astype(BF16) for bo in both]
        n_sq *= 2
    tinvs = each(lambda ti, pb: (ti + _dot(pb, ti.astype(BF16))).astype(BF16), tinvs, pbs)
    mkvs = each(_dot, mks, vss)
    kvs = each(_dot_tn, vss, khs)

    states = [state_ref[p] for p in pairs]
    ys = [[] for _ in pairs]
    n_ch = len(chunks)
    for ch in chunks:
        ids = [p * n_ch + ch for p in pairs]
        sbs = [s.astype(BF16) for s in states]
        withs = [_dot_nt(xars[i], sb) + mkvs[i] for i, sb in zip(ids, sbs)]
        ubs = [_dot(tinvs[i], w[:n2].astype(BF16)).astype(BF16) for i, w in zip(ids, withs)]
        states = [s * jnp.exp(ltots[i]) + _dot_tn(ub, bhs[i]) + kvs[i]
                  for i, s, ub in zip(ids, states, ubs)]
        for p, i, w, ub in zip(pairs, ids, withs, ubs):
            ystack = w[n2:] + _dot(mrbs[i], ub)
            ys[p].append(ystack[:c] + ystack[c:])
    for p, s in zip(pairs, states):
        state_ref[p] = s

    ycat = [jnp.concatenate(y, axis=0) for y in ys]
    inv_n = 1.0 / RWKV_HEAD
    means = [m * inv_n for m in group_sum(ycat)]
    ds = each(lambda y, m: y - m, ycat, means)
    variances = [v * inv_n for v in group_sum(each(lambda d: d * d, ds))]
    outs = []
    for d, var, vec, bs, v, g in zip(ds, variances, vecs, bonus_sums, vs_in, gs):
        yn = d * lax.rsqrt(var + LNX_EPS) * vec[5:6, :] + vec[6:7, :]
        outs.append((yn + bs * v) * g)
    return outs


def _rwkv(p3, mu_pad, vecs, w2a2, g2):
    bsz, s, _ = p3.shape
    tb = _pick(s, (512, 256, 128, 64))
    npairs = RWKV_PAIRS_PER_STEP
    wd = npairs * LANES
    groups = RWKV_HEADS // 2 // npairs
    colblk = lambda off: (lambda b, hp, t: (b, t, off // wd + hp))
    mublk = lambda off: (lambda b, hp, t: (0, off // wd + hp))
    return pl.pallas_call(
        functools.partial(_rwkv_body, tb=tb, npairs=npairs),
        out_shape=jax.ShapeDtypeStruct((bsz, s, RWKV_WIDTH), BF16),
        grid=(bsz, groups, s // tb),
        in_specs=[
            pl.BlockSpec((None, tb, wd), colblk(P_R)),
            pl.BlockSpec((None, tb, wd), colblk(P_K)),
            pl.BlockSpec((None, tb, wd), colblk(P_V)),
            pl.BlockSpec((None, tb, 2 * LANES), lambda b, hp, t: (b, t, P_LORA // (2 * LANES))),
            pl.BlockSpec((1, wd), mublk(P_R)),
            pl.BlockSpec((1, wd), mublk(P_K)),
            pl.BlockSpec((1, wd), mublk(P_V)),
            pl.BlockSpec((1, 2 * LANES), lambda b, hp, t: (0, P_LORA // (2 * LANES))),
            pl.BlockSpec((SUBLANES, wd), lambda b, hp, t: (0, hp)),
            pl.BlockSpec((LANES, wd), lambda b, hp, t: (0, hp)),
            pl.BlockSpec((LANES, wd), lambda b, hp, t: (0, hp)),
        ],
        out_specs=pl.BlockSpec((None, tb, wd), lambda b, hp, t: (b, t, hp)),
        scratch_shapes=[pltpu.VMEM((npairs, LANES, LANES), F32),
                        pltpu.VMEM((1, wd), F32), pltpu.VMEM((1, wd), F32),
                        pltpu.VMEM((1, wd), F32), pltpu.VMEM((1, 2 * LANES), F32)],
        name="rwkv7",
        compiler_params=_cparams(("parallel", "parallel", "arbitrary")),
    )(p3, p3, p3, p3, mu_pad, mu_pad, mu_pad, mu_pad, vecs, w2a2, g2)


def _conv_body(h_ref, wb_ref, wc_ref, wu_ref, cw_ref, o_ref, tail_ref, *, tm, seq):
    i = pl.program_id(1)
    a = h_ref[...]
    gate_b = _dot(a, wb_ref[...])
    uu = _dot(a, wc_ref[...]) * _dot(a, wu_ref[...])

    @pl.when((i * tm) % seq == 0)
    def _():
        tail_ref[...] = jnp.zeros(tail_ref.shape, F32)

    tail = tail_ref[...]
    row = lax.broadcasted_iota(I32, uu.shape, 0)
    h1 = tail[SUBLANES - 1:SUBLANES, :]
    h2 = tail[SUBLANES - 2:SUBLANES - 1, :]
    s1 = jnp.where(row == 0, h1, pltpu.roll(uu, 1, axis=0))
    s2 = jnp.where(row == 0, h2, jnp.where(row == 1, h1, pltpu.roll(uu, 2, axis=0)))
    y = cw_ref[0:1, :] * s2 + cw_ref[1:2, :] * s1 + cw_ref[2:3, :] * uu
    o_ref[...] = (gate_b * y).astype(o_ref.dtype)
    tail_ref[...] = uu[tm - SUBLANES:tm, :]


def _conv_mixer(h, w_in, conv_w, s):
    t, d = h.shape
    tm = _pick(s, (512, 256, 128))
    tn = 512
    nblk = d // tn
    w = w_in.astype(BF16)
    cw = jnp.concatenate([conv_w, jnp.zeros((SUBLANES - CONV_WIDTH, d), F32)], axis=0)
    wspec = lambda off: pl.BlockSpec((d, tn), lambda j, i: (0, j + off))
    return pl.pallas_call(
        functools.partial(_conv_body, tm=tm, seq=s),
        out_shape=jax.ShapeDtypeStruct((t, d), BF16),
        grid=(nblk, t // tm),
        in_specs=[pl.BlockSpec((tm, d), lambda j, i: (i, 0)),
                  wspec(0), wspec(nblk), wspec(2 * nblk),
                  pl.BlockSpec((SUBLANES, tn), lambda j, i: (0, j))],
        out_specs=pl.BlockSpec((tm, tn), lambda j, i: (i, j)),
        scratch_shapes=[pltpu.VMEM((SUBLANES, tn), F32)],
        name="conv_mixer",
        compiler_params=_cparams(("parallel", "arbitrary")),
    )(h, w, w, w, cw)


def _rank_pairs():
    kt = PEER_TOPK
    return [(a, b) for a in range(kt) for b in range(kt) if (a + 1) * (b + 1) <= kt]


PAIR_ROWS = -(-len(_rank_pairs()) // SUBLANES) * SUBLANES


def _pair_matrices():
    m1 = np.zeros((PAIR_ROWS, PEER_TOPK), np.float32)
    m2 = np.zeros((PAIR_ROWS, PEER_TOPK), np.float32)
    for p, (a, b) in enumerate(_rank_pairs()):
        m1[p, a] = 1.0
        m2[p, b] = 1.0
    return m1, m2


def _exact_dot_left(m, x):
    hi = x.astype(BF16)
    r1 = x - hi.astype(F32)
    mid = r1.astype(BF16)
    lo = (r1 - mid.astype(F32)).astype(BF16)
    return (_dot(m, hi) + _dot(m, mid)) + _dot(m, lo)


def _extract_top_rows(xs, row_iota, n, rank_iota, extras=None):
    xs = list(xs)
    sentinel = float(xs[0].shape[0])
    zeros = lambda: [jnp.zeros((n, x.shape[1]), F32) for x in xs]
    vals, idxs, exts = zeros(), zeros(), zeros()
    for kk in range(n):
        here = rank_iota == kk
        for j, x in enumerate(xs):
            m = jnp.max(x, axis=0, keepdims=True)
            idx = jnp.min(jnp.where(x == m, row_iota, sentinel), axis=0, keepdims=True)
            hit = row_iota == idx
            if extras is not None:
                ext = jnp.max(jnp.where(hit, extras[j], -1.0), axis=0, keepdims=True)
                exts[j] = jnp.where(here, ext, exts[j])
            xs[j] = jnp.where(hit, -jnp.inf, x)
            vals[j] = jnp.where(here, m, vals[j])
            idxs[j] = jnp.where(here, idx, idxs[j])
    return list(zip(vals, idxs, exts))


def _route_body(q_ref, keys_ref, m1_ref, m2_ref, e1_ref, e2_ref, gw_ref):
    kt = PEER_TOPK
    tb = q_ref.shape[0]
    nh = ROUTE_HEADS_PER_STEP
    key_iota = lax.broadcasted_iota(I32, (N_KEYS, tb), 0).astype(F32)
    rank_iota = lax.broadcasted_iota(I32, (kt, tb), 0)
    scs = []
    for hp in range(2 * nh):
        qh = q_ref[:, hp * PEER_HALF:(hp + 1) * PEER_HALF].astype(BF16)
        scs.append(_dot_nt(keys_ref[hp], qh))
    tops = _extract_top_rows(scs, key_iota, kt, rank_iota)
    m1 = m1_ref[...]
    m2 = m2_ref[...]
    pair_iota = lax.broadcasted_iota(I32, (PAIR_ROWS, tb), 0).astype(F32)
    cands, eids = [], []
    for h in range(nh):
        (sv1, si1, _), (sv2, si2, _) = tops[2 * h], tops[2 * h + 1]
        cand = _exact_dot_left(m1, sv1) + _exact_dot_left(m2, sv2)
        cands.append(jnp.where(pair_iota < float(len(_rank_pairs())), cand, -jnp.inf))
        eids.append(_dot(m1, si1.astype(BF16)) * float(N_KEYS) + _dot(m2, si2.astype(BF16)))
    picks = _extract_top_rows(cands, pair_iota, kt, rank_iota, extras=eids)
    for h, (cv, _, ev) in enumerate(picks):
        rows = slice(h * kt, (h + 1) * kt)
        ex = jnp.exp(cv - cv[0:1, :])
        gw_ref[rows, :] = ex / jnp.sum(ex, axis=0, keepdims=True)
        ei = ev.astype(I32)
        e1_ref[rows, :] = (ei >> 7).astype(F32)
        e2_ref[rows, :] = (ei & (N_KEYS - 1)).astype(F32)


def _peer_route(q, keys):
    t = q.shape[0]
    tb = LANES
    m1, m2 = _pair_matrices()
    slots = PEER_HEADS * PEER_TOPK
    out = jax.ShapeDtypeStruct((slots, t), F32)
    nh = ROUTE_HEADS_PER_STEP
    pair_spec = pl.BlockSpec((PAIR_ROWS, PEER_TOPK), lambda i, h: (0, 0))
    out_spec = pl.BlockSpec((nh * PEER_TOPK, tb), lambda i, h: (h, i))
    return pl.pallas_call(
        _route_body,
        out_shape=[out, out, out],
        grid=(t // tb, PEER_HEADS // nh),
        in_specs=[pl.BlockSpec((tb, nh * 2 * PEER_HALF), lambda i, h: (i, h)),
                  pl.BlockSpec((nh * 2, N_KEYS, PEER_HALF), lambda i, h: (h, 0, 0)),
                  pair_spec, pair_spec],
        out_specs=[out_spec, out_spec, out_spec],
        name="peer_route",
        compiler_params=_cparams(("parallel", "parallel")),
    )(q, keys, jnp.asarray(m1, BF16), jnp.asarray(m2, BF16))


def _swap_token_and_row(xs):
    xs = list(xs)
    row = lax.broadcasted_iota(I32, xs[0].shape, 1)
    d = SUBLANES // 2
    while d >= 1:
        low = (row % (2 * d)) < d
        for s in range(SUBLANES):
            if (s // d) % 2 == 0:
                a, b = xs[s], xs[s + d]
                xs[s] = jnp.where(low, a, pltpu.roll(b, d, axis=1))
                xs[s + d] = jnp.where(low, pltpu.roll(a, SUBLANES - d, axis=1), b)
        d //= 2
    return xs


def _gate_body(e1t_ref, e2t_ref, gwt_ref, o_ref, e1_s, e2_s, gw_s):
    tb = o_ref.shape[2]
    e1_s[...] = e1t_ref[...].T
    e2_s[...] = e2t_ref[...].T
    gw_s[...] = gwt_ref[...].T
    sub = lax.broadcasted_iota(I32, (N_KEYS, LANES), 0).astype(F32)
    nblk = N_KEYS // SUBLANES

    def eight_tokens(g, carry):
        t0 = pl.multiple_of(g * SUBLANES, SUBLANES)
        lhs, rhs = [], []
        for s in range(SUBLANES):
            e1 = e1_s[pl.ds(t0 + s, 1), :]
            e2 = e2_s[pl.ds(t0 + s, 1), :]
            gw = gw_s[pl.ds(t0 + s, 1), :]
            lhs.append(jnp.where(e1 == sub, gw, 0.0).astype(BF16))
            rhs.append(jnp.where(e2 == sub, 1.0, 0.0).astype(BF16))
        per_token = [_dot_nt(a, b).reshape(nblk, SUBLANES, N_KEYS) for a, b in zip(lhs, rhs)]
        for r, y in enumerate(_swap_token_and_row(per_token)):
            o_ref[:, r, pl.ds(t0, SUBLANES), :] = y
        return carry

    lax.fori_loop(0, tb // SUBLANES, eight_tokens, 0, unroll=8)


def _peer_gates(e1t, e2t, gwt):
    t = e1t.shape[1]
    tb = LANES
    nblk = N_KEYS // SUBLANES
    in_spec = pl.BlockSpec((LANES, tb), lambda i: (0, i))
    out = pl.pallas_call(
        _gate_body,
        out_shape=jax.ShapeDtypeStruct((nblk, SUBLANES, t, N_KEYS), F32),
        grid=(t // tb,),
        in_specs=[in_spec, in_spec, in_spec],
        out_specs=pl.BlockSpec((nblk, SUBLANES, tb, N_KEYS), lambda i: (0, 0, i, 0)),
        scratch_shapes=[pltpu.VMEM((tb, LANES), F32)] * 3,
        name="peer_gates",
        compiler_params=_cparams(("parallel",)),
    )(e1t, e2t, gwt)
    return out.reshape(N_KEYS, t, N_KEYS)


def _peer_up_body(h_ref, u_ref, g_ref, o_ref):
    s = _dot_nt(h_ref[...], u_ref[...])
    act = 0.5 * s * (1.0 + lax.erf(s * (2.0 ** -0.5)))
    for j in range(o_ref.shape[1] // N_KEYS):
        cols = slice(j * N_KEYS, (j + 1) * N_KEYS)
        o_ref[:, cols] = (act[:, cols] * g_ref[j]).astype(o_ref.dtype)


def _peer_up(h, u_all, layer, g3):
    t, d = h.shape
    n_exp = u_all.shape[1]
    tm = _pick(t, (1024, 512, 256, 128))
    tn = SUBLANES * N_KEYS
    return pl.pallas_call(
        _peer_up_body,
        out_shape=jax.ShapeDtypeStruct((t, n_exp), BF16),
        grid=(t // tm, n_exp // tn),
        in_specs=[pl.BlockSpec((tm, d), lambda i, j: (i, 0), pipeline_mode=pl.Buffered(1)),
                  pl.BlockSpec((None, tn, d), lambda i, j: (layer, j, 0)),
                  pl.BlockSpec((tn // N_KEYS, tm, N_KEYS), lambda i, j: (j, i, 0))],
        out_specs=pl.BlockSpec((tm, tn), lambda i, j: (i, j)),
        name="peer_up",
        compiler_params=_cparams(("parallel", "parallel")),
    )(h, u_all, g3)


def _residual(acc, x, gate):
    return x + gate * acc


def _gated_residual_matmul(a, w_all, layer, x2, gate, s, *, tk, name):
    t, d = x2.shape
    bsz = gate.shape[0]
    tm = _pick(s, (1024, 512, 256, 128))
    tn = 1024
    (out,) = _matmul(
        a, [w_all], [0], d, tm=tm, tn=tn, tk=tk, out_dtypes=[F32], epilogue=_residual,
        extras=[(x2, (tm, tn), lambda i, j: (i, j)),
                (gate.reshape(bsz, 1, d), (None, 1, tn), lambda i, j: (i * tm // s, 0, j))],
        b_lead=layer, name=name)
    return out


def _peer_ffn(x, w_norm, shift, scale, gate, layer, w_q_all, keys, u_all, v_all):
    bsz, s, d = x.shape
    t = bsz * s
    h = _norm_mod(x, w_norm, shift, scale).reshape(t, d)
    tm = _pick(s, (1024, 512, 256, 128))
    (q,) = _matmul(h, [w_q_all], [0], w_q_all.shape[2], tm=tm, tn=1024, tk=d,
                   out_dtypes=[F32], b_lead=layer, name="peer_query")
    keys_b = keys.reshape(PEER_HEADS * 2, N_KEYS, PEER_HALF).astype(BF16)
    e1t, e2t, gwt = _peer_route(q, keys_b)
    g3 = _peer_gates(e1t, e2t, gwt)
    pmat = _peer_up(h, u_all, layer, g3)
    out = _gated_residual_matmul(pmat, v_all, layer, x.reshape(t, d), gate, s,
                                 tk=2048, name="peer_down")
    return out.reshape(bsz, s, d)


def _pad_in_proj(w):
    o_kv = Q_LORA
    o_pe = Q_LORA + KV_LORA
    o_rw = o_pe + QK_ROPE
    half = QK_ROPE // 2
    rows = w.shape[0]
    z = lambda n: jnp.zeros((rows, n), w.dtype)
    cq = w[:, :Q_LORA]
    ckv = w[:, o_kv:o_pe]
    kpe = w[:, o_pe:o_rw]
    kpe_swap = jnp.concatenate([kpe[:, half:], kpe[:, :half]], axis=1)
    r = w[:, o_rw:o_rw + RWKV_WIDTH]
    k = w[:, o_rw + RWKV_WIDTH:o_rw + 2 * RWKV_WIDTH]
    v = w[:, o_rw + 2 * RWKV_WIDTH:o_rw + 3 * RWKV_WIDTH]
    lora = w[:, o_rw + 3 * RWKV_WIDTH:]
    pad = LANES - QK_ROPE
    out = jnp.concatenate([ckv, kpe, z(pad), kpe_swap, z(pad), cq, lora,
                           z(P_R - P_LORA - lora.shape[1]), r, k, v], axis=1)
    assert out.shape[1] == P_WIDTH
    return out


def _hybrid_mixer(h, positions, w_in, q_norm_w, w_uq, kv_norm_w, w_ukv, qk_q_w, qk_k_w,
                  mu, w0, w2, a0, a2, g2, k_k, k_a, r_k, lnx_w, lnx_b):
    bsz, s, d = h.shape
    t = bsz * s
    tm = _pick(s, (1024, 512, 256, 128))
    w_pad = _pad_in_proj(w_in).astype(BF16)
    (p,) = _matmul(h.reshape(t, d), [w_pad], [0], P_WIDTH, tm=tm, tn=1024, tk=d,
                   out_dtypes=[F32], name="hyb_in_proj")
    q, k, v = _mla_qkv(p, positions, q_norm_w, w_uq, kv_norm_w, w_ukv, qk_q_w, qk_k_w)
    y_mla = _attention(q.reshape(bsz, s, -1), k.reshape(bsz, s, -1), v.reshape(bsz, s, -1), bsz, s)

    mu_full = jnp.concatenate([jnp.zeros((Q_LORA + KV_LORA + QK_ROPE,), F32), mu]).reshape(1, -1)
    mu_pad = _pad_in_proj(mu_full)
    zrow = jnp.zeros((RWKV_WIDTH,), F32)
    vecs = jnp.stack([w0, a0, k_k, k_a, r_k.reshape(-1), lnx_w, lnx_b, zrow])
    w2a2 = jnp.concatenate([w2, a2], axis=0).astype(BF16)
    y_rwkv = _rwkv(p.reshape(bsz, s, P_WIDTH), mu_pad, vecs, w2a2, g2.astype(BF16))
    return jnp.concatenate([y_mla, y_rwkv], axis=-1).reshape(t, -1)


def kernel(x, c, positions, ada_w, ada_b, norm_mix_w, norm_ffn_w, hyb_w_in, mla_q_norm_w, mla_w_uq, mla_kv_norm_w, mla_w_ukv, mla_qk_q_w, mla_qk_k_w, rwkv_mu, rwkv_w0, rwkv_w2, rwkv_a0, rwkv_a2, rwkv_g2, rwkv_k_k, rwkv_k_a, rwkv_r_k, rwkv_lnx_w, rwkv_lnx_b, hyb_w_out, conv_w_in, conv_w, conv_w_out, peer_w_q, peer_keys, peer_u, peer_v):
    bsz, s, d = x.shape
    t = bsz * s
    depth = ada_w.shape[0]
    mod = _ada_mod(c, ada_w, ada_b)
    w_q_all = peer_w_q.astype(BF16)
    u_all = peer_u.astype(BF16)
    v_all = peer_v
    for layer in range(depth):
        sh_m, sc_m, gt_m, sh_f, sc_f, gt_f = [mod[layer, :, j * d:(j + 1) * d] for j in range(6)]
        h = _norm_mod(x, norm_mix_w[layer], sh_m, sc_m)
        i = layer // 2
        if layer % 2 == 0:
            y = _hybrid_mixer(h, positions, hyb_w_in[i], mla_q_norm_w[i], mla_w_uq[i],
                              mla_kv_norm_w[i], mla_w_ukv[i], mla_qk_q_w[i], mla_qk_k_w[i],
                              rwkv_mu[i], rwkv_w0[i], rwkv_w2[i], rwkv_a0[i], rwkv_a2[i],
                              rwkv_g2[i], rwkv_k_k[i], rwkv_k_a[i], rwkv_r_k[i],
                              rwkv_lnx_w[i], rwkv_lnx_b[i])
            w_out = hyb_w_out.astype(BF16)
        else:
            y = _conv_mixer(h.reshape(t, d), conv_w_in[i], conv_w[i], s)
            w_out = conv_w_out.astype(BF16)
        x = _gated_residual_matmul(y, w_out, i, x.reshape(t, d), gt_m, s, tk=d,
                                   name="mixer_out_proj").reshape(bsz, s, d)
        x = _peer_ffn(x, norm_ffn_w[layer], sh_f, sc_f, gt_f, layer, w_q_all, peer_keys[layer],
                      u_all, v_all)
    return x
```

```python
import functools

import numpy as np
import jax
import jax.numpy as jnp
from jax import lax
from jax.experimental import pallas as pl
from jax.experimental.pallas import tpu as pltpu

F32 = jnp.float32
BF16 = jnp.bfloat16
I32 = jnp.int32

MLA_HEADS = 16
QK_NOPE = 128
QK_ROPE = 64
QK_HEAD = QK_NOPE + QK_ROPE
V_HEAD = 128
Q_LORA = 768
KV_LORA = 512
ROPE_THETA = 10000.0
RWKV_HEAD = 64
RWKV_HEADS = 32
RWKV_WIDTH = RWKV_HEADS * RWKV_HEAD
W_LORA = 64
A_LORA = 64
G_LORA = 128
LNX_EPS = 64e-5
CONV_WIDTH = 3
PEER_HEADS = 8
N_KEYS = 128
PEER_TOPK = 16
PEER_HALF = 128
NORM_EPS = 1e-6

LANES = 128
SUBLANES = 8
VMEM_LIMIT = 56 * 1024 * 1024

P_CKV = 0
P_KPE = 512
P_CQ = 768
P_LORA = 1536
P_WIDTH = 2048
RWKV_CHUNK = 64
RWKV_PAIRS_PER_STEP = 8
ATTN_HEADS_PER_STEP = 4
ROUTE_HEADS_PER_STEP = 4


def _cparams(sem):
    return pltpu.CompilerParams(dimension_semantics=sem, vmem_limit_bytes=VMEM_LIMIT)


def _dot(a, b):
    return lax.dot_general(a, b, (((1,), (0,)), ((), ())), preferred_element_type=F32)


def _dot_nt(a, b):
    return lax.dot_general(a, b, (((1,), (1,)), ((), ())), preferred_element_type=F32)


def _dot_tn(a, b):
    return lax.dot_general(a, b, (((0,), (0,)), ((), ())), preferred_element_type=F32)


def _mm_body(*refs, nb, nk, n_extra, trans_b, epilogue):
    a_ref = refs[0]
    b_refs = refs[1:1 + nb]
    extra = refs[1 + nb:1 + nb + n_extra]
    o_refs = refs[1 + nb + n_extra:]
    a = a_ref[...].astype(BF16)
    parts = []
    for b_ref in b_refs:
        b = b_ref[...].astype(BF16)
        parts.append(_dot_nt(a, b) if trans_b else _dot(a, b))

    def finish(accs):
        outs = epilogue(*accs, *[e[...] for e in extra])
        if not isinstance(outs, (tuple, list)):
            outs = (outs,)
        for o_ref, o in zip(o_refs, outs):
            o_ref[...] = o.astype(o_ref.dtype)

    if nk == 1:
        finish(parts)
        return
    n_out = len(o_refs) - nb
    acc_refs = o_refs[n_out:]
    o_refs = o_refs[:n_out]
    k = pl.program_id(2)

    @pl.when(k == 0)
    def _():
        for acc, p in zip(acc_refs, parts):
            acc[...] = p

    @pl.when(k > 0)
    def _():
        for acc, p in zip(acc_refs, parts):
            acc[...] += p

    @pl.when(k == nk - 1)
    def _():
        finish([acc[...] for acc in acc_refs])


def _matmul(a, bs, b_col_blocks, n_out_cols, *, tm, tn, tk, out_dtypes, epilogue=None,
            extras=(), trans_b=False, b_lead=None, name="matmul"):
    m, kdim = a.shape
    nb = len(bs)
    nk = kdim // tk
    assert m % tm == 0 and kdim % tk == 0 and n_out_cols % tn == 0
    if epilogue is None:
        epilogue = lambda acc: acc
    grid = (m // tm, n_out_cols // tn, nk)
    in_specs = [pl.BlockSpec((tm, tk), lambda i, j, k: (i, k))]
    for off in b_col_blocks:
        if b_lead is not None:
            assert not trans_b
            in_specs.append(pl.BlockSpec((None, tk, tn),
                                         lambda i, j, k, off=off: (b_lead, k, j + off)))
        elif trans_b:
            in_specs.append(pl.BlockSpec((tn, tk), lambda i, j, k, off=off: (j + off, k)))
        else:
            in_specs.append(pl.BlockSpec((tk, tn), lambda i, j, k, off=off: (k, j + off)))
    extra_arrays = []
    for arr, bshape, imap in extras:
        in_specs.append(pl.BlockSpec(bshape, lambda i, j, k, imap=imap: imap(i, j)))
        extra_arrays.append(arr)
    out_shape = [jax.ShapeDtypeStruct((m, n_out_cols), dt) for dt in out_dtypes]
    out_specs = [pl.BlockSpec((tm, tn), lambda i, j, k: (i, j)) for _ in out_dtypes]
    scratch = [pltpu.VMEM((tm, tn), F32) for _ in range(nb)] if nk > 1 else []
    body = functools.partial(_mm_body, nb=nb, nk=nk, n_extra=len(extras), trans_b=trans_b,
                             epilogue=epilogue)
    outs = pl.pallas_call(
        body, out_shape=out_shape, grid=grid, in_specs=in_specs, out_specs=out_specs,
        scratch_shapes=scratch, name=name,
        compiler_params=_cparams(("parallel", "parallel", "arbitrary")),
    )(a, *bs, *extra_arrays)
    return outs


def _pick(n, prefs):
    for p in prefs:
        if n % p == 0:
            return p
    raise ValueError(f"no tile for {n}")


def _ada_body(c_ref, w_ref, b_ref, o_ref):
    c = c_ref[...]
    cond = (c * (1.0 / (1.0 + jnp.exp(-c)))).astype(BF16)
    o_ref[...] = _dot(cond, w_ref[...].astype(BF16)) + b_ref[...]


def _ada_mod(c, ada_w, ada_b):
    depth, d, n = ada_w.shape
    bsz = c.shape[0]
    tn = 512
    return pl.pallas_call(
        _ada_body,
        out_shape=jax.ShapeDtypeStruct((depth, bsz, n), F32),
        grid=(depth, n // tn),
        in_specs=[
            pl.BlockSpec((bsz, d), lambda l, j: (0, 0)),
            pl.BlockSpec((None, d, tn), lambda l, j: (l, 0, j)),
            pl.BlockSpec((None, 1, tn), lambda l, j: (l, 0, j)),
        ],
        out_specs=pl.BlockSpec((None, bsz, tn), lambda l, j: (l, 0, j)),
        name="ada_mod",
        compiler_params=_cparams(("parallel", "parallel")),
    )(c, ada_w, ada_b.reshape(depth, 1, n))


def _norm_mod_body(x_ref, w_ref, sh_ref, sc_ref, o_ref):
    x = x_ref[...]
    ms = jnp.mean(x * x, axis=-1, keepdims=True)
    y = x * lax.rsqrt(ms + NORM_EPS) * w_ref[...]
    o_ref[...] = (y * (1.0 + sc_ref[...]) + sh_ref[...]).astype(o_ref.dtype)


def _norm_mod(x, w, shift, scale):
    bsz, s, d = x.shape
    ts = _pick(s, (512, 256, 128))
    vec = lambda v: v.reshape(bsz, 1, d)
    return pl.pallas_call(
        _norm_mod_body,
        out_shape=jax.ShapeDtypeStruct((bsz, s, d), BF16),
        grid=(bsz, s // ts),
        in_specs=[
            pl.BlockSpec((None, ts, d), lambda b, i: (b, i, 0)),
            pl.BlockSpec((1, d), lambda b, i: (0, 0)),
            pl.BlockSpec((None, 1, d), lambda b, i: (b, 0, 0)),
            pl.BlockSpec((None, 1, d), lambda b, i: (b, 0, 0)),
        ],
        out_specs=pl.BlockSpec((None, ts, d), lambda b, i: (b, i, 0)),
        name="norm_mod",
        compiler_params=_cparams(("parallel", "parallel")),
    )(x, w.reshape(1, d), vec(shift), vec(scale))


def _rope_tables(pos_ref, freq_ref):
    ang = pos_ref[...].astype(F32) * freq_ref[...]
    lane = lax.broadcasted_iota(I32, ang.shape, 1)
    cos = jnp.where(lane < QK_ROPE, jnp.cos(ang), 0.0)
    sin = jnp.sin(ang)
    sin_signed = jnp.where(lane < QK_ROPE // 2, -sin, jnp.where(lane < QK_ROPE, sin, 0.0))
    return cos, sin_signed


def _mla_q_body(cq_ref, nw_ref, wuq_ref, hw_ref, pos_ref, freq_ref, o_ref, *, scale):
    cq = cq_ref[...]
    ms = jnp.mean(cq * cq, axis=-1, keepdims=True)
    cqn = (cq * lax.rsqrt(ms + NORM_EPS) * nw_ref[...]).astype(BF16)
    cos, sin_signed = _rope_tables(pos_ref, freq_ref)
    w_nope = hw_ref[0:1, :]
    w_rope = hw_ref[1:2, :]
    w_swap = hw_ref[2:3, :]
    for h in range(MLA_HEADS):
        qh = _dot(cqn, wuq_ref[:, h * 3 * LANES:(h + 1) * 3 * LANES])
        nope = qh[:, :LANES]
        rope = qh[:, LANES:2 * LANES]
        swap = qh[:, 2 * LANES:]
        ss = jnp.sum(nope * nope + rope * rope, axis=-1, keepdims=True)
        fac = lax.rsqrt(ss * (1.0 / QK_HEAD) + NORM_EPS) * scale
        o_ref[:, h * 2 * LANES:h * 2 * LANES + LANES] = (nope * fac * w_nope).astype(o_ref.dtype)
        roped = (rope * w_rope * cos + swap * w_swap * sin_signed) * fac
        o_ref[:, h * 2 * LANES + LANES:(h + 1) * 2 * LANES] = roped.astype(o_ref.dtype)


def _mla_kv_body(ckv_ref, kpe_ref, nw_ref, wukv_ref, hw_ref, pos_ref, freq_ref, k_ref, v_ref):
    ckv = ckv_ref[...]
    ms = jnp.mean(ckv * ckv, axis=-1, keepdims=True)
    ckvn = (ckv * lax.rsqrt(ms + NORM_EPS) * nw_ref[...]).astype(BF16)
    cos, sin_signed = _rope_tables(pos_ref, freq_ref)
    kpe = kpe_ref[:, :LANES]
    kpe_swap = kpe_ref[:, LANES:]
    pe_ss = jnp.sum(kpe * kpe, axis=-1, keepdims=True)
    pe_roped = kpe * hw_ref[1:2, :] * cos + kpe_swap * hw_ref[2:3, :] * sin_signed
    w_nope = hw_ref[0:1, :]
    for h in range(MLA_HEADS):
        kn = _dot(ckvn, wukv_ref[:, h * LANES:(h + 1) * LANES])
        ss = jnp.sum(kn * kn, axis=-1, keepdims=True) + pe_ss
        fac = lax.rsqrt(ss * (1.0 / QK_HEAD) + NORM_EPS)
        k_ref[:, h * 2 * LANES:h * 2 * LANES + LANES] = (kn * fac * w_nope).astype(k_ref.dtype)
        k_ref[:, h * 2 * LANES + LANES:(h + 1) * 2 * LANES] = (pe_roped * fac).astype(k_ref.dtype)
    nv = MLA_HEADS * V_HEAD
    v_ref[...] = _dot(ckvn, wukv_ref[:, nv:]).astype(v_ref.dtype)


def _head_norm_rows(w):
    half = QK_ROPE // 2
    zeros = jnp.zeros((LANES - QK_ROPE,), F32)
    rope = jnp.concatenate([w[QK_NOPE:], zeros])
    swap = jnp.concatenate([w[QK_NOPE + half:], w[QK_NOPE:QK_NOPE + half], zeros])
    rows = jnp.stack([w[:QK_NOPE], rope, swap])
    return jnp.concatenate([rows, jnp.zeros((SUBLANES - 3, LANES), F32)], axis=0)


def _rope_freq_row():
    half = QK_ROPE // 2
    f = ROPE_THETA ** (-jnp.arange(half, dtype=F32) / half)
    return jnp.concatenate([f, f, jnp.zeros((LANES - QK_ROPE,), F32)]).reshape(1, LANES)


def _wuq_layout():
    half = QK_ROPE // 2
    idx, mask = [], []
    for h in range(MLA_HEADS):
        base = h * QK_HEAD
        nope = list(range(base, base + QK_NOPE))
        rope = list(range(base + QK_NOPE, base + QK_HEAD))
        swap = rope[half:] + rope[:half]
        pad = [0] * (LANES - QK_ROPE)
        idx += nope + rope + pad + swap + pad
        mask += [1] * QK_NOPE + [1] * QK_ROPE + [0] * len(pad) + [1] * QK_ROPE + [0] * len(pad)
    return np.asarray(idx, np.int32), np.asarray(mask, np.float32)


def _wukv_layout():
    per = QK_NOPE + V_HEAD
    k_idx = [h * per + i for h in range(MLA_HEADS) for i in range(QK_NOPE)]
    v_idx = [h * per + QK_NOPE + i for h in range(MLA_HEADS) for i in range(V_HEAD)]
    return np.asarray(k_idx + v_idx, np.int32)


def _mla_qkv(p, positions, q_norm_w, w_uq, kv_norm_w, w_ukv, qk_q_w, qk_k_w):
    t = p.shape[0]
    tm = _pick(t, (256, 128))
    idx, mask = _wuq_layout()
    wuq = (w_uq[:, idx] * mask[None, :]).astype(BF16)
    wukv = w_ukv[:, _wukv_layout()].astype(BF16)
    pos = positions.reshape(t, 1)
    freq = _rope_freq_row()
    hq = _head_norm_rows(qk_q_w)
    hk = _head_norm_rows(qk_k_w)
    qw = MLA_HEADS * 2 * LANES
    row = lambda i: (i, 0)
    fixed = lambda i: (0, 0)
    q = pl.pallas_call(
        functools.partial(_mla_q_body, scale=QK_HEAD ** -0.5),
        out_shape=jax.ShapeDtypeStruct((t, qw), BF16),
        grid=(t // tm,),
        in_specs=[
            pl.BlockSpec((tm, Q_LORA), lambda i: (i, P_CQ // Q_LORA)),
            pl.BlockSpec((1, Q_LORA), fixed),
            pl.BlockSpec(wuq.shape, fixed),
            pl.BlockSpec((SUBLANES, LANES), fixed),
            pl.BlockSpec((tm, 1), row),
            pl.BlockSpec((1, LANES), fixed),
        ],
        out_specs=pl.BlockSpec((tm, qw), row),
        name="mla_q",
        compiler_params=_cparams(("parallel",)),
    )(p, q_norm_w.reshape(1, Q_LORA), wuq, hq, pos, freq)
    k, v = pl.pallas_call(
        _mla_kv_body,
        out_shape=[jax.ShapeDtypeStruct((t, qw), BF16),
                   jax.ShapeDtypeStruct((t, MLA_HEADS * V_HEAD), BF16)],
        grid=(t // tm,),
        in_specs=[
            pl.BlockSpec((tm, KV_LORA), lambda i: (i, P_CKV // KV_LORA)),
            pl.BlockSpec((tm, 2 * LANES), lambda i: (i, P_KPE // (2 * LANES))),
            pl.BlockSpec((1, KV_LORA), fixed),
            pl.BlockSpec(wukv.shape, fixed),
            pl.BlockSpec((SUBLANES, LANES), fixed),
            pl.BlockSpec((tm, 1), row),
            pl.BlockSpec((1, LANES), fixed),
        ],
        out_specs=[pl.BlockSpec((tm, qw), row), pl.BlockSpec((tm, MLA_HEADS * V_HEAD), row)],
        name="mla_kv",
        compiler_params=_cparams(("parallel",)),
    )(p, p, kv_norm_w.reshape(1, KV_LORA), wukv, hk, pos, freq)
    return q, k, v


def _attn_body(qi_ref, ki_ref, q_ref, k_ref, v_ref, o_ref, m_ref, l_ref, acc_ref, *, tq):
    step_id = pl.program_id(2)
    qi = qi_ref[step_id]
    ki = ki_ref[step_id]

    @pl.when(ki == 0)
    def _():
        m_ref[...] = jnp.full(m_ref.shape, -jnp.inf, F32)
        l_ref[...] = jnp.zeros(l_ref.shape, F32)
        acc_ref[...] = jnp.zeros(acc_ref.shape, F32)

    heads = range(ATTN_HEADS_PER_STEP)
    qk = lambda h: slice(h * 2 * LANES, (h + 1) * 2 * LANES)
    vo = lambda h: slice(h * V_HEAD, (h + 1) * V_HEAD)

    def step(masked):
        ss = [_dot_nt(q_ref[:, qk(h)], k_ref[:, qk(h)]) for h in heads]
        if masked:
            row = lax.broadcasted_iota(I32, ss[0].shape, 0)
            col = lax.broadcasted_iota(I32, ss[0].shape, 1)
            ss = [jnp.where(col <= row, s, -jnp.inf) for s in ss]
        reps = tq // LANES
        prs, alphas = [], []
        for h, s in zip(heads, ss):
            m_prev = m_ref[h]
            m_new = jnp.maximum(m_prev, jnp.max(s, axis=-1, keepdims=True))
            alpha = jnp.exp(m_prev - m_new)
            pr = jnp.exp(s - jnp.tile(m_new, (1, reps)))
            l_ref[h] = alpha * l_ref[h] + jnp.sum(pr, axis=-1, keepdims=True)
            m_ref[h] = m_new
            prs.append(pr.astype(BF16))
            alphas.append(alpha)
        pvs = [_dot(pr, v_ref[:, vo(h)]) for h, pr in zip(heads, prs)]
        for h, alpha, pv in zip(heads, alphas, pvs):
            acc_ref[h] = alpha * acc_ref[h] + pv

    @pl.when(ki < qi)
    def _():
        step(False)

    @pl.when(ki == qi)
    def _():
        step(True)
        for h in heads:
            o_ref[:, vo(h)] = (acc_ref[h] / l_ref[h]).astype(o_ref.dtype)


def _attention(q, k, v, bsz, s):
    tq = _pick(s, (512, 256, 128))
    n = s // tq
    pairs = [(i, j) for i in range(n) for j in range(i + 1)]
    qi_tab = jnp.asarray([i for i, _ in pairs], I32)
    ki_tab = jnp.asarray([j for _, j in pairs], I32)
    nh = ATTN_HEADS_PER_STEP
    grid_spec = pltpu.PrefetchScalarGridSpec(
        num_scalar_prefetch=2,
        grid=(bsz, MLA_HEADS // nh, len(pairs)),
        in_specs=[
            pl.BlockSpec((None, tq, nh * 2 * LANES), lambda b, h, p, qi, ki: (b, qi[p], h)),
            pl.BlockSpec((None, tq, nh * 2 * LANES), lambda b, h, p, qi, ki: (b, ki[p], h)),
            pl.BlockSpec((None, tq, nh * V_HEAD), lambda b, h, p, qi, ki: (b, ki[p], h)),
        ],
        out_specs=pl.BlockSpec((None, tq, nh * V_HEAD), lambda b, h, p, qi, ki: (b, qi[p], h)),
        scratch_shapes=[pltpu.VMEM((nh, tq, LANES), F32), pltpu.VMEM((nh, tq, LANES), F32),
                        pltpu.VMEM((nh, tq, V_HEAD), F32)],
    )
    return pl.pallas_call(
        functools.partial(_attn_body, tq=tq),
        out_shape=jax.ShapeDtypeStruct((bsz, s, MLA_HEADS * V_HEAD), BF16),
        grid_spec=grid_spec,
        name="mla_attention",
        compiler_params=_cparams(("parallel", "parallel", "arbitrary")),
    )(qi_tab, ki_tab, q, k, v)


def _rwkv_body(r_ref, k_ref, v_ref, xl_ref, mu_r_ref, mu_k_ref, mu_v_ref, mu_l_ref, vec_ref,
               w2a2_ref, g2_ref, o_ref, state_ref, pr_ref, pk_ref, pv_ref, pl_ref, *, tb, npairs):
    t_idx = pl.program_id(2)

    @pl.when(t_idx == 0)
    def _():
        state_ref[...] = jnp.zeros(state_ref.shape, F32)
        pr_ref[...] = jnp.zeros(pr_ref.shape, F32)
        pk_ref[...] = jnp.zeros(pk_ref.shape, F32)
        pv_ref[...] = jnp.zeros(pv_ref.shape, F32)
        pl_ref[...] = jnp.zeros(pl_ref.shape, F32)

    def mix(x_ref, prev_ref, mu_ref):
        x = x_ref[...]
        row = lax.broadcasted_iota(I32, x.shape, 0)
        shifted = jnp.where(row == 0, prev_ref[...], pltpu.roll(x, 1, axis=0))
        prev_ref[...] = x[tb - 1:tb, :]
        return x + (shifted - x) * mu_ref[...]

    r = mix(r_ref, pr_ref, mu_r_ref)
    k = mix(k_ref, pk_ref, mu_k_ref)
    v = mix(v_ref, pv_ref, mu_v_ref)
    xl = mix(xl_ref, pl_ref, mu_l_ref)

    lane = lax.broadcasted_iota(I32, (tb, LANES), 1)
    first = lane < RWKV_HEAD
    x0 = xl[:, :LANES]
    lhs_w = jnp.where(first, jnp.tanh(x0), 0.0).astype(BF16)
    lhs_a = jnp.where(first, 0.0, x0).astype(BF16)
    xg = xl[:, LANES:]
    lhs_g = (1.0 / (1.0 + jnp.exp(-xg))).astype(BF16)
    cols = [slice(p * LANES, (p + 1) * LANES) for p in range(npairs)]
    outs = _rwkv_pairs([r[:, c] for c in cols], [k[:, c] for c in cols], [v[:, c] for c in cols],
                       lhs_w, lhs_a, lhs_g, [vec_ref[:, c] for c in cols],
                       [w2a2_ref[:, c] for c in cols], [g2_ref[:, c] for c in cols],
                       state_ref, tb)
    for c, out in zip(cols, outs):
        o_ref[:, c] = out.astype(o_ref.dtype)


def _rwkv_pairs(rs, ks, vs_in, lhs_w, lhs_a, lhs_g, vecs, w2a2s, g2s, state_ref, tb):
    c = RWKV_CHUNK
    pairs = range(len(rs))
    chunks = range(tb // c)
    each = lambda f, *lists: [f(*args) for args in zip(*lists)]

    ri = lax.broadcasted_iota(I32, (LANES, LANES), 0)
    ci = lax.broadcasted_iota(I32, (LANES, LANES), 1)
    same = (ri // RWKV_HEAD) == (ci // RWKV_HEAD)
    ones_bd = jnp.where(same, 1.0, 0.0).astype(BF16)
    tri_r = lax.broadcasted_iota(I32, (c, c), 0)
    tri_c = lax.broadcasted_iota(I32, (c, c), 1)
    tri = jnp.where(tri_c <= tri_r, 1.0, 0.0).astype(BF16)
    eye = jnp.where(ri == ci, 1.0, 0.0)
    first_c = lax.broadcasted_iota(I32, (c, LANES), 1) < RWKV_HEAD

    def stack(z):
        return jnp.concatenate([jnp.where(first_c, z, 0.0), jnp.where(first_c, 0.0, z)],
                               axis=0).astype(BF16)

    def split(x):
        hi = x.astype(BF16)
        return hi, (x - hi.astype(F32)).astype(BF16)

    def group_sum(xs):
        parts = each(split, xs)
        his = [_dot(hi, ones_bd) for hi, _ in parts]
        los = [_dot(lo, ones_bd) for _, lo in parts]
        return each(lambda a, b: a + b, his, los)

    lws = [_dot(lhs_w, w) for w in w2a2s]
    las = [_dot(lhs_a, w) for w in w2a2s]
    gs = [_dot(lhs_g, g2) for g2 in g2s]
    kks = each(lambda k, vec: k * vec[2:3, :], ks, vecs)
    nrms = group_sum(each(lambda kk: kk * kk, kks))
    kks = each(lambda kk, n: kk / jnp.maximum(jnp.sqrt(n), 1e-12), kks, nrms)

    def decay_and_gate(lw, la, vec):
        zw = -(vec[0:1, :] + lw)
        softplus = jnp.maximum(zw, 0.0) + jnp.log(1.0 + jnp.exp(-jnp.abs(zw)))
        a = 1.0 / (1.0 + jnp.exp(-(vec[1:2, :] + la)))
        return -jnp.exp(-softplus - 0.5), a

    dg = each(decay_and_gate, lws, las, vecs)
    logds = [d for d, _ in dg]
    a_gates = [a for _, a in dg]
    kps = each(lambda k, a, vec: k * (1.0 + (a - 1.0) * vec[3:4, :]), ks, a_gates, vecs)
    bvecs = each(lambda kk, a: kk * a, kks, a_gates)
    bonus_sums = group_sum(each(lambda r, kp, vec: r * kp * vec[4:5, :], rs, kps, vecs))

    inst = [(p, ch) for p in pairs for ch in chunks]
    rows = lambda x, ch: x[ch * c:(ch + 1) * c]
    lds = [rows(logds[p], ch) for p, ch in inst]
    ld_parts = each(split, lds)
    lc_hi = [_dot(tri, hi) for hi, _ in ld_parts]
    lc_lo = [_dot(tri, lo) for _, lo in ld_parts]
    lcs = each(lambda a, b: a + b, lc_hi, lc_lo)
    ltots = [lc[c - 1:c, :] for lc in lcs]
    e_negs = [jnp.exp(-lc) for lc in lcs]
    e_rems = each(lambda lt, lc: jnp.exp(lt - lc), ltots, lcs)
    xas = [stack(-rows(kks[p], ch) * jnp.exp(lc - ld)) for (p, ch), lc, ld in zip(inst, lcs, lds)]
    xrs = [stack(rows(rs[p], ch) * jnp.exp(lc)) for (p, ch), lc in zip(inst, lcs)]
    ybs = [stack(rows(bvecs[p], ch) * e) for (p, ch), e in zip(inst, e_negs)]
    yks = [stack(rows(kps[p], ch) * e) for (p, ch), e in zip(inst, e_negs)]
    vss = [stack(rows(vs_in[p], ch)) for p, ch in inst]
    bhs = [stack(rows(bvecs[p], ch) * e) for (p, ch), e in zip(inst, e_rems)]
    khs = [stack(rows(kps[p], ch) * e) for (p, ch), e in zip(inst, e_rems)]

    n2 = 2 * c
    xars = each(lambda a, b: jnp.concatenate([a, b], axis=0), xas, xrs)
    rr = lax.broadcasted_iota(I32, (2 * n2, n2), 0)
    cc = lax.broadcasted_iota(I32, (2 * n2, n2), 1)
    rr_in = rr % n2
    tri_mask = ((rr_in // c) == (cc // c)) & ((cc % c) < (rr_in % c) + rr // n2)
    mbs = each(lambda a, b: jnp.where(tri_mask, _dot_nt(a, b), 0.0), xars, ybs)
    mks = each(lambda a, b: jnp.where(tri_mask, _dot_nt(a, b), 0.0).astype(BF16), xars, yks)
    labs = [mb[:n2] for mb in mbs]
    mrbs = [mb[n2:].astype(BF16) for mb in mbs]

    pbs = [lab.astype(BF16) for lab in labs]
    tinvs = [eye + lab for lab in labs]
    pbs = [_dot(pb, pb).astype(BF16) for pb in pbs]
    n_sq = 2
    while n_sq * 2 < c:
        both = each(lambda pb, ti: _dot(pb, jnp.concatenate([ti.astype(BF16), pb], axis=1)),
                    pbs, tinvs)
        tinvs = each(lambda ti, bo: ti + bo[:, :n2], tinvs, both)
        pbs = [bo[:, n2:].astype(BF16) for bo in both]
        n_sq *= 2
    tinvs = each(lambda ti, pb: (ti + _dot(pb, ti.astype(BF16))).astype(BF16), tinvs, pbs)
    mkvs = each(_dot, mks, vss)
    kvs = each(_dot_tn, vss, khs)

    states = [state_ref[p] for p in pairs]
    ys = [[] for _ in pairs]
    n_ch = len(chunks)
    for ch in chunks:
        ids = [p * n_ch + ch for p in pairs]
        sbs = [s.astype(BF16) for s in states]
        withs = [_dot_nt(xars[i], sb) + mkvs[i] for i, sb in zip(ids, sbs)]
        ubs = [_dot(tinvs[i], w[:n2].astype(BF16)).astype(BF16) for i, w in zip(ids, withs)]
        states = [s * jnp.exp(ltots[i]) + _dot_tn(ub, bhs[i]) + kvs[i]
                  for i, s, ub in zip(ids, states, ubs)]
        for p, i, w, ub in zip(pairs, ids, withs, ubs):
            ystack = w[n2:] + _dot(mrbs[i], ub)
            ys[p].append(ystack[:c] + ystack[c:])
    for p, s in zip(pairs, states):
        state_ref[p] = s

    ycat = [jnp.concatenate(y, axis=0) for y in ys]
    inv_n = 1.0 / RWKV_HEAD
    means = [m * inv_n for m in group_sum(ycat)]
    ds = each(lambda y, m: y - m, ycat, means)
    variances = [v * inv_n for v in group_sum(each(lambda d: d * d, ds))]
    outs = []
    for d, var, vec, bs, v, g in zip(ds, variances, vecs, bonus_sums, vs_in, gs):
        yn = d * lax.rsqrt(var + LNX_EPS) * vec[5:6, :] + vec[6:7, :]
        outs.append((yn + bs * v) * g)
    return outs


def _rwkv(p_rkv, p_mla, mu_rkv, mu_lora, vecs, w2a2, g2):
    bsz, s, _ = p_rkv.shape
    tb = _pick(s, (512, 256, 128, 64))
    npairs = RWKV_PAIRS_PER_STEP
    wd = npairs * LANES
    groups = RWKV_HEADS // 2 // npairs
    colblk = lambda off: (lambda b, hp, t: (b, t, off // wd + hp))
    mublk = lambda off: (lambda b, hp, t: (0, off // wd + hp))
    o_r, o_k, o_v = 0, RWKV_WIDTH, 2 * RWKV_WIDTH
    return pl.pallas_call(
        functools.partial(_rwkv_body, tb=tb, npairs=npairs),
        out_shape=jax.ShapeDtypeStruct((bsz, s, RWKV_WIDTH), BF16),
        grid=(bsz, groups, s // tb),
        in_specs=[
            pl.BlockSpec((None, tb, wd), colblk(o_r)),
            pl.BlockSpec((None, tb, wd), colblk(o_k)),
            pl.BlockSpec((None, tb, wd), colblk(o_v)),
            pl.BlockSpec((None, tb, 2 * LANES), lambda b, hp, t: (b, t, P_LORA // (2 * LANES))),
            pl.BlockSpec((1, wd), mublk(o_r)),
            pl.BlockSpec((1, wd), mublk(o_k)),
            pl.BlockSpec((1, wd), mublk(o_v)),
            pl.BlockSpec((1, 2 * LANES), lambda b, hp, t: (0, 0)),
            pl.BlockSpec((SUBLANES, wd), lambda b, hp, t: (0, hp)),
            pl.BlockSpec((LANES, wd), lambda b, hp, t: (0, hp)),
            pl.BlockSpec((LANES, wd), lambda b, hp, t: (0, hp)),
        ],
        out_specs=pl.BlockSpec((None, tb, wd), lambda b, hp, t: (b, t, hp)),
        scratch_shapes=[pltpu.VMEM((npairs, LANES, LANES), F32),
                        pltpu.VMEM((1, wd), F32), pltpu.VMEM((1, wd), F32),
                        pltpu.VMEM((1, wd), F32), pltpu.VMEM((1, 2 * LANES), F32)],
        name="rwkv7",
        compiler_params=_cparams(("parallel", "parallel", "arbitrary")),
    )(p_rkv, p_rkv, p_rkv, p_mla, mu_rkv, mu_rkv, mu_rkv, mu_lora, vecs, w2a2, g2)


def _conv_body(h_ref, wb_ref, wc_ref, wu_ref, cw_ref, o_ref, tail_ref, *, tm, seq):
    i = pl.program_id(1)
    a = h_ref[...]
    gate_b = _dot(a, wb_ref[...])
    uu = _dot(a, wc_ref[...]) * _dot(a, wu_ref[...])

    @pl.when((i * tm) % seq == 0)
    def _():
        tail_ref[...] = jnp.zeros(tail_ref.shape, F32)

    tail = tail_ref[...]
    row = lax.broadcasted_iota(I32, uu.shape, 0)
    h1 = tail[SUBLANES - 1:SUBLANES, :]
    h2 = tail[SUBLANES - 2:SUBLANES - 1, :]
    s1 = jnp.where(row == 0, h1, pltpu.roll(uu, 1, axis=0))
    s2 = jnp.where(row == 0, h2, jnp.where(row == 1, h1, pltpu.roll(uu, 2, axis=0)))
    y = cw_ref[0:1, :] * s2 + cw_ref[1:2, :] * s1 + cw_ref[2:3, :] * uu
    o_ref[...] = (gate_b * y).astype(o_ref.dtype)
    tail_ref[...] = uu[tm - SUBLANES:tm, :]


def _conv_mixer(h, w_in, conv_w, s):
    t, d = h.shape
    tm = _pick(s, (512, 256, 128))
    tn = 512
    nblk = d // tn
    w = w_in.astype(BF16)
    cw = jnp.concatenate([conv_w, jnp.zeros((SUBLANES - CONV_WIDTH, d), F32)], axis=0)
    wspec = lambda off: pl.BlockSpec((d, tn), lambda j, i: (0, j + off))
    return pl.pallas_call(
        functools.partial(_conv_body, tm=tm, seq=s),
        out_shape=jax.ShapeDtypeStruct((t, d), BF16),
        grid=(nblk, t // tm),
        in_specs=[pl.BlockSpec((tm, d), lambda j, i: (i, 0)),
                  wspec(0), wspec(nblk), wspec(2 * nblk),
                  pl.BlockSpec((SUBLANES, tn), lambda j, i: (0, j))],
        out_specs=pl.BlockSpec((tm, tn), lambda j, i: (i, j)),
        scratch_shapes=[pltpu.VMEM((SUBLANES, tn), F32)],
        name="conv_mixer",
        compiler_params=_cparams(("parallel", "arbitrary")),
    )(h, w, w, w, cw)


def _rank_pairs():
    kt = PEER_TOPK
    return [(a, b) for a in range(kt) for b in range(kt) if (a + 1) * (b + 1) <= kt]


PAIR_ROWS = -(-len(_rank_pairs()) // SUBLANES) * SUBLANES


def _pair_matrices():
    m1 = np.zeros((PAIR_ROWS, PEER_TOPK), np.float32)
    m2 = np.zeros((PAIR_ROWS, PEER_TOPK), np.float32)
    for p, (a, b) in enumerate(_rank_pairs()):
        m1[p, a] = 1.0
        m2[p, b] = 1.0
    return m1, m2


def _exact_dot_left(m, x):
    hi = x.astype(BF16)
    r1 = x - hi.astype(F32)
    mid = r1.astype(BF16)
    lo = (r1 - mid.astype(F32)).astype(BF16)
    return (_dot(m, hi) + _dot(m, mid)) + _dot(m, lo)


def _extract_top_rows(xs, row_iota, n, rank_iota, extras=None):
    xs = list(xs)
    sentinel = float(xs[0].shape[0])
    zeros = lambda: [jnp.zeros((n, x.shape[1]), F32) for x in xs]
    vals, idxs, exts = zeros(), zeros(), zeros()
    for kk in range(n):
        here = rank_iota == kk
        for j, x in enumerate(xs):
            m = jnp.max(x, axis=0, keepdims=True)
            idx = jnp.min(jnp.where(x == m, row_iota, sentinel), axis=0, keepdims=True)
            hit = row_iota == idx
            if extras is not None:
                ext = jnp.max(jnp.where(hit, extras[j], -1.0), axis=0, keepdims=True)
                exts[j] = jnp.where(here, ext, exts[j])
            xs[j] = jnp.where(hit, -jnp.inf, x)
            vals[j] = jnp.where(here, m, vals[j])
            idxs[j] = jnp.where(here, idx, idxs[j])
    return list(zip(vals, idxs, exts))


def _route_body(q_ref, keys_ref, m1_ref, m2_ref, e1_ref, e2_ref, gw_ref):
    kt = PEER_TOPK
    tb = q_ref.shape[0]
    nh = ROUTE_HEADS_PER_STEP
    key_iota = lax.broadcasted_iota(I32, (N_KEYS, tb), 0).astype(F32)
    rank_iota = lax.broadcasted_iota(I32, (kt, tb), 0)
    scs = []
    for hp in range(2 * nh):
        qh = q_ref[:, hp * PEER_HALF:(hp + 1) * PEER_HALF].astype(BF16)
        scs.append(_dot_nt(keys_ref[hp], qh))
    tops = _extract_top_rows(scs, key_iota, kt, rank_iota)
    m1 = m1_ref[...]
    m2 = m2_ref[...]
    pair_iota = lax.broadcasted_iota(I32, (PAIR_ROWS, tb), 0).astype(F32)
    cands, eids = [], []
    for h in range(nh):
        (sv1, si1, _), (sv2, si2, _) = tops[2 * h], tops[2 * h + 1]
        cand = _exact_dot_left(m1, sv1) + _exact_dot_left(m2, sv2)
        cands.append(jnp.where(pair_iota < float(len(_rank_pairs())), cand, -jnp.inf))
        eids.append(_dot(m1, si1.astype(BF16)) * float(N_KEYS) + _dot(m2, si2.astype(BF16)))
    picks = _extract_top_rows(cands, pair_iota, kt, rank_iota, extras=eids)
    for h, (cv, _, ev) in enumerate(picks):
        rows = slice(h * kt, (h + 1) * kt)
        ex = jnp.exp(cv - cv[0:1, :])
        gw_ref[rows, :] = ex / jnp.sum(ex, axis=0, keepdims=True)
        ei = ev.astype(I32)
        e1_ref[rows, :] = (ei >> 7).astype(F32)
        e2_ref[rows, :] = (ei & (N_KEYS - 1)).astype(F32)


def _peer_route(q, keys):
    t = q.shape[0]
    tb = LANES
    m1, m2 = _pair_matrices()
    slots = PEER_HEADS * PEER_TOPK
    out = jax.ShapeDtypeStruct((slots, t), F32)
    nh = ROUTE_HEADS_PER_STEP
    pair_spec = pl.BlockSpec((PAIR_ROWS, PEER_TOPK), lambda i, h: (0, 0))
    out_spec = pl.BlockSpec((nh * PEER_TOPK, tb), lambda i, h: (h, i))
    return pl.pallas_call(
        _route_body,
        out_shape=[out, out, out],
        grid=(t // tb, PEER_HEADS // nh),
        in_specs=[pl.BlockSpec((tb, nh * 2 * PEER_HALF), lambda i, h: (i, h)),
                  pl.BlockSpec((nh * 2, N_KEYS, PEER_HALF), lambda i, h: (h, 0, 0)),
                  pair_spec, pair_spec],
        out_specs=[out_spec, out_spec, out_spec],
        name="peer_route",
        compiler_params=_cparams(("parallel", "parallel")),
    )(q, keys, jnp.asarray(m1, BF16), jnp.asarray(m2, BF16))


def _swap_token_and_row(xs):
    xs = list(xs)
    row = lax.broadcasted_iota(I32, xs[0].shape, 1)
    d = SUBLANES // 2
    while d >= 1:
        low = (row % (2 * d)) < d
        for s in range(SUBLANES):
            if (s // d) % 2 == 0:
                a, b = xs[s], xs[s + d]
                xs[s] = jnp.where(low, a, pltpu.roll(b, d, axis=1))
                xs[s + d] = jnp.where(low, pltpu.roll(a, SUBLANES - d, axis=1), b)
        d //= 2
    return xs


GATE_TOKENS = 2 * SUBLANES


def _gate_body(e1t_ref, e2t_ref, gwt_ref, uf_ref, o_ref, ub_ref, e1_s, e2_s, gw_s):
    ub_ref[...] = uf_ref[...].astype(ub_ref.dtype)
    tb = o_ref.shape[2]
    e1_s[...] = e1t_ref[...].T
    e2_s[...] = e2t_ref[...].T
    gw_s[...] = gwt_ref[...].T
    sub = lax.broadcasted_iota(I32, (N_KEYS, LANES), 0).astype(F32)
    nblk = N_KEYS // SUBLANES

    def eight_tokens(t0):
        lhs, rhs = [], []
        for s in range(SUBLANES):
            e1 = e1_s[pl.ds(t0 + s, 1), :]
            e2 = e2_s[pl.ds(t0 + s, 1), :]
            gw = gw_s[pl.ds(t0 + s, 1), :]
            lhs.append(jnp.where(e1 == sub, gw, 0.0).astype(BF16))
            rhs.append(jnp.where(e2 == sub, 1.0, 0.0).astype(BF16))
        per_token = [_dot_nt(a, b).reshape(nblk, SUBLANES, N_KEYS) for a, b in zip(lhs, rhs)]
        return _swap_token_and_row(per_token)

    def sixteen_tokens(g, carry):
        t0 = pl.multiple_of(g * GATE_TOKENS, GATE_TOKENS)
        first, second = eight_tokens(t0), eight_tokens(t0 + SUBLANES)
        for r, (y0, y1) in enumerate(zip(first, second)):
            o_ref[:, r, pl.ds(t0, GATE_TOKENS), :] = jnp.concatenate(
                [y0, y1], axis=1).astype(o_ref.dtype)
        return carry

    lax.fori_loop(0, tb // GATE_TOKENS, sixteen_tokens, 0, unroll=4)


def _peer_gates(e1t, e2t, gwt, u_all, layer):
    t = e1t.shape[1]
    tb = LANES
    steps = t // tb
    nblk = N_KEYS // SUBLANES
    n_exp, d = u_all.shape[1:]
    rows = n_exp // steps
    assert rows * steps == n_exp and rows % GATE_TOKENS == 0
    in_spec = pl.BlockSpec((LANES, tb), lambda i: (0, i))
    g, u_bf16 = pl.pallas_call(
        _gate_body,
        out_shape=[jax.ShapeDtypeStruct((nblk, SUBLANES, t, N_KEYS), BF16),
                   jax.ShapeDtypeStruct((n_exp, d), BF16)],
        grid=(steps,),
        in_specs=[in_spec, in_spec, in_spec,
                  pl.BlockSpec((None, rows, d), lambda i: (layer, i, 0))],
        out_specs=[pl.BlockSpec((nblk, SUBLANES, tb, N_KEYS), lambda i: (0, 0, i, 0)),
                   pl.BlockSpec((rows, d), lambda i: (i, 0))],
        scratch_shapes=[pltpu.VMEM((tb, LANES), F32)] * 3,
        name="peer_gates",
        compiler_params=_cparams(("parallel",)),
    )(e1t, e2t, gwt, u_all)
    return g.reshape(N_KEYS, t, N_KEYS), u_bf16


def _peer_up_body(h_ref, u_ref, g_ref, o_ref):
    s = _dot_nt(h_ref[...], u_ref[...])
    act = 0.5 * s * (1.0 + lax.erf(s * (2.0 ** -0.5)))
    for j in range(o_ref.shape[1] // N_KEYS):
        cols = slice(j * N_KEYS, (j + 1) * N_KEYS)
        o_ref[:, cols] = (act[:, cols] * g_ref[j]).astype(o_ref.dtype)


def _peer_up(h, u_tab, g3):
    t, d = h.shape
    n_exp = u_tab.shape[0]
    tm = _pick(t, (1024, 512, 256, 128))
    tn = SUBLANES * N_KEYS
    return pl.pallas_call(
        _peer_up_body,
        out_shape=jax.ShapeDtypeStruct((t, n_exp), BF16),
        grid=(t // tm, n_exp // tn),
        in_specs=[pl.BlockSpec((tm, d), lambda i, j: (i, 0), pipeline_mode=pl.Buffered(1)),
                  pl.BlockSpec((tn, d), lambda i, j: (j, 0)),
                  pl.BlockSpec((tn // N_KEYS, tm, N_KEYS), lambda i, j: (j, i, 0))],
        out_specs=pl.BlockSpec((tm, tn), lambda i, j: (i, j)),
        name="peer_up",
        compiler_params=_cparams(("parallel", "parallel")),
    )(h, u_tab, g3)


def _residual(acc, x, gate):
    return x + gate * acc


def _gated_residual_matmul(a, w_all, layer, x2, gate, s, *, tk, name):
    t, d = x2.shape
    bsz = gate.shape[0]
    tm = _pick(s, (1024, 512, 256, 128))
    tn = 1024
    (out,) = _matmul(
        a, [w_all], [0], d, tm=tm, tn=tn, tk=tk, out_dtypes=[F32], epilogue=_residual,
        extras=[(x2, (tm, tn), lambda i, j: (i, j)),
                (gate.reshape(bsz, 1, d), (None, 1, tn), lambda i, j: (i * tm // s, 0, j))],
        b_lead=layer, name=name)
    return out


def _peer_ffn(x, w_norm, shift, scale, gate, layer, w_q_all, keys, u_all, v_all):
    bsz, s, d = x.shape
    t = bsz * s
    h = _norm_mod(x, w_norm, shift, scale).reshape(t, d)
    tm = _pick(s, (1024, 512, 256, 128))
    (q,) = _matmul(h, [w_q_all], [0], w_q_all.shape[2], tm=tm, tn=1024, tk=d,
                   out_dtypes=[F32], b_lead=layer, name="peer_query")
    keys_b = keys.reshape(PEER_HEADS * 2, N_KEYS, PEER_HALF).astype(BF16)
    e1t, e2t, gwt = _peer_route(q, keys_b)
    g3, u_tab = _peer_gates(e1t, e2t, gwt, u_all, layer)
    pmat = _peer_up(h, u_tab, g3)
    out = _gated_residual_matmul(pmat, v_all, layer, x.reshape(t, d), gate, s,
                                 tk=2048, name="peer_down")
    return out.reshape(bsz, s, d)


def _pad_mla_proj(w):
    o_kv = Q_LORA
    o_pe = Q_LORA + KV_LORA
    o_rw = o_pe + QK_ROPE
    half = QK_ROPE // 2
    z = lambda n: jnp.zeros((w.shape[0], n), w.dtype)
    cq = w[:, :Q_LORA]
    ckv = w[:, o_kv:o_pe]
    kpe = w[:, o_pe:o_rw]
    kpe_swap = jnp.concatenate([kpe[:, half:], kpe[:, :half]], axis=1)
    lora = w[:, o_rw + 3 * RWKV_WIDTH:]
    pad = LANES - QK_ROPE
    out = jnp.concatenate([ckv, kpe, z(pad), kpe_swap, z(pad), cq, lora,
                           z(P_WIDTH - P_LORA - lora.shape[1])], axis=1)
    assert out.shape[1] == P_WIDTH
    return out


def _hybrid_mixer(h, positions, w_in, q_norm_w, w_uq, kv_norm_w, w_ukv, qk_q_w, qk_k_w,
                  mu, w0, w2, a0, a2, g2, k_k, k_a, r_k, lnx_w, lnx_b):
    bsz, s, d = h.shape
    t = bsz * s
    tm = _pick(s, (1024, 512, 256, 128))
    o_rw = Q_LORA + KV_LORA + QK_ROPE
    n_rkv = 3 * RWKV_WIDTH
    h2 = h.reshape(t, d)
    (p_mla,) = _matmul(h2, [_pad_mla_proj(w_in).astype(BF16)], [0], P_WIDTH, tm=tm, tn=1024,
                       tk=d, out_dtypes=[F32], name="hyb_in_proj_mla")
    (p_rkv,) = _matmul(h2, [w_in[:, o_rw:o_rw + n_rkv].astype(BF16)], [0], n_rkv, tm=tm, tn=1024,
                       tk=d, out_dtypes=[F32], name="hyb_in_proj_rkv")
    q, k, v = _mla_qkv(p_mla, positions, q_norm_w, w_uq, kv_norm_w, w_ukv, qk_q_w, qk_k_w)
    y_mla = _attention(q.reshape(bsz, s, -1), k.reshape(bsz, s, -1), v.reshape(bsz, s, -1), bsz, s)

    zrow = jnp.zeros((RWKV_WIDTH,), F32)
    vecs = jnp.stack([w0, a0, k_k, k_a, r_k.reshape(-1), lnx_w, lnx_b, zrow])
    w2a2 = jnp.concatenate([w2, a2], axis=0).astype(BF16)
    y_rwkv = _rwkv(p_rkv.reshape(bsz, s, n_rkv), p_mla.reshape(bsz, s, P_WIDTH),
                   mu[:n_rkv].reshape(1, n_rkv), mu[n_rkv:].reshape(1, -1), vecs, w2a2,
                   g2.astype(BF16))
    return jnp.concatenate([y_mla, y_rwkv], axis=-1).reshape(t, -1)


def kernel(x, c, positions, ada_w, ada_b, norm_mix_w, norm_ffn_w, hyb_w_in, mla_q_norm_w, mla_w_uq, mla_kv_norm_w, mla_w_ukv, mla_qk_q_w, mla_qk_k_w, rwkv_mu, rwkv_w0, rwkv_w2, rwkv_a0, rwkv_a2, rwkv_g2, rwkv_k_k, rwkv_k_a, rwkv_r_k, rwkv_lnx_w, rwkv_lnx_b, hyb_w_out, conv_w_in, conv_w, conv_w_out, peer_w_q, peer_keys, peer_u, peer_v):
    bsz, s, d = x.shape
    t = bsz * s
    depth = ada_w.shape[0]
    mod = _ada_mod(c, ada_w, ada_b)
    w_q_all = peer_w_q.astype(BF16)
    u_all = peer_u
    v_all = peer_v
    for layer in range(depth):
        sh_m, sc_m, gt_m, sh_f, sc_f, gt_f = [mod[layer, :, j * d:(j + 1) * d] for j in range(6)]
        h = _norm_mod(x, norm_mix_w[layer], sh_m, sc_m)
        i = layer // 2
        if layer % 2 == 0:
            y = _hybrid_mixer(h, positions, hyb_w_in[i], mla_q_norm_w[i], mla_w_uq[i],
                              mla_kv_norm_w[i], mla_w_ukv[i], mla_qk_q_w[i], mla_qk_k_w[i],
                              rwkv_mu[i], rwkv_w0[i], rwkv_w2[i], rwkv_a0[i], rwkv_a2[i],
                              rwkv_g2[i], rwkv_k_k[i], rwkv_k_a[i], rwkv_r_k[i],
                              rwkv_lnx_w[i], rwkv_lnx_b[i])
            w_out = hyb_w_out.astype(BF16)
        else:
            y = _conv_mixer(h.reshape(t, d), conv_w_in[i], conv_w[i], s)
            w_out = conv_w_out.astype(BF16)
        x = _gated_residual_matmul(y, w_out, i, x.reshape(t, d), gt_m, s, tk=d,
                                   name="mixer_out_proj").reshape(bsz, s, d)
        x = _peer_ffn(x, norm_ffn_w[layer], sh_f, sc_f, gt_f, layer, w_q_all, peer_keys[layer],
                      u_all, v_all)
    return x
```

```python
import functools

import numpy as np
import jax
import jax.numpy as jnp
from jax import lax
from jax.experimental import pallas as pl
from jax.experimental.pallas import tpu as pltpu

F32 = jnp.float32
BF16 = jnp.bfloat16
I32 = jnp.int32

MLA_HEADS = 16
QK_NOPE = 128
QK_ROPE = 64
QK_HEAD = QK_NOPE + QK_ROPE
V_HEAD = 128
Q_LORA = 768
KV_LORA = 512
ROPE_THETA = 10000.0
RWKV_HEAD = 64
RWKV_HEADS = 32
RWKV_WIDTH = RWKV_HEADS * RWKV_HEAD
W_LORA = 64
A_LORA = 64
G_LORA = 128
LNX_EPS = 64e-5
CONV_WIDTH = 3
PEER_HEADS = 8
N_KEYS = 128
PEER_TOPK = 16
PEER_HALF = 128
NORM_EPS = 1e-6

LANES = 128
SUBLANES = 8
VMEM_LIMIT = 56 * 1024 * 1024

P_CKV = 0
P_KPE = 512
P_CQ = 768
P_LORA = 1536
P_WIDTH = 2048
RWKV_CHUNK = 64
RWKV_PAIRS_PER_STEP = 8
ATTN_HEADS_PER_STEP = 4
ROUTE_HEADS_PER_STEP = 4


def _cparams(sem):
    return pltpu.CompilerParams(dimension_semantics=sem, vmem_limit_bytes=VMEM_LIMIT)


def _dot(a, b):
    return lax.dot_general(a, b, (((1,), (0,)), ((), ())), preferred_element_type=F32)


def _dot_nt(a, b):
    return lax.dot_general(a, b, (((1,), (1,)), ((), ())), preferred_element_type=F32)


def _dot_tn(a, b):
    return lax.dot_general(a, b, (((0,), (0,)), ((), ())), preferred_element_type=F32)


def _mm_body(*refs, nk, n_extra, epilogue):
    a_ref, b_ref = refs[:2]
    extra = refs[2:2 + n_extra]
    o_ref = refs[2 + n_extra]
    part = _dot(a_ref[...].astype(BF16), b_ref[...].astype(BF16))

    def finish(acc):
        o_ref[...] = epilogue(acc, *[e[...] for e in extra]).astype(o_ref.dtype)

    if nk == 1:
        finish(part)
        return
    acc_ref = refs[3 + n_extra]
    k = pl.program_id(2)

    @pl.when(k == 0)
    def _():
        acc_ref[...] = part

    @pl.when(k > 0)
    def _():
        acc_ref[...] += part

    @pl.when(k == nk - 1)
    def _():
        finish(acc_ref[...])


def _matmul(a, b, *, tm, tn, tk, out_dtype, epilogue=None, extras=(), b_lead=None,
            name="matmul"):
    m, kdim = a.shape
    n = b.shape[-1]
    nk = kdim // tk
    assert m % tm == 0 and kdim % tk == 0 and n % tn == 0
    if epilogue is None:
        epilogue = lambda acc: acc
    if b_lead is None:
        b_spec = pl.BlockSpec((tk, tn), lambda i, j, k: (k, j))
    else:
        b_spec = pl.BlockSpec((None, tk, tn), lambda i, j, k: (b_lead, k, j))
    in_specs = [pl.BlockSpec((tm, tk), lambda i, j, k: (i, k)), b_spec]
    extra_arrays = []
    for arr, bshape, imap in extras:
        in_specs.append(pl.BlockSpec(bshape, lambda i, j, k, imap=imap: imap(i, j)))
        extra_arrays.append(arr)
    return pl.pallas_call(
        functools.partial(_mm_body, nk=nk, n_extra=len(extras), epilogue=epilogue),
        out_shape=jax.ShapeDtypeStruct((m, n), out_dtype),
        grid=(m // tm, n // tn, nk),
        in_specs=in_specs,
        out_specs=pl.BlockSpec((tm, tn), lambda i, j, k: (i, j)),
        scratch_shapes=[pltpu.VMEM((tm, tn), F32)] if nk > 1 else [],
        name=name,
        compiler_params=_cparams(("parallel", "parallel", "arbitrary")),
    )(a, b, *extra_arrays)


def _pick(n, prefs):
    for p in prefs:
        if n % p == 0:
            return p
    raise ValueError(f"no tile for {n}")


def _ada_body(c_ref, w_ref, b_ref, o_ref):
    c = c_ref[...]
    cond = (c * (1.0 / (1.0 + jnp.exp(-c)))).astype(BF16)
    o_ref[...] = _dot(cond, w_ref[...].astype(BF16)) + b_ref[...]


def _ada_mod(c, ada_w, ada_b):
    depth, d, n = ada_w.shape
    bsz = c.shape[0]
    tn = 512
    return pl.pallas_call(
        _ada_body,
        out_shape=jax.ShapeDtypeStruct((depth, bsz, n), F32),
        grid=(depth, n // tn),
        in_specs=[
            pl.BlockSpec((bsz, d), lambda l, j: (0, 0)),
            pl.BlockSpec((None, d, tn), lambda l, j: (l, 0, j)),
            pl.BlockSpec((None, 1, tn), lambda l, j: (l, 0, j)),
        ],
        out_specs=pl.BlockSpec((None, bsz, tn), lambda l, j: (l, 0, j)),
        name="ada_mod",
        compiler_params=_cparams(("parallel", "parallel")),
    )(c, ada_w, ada_b.reshape(depth, 1, n))


def _norm_mod_body(x_ref, w_ref, sh_ref, sc_ref, o_ref):
    x = x_ref[...]
    ms = jnp.mean(x * x, axis=-1, keepdims=True)
    y = x * lax.rsqrt(ms + NORM_EPS) * w_ref[...]
    o_ref[...] = (y * (1.0 + sc_ref[...]) + sh_ref[...]).astype(o_ref.dtype)


def _norm_mod(x, w, shift, scale):
    bsz, s, d = x.shape
    ts = _pick(s, (512, 256, 128))
    vec = lambda v: v.reshape(bsz, 1, d)
    return pl.pallas_call(
        _norm_mod_body,
        out_shape=jax.ShapeDtypeStruct((bsz, s, d), BF16),
        grid=(bsz, s // ts),
        in_specs=[
            pl.BlockSpec((None, ts, d), lambda b, i: (b, i, 0)),
            pl.BlockSpec((1, d), lambda b, i: (0, 0)),
            pl.BlockSpec((None, 1, d), lambda b, i: (b, 0, 0)),
            pl.BlockSpec((None, 1, d), lambda b, i: (b, 0, 0)),
        ],
        out_specs=pl.BlockSpec((None, ts, d), lambda b, i: (b, i, 0)),
        name="norm_mod",
        compiler_params=_cparams(("parallel", "parallel")),
    )(x, w.reshape(1, d), vec(shift), vec(scale))


def _rope_tables(pos_ref, freq_ref):
    ang = pos_ref[...].astype(F32) * freq_ref[...]
    lane = lax.broadcasted_iota(I32, ang.shape, 1)
    cos = jnp.where(lane < QK_ROPE, jnp.cos(ang), 0.0)
    sin = jnp.sin(ang)
    sin_signed = jnp.where(lane < QK_ROPE // 2, -sin, jnp.where(lane < QK_ROPE, sin, 0.0))
    return cos, sin_signed


def _mla_q_body(cq_ref, nw_ref, wuq_ref, hw_ref, pos_ref, freq_ref, o_ref, *, scale):
    cq = cq_ref[...]
    ms = jnp.mean(cq * cq, axis=-1, keepdims=True)
    cqn = (cq * lax.rsqrt(ms + NORM_EPS) * nw_ref[...]).astype(BF16)
    cos, sin_signed = _rope_tables(pos_ref, freq_ref)
    w_nope = hw_ref[0:1, :]
    w_rope = hw_ref[1:2, :]
    w_swap = hw_ref[2:3, :]
    for h in range(MLA_HEADS):
        qh = _dot(cqn, wuq_ref[:, h * 3 * LANES:(h + 1) * 3 * LANES])
        nope = qh[:, :LANES]
        rope = qh[:, LANES:2 * LANES]
        swap = qh[:, 2 * LANES:]
        ss = jnp.sum(nope * nope + rope * rope, axis=-1, keepdims=True)
        fac = lax.rsqrt(ss * (1.0 / QK_HEAD) + NORM_EPS) * scale
        o_ref[:, h * 2 * LANES:h * 2 * LANES + LANES] = (nope * fac * w_nope).astype(o_ref.dtype)
        roped = (rope * w_rope * cos + swap * w_swap * sin_signed) * fac
        o_ref[:, h * 2 * LANES + LANES:(h + 1) * 2 * LANES] = roped.astype(o_ref.dtype)


def _mla_kv_body(ckv_ref, kpe_ref, nw_ref, wukv_ref, hw_ref, pos_ref, freq_ref, k_ref, v_ref):
    ckv = ckv_ref[...]
    ms = jnp.mean(ckv * ckv, axis=-1, keepdims=True)
    ckvn = (ckv * lax.rsqrt(ms + NORM_EPS) * nw_ref[...]).astype(BF16)
    cos, sin_signed = _rope_tables(pos_ref, freq_ref)
    kpe = kpe_ref[:, :LANES]
    kpe_swap = kpe_ref[:, LANES:]
    pe_ss = jnp.sum(kpe * kpe, axis=-1, keepdims=True)
    pe_roped = kpe * hw_ref[1:2, :] * cos + kpe_swap * hw_ref[2:3, :] * sin_signed
    w_nope = hw_ref[0:1, :]
    for h in range(MLA_HEADS):
        kn = _dot(ckvn, wukv_ref[:, h * LANES:(h + 1) * LANES])
        ss = jnp.sum(kn * kn, axis=-1, keepdims=True) + pe_ss
        fac = lax.rsqrt(ss * (1.0 / QK_HEAD) + NORM_EPS)
        k_ref[:, h * 2 * LANES:h * 2 * LANES + LANES] = (kn * fac * w_nope).astype(k_ref.dtype)
        k_ref[:, h * 2 * LANES + LANES:(h + 1) * 2 * LANES] = (pe_roped * fac).astype(k_ref.dtype)
    nv = MLA_HEADS * V_HEAD
    v_ref[...] = _dot(ckvn, wukv_ref[:, nv:]).astype(v_ref.dtype)


def _head_norm_rows(w):
    half = QK_ROPE // 2
    zeros = jnp.zeros((LANES - QK_ROPE,), F32)
    rope = jnp.concatenate([w[QK_NOPE:], zeros])
    swap = jnp.concatenate([w[QK_NOPE + half:], w[QK_NOPE:QK_NOPE + half], zeros])
    rows = jnp.stack([w[:QK_NOPE], rope, swap])
    return jnp.concatenate([rows, jnp.zeros((SUBLANES - 3, LANES), F32)], axis=0)


def _rope_freq_row():
    half = QK_ROPE // 2
    f = ROPE_THETA ** (-jnp.arange(half, dtype=F32) / half)
    return jnp.concatenate([f, f, jnp.zeros((LANES - QK_ROPE,), F32)]).reshape(1, LANES)


def _wuq_layout():
    half = QK_ROPE // 2
    idx, mask = [], []
    for h in range(MLA_HEADS):
        base = h * QK_HEAD
        nope = list(range(base, base + QK_NOPE))
        rope = list(range(base + QK_NOPE, base + QK_HEAD))
        swap = rope[half:] + rope[:half]
        pad = [0] * (LANES - QK_ROPE)
        idx += nope + rope + pad + swap + pad
        mask += [1] * QK_NOPE + [1] * QK_ROPE + [0] * len(pad) + [1] * QK_ROPE + [0] * len(pad)
    return np.asarray(idx, np.int32), np.asarray(mask, np.float32)


def _wukv_layout():
    per = QK_NOPE + V_HEAD
    k_idx = [h * per + i for h in range(MLA_HEADS) for i in range(QK_NOPE)]
    v_idx = [h * per + QK_NOPE + i for h in range(MLA_HEADS) for i in range(V_HEAD)]
    return np.asarray(k_idx + v_idx, np.int32)


def _mla_qkv(p, positions, q_norm_w, w_uq, kv_norm_w, w_ukv, qk_q_w, qk_k_w):
    t = p.shape[0]
    tm = _pick(t, (256, 128))
    idx, mask = _wuq_layout()
    wuq = (w_uq[:, idx] * mask[None, :]).astype(BF16)
    wukv = w_ukv[:, _wukv_layout()].astype(BF16)
    pos = positions.reshape(t, 1)
    freq = _rope_freq_row()
    hq = _head_norm_rows(qk_q_w)
    hk = _head_norm_rows(qk_k_w)
    qw = MLA_HEADS * 2 * LANES
    row = lambda i: (i, 0)
    fixed = lambda i: (0, 0)
    q = pl.pallas_call(
        functools.partial(_mla_q_body, scale=QK_HEAD ** -0.5),
        out_shape=jax.ShapeDtypeStruct((t, qw), BF16),
        grid=(t // tm,),
        in_specs=[
            pl.BlockSpec((tm, Q_LORA), lambda i: (i, P_CQ // Q_LORA)),
            pl.BlockSpec((1, Q_LORA), fixed),
            pl.BlockSpec(wuq.shape, fixed),
            pl.BlockSpec((SUBLANES, LANES), fixed),
            pl.BlockSpec((tm, 1), row),
            pl.BlockSpec((1, LANES), fixed),
        ],
        out_specs=pl.BlockSpec((tm, qw), row),
        name="mla_q",
        compiler_params=_cparams(("parallel",)),
    )(p, q_norm_w.reshape(1, Q_LORA), wuq, hq, pos, freq)
    k, v = pl.pallas_call(
        _mla_kv_body,
        out_shape=[jax.ShapeDtypeStruct((t, qw), BF16),
                   jax.ShapeDtypeStruct((t, MLA_HEADS * V_HEAD), BF16)],
        grid=(t // tm,),
        in_specs=[
            pl.BlockSpec((tm, KV_LORA), lambda i: (i, P_CKV // KV_LORA)),
            pl.BlockSpec((tm, 2 * LANES), lambda i: (i, P_KPE // (2 * LANES))),
            pl.BlockSpec((1, KV_LORA), fixed),
            pl.BlockSpec(wukv.shape, fixed),
            pl.BlockSpec((SUBLANES, LANES), fixed),
            pl.BlockSpec((tm, 1), row),
            pl.BlockSpec((1, LANES), fixed),
        ],
        out_specs=[pl.BlockSpec((tm, qw), row), pl.BlockSpec((tm, MLA_HEADS * V_HEAD), row)],
        name="mla_kv",
        compiler_params=_cparams(("parallel",)),
    )(p, p, kv_norm_w.reshape(1, KV_LORA), wukv, hk, pos, freq)
    return q, k, v


def _attn_body(qi_ref, ki_ref, q_ref, k_ref, v_ref, o_ref, m_ref, l_ref, acc_ref, *, tq):
    step_id = pl.program_id(2)
    qi = qi_ref[step_id]
    ki = ki_ref[step_id]

    @pl.when(ki == 0)
    def _():
        m_ref[...] = jnp.full(m_ref.shape, -jnp.inf, F32)
        l_ref[...] = jnp.zeros(l_ref.shape, F32)
        acc_ref[...] = jnp.zeros(acc_ref.shape, F32)

    heads = range(ATTN_HEADS_PER_STEP)
    qk = lambda h: slice(h * 2 * LANES, (h + 1) * 2 * LANES)
    vo = lambda h: slice(h * V_HEAD, (h + 1) * V_HEAD)

    def step(masked):
        ss = [_dot_nt(q_ref[:, qk(h)], k_ref[:, qk(h)]) for h in heads]
        if masked:
            row = lax.broadcasted_iota(I32, ss[0].shape, 0)
            col = lax.broadcasted_iota(I32, ss[0].shape, 1)
            ss = [jnp.where(col <= row, s, -jnp.inf) for s in ss]
        reps = tq // LANES
        prs, alphas = [], []
        for h, s in zip(heads, ss):
            m_prev = m_ref[h]
            m_new = jnp.maximum(m_prev, jnp.max(s, axis=-1, keepdims=True))
            alpha = jnp.exp(m_prev - m_new)
            pr = jnp.exp(s - jnp.tile(m_new, (1, reps)))
            l_ref[h] = alpha * l_ref[h] + jnp.sum(pr, axis=-1, keepdims=True)
            m_ref[h] = m_new
            prs.append(pr.astype(BF16))
            alphas.append(alpha)
        pvs = [_dot(pr, v_ref[:, vo(h)]) for h, pr in zip(heads, prs)]
        for h, alpha, pv in zip(heads, alphas, pvs):
            acc_ref[h] = alpha * acc_ref[h] + pv

    @pl.when(ki < qi)
    def _():
        step(False)

    @pl.when(ki == qi)
    def _():
        step(True)
        for h in heads:
            o_ref[:, vo(h)] = (acc_ref[h] / l_ref[h]).astype(o_ref.dtype)


def _attention(q, k, v, bsz, s):
    tq = _pick(s, (512, 256, 128))
    n = s // tq
    pairs = [(i, j) for i in range(n) for j in range(i + 1)]
    qi_tab = jnp.asarray([i for i, _ in pairs], I32)
    ki_tab = jnp.asarray([j for _, j in pairs], I32)
    nh = ATTN_HEADS_PER_STEP
    grid_spec = pltpu.PrefetchScalarGridSpec(
        num_scalar_prefetch=2,
        grid=(bsz, MLA_HEADS // nh, len(pairs)),
        in_specs=[
            pl.BlockSpec((None, tq, nh * 2 * LANES), lambda b, h, p, qi, ki: (b, qi[p], h)),
            pl.BlockSpec((None, tq, nh * 2 * LANES), lambda b, h, p, qi, ki: (b, ki[p], h)),
            pl.BlockSpec((None, tq, nh * V_HEAD), lambda b, h, p, qi, ki: (b, ki[p], h)),
        ],
        out_specs=pl.BlockSpec((None, tq, nh * V_HEAD), lambda b, h, p, qi, ki: (b, qi[p], h)),
        scratch_shapes=[pltpu.VMEM((nh, tq, LANES), F32), pltpu.VMEM((nh, tq, LANES), F32),
                        pltpu.VMEM((nh, tq, V_HEAD), F32)],
    )
    return pl.pallas_call(
        functools.partial(_attn_body, tq=tq),
        out_shape=jax.ShapeDtypeStruct((bsz, s, MLA_HEADS * V_HEAD), BF16),
        grid_spec=grid_spec,
        name="mla_attention",
        compiler_params=_cparams(("parallel", "parallel", "arbitrary")),
    )(qi_tab, ki_tab, q, k, v)


def _rwkv_body(r_ref, k_ref, v_ref, xl_ref, mu_r_ref, mu_k_ref, mu_v_ref, mu_l_ref, vec_ref,
               w2a2_ref, g2_ref, o_ref, state_ref, pr_ref, pk_ref, pv_ref, pl_ref, *, tb, npairs):
    t_idx = pl.program_id(2)

    @pl.when(t_idx == 0)
    def _():
        state_ref[...] = jnp.zeros(state_ref.shape, F32)
        pr_ref[...] = jnp.zeros(pr_ref.shape, F32)
        pk_ref[...] = jnp.zeros(pk_ref.shape, F32)
        pv_ref[...] = jnp.zeros(pv_ref.shape, F32)
        pl_ref[...] = jnp.zeros(pl_ref.shape, F32)

    def mix(x_ref, prev_ref, mu_ref):
        x = x_ref[...]
        row = lax.broadcasted_iota(I32, x.shape, 0)
        shifted = jnp.where(row == 0, prev_ref[...], pltpu.roll(x, 1, axis=0))
        prev_ref[...] = x[tb - 1:tb, :]
        return x + (shifted - x) * mu_ref[...]

    r = mix(r_ref, pr_ref, mu_r_ref)
    k = mix(k_ref, pk_ref, mu_k_ref)
    v = mix(v_ref, pv_ref, mu_v_ref)
    xl = mix(xl_ref, pl_ref, mu_l_ref)

    lane = lax.broadcasted_iota(I32, (tb, LANES), 1)
    first = lane < W_LORA
    x0 = xl[:, :LANES]
    lhs_w = jnp.where(first, jnp.tanh(x0), 0.0).astype(BF16)
    lhs_a = jnp.where(first, 0.0, x0).astype(BF16)
    xg = xl[:, LANES:]
    lhs_g = (1.0 / (1.0 + jnp.exp(-xg))).astype(BF16)
    cols = [slice(p * LANES, (p + 1) * LANES) for p in range(npairs)]
    outs = _rwkv_pairs([r[:, c] for c in cols], [k[:, c] for c in cols], [v[:, c] for c in cols],
                       lhs_w, lhs_a, lhs_g, [vec_ref[:, c] for c in cols],
                       [w2a2_ref[:, c] for c in cols], [g2_ref[:, c] for c in cols],
                       state_ref, tb)
    for c, out in zip(cols, outs):
        o_ref[:, c] = out.astype(o_ref.dtype)


def _rwkv_pairs(rs, ks, vs_in, lhs_w, lhs_a, lhs_g, vecs, w2a2s, g2s, state_ref, tb):
    c = RWKV_CHUNK
    pairs = range(len(rs))
    chunks = range(tb // c)
    each = lambda f, *lists: [f(*args) for args in zip(*lists)]

    ri = lax.broadcasted_iota(I32, (LANES, LANES), 0)
    ci = lax.broadcasted_iota(I32, (LANES, LANES), 1)
    same = (ri // RWKV_HEAD) == (ci // RWKV_HEAD)
    ones_bd = jnp.where(same, 1.0, 0.0).astype(BF16)
    tri_r = lax.broadcasted_iota(I32, (c, c), 0)
    tri_c = lax.broadcasted_iota(I32, (c, c), 1)
    tri = jnp.where(tri_c <= tri_r, 1.0, 0.0).astype(BF16)
    eye = jnp.where(ri == ci, 1.0, 0.0)
    first_c = lax.broadcasted_iota(I32, (c, LANES), 1) < RWKV_HEAD

    def stack(z):
        return jnp.concatenate([jnp.where(first_c, z, 0.0), jnp.where(first_c, 0.0, z)],
                               axis=0).astype(BF16)

    def split(x):
        hi = x.astype(BF16)
        return hi, (x - hi.astype(F32)).astype(BF16)

    def group_sum(xs):
        parts = each(split, xs)
        his = [_dot(hi, ones_bd) for hi, _ in parts]
        los = [_dot(lo, ones_bd) for _, lo in parts]
        return each(lambda a, b: a + b, his, los)

    lws = [_dot(lhs_w, w) for w in w2a2s]
    las = [_dot(lhs_a, w) for w in w2a2s]
    gs = [_dot(lhs_g, g2) for g2 in g2s]
    kks = each(lambda k, vec: k * vec[2:3, :], ks, vecs)
    nrms = group_sum(each(lambda kk: kk * kk, kks))
    kks = each(lambda kk, n: kk / jnp.maximum(jnp.sqrt(n), 1e-12), kks, nrms)

    def decay_and_gate(lw, la, vec):
        zw = -(vec[0:1, :] + lw)
        softplus = jnp.maximum(zw, 0.0) + jnp.log(1.0 + jnp.exp(-jnp.abs(zw)))
        a = 1.0 / (1.0 + jnp.exp(-(vec[1:2, :] + la)))
        return -jnp.exp(-softplus - 0.5), a

    dg = each(decay_and_gate, lws, las, vecs)
    logds = [d for d, _ in dg]
    a_gates = [a for _, a in dg]
    kps = each(lambda k, a, vec: k * (1.0 + (a - 1.0) * vec[3:4, :]), ks, a_gates, vecs)
    bvecs = each(lambda kk, a: kk * a, kks, a_gates)
    bonus_sums = group_sum(each(lambda r, kp, vec: r * kp * vec[4:5, :], rs, kps, vecs))

    inst = [(p, ch) for p in pairs for ch in chunks]
    rows = lambda x, ch: x[ch * c:(ch + 1) * c]
    lds = [rows(logds[p], ch) for p, ch in inst]
    ld_parts = each(split, lds)
    lc_hi = [_dot(tri, hi) for hi, _ in ld_parts]
    lc_lo = [_dot(tri, lo) for _, lo in ld_parts]
    lcs = each(lambda a, b: a + b, lc_hi, lc_lo)
    ltots = [lc[c - 1:c, :] for lc in lcs]
    e_negs = [jnp.exp(-lc) for lc in lcs]
    e_rems = each(lambda lt, lc: jnp.exp(lt - lc), ltots, lcs)
    xas = [stack(-rows(kks[p], ch) * jnp.exp(lc - ld)) for (p, ch), lc, ld in zip(inst, lcs, lds)]
    xrs = [stack(rows(rs[p], ch) * jnp.exp(lc)) for (p, ch), lc in zip(inst, lcs)]
    ybs = [stack(rows(bvecs[p], ch) * e) for (p, ch), e in zip(inst, e_negs)]
    yks = [stack(rows(kps[p], ch) * e) for (p, ch), e in zip(inst, e_negs)]
    vss = [stack(rows(vs_in[p], ch)) for p, ch in inst]
    bhs = [stack(rows(bvecs[p], ch) * e) for (p, ch), e in zip(inst, e_rems)]
    khs = [stack(rows(kps[p], ch) * e) for (p, ch), e in zip(inst, e_rems)]

    n2 = 2 * c
    xars = each(lambda a, b: jnp.concatenate([a, b], axis=0), xas, xrs)
    rr = lax.broadcasted_iota(I32, (2 * n2, n2), 0)
    cc = lax.broadcasted_iota(I32, (2 * n2, n2), 1)
    rr_in = rr % n2
    tri_mask = ((rr_in // c) == (cc // c)) & ((cc % c) < (rr_in % c) + rr // n2)
    mbs = each(lambda a, b: jnp.where(tri_mask, _dot_nt(a, b), 0.0), xars, ybs)
    mks = each(lambda a, b: jnp.where(tri_mask, _dot_nt(a, b), 0.0).astype(BF16), xars, yks)
    labs = [mb[:n2] for mb in mbs]
    mrbs = [mb[n2:].astype(BF16) for mb in mbs]

    pbs = [lab.astype(BF16) for lab in labs]
    tinvs = [eye + lab for lab in labs]
    pbs = [_dot(pb, pb).astype(BF16) for pb in pbs]
    n_sq = 2
    while n_sq * 2 < c:
        both = each(lambda pb, ti: _dot(pb, jnp.concatenate([ti.astype(BF16), pb], axis=1)),
                    pbs, tinvs)
        tinvs = each(lambda ti, bo: ti + bo[:, :n2], tinvs, both)
        pbs = [bo[:, n2:].astype(BF16) for bo in both]
        n_sq *= 2
    tinvs = each(lambda ti, pb: (ti + _dot(pb, ti.astype(BF16))).astype(BF16), tinvs, pbs)
    mkvs = each(_dot, mks, vss)
    kvs = each(_dot_tn, vss, khs)

    states = [state_ref[p] for p in pairs]
    ys = [[] for _ in pairs]
    n_ch = len(chunks)
    for ch in chunks:
        ids = [p * n_ch + ch for p in pairs]
        sbs = [s.astype(BF16) for s in states]
        withs = [_dot_nt(xars[i], sb) + mkvs[i] for i, sb in zip(ids, sbs)]
        ubs = [_dot(tinvs[i], w[:n2].astype(BF16)).astype(BF16) for i, w in zip(ids, withs)]
        states = [s * jnp.exp(ltots[i]) + _dot_tn(ub, bhs[i]) + kvs[i]
                  for i, s, ub in zip(ids, states, ubs)]
        for p, i, w, ub in zip(pairs, ids, withs, ubs):
            ystack = w[n2:] + _dot(mrbs[i], ub)
            ys[p].append(ystack[:c] + ystack[c:])
    for p, s in zip(pairs, states):
        state_ref[p] = s

    ycat = [jnp.concatenate(y, axis=0) for y in ys]
    inv_n = 1.0 / RWKV_HEAD
    means = [m * inv_n for m in group_sum(ycat)]
    ds = each(lambda y, m: y - m, ycat, means)
    variances = [v * inv_n for v in group_sum(each(lambda d: d * d, ds))]
    outs = []
    for d, var, vec, bs, v, g in zip(ds, variances, vecs, bonus_sums, vs_in, gs):
        yn = d * lax.rsqrt(var + LNX_EPS) * vec[5:6, :] + vec[6:7, :]
        outs.append((yn + bs * v) * g)
    return outs


def _rwkv(p_rkv, p_mla, mu_rkv, mu_lora, vecs, w2a2, g2):
    assert W_LORA + A_LORA == LANES and G_LORA == LANES
    bsz, s, _ = p_rkv.shape
    tb = _pick(s, (512, 256, 128, 64))
    npairs = RWKV_PAIRS_PER_STEP
    wd = npairs * LANES
    groups = RWKV_HEADS // 2 // npairs
    colblk = lambda off: (lambda b, hp, t: (b, t, off // wd + hp))
    mublk = lambda off: (lambda b, hp, t: (0, off // wd + hp))
    o_r, o_k, o_v = 0, RWKV_WIDTH, 2 * RWKV_WIDTH
    return pl.pallas_call(
        functools.partial(_rwkv_body, tb=tb, npairs=npairs),
        out_shape=jax.ShapeDtypeStruct((bsz, s, RWKV_WIDTH), BF16),
        grid=(bsz, groups, s // tb),
        in_specs=[
            pl.BlockSpec((None, tb, wd), colblk(o_r)),
            pl.BlockSpec((None, tb, wd), colblk(o_k)),
            pl.BlockSpec((None, tb, wd), colblk(o_v)),
            pl.BlockSpec((None, tb, 2 * LANES), lambda b, hp, t: (b, t, P_LORA // (2 * LANES))),
            pl.BlockSpec((1, wd), mublk(o_r)),
            pl.BlockSpec((1, wd), mublk(o_k)),
            pl.BlockSpec((1, wd), mublk(o_v)),
            pl.BlockSpec((1, 2 * LANES), lambda b, hp, t: (0, 0)),
            pl.BlockSpec((SUBLANES, wd), lambda b, hp, t: (0, hp)),
            pl.BlockSpec((LANES, wd), lambda b, hp, t: (0, hp)),
            pl.BlockSpec((LANES, wd), lambda b, hp, t: (0, hp)),
        ],
        out_specs=pl.BlockSpec((None, tb, wd), lambda b, hp, t: (b, t, hp)),
        scratch_shapes=[pltpu.VMEM((npairs, LANES, LANES), F32),
                        pltpu.VMEM((1, wd), F32), pltpu.VMEM((1, wd), F32),
                        pltpu.VMEM((1, wd), F32), pltpu.VMEM((1, 2 * LANES), F32)],
        name="rwkv7",
        compiler_params=_cparams(("parallel", "parallel", "arbitrary")),
    )(p_rkv, p_rkv, p_rkv, p_mla, mu_rkv, mu_rkv, mu_rkv, mu_lora, vecs, w2a2, g2)


def _conv_body(h_ref, wb_ref, wc_ref, wu_ref, cw_ref, o_ref, tail_ref, *, tm, seq):
    i = pl.program_id(1)
    a = h_ref[...]
    gate_b = _dot(a, wb_ref[...])
    uu = _dot(a, wc_ref[...]) * _dot(a, wu_ref[...])

    @pl.when((i * tm) % seq == 0)
    def _():
        tail_ref[...] = jnp.zeros(tail_ref.shape, F32)

    tail = tail_ref[...]
    row = lax.broadcasted_iota(I32, uu.shape, 0)
    h1 = tail[SUBLANES - 1:SUBLANES, :]
    h2 = tail[SUBLANES - 2:SUBLANES - 1, :]
    s1 = jnp.where(row == 0, h1, pltpu.roll(uu, 1, axis=0))
    s2 = jnp.where(row == 0, h2, jnp.where(row == 1, h1, pltpu.roll(uu, 2, axis=0)))
    y = cw_ref[0:1, :] * s2 + cw_ref[1:2, :] * s1 + cw_ref[2:3, :] * uu
    o_ref[...] = (gate_b * y).astype(o_ref.dtype)
    tail_ref[...] = uu[tm - SUBLANES:tm, :]


def _conv_mixer(h, w_in, conv_w, s):
    t, d = h.shape
    tm = _pick(s, (512, 256, 128))
    tn = 512
    nblk = d // tn
    w = w_in.astype(BF16)
    cw = jnp.concatenate([conv_w, jnp.zeros((SUBLANES - CONV_WIDTH, d), F32)], axis=0)
    wspec = lambda off: pl.BlockSpec((d, tn), lambda j, i: (0, j + off))
    return pl.pallas_call(
        functools.partial(_conv_body, tm=tm, seq=s),
        out_shape=jax.ShapeDtypeStruct((t, d), BF16),
        grid=(nblk, t // tm),
        in_specs=[pl.BlockSpec((tm, d), lambda j, i: (i, 0)),
                  wspec(0), wspec(nblk), wspec(2 * nblk),
                  pl.BlockSpec((SUBLANES, tn), lambda j, i: (0, j))],
        out_specs=pl.BlockSpec((tm, tn), lambda j, i: (i, j)),
        scratch_shapes=[pltpu.VMEM((SUBLANES, tn), F32)],
        name="conv_mixer",
        compiler_params=_cparams(("parallel", "arbitrary")),
    )(h, w, w, w, cw)


def _rank_pairs():
    kt = PEER_TOPK
    return [(a, b) for a in range(kt) for b in range(kt) if (a + 1) * (b + 1) <= kt]


PAIR_ROWS = -(-len(_rank_pairs()) // SUBLANES) * SUBLANES


def _pair_matrices():
    m1 = np.zeros((PAIR_ROWS, PEER_TOPK), np.float32)
    m2 = np.zeros((PAIR_ROWS, PEER_TOPK), np.float32)
    for p, (a, b) in enumerate(_rank_pairs()):
        m1[p, a] = 1.0
        m2[p, b] = 1.0
    return m1, m2


def _exact_dot_left(m, x):
    hi = x.astype(BF16)
    r1 = x - hi.astype(F32)
    mid = r1.astype(BF16)
    lo = (r1 - mid.astype(F32)).astype(BF16)
    return (_dot(m, hi) + _dot(m, mid)) + _dot(m, lo)


def _extract_top_rows(xs, row_iota, n, rank_iota, extras=None):
    xs = list(xs)
    sentinel = float(xs[0].shape[0])
    zeros = lambda: [jnp.zeros((n, x.shape[1]), F32) for x in xs]
    vals, idxs, exts = zeros(), zeros(), zeros()
    for kk in range(n):
        here = rank_iota == kk
        for j, x in enumerate(xs):
            m = jnp.max(x, axis=0, keepdims=True)
            idx = jnp.min(jnp.where(x == m, row_iota, sentinel), axis=0, keepdims=True)
            hit = row_iota == idx
            if extras is not None:
                ext = jnp.max(jnp.where(hit, extras[j], -1.0), axis=0, keepdims=True)
                exts[j] = jnp.where(here, ext, exts[j])
            xs[j] = jnp.where(hit, -jnp.inf, x)
            vals[j] = jnp.where(here, m, vals[j])
            idxs[j] = jnp.where(here, idx, idxs[j])
    return list(zip(vals, idxs, exts))


def _route_body(q_ref, keys_ref, m1_ref, m2_ref, e1_ref, e2_ref, gw_ref):
    kt = PEER_TOPK
    tb = q_ref.shape[0]
    nh = ROUTE_HEADS_PER_STEP
    key_iota = lax.broadcasted_iota(I32, (N_KEYS, tb), 0).astype(F32)
    rank_iota = lax.broadcasted_iota(I32, (kt, tb), 0)
    scs = []
    for hp in range(2 * nh):
        qh = q_ref[:, hp * PEER_HALF:(hp + 1) * PEER_HALF].astype(BF16)
        scs.append(_dot_nt(keys_ref[hp], qh))
    tops = _extract_top_rows(scs, key_iota, kt, rank_iota)
    m1 = m1_ref[...]
    m2 = m2_ref[...]
    pair_iota = lax.broadcasted_iota(I32, (PAIR_ROWS, tb), 0).astype(F32)
    cands, eids = [], []
    for h in range(nh):
        (sv1, si1, _), (sv2, si2, _) = tops[2 * h], tops[2 * h + 1]
        cand = _exact_dot_left(m1, sv1) + _exact_dot_left(m2, sv2)
        cands.append(jnp.where(pair_iota < float(len(_rank_pairs())), cand, -jnp.inf))
        eids.append(_dot(m1, si1.astype(BF16)) * float(N_KEYS) + _dot(m2, si2.astype(BF16)))
    picks = _extract_top_rows(cands, pair_iota, kt, rank_iota, extras=eids)
    for h, (cv, _, ev) in enumerate(picks):
        rows = slice(h * kt, (h + 1) * kt)
        ex = jnp.exp(cv - cv[0:1, :])
        gw_ref[rows, :] = ex / jnp.sum(ex, axis=0, keepdims=True)
        ei = ev.astype(I32)
        e1_ref[rows, :] = (ei >> 7).astype(F32)
        e2_ref[rows, :] = (ei & (N_KEYS - 1)).astype(F32)


def _peer_route(q, keys):
    t = q.shape[0]
    tb = LANES
    m1, m2 = _pair_matrices()
    slots = PEER_HEADS * PEER_TOPK
    out = jax.ShapeDtypeStruct((slots, t), F32)
    nh = ROUTE_HEADS_PER_STEP
    pair_spec = pl.BlockSpec((PAIR_ROWS, PEER_TOPK), lambda i, h: (0, 0))
    out_spec = pl.BlockSpec((nh * PEER_TOPK, tb), lambda i, h: (h, i))
    return pl.pallas_call(
        _route_body,
        out_shape=[out, out, out],
        grid=(t // tb, PEER_HEADS // nh),
        in_specs=[pl.BlockSpec((tb, nh * 2 * PEER_HALF), lambda i, h: (i, h)),
                  pl.BlockSpec((nh * 2, N_KEYS, PEER_HALF), lambda i, h: (h, 0, 0)),
                  pair_spec, pair_spec],
        out_specs=[out_spec, out_spec, out_spec],
        name="peer_route",
        compiler_params=_cparams(("parallel", "parallel")),
    )(q, keys, jnp.asarray(m1, BF16), jnp.asarray(m2, BF16))


def _swap_token_and_row(xs):
    xs = list(xs)
    row = lax.broadcasted_iota(I32, xs[0].shape, 1)
    d = SUBLANES // 2
    while d >= 1:
        low = (row % (2 * d)) < d
        for s in range(SUBLANES):
            if (s // d) % 2 == 0:
                a, b = xs[s], xs[s + d]
                xs[s] = jnp.where(low, a, pltpu.roll(b, d, axis=1))
                xs[s + d] = jnp.where(low, pltpu.roll(a, SUBLANES - d, axis=1), b)
        d //= 2
    return xs


GATE_TOKENS = 2 * SUBLANES


def _gate_body(e1t_ref, e2t_ref, gwt_ref, uf_ref, o_ref, ub_ref, e1_s, e2_s, gw_s):
    ub_ref[...] = uf_ref[...].astype(ub_ref.dtype)
    tb = o_ref.shape[2]
    e1_s[...] = e1t_ref[...].T
    e2_s[...] = e2t_ref[...].T
    gw_s[...] = gwt_ref[...].T
    sub = lax.broadcasted_iota(I32, (N_KEYS, LANES), 0).astype(F32)
    nblk = N_KEYS // SUBLANES

    def eight_tokens(t0):
        lhs, rhs = [], []
        for s in range(SUBLANES):
            e1 = e1_s[pl.ds(t0 + s, 1), :]
            e2 = e2_s[pl.ds(t0 + s, 1), :]
            gw = gw_s[pl.ds(t0 + s, 1), :]
            lhs.append(jnp.where(e1 == sub, gw, 0.0).astype(BF16))
            rhs.append(jnp.where(e2 == sub, 1.0, 0.0).astype(BF16))
        per_token = [_dot_nt(a, b).reshape(nblk, SUBLANES, N_KEYS) for a, b in zip(lhs, rhs)]
        return _swap_token_and_row(per_token)

    def sixteen_tokens(g, carry):
        t0 = pl.multiple_of(g * GATE_TOKENS, GATE_TOKENS)
        first, second = eight_tokens(t0), eight_tokens(t0 + SUBLANES)
        for r, (y0, y1) in enumerate(zip(first, second)):
            o_ref[:, r, pl.ds(t0, GATE_TOKENS), :] = jnp.concatenate(
                [y0, y1], axis=1).astype(o_ref.dtype)
        return carry

    lax.fori_loop(0, tb // GATE_TOKENS, sixteen_tokens, 0, unroll=4)


def _peer_gates(e1t, e2t, gwt, u_all, layer):
    t = e1t.shape[1]
    tb = LANES
    steps = t // tb
    nblk = N_KEYS // SUBLANES
    n_exp, d = u_all.shape[1:]
    rows = n_exp // steps
    assert rows * steps == n_exp and rows % GATE_TOKENS == 0
    in_spec = pl.BlockSpec((LANES, tb), lambda i: (0, i))
    g, u_bf16 = pl.pallas_call(
        _gate_body,
        out_shape=[jax.ShapeDtypeStruct((nblk, SUBLANES, t, N_KEYS), BF16),
                   jax.ShapeDtypeStruct((n_exp, d), BF16)],
        grid=(steps,),
        in_specs=[in_spec, in_spec, in_spec,
                  pl.BlockSpec((None, rows, d), lambda i: (layer, i, 0))],
        out_specs=[pl.BlockSpec((nblk, SUBLANES, tb, N_KEYS), lambda i: (0, 0, i, 0)),
                   pl.BlockSpec((rows, d), lambda i: (i, 0))],
        scratch_shapes=[pltpu.VMEM((tb, LANES), F32)] * 3,
        name="peer_gates",
        compiler_params=_cparams(("parallel",)),
    )(e1t, e2t, gwt, u_all)
    return g.reshape(N_KEYS, t, N_KEYS), u_bf16


def _peer_up_body(h_ref, u_ref, g_ref, o_ref):
    s = _dot_nt(h_ref[...], u_ref[...])
    act = 0.5 * s * (1.0 + lax.erf(s * (2.0 ** -0.5)))
    for j in range(o_ref.shape[1] // N_KEYS):
        cols = slice(j * N_KEYS, (j + 1) * N_KEYS)
        o_ref[:, cols] = (act[:, cols] * g_ref[j]).astype(o_ref.dtype)


def _peer_up(h, u_tab, g3):
    t, d = h.shape
    n_exp = u_tab.shape[0]
    tm = _pick(t, (1024, 512, 256, 128))
    tn = SUBLANES * N_KEYS
    return pl.pallas_call(
        _peer_up_body,
        out_shape=jax.ShapeDtypeStruct((t, n_exp), BF16),
        grid=(t // tm, n_exp // tn),
        in_specs=[pl.BlockSpec((tm, d), lambda i, j: (i, 0), pipeline_mode=pl.Buffered(1)),
                  pl.BlockSpec((tn, d), lambda i, j: (j, 0)),
                  pl.BlockSpec((tn // N_KEYS, tm, N_KEYS), lambda i, j: (j, i, 0))],
        out_specs=pl.BlockSpec((tm, tn), lambda i, j: (i, j)),
        name="peer_up",
        compiler_params=_cparams(("parallel", "parallel")),
    )(h, u_tab, g3)


def _residual(acc, x, gate):
    return x + gate * acc


def _gated_residual_matmul(a, w_all, layer, x2, gate, s, *, tk, name):
    t, d = x2.shape
    bsz = gate.shape[0]
    tm = _pick(s, (1024, 512, 256, 128))
    tn = 1024
    return _matmul(
        a, w_all, tm=tm, tn=tn, tk=tk, out_dtype=F32, epilogue=_residual,
        extras=[(x2, (tm, tn), lambda i, j: (i, j)),
                (gate.reshape(bsz, 1, d), (None, 1, tn), lambda i, j: (i * tm // s, 0, j))],
        b_lead=layer, name=name)


def _peer_ffn(x, w_norm, shift, scale, gate, layer, w_q_all, keys, u_all, v_all):
    bsz, s, d = x.shape
    t = bsz * s
    h = _norm_mod(x, w_norm, shift, scale).reshape(t, d)
    tm = _pick(s, (1024, 512, 256, 128))
    q = _matmul(h, w_q_all, tm=tm, tn=1024, tk=d, out_dtype=F32, b_lead=layer,
                name="peer_query")
    keys_b = keys.reshape(PEER_HEADS * 2, N_KEYS, PEER_HALF).astype(BF16)
    e1t, e2t, gwt = _peer_route(q, keys_b)
    g3, u_tab = _peer_gates(e1t, e2t, gwt, u_all, layer)
    pmat = _peer_up(h, u_tab, g3)
    out = _gated_residual_matmul(pmat, v_all, layer, x.reshape(t, d), gate, s,
                                 tk=2048, name="peer_down")
    return out.reshape(bsz, s, d)


def _pad_mla_proj(w):
    o_kv = Q_LORA
    o_pe = Q_LORA + KV_LORA
    o_rw = o_pe + QK_ROPE
    half = QK_ROPE // 2
    z = lambda n: jnp.zeros((w.shape[0], n), w.dtype)
    cq = w[:, :Q_LORA]
    ckv = w[:, o_kv:o_pe]
    kpe = w[:, o_pe:o_rw]
    kpe_swap = jnp.concatenate([kpe[:, half:], kpe[:, :half]], axis=1)
    lora = w[:, o_rw + 3 * RWKV_WIDTH:]
    pad = LANES - QK_ROPE
    out = jnp.concatenate([ckv, kpe, z(pad), kpe_swap, z(pad), cq, lora,
                           z(P_WIDTH - P_LORA - lora.shape[1])], axis=1)
    assert out.shape[1] == P_WIDTH
    return out


def _hybrid_mixer(h, positions, w_in, q_norm_w, w_uq, kv_norm_w, w_ukv, qk_q_w, qk_k_w,
                  mu, w0, w2, a0, a2, g2, k_k, k_a, r_k, lnx_w, lnx_b):
    bsz, s, d = h.shape
    t = bsz * s
    tm = _pick(s, (1024, 512, 256, 128))
    o_rw = Q_LORA + KV_LORA + QK_ROPE
    n_rkv = 3 * RWKV_WIDTH
    h2 = h.reshape(t, d)
    p_mla = _matmul(h2, _pad_mla_proj(w_in).astype(BF16), tm=tm, tn=1024, tk=d, out_dtype=F32,
                    name="hyb_in_proj_mla")
    p_rkv = _matmul(h2, w_in[:, o_rw:o_rw + n_rkv].astype(BF16), tm=tm, tn=1024, tk=d,
                    out_dtype=F32, name="hyb_in_proj_rkv")
    q, k, v = _mla_qkv(p_mla, positions, q_norm_w, w_uq, kv_norm_w, w_ukv, qk_q_w, qk_k_w)
    y_mla = _attention(q.reshape(bsz, s, -1), k.reshape(bsz, s, -1), v.reshape(bsz, s, -1), bsz, s)

    zrow = jnp.zeros((RWKV_WIDTH,), F32)
    vecs = jnp.stack([w0, a0, k_k, k_a, r_k.reshape(-1), lnx_w, lnx_b, zrow])
    w2a2 = jnp.concatenate([w2, a2], axis=0).astype(BF16)
    y_rwkv = _rwkv(p_rkv.reshape(bsz, s, n_rkv), p_mla.reshape(bsz, s, P_WIDTH),
                   mu[:n_rkv].reshape(1, n_rkv), mu[n_rkv:].reshape(1, -1), vecs, w2a2,
                   g2.astype(BF16))
    return jnp.concatenate([y_mla, y_rwkv], axis=-1).reshape(t, -1)


def kernel(x, c, positions, ada_w, ada_b, norm_mix_w, norm_ffn_w, hyb_w_in, mla_q_norm_w, mla_w_uq, mla_kv_norm_w, mla_w_ukv, mla_qk_q_w, mla_qk_k_w, rwkv_mu, rwkv_w0, rwkv_w2, rwkv_a0, rwkv_a2, rwkv_g2, rwkv_k_k, rwkv_k_a, rwkv_r_k, rwkv_lnx_w, rwkv_lnx_b, hyb_w_out, conv_w_in, conv_w, conv_w_out, peer_w_q, peer_keys, peer_u, peer_v):
    bsz, s, d = x.shape
    t = bsz * s
    depth = ada_w.shape[0]
    mod = _ada_mod(c, ada_w, ada_b)
    w_q_all = peer_w_q.astype(BF16)
    u_all = peer_u
    v_all = peer_v
    for layer in range(depth):
        sh_m, sc_m, gt_m, sh_f, sc_f, gt_f = [mod[layer, :, j * d:(j + 1) * d] for j in range(6)]
        h = _norm_mod(x, norm_mix_w[layer], sh_m, sc_m)
        i = layer // 2
        if layer % 2 == 0:
            y = _hybrid_mixer(h, positions, hyb_w_in[i], mla_q_norm_w[i], mla_w_uq[i],
                              mla_kv_norm_w[i], mla_w_ukv[i], mla_qk_q_w[i], mla_qk_k_w[i],
                              rwkv_mu[i], rwkv_w0[i], rwkv_w2[i], rwkv_a0[i], rwkv_a2[i],
                              rwkv_g2[i], rwkv_k_k[i], rwkv_k_a[i], rwkv_r_k[i],
                              rwkv_lnx_w[i], rwkv_lnx_b[i])
            w_out = hyb_w_out.astype(BF16)
        else:
            y = _conv_mixer(h.reshape(t, d), conv_w_in[i], conv_w[i], s)
            w_out = conv_w_out.astype(BF16)
        x = _gated_residual_matmul(y, w_out, i, x.reshape(t, d), gt_m, s, tk=d,
                                   name="mixer_out_proj").reshape(bsz, s, d)
        x = _peer_ffn(x, norm_ffn_w[layer], sh_f, sc_f, gt_f, layer, w_q_all, peer_keys[layer],
                      u_all, v_all)
    return x
```

```python
import functools

import numpy as np
import jax
import jax.numpy as jnp
from jax import lax
from jax.experimental import pallas as pl
from jax.experimental.pallas import tpu as pltpu

F32 = jnp.float32
BF16 = jnp.bfloat16
I32 = jnp.int32

MLA_HEADS = 16
QK_NOPE = 128
QK_ROPE = 64
QK_HEAD = QK_NOPE + QK_ROPE
V_HEAD = 128
Q_LORA = 768
KV_LORA = 512
ROPE_THETA = 10000.0
RWKV_HEAD = 64
RWKV_HEADS = 32
RWKV_WIDTH = RWKV_HEADS * RWKV_HEAD
W_LORA = 64
A_LORA = 64
G_LORA = 128
LNX_EPS = 64e-5
CONV_WIDTH = 3
PEER_HEADS = 8
N_KEYS = 128
PEER_TOPK = 16
PEER_HALF = 128
NORM_EPS = 1e-6

LANES = 128
SUBLANES = 8
VMEM_LIMIT = 56 * 1024 * 1024

P_CKV = 0
P_KPE = 512
P_CQ = 768
P_LORA = 1536
P_WIDTH = 2048
RWKV_CHUNK = 64
RWKV_PAIRS_PER_STEP = 8
ATTN_HEADS_PER_STEP = 4
ROUTE_HEADS_PER_STEP = 4


def _cparams(sem):
    return pltpu.CompilerParams(dimension_semantics=sem, vmem_limit_bytes=VMEM_LIMIT)


def _dot(a, b):
    return lax.dot_general(a, b, (((1,), (0,)), ((), ())), preferred_element_type=F32)


def _dot_nt(a, b):
    return lax.dot_general(a, b, (((1,), (1,)), ((), ())), preferred_element_type=F32)


def _dot_tn(a, b):
    return lax.dot_general(a, b, (((0,), (0,)), ((), ())), preferred_element_type=F32)


def _mm_body(*refs, nk, n_extra, epilogue):
    a_ref, b_ref = refs[:2]
    extra = refs[2:2 + n_extra]
    o_ref = refs[2 + n_extra]
    part = _dot(a_ref[...].astype(BF16), b_ref[...].astype(BF16))

    def finish(acc):
        o_ref[...] = epilogue(acc, *[e[...] for e in extra]).astype(o_ref.dtype)

    if nk == 1:
        finish(part)
        return
    acc_ref = refs[3 + n_extra]
    k = pl.program_id(2)

    @pl.when(k == 0)
    def _():
        acc_ref[...] = part

    @pl.when(k > 0)
    def _():
        acc_ref[...] += part

    @pl.when(k == nk - 1)
    def _():
        finish(acc_ref[...])


def _matmul(a, b, *, tm, tn, tk, out_dtype, epilogue=None, extras=(), b_lead=None,
            name="matmul"):
    m, kdim = a.shape
    n = b.shape[-1]
    nk = kdim // tk
    assert m % tm == 0 and kdim % tk == 0 and n % tn == 0
    if epilogue is None:
        epilogue = lambda acc: acc
    if b_lead is None:
        b_spec = pl.BlockSpec((tk, tn), lambda i, j, k: (k, j))
    else:
        b_spec = pl.BlockSpec((None, tk, tn), lambda i, j, k: (b_lead, k, j))
    in_specs = [pl.BlockSpec((tm, tk), lambda i, j, k: (i, k)), b_spec]
    extra_arrays = []
    for arr, bshape, imap in extras:
        in_specs.append(pl.BlockSpec(bshape, lambda i, j, k, imap=imap: imap(i, j)))
        extra_arrays.append(arr)
    return pl.pallas_call(
        functools.partial(_mm_body, nk=nk, n_extra=len(extras), epilogue=epilogue),
        out_shape=jax.ShapeDtypeStruct((m, n), out_dtype),
        grid=(m // tm, n // tn, nk),
        in_specs=in_specs,
        out_specs=pl.BlockSpec((tm, tn), lambda i, j, k: (i, j)),
        scratch_shapes=[pltpu.VMEM((tm, tn), F32)] if nk > 1 else [],
        name=name,
        compiler_params=_cparams(("parallel", "parallel", "arbitrary")),
    )(a, b, *extra_arrays)


def _pick(n, prefs):
    for p in prefs:
        if n % p == 0:
            return p
    raise ValueError(f"no tile for {n}")


def _ada_body(c_ref, w_ref, b_ref, o_ref):
    c = c_ref[...]
    cond = (c * (1.0 / (1.0 + jnp.exp(-c)))).astype(BF16)
    o_ref[...] = _dot(cond, w_ref[...].astype(BF16)) + b_ref[...]


def _ada_mod(c, ada_w, ada_b):
    depth, d, n = ada_w.shape
    bsz = c.shape[0]
    tn = 512
    return pl.pallas_call(
        _ada_body,
        out_shape=jax.ShapeDtypeStruct((depth, bsz, n), F32),
        grid=(depth, n // tn),
        in_specs=[
            pl.BlockSpec((bsz, d), lambda l, j: (0, 0)),
            pl.BlockSpec((None, d, tn), lambda l, j: (l, 0, j)),
            pl.BlockSpec((None, 1, tn), lambda l, j: (l, 0, j)),
        ],
        out_specs=pl.BlockSpec((None, bsz, tn), lambda l, j: (l, 0, j)),
        name="ada_mod",
        compiler_params=_cparams(("parallel", "parallel")),
    )(c, ada_w, ada_b.reshape(depth, 1, n))


def _norm_mod_body(x_ref, w_ref, sh_ref, sc_ref, o_ref):
    x = x_ref[...]
    ms = jnp.mean(x * x, axis=-1, keepdims=True)
    y = x * lax.rsqrt(ms + NORM_EPS) * w_ref[...]
    o_ref[...] = (y * (1.0 + sc_ref[...]) + sh_ref[...]).astype(o_ref.dtype)


def _norm_mod(x, w, shift, scale):
    bsz, s, d = x.shape
    ts = _pick(s, (512, 256, 128))
    vec = lambda v: v.reshape(bsz, 1, d)
    return pl.pallas_call(
        _norm_mod_body,
        out_shape=jax.ShapeDtypeStruct((bsz, s, d), BF16),
        grid=(bsz, s // ts),
        in_specs=[
            pl.BlockSpec((None, ts, d), lambda b, i: (b, i, 0)),
            pl.BlockSpec((1, d), lambda b, i: (0, 0)),
            pl.BlockSpec((None, 1, d), lambda b, i: (b, 0, 0)),
            pl.BlockSpec((None, 1, d), lambda b, i: (b, 0, 0)),
        ],
        out_specs=pl.BlockSpec((None, ts, d), lambda b, i: (b, i, 0)),
        name="norm_mod",
        compiler_params=_cparams(("parallel", "parallel")),
    )(x, w.reshape(1, d), vec(shift), vec(scale))


def _rope_body(pos_ref, freq_ref, cos_ref, sin_ref):
    ang = pos_ref[...].astype(F32) * freq_ref[...]
    lane = lax.broadcasted_iota(I32, ang.shape, 1)
    cos_ref[...] = jnp.where(lane < QK_ROPE, jnp.cos(ang), 0.0)
    sin = jnp.sin(ang)
    sin_ref[...] = jnp.where(lane < QK_ROPE // 2, -sin, jnp.where(lane < QK_ROPE, sin, 0.0))


def _rope_tables(pos, freq):
    t = pos.shape[0]
    tm = _pick(t, (1024, 512, 256, 128))
    out = jax.ShapeDtypeStruct((t, LANES), F32)
    row = pl.BlockSpec((tm, LANES), lambda i: (i, 0))
    return pl.pallas_call(
        _rope_body, out_shape=[out, out], grid=(t // tm,),
        in_specs=[pl.BlockSpec((tm, 1), lambda i: (i, 0)), pl.BlockSpec((1, LANES), lambda i: (0, 0))],
        out_specs=[row, row], name="rope_tables", compiler_params=_cparams(("parallel",)),
    )(pos, freq)


def _mla_q_body(cq_ref, nw_ref, wuq_ref, hw_ref, cos_ref, sin_ref, o_ref, *, scale):
    cq = cq_ref[...]
    ms = jnp.mean(cq * cq, axis=-1, keepdims=True)
    cqn = (cq * lax.rsqrt(ms + NORM_EPS) * nw_ref[...]).astype(BF16)
    cos, sin_signed = cos_ref[...], sin_ref[...]
    w_nope = hw_ref[0:1, :]
    w_rope = hw_ref[1:2, :]
    w_swap = hw_ref[2:3, :]
    for h in range(MLA_HEADS):
        qh = _dot(cqn, wuq_ref[:, h * 3 * LANES:(h + 1) * 3 * LANES])
        nope = qh[:, :LANES]
        rope = qh[:, LANES:2 * LANES]
        swap = qh[:, 2 * LANES:]
        ss = jnp.sum(nope * nope + rope * rope, axis=-1, keepdims=True)
        fac = lax.rsqrt(ss * (1.0 / QK_HEAD) + NORM_EPS) * scale
        o_ref[:, h * 2 * LANES:h * 2 * LANES + LANES] = (nope * fac * w_nope).astype(o_ref.dtype)
        roped = (rope * w_rope * cos + swap * w_swap * sin_signed) * fac
        o_ref[:, h * 2 * LANES + LANES:(h + 1) * 2 * LANES] = roped.astype(o_ref.dtype)


def _mla_kv_body(ckv_ref, kpe_ref, nw_ref, wukv_ref, hw_ref, cos_ref, sin_ref, k_ref, v_ref):
    ckv = ckv_ref[...]
    ms = jnp.mean(ckv * ckv, axis=-1, keepdims=True)
    ckvn = (ckv * lax.rsqrt(ms + NORM_EPS) * nw_ref[...]).astype(BF16)
    cos, sin_signed = cos_ref[...], sin_ref[...]
    kpe = kpe_ref[:, :LANES]
    kpe_swap = kpe_ref[:, LANES:]
    pe_ss = jnp.sum(kpe * kpe, axis=-1, keepdims=True)
    pe_roped = kpe * hw_ref[1:2, :] * cos + kpe_swap * hw_ref[2:3, :] * sin_signed
    w_nope = hw_ref[0:1, :]
    for h in range(MLA_HEADS):
        kn = _dot(ckvn, wukv_ref[:, h * LANES:(h + 1) * LANES])
        ss = jnp.sum(kn * kn, axis=-1, keepdims=True) + pe_ss
        fac = lax.rsqrt(ss * (1.0 / QK_HEAD) + NORM_EPS)
        k_ref[:, h * 2 * LANES:h * 2 * LANES + LANES] = (kn * fac * w_nope).astype(k_ref.dtype)
        k_ref[:, h * 2 * LANES + LANES:(h + 1) * 2 * LANES] = (pe_roped * fac).astype(k_ref.dtype)
    nv = MLA_HEADS * V_HEAD
    v_ref[...] = _dot(ckvn, wukv_ref[:, nv:]).astype(v_ref.dtype)


def _head_norm_rows(w):
    half = QK_ROPE // 2
    zeros = jnp.zeros((LANES - QK_ROPE,), F32)
    rope = jnp.concatenate([w[QK_NOPE:], zeros])
    swap = jnp.concatenate([w[QK_NOPE + half:], w[QK_NOPE:QK_NOPE + half], zeros])
    rows = jnp.stack([w[:QK_NOPE], rope, swap])
    return jnp.concatenate([rows, jnp.zeros((SUBLANES - 3, LANES), F32)], axis=0)


def _rope_freq_row():
    half = QK_ROPE // 2
    f = ROPE_THETA ** (-jnp.arange(half, dtype=F32) / half)
    return jnp.concatenate([f, f, jnp.zeros((LANES - QK_ROPE,), F32)]).reshape(1, LANES)


def _wuq_layout():
    half = QK_ROPE // 2
    idx, mask = [], []
    for h in range(MLA_HEADS):
        base = h * QK_HEAD
        nope = list(range(base, base + QK_NOPE))
        rope = list(range(base + QK_NOPE, base + QK_HEAD))
        swap = rope[half:] + rope[:half]
        pad = [0] * (LANES - QK_ROPE)
        idx += nope + rope + pad + swap + pad
        mask += [1] * QK_NOPE + [1] * QK_ROPE + [0] * len(pad) + [1] * QK_ROPE + [0] * len(pad)
    return np.asarray(idx, np.int32), np.asarray(mask, np.float32)


def _wukv_layout():
    per = QK_NOPE + V_HEAD
    k_idx = [h * per + i for h in range(MLA_HEADS) for i in range(QK_NOPE)]
    v_idx = [h * per + QK_NOPE + i for h in range(MLA_HEADS) for i in range(V_HEAD)]
    return np.asarray(k_idx + v_idx, np.int32)


def _mla_qkv(p, positions, q_norm_w, w_uq, kv_norm_w, w_ukv, qk_q_w, qk_k_w):
    t = p.shape[0]
    tm = _pick(t, (256, 128))
    idx, mask = _wuq_layout()
    wuq = (w_uq[:, idx] * mask[None, :]).astype(BF16)
    wukv = w_ukv[:, _wukv_layout()].astype(BF16)
    cos, sin_signed = _rope_tables(positions.reshape(t, 1), _rope_freq_row())
    hq = _head_norm_rows(qk_q_w)
    hk = _head_norm_rows(qk_k_w)
    qw = MLA_HEADS * 2 * LANES
    row = lambda i: (i, 0)
    fixed = lambda i: (0, 0)
    q = pl.pallas_call(
        functools.partial(_mla_q_body, scale=QK_HEAD ** -0.5),
        out_shape=jax.ShapeDtypeStruct((t, qw), BF16),
        grid=(t // tm,),
        in_specs=[
            pl.BlockSpec((tm, Q_LORA), lambda i: (i, P_CQ // Q_LORA)),
            pl.BlockSpec((1, Q_LORA), fixed),
            pl.BlockSpec(wuq.shape, fixed),
            pl.BlockSpec((SUBLANES, LANES), fixed),
            pl.BlockSpec((tm, LANES), row),
            pl.BlockSpec((tm, LANES), row),
        ],
        out_specs=pl.BlockSpec((tm, qw), row),
        name="mla_q",
        compiler_params=_cparams(("parallel",)),
    )(p, q_norm_w.reshape(1, Q_LORA), wuq, hq, cos, sin_signed)
    k, v = pl.pallas_call(
        _mla_kv_body,
        out_shape=[jax.ShapeDtypeStruct((t, qw), BF16),
                   jax.ShapeDtypeStruct((t, MLA_HEADS * V_HEAD), BF16)],
        grid=(t // tm,),
        in_specs=[
            pl.BlockSpec((tm, KV_LORA), lambda i: (i, P_CKV // KV_LORA)),
            pl.BlockSpec((tm, 2 * LANES), lambda i: (i, P_KPE // (2 * LANES))),
            pl.BlockSpec((1, KV_LORA), fixed),
            pl.BlockSpec(wukv.shape, fixed),
            pl.BlockSpec((SUBLANES, LANES), fixed),
            pl.BlockSpec((tm, LANES), row),
            pl.BlockSpec((tm, LANES), row),
        ],
        out_specs=[pl.BlockSpec((tm, qw), row), pl.BlockSpec((tm, MLA_HEADS * V_HEAD), row)],
        name="mla_kv",
        compiler_params=_cparams(("parallel",)),
    )(p, p, kv_norm_w.reshape(1, KV_LORA), wukv, hk, cos, sin_signed)
    return q, k, v


def _attn_body(qi_ref, ki_ref, q_ref, k_ref, v_ref, o_ref, m_ref, l_ref, acc_ref, *, tq):
    step_id = pl.program_id(2)
    qi = qi_ref[step_id]
    ki = ki_ref[step_id]

    @pl.when(ki == 0)
    def _():
        m_ref[...] = jnp.full(m_ref.shape, -jnp.inf, F32)
        l_ref[...] = jnp.zeros(l_ref.shape, F32)
        acc_ref[...] = jnp.zeros(acc_ref.shape, F32)

    heads = range(ATTN_HEADS_PER_STEP)
    qk = lambda h: slice(h * 2 * LANES, (h + 1) * 2 * LANES)
    vo = lambda h: slice(h * V_HEAD, (h + 1) * V_HEAD)

    def step(masked):
        ss = [_dot_nt(q_ref[:, qk(h)], k_ref[:, qk(h)]) for h in heads]
        if masked:
            row = lax.broadcasted_iota(I32, ss[0].shape, 0)
            col = lax.broadcasted_iota(I32, ss[0].shape, 1)
            ss = [jnp.where(col <= row, s, -jnp.inf) for s in ss]
        reps = tq // LANES
        prs, alphas = [], []
        for h, s in zip(heads, ss):
            m_prev = m_ref[h]
            m_new = jnp.maximum(m_prev, jnp.max(s, axis=-1, keepdims=True))
            alpha = jnp.exp(m_prev - m_new)
            pr = jnp.exp(s - jnp.tile(m_new, (1, reps)))
            l_ref[h] = alpha * l_ref[h] + jnp.sum(pr, axis=-1, keepdims=True)
            m_ref[h] = m_new
            prs.append(pr.astype(BF16))
            alphas.append(alpha)
        pvs = [_dot(pr, v_ref[:, vo(h)]) for h, pr in zip(heads, prs)]
        for h, alpha, pv in zip(heads, alphas, pvs):
            acc_ref[h] = alpha * acc_ref[h] + pv

    @pl.when(ki < qi)
    def _():
        step(False)

    @pl.when(ki == qi)
    def _():
        step(True)
        for h in heads:
            o_ref[:, vo(h)] = (acc_ref[h] / l_ref[h]).astype(o_ref.dtype)


def _attention(q, k, v, bsz, s):
    tq = _pick(s, (512, 256, 128))
    n = s // tq
    pairs = [(i, j) for i in range(n) for j in range(i + 1)]
    qi_tab = jnp.asarray([i for i, _ in pairs], I32)
    ki_tab = jnp.asarray([j for _, j in pairs], I32)
    nh = ATTN_HEADS_PER_STEP
    grid_spec = pltpu.PrefetchScalarGridSpec(
        num_scalar_prefetch=2,
        grid=(bsz, MLA_HEADS // nh, len(pairs)),
        in_specs=[
            pl.BlockSpec((None, tq, nh * 2 * LANES), lambda b, h, p, qi, ki: (b, qi[p], h)),
            pl.BlockSpec((None, tq, nh * 2 * LANES), lambda b, h, p, qi, ki: (b, ki[p], h)),
            pl.BlockSpec((None, tq, nh * V_HEAD), lambda b, h, p, qi, ki: (b, ki[p], h)),
        ],
        out_specs=pl.BlockSpec((None, tq, nh * V_HEAD), lambda b, h, p, qi, ki: (b, qi[p], h)),
        scratch_shapes=[pltpu.VMEM((nh, tq, LANES), F32), pltpu.VMEM((nh, tq, LANES), F32),
                        pltpu.VMEM((nh, tq, V_HEAD), F32)],
    )
    return pl.pallas_call(
        functools.partial(_attn_body, tq=tq),
        out_shape=jax.ShapeDtypeStruct((bsz, s, MLA_HEADS * V_HEAD), BF16),
        grid_spec=grid_spec,
        name="mla_attention",
        compiler_params=_cparams(("parallel", "parallel", "arbitrary")),
    )(qi_tab, ki_tab, q, k, v)


def _rwkv_body(r_ref, k_ref, v_ref, xl_ref, mu_r_ref, mu_k_ref, mu_v_ref, mu_l_ref, vec_ref,
               w2a2_ref, g2_ref, o_ref, state_ref, pr_ref, pk_ref, pv_ref, pl_ref, *, tb, npairs):
    t_idx = pl.program_id(2)

    @pl.when(t_idx == 0)
    def _():
        state_ref[...] = jnp.zeros(state_ref.shape, F32)
        pr_ref[...] = jnp.zeros(pr_ref.shape, F32)
        pk_ref[...] = jnp.zeros(pk_ref.shape, F32)
        pv_ref[...] = jnp.zeros(pv_ref.shape, F32)
        pl_ref[...] = jnp.zeros(pl_ref.shape, F32)

    def mix(x_ref, prev_ref, mu_ref):
        x = x_ref[...]
        row = lax.broadcasted_iota(I32, x.shape, 0)
        shifted = jnp.where(row == 0, prev_ref[...], pltpu.roll(x, 1, axis=0))
        prev_ref[...] = x[tb - 1:tb, :]
        return x + (shifted - x) * mu_ref[...]

    r = mix(r_ref, pr_ref, mu_r_ref)
    k = mix(k_ref, pk_ref, mu_k_ref)
    v = mix(v_ref, pv_ref, mu_v_ref)
    xl = mix(xl_ref, pl_ref, mu_l_ref)

    lane = lax.broadcasted_iota(I32, (tb, LANES), 1)
    first = lane < W_LORA
    x0 = xl[:, :LANES]
    lhs_w = jnp.where(first, jnp.tanh(x0), 0.0).astype(BF16)
    lhs_a = jnp.where(first, 0.0, x0).astype(BF16)
    xg = xl[:, LANES:]
    lhs_g = (1.0 / (1.0 + jnp.exp(-xg))).astype(BF16)
    cols = [slice(p * LANES, (p + 1) * LANES) for p in range(npairs)]
    outs = _rwkv_pairs([r[:, c] for c in cols], [k[:, c] for c in cols], [v[:, c] for c in cols],
                       lhs_w, lhs_a, lhs_g, [vec_ref[:, c] for c in cols],
                       [w2a2_ref[:, c] for c in cols], [g2_ref[:, c] for c in cols],
                       state_ref, tb)
    for c, out in zip(cols, outs):
        o_ref[:, c] = out.astype(o_ref.dtype)


def _rwkv_pairs(rs, ks, vs_in, lhs_w, lhs_a, lhs_g, vecs, w2a2s, g2s, state_ref, tb):
    c = RWKV_CHUNK
    pairs = range(len(rs))
    chunks = range(tb // c)
    each = lambda f, *lists: [f(*args) for args in zip(*lists)]

    ri = lax.broadcasted_iota(I32, (LANES, LANES), 0)
    ci = lax.broadcasted_iota(I32, (LANES, LANES), 1)
    same = (ri // RWKV_HEAD) == (ci // RWKV_HEAD)
    ones_bd = jnp.where(same, 1.0, 0.0).astype(BF16)
    tri_r = lax.broadcasted_iota(I32, (c, c), 0)
    tri_c = lax.broadcasted_iota(I32, (c, c), 1)
    tri = jnp.where(tri_c <= tri_r, 1.0, 0.0).astype(BF16)
    eye = jnp.where(ri == ci, 1.0, 0.0)
    first_c = lax.broadcasted_iota(I32, (c, LANES), 1) < RWKV_HEAD

    def stack(z):
        return jnp.concatenate([jnp.where(first_c, z, 0.0), jnp.where(first_c, 0.0, z)],
                               axis=0).astype(BF16)

    def split(x):
        hi = x.astype(BF16)
        return hi, (x - hi.astype(F32)).astype(BF16)

    def group_sum(xs):
        parts = each(split, xs)
        his = [_dot(hi, ones_bd) for hi, _ in parts]
        los = [_dot(lo, ones_bd) for _, lo in parts]
        return each(lambda a, b: a + b, his, los)

    lws = [_dot(lhs_w, w) for w in w2a2s]
    las = [_dot(lhs_a, w) for w in w2a2s]
    gs = [_dot(lhs_g, g2) for g2 in g2s]
    kks = each(lambda k, vec: k * vec[2:3, :], ks, vecs)
    nrms = group_sum(each(lambda kk: kk * kk, kks))
    kks = each(lambda kk, n: kk / jnp.maximum(jnp.sqrt(n), 1e-12), kks, nrms)

    def decay_and_gate(lw, la, vec):
        zw = -(vec[0:1, :] + lw)
        softplus = jnp.maximum(zw, 0.0) + jnp.log(1.0 + jnp.exp(-jnp.abs(zw)))
        a = 1.0 / (1.0 + jnp.exp(-(vec[1:2, :] + la)))
        return -jnp.exp(-softplus - 0.5), a

    dg = each(decay_and_gate, lws, las, vecs)
    logds = [d for d, _ in dg]
    a_gates = [a for _, a in dg]
    kps = each(lambda k, a, vec: k * (1.0 + (a - 1.0) * vec[3:4, :]), ks, a_gates, vecs)
    bvecs = each(lambda kk, a: kk * a, kks, a_gates)
    bonus_sums = group_sum(each(lambda r, kp, vec: r * kp * vec[4:5, :], rs, kps, vecs))

    inst = [(p, ch) for p in pairs for ch in chunks]
    rows = lambda x, ch: x[ch * c:(ch + 1) * c]
    lds = [rows(logds[p], ch) for p, ch in inst]
    ld_parts = each(split, lds)
    lc_hi = [_dot(tri, hi) for hi, _ in ld_parts]
    lc_lo = [_dot(tri, lo) for _, lo in ld_parts]
    lcs = each(lambda a, b: a + b, lc_hi, lc_lo)
    ltots = [lc[c - 1:c, :] for lc in lcs]
    e_negs = [jnp.exp(-lc) for lc in lcs]
    e_rems = each(lambda lt, lc: jnp.exp(lt - lc), ltots, lcs)
    xas = [stack(-rows(kks[p], ch) * jnp.exp(lc - ld)) for (p, ch), lc, ld in zip(inst, lcs, lds)]
    xrs = [stack(rows(rs[p], ch) * jnp.exp(lc)) for (p, ch), lc in zip(inst, lcs)]
    ybs = [stack(rows(bvecs[p], ch) * e) for (p, ch), e in zip(inst, e_negs)]
    yks = [stack(rows(kps[p], ch) * e) for (p, ch), e in zip(inst, e_negs)]
    vss = [stack(rows(vs_in[p], ch)) for p, ch in inst]
    bhs = [stack(rows(bvecs[p], ch) * e) for (p, ch), e in zip(inst, e_rems)]
    khs = [stack(rows(kps[p], ch) * e) for (p, ch), e in zip(inst, e_rems)]

    n2 = 2 * c
    xars = each(lambda a, b: jnp.concatenate([a, b], axis=0), xas, xrs)
    rr = lax.broadcasted_iota(I32, (2 * n2, n2), 0)
    cc = lax.broadcasted_iota(I32, (2 * n2, n2), 1)
    rr_in = rr % n2
    tri_mask = ((rr_in // c) == (cc // c)) & ((cc % c) < (rr_in % c) + rr // n2)
    mbs = each(lambda a, b: jnp.where(tri_mask, _dot_nt(a, b), 0.0), xars, ybs)
    mks = each(lambda a, b: jnp.where(tri_mask, _dot_nt(a, b), 0.0).astype(BF16), xars, yks)
    labs = [mb[:n2] for mb in mbs]
    mrbs = [mb[n2:].astype(BF16) for mb in mbs]

    pbs = [lab.astype(BF16) for lab in labs]
    tinvs = [eye + lab for lab in labs]
    pbs = [_dot(pb, pb).astype(BF16) for pb in pbs]
    n_sq = 2
    while n_sq * 2 < c:
        both = each(lambda pb, ti: _dot(pb, jnp.concatenate([ti.astype(BF16), pb], axis=1)),
                    pbs, tinvs)
        tinvs = each(lambda ti, bo: ti + bo[:, :n2], tinvs, both)
        pbs = [bo[:, n2:].astype(BF16) for bo in both]
        n_sq *= 2
    tinvs = each(lambda ti, pb: (ti + _dot(pb, ti.astype(BF16))).astype(BF16), tinvs, pbs)
    mkvs = each(_dot, mks, vss)
    kvs = each(_dot_tn, vss, khs)

    states = [state_ref[p] for p in pairs]
    ys = [[] for _ in pairs]
    n_ch = len(chunks)
    for ch in chunks:
        ids = [p * n_ch + ch for p in pairs]
        sbs = [s.astype(BF16) for s in states]
        withs = [_dot_nt(xars[i], sb) + mkvs[i] for i, sb in zip(ids, sbs)]
        ubs = [_dot(tinvs[i], w[:n2].astype(BF16)).astype(BF16) for i, w in zip(ids, withs)]
        states = [s * jnp.exp(ltots[i]) + _dot_tn(ub, bhs[i]) + kvs[i]
                  for i, s, ub in zip(ids, states, ubs)]
        for p, i, w, ub in zip(pairs, ids, withs, ubs):
            ystack = w[n2:] + _dot(mrbs[i], ub)
            ys[p].append(ystack[:c] + ystack[c:])
    for p, s in zip(pairs, states):
        state_ref[p] = s

    ycat = [jnp.concatenate(y, axis=0) for y in ys]
    inv_n = 1.0 / RWKV_HEAD
    means = [m * inv_n for m in group_sum(ycat)]
    ds = each(lambda y, m: y - m, ycat, means)
    variances = [v * inv_n for v in group_sum(each(lambda d: d * d, ds))]
    outs = []
    for d, var, vec, bs, v, g in zip(ds, variances, vecs, bonus_sums, vs_in, gs):
        yn = d * lax.rsqrt(var + LNX_EPS) * vec[5:6, :] + vec[6:7, :]
        outs.append((yn + bs * v) * g)
    return outs


def _rwkv(p_rkv, p_mla, mu_rkv, mu_lora, vecs, w2a2, g2):
    assert W_LORA + A_LORA == LANES and G_LORA == LANES
    bsz, s, _ = p_rkv.shape
    tb = _pick(s, (512, 256, 128, 64))
    npairs = RWKV_PAIRS_PER_STEP
    wd = npairs * LANES
    groups = RWKV_HEADS // 2 // npairs
    colblk = lambda off: (lambda b, hp, t: (b, t, off // wd + hp))
    mublk = lambda off: (lambda b, hp, t: (0, off // wd + hp))
    o_r, o_k, o_v = 0, RWKV_WIDTH, 2 * RWKV_WIDTH
    return pl.pallas_call(
        functools.partial(_rwkv_body, tb=tb, npairs=npairs),
        out_shape=jax.ShapeDtypeStruct((bsz, s, RWKV_WIDTH), BF16),
        grid=(bsz, groups, s // tb),
        in_specs=[
            pl.BlockSpec((None, tb, wd), colblk(o_r)),
            pl.BlockSpec((None, tb, wd), colblk(o_k)),
            pl.BlockSpec((None, tb, wd), colblk(o_v)),
            pl.BlockSpec((None, tb, 2 * LANES), lambda b, hp, t: (b, t, P_LORA // (2 * LANES))),
            pl.BlockSpec((1, wd), mublk(o_r)),
            pl.BlockSpec((1, wd), mublk(o_k)),
            pl.BlockSpec((1, wd), mublk(o_v)),
            pl.BlockSpec((1, 2 * LANES), lambda b, hp, t: (0, 0)),
            pl.BlockSpec((SUBLANES, wd), lambda b, hp, t: (0, hp)),
            pl.BlockSpec((LANES, wd), lambda b, hp, t: (0, hp)),
            pl.BlockSpec((LANES, wd), lambda b, hp, t: (0, hp)),
        ],
        out_specs=pl.BlockSpec((None, tb, wd), lambda b, hp, t: (b, t, hp)),
        scratch_shapes=[pltpu.VMEM((npairs, LANES, LANES), F32),
                        pltpu.VMEM((1, wd), F32), pltpu.VMEM((1, wd), F32),
                        pltpu.VMEM((1, wd), F32), pltpu.VMEM((1, 2 * LANES), F32)],
        name="rwkv7",
        compiler_params=_cparams(("parallel", "parallel", "arbitrary")),
    )(p_rkv, p_rkv, p_rkv, p_mla, mu_rkv, mu_rkv, mu_rkv, mu_lora, vecs, w2a2, g2)


def _conv_body(h_ref, wb_ref, wc_ref, wu_ref, cw_ref, o_ref, tail_ref, *, tm, seq):
    i = pl.program_id(1)
    a = h_ref[...]
    gate_b = _dot(a, wb_ref[...])
    uu = _dot(a, wc_ref[...]) * _dot(a, wu_ref[...])

    @pl.when((i * tm) % seq == 0)
    def _():
        tail_ref[...] = jnp.zeros(tail_ref.shape, F32)

    tail = tail_ref[...]
    row = lax.broadcasted_iota(I32, uu.shape, 0)
    h1 = tail[SUBLANES - 1:SUBLANES, :]
    h2 = tail[SUBLANES - 2:SUBLANES - 1, :]
    s1 = jnp.where(row == 0, h1, pltpu.roll(uu, 1, axis=0))
    s2 = jnp.where(row == 0, h2, jnp.where(row == 1, h1, pltpu.roll(uu, 2, axis=0)))
    y = cw_ref[0:1, :] * s2 + cw_ref[1:2, :] * s1 + cw_ref[2:3, :] * uu
    o_ref[...] = (gate_b * y).astype(o_ref.dtype)
    tail_ref[...] = uu[tm - SUBLANES:tm, :]


def _conv_mixer(h, w_in, conv_w, s):
    t, d = h.shape
    tm = _pick(s, (512, 256, 128))
    tn = 512
    nblk = d // tn
    w = w_in.astype(BF16)
    cw = jnp.concatenate([conv_w, jnp.zeros((SUBLANES - CONV_WIDTH, d), F32)], axis=0)
    wspec = lambda off: pl.BlockSpec((d, tn), lambda j, i: (0, j + off))
    return pl.pallas_call(
        functools.partial(_conv_body, tm=tm, seq=s),
        out_shape=jax.ShapeDtypeStruct((t, d), BF16),
        grid=(nblk, t // tm),
        in_specs=[pl.BlockSpec((tm, d), lambda j, i: (i, 0)),
                  wspec(0), wspec(nblk), wspec(2 * nblk),
                  pl.BlockSpec((SUBLANES, tn), lambda j, i: (0, j))],
        out_specs=pl.BlockSpec((tm, tn), lambda j, i: (i, j)),
        scratch_shapes=[pltpu.VMEM((SUBLANES, tn), F32)],
        name="conv_mixer",
        compiler_params=_cparams(("parallel", "arbitrary")),
    )(h, w, w, w, cw)


def _rank_pairs():
    kt = PEER_TOPK
    return [(a, b) for a in range(kt) for b in range(kt) if (a + 1) * (b + 1) <= kt]


PAIR_ROWS = -(-len(_rank_pairs()) // SUBLANES) * SUBLANES


def _pair_matrices():
    m1 = np.zeros((PAIR_ROWS, PEER_TOPK), np.float32)
    m2 = np.zeros((PAIR_ROWS, PEER_TOPK), np.float32)
    for p, (a, b) in enumerate(_rank_pairs()):
        m1[p, a] = 1.0
        m2[p, b] = 1.0
    return m1, m2


def _exact_dot_left(m, x):
    hi = x.astype(BF16)
    r1 = x - hi.astype(F32)
    mid = r1.astype(BF16)
    lo = (r1 - mid.astype(F32)).astype(BF16)
    return (_dot(m, hi) + _dot(m, mid)) + _dot(m, lo)


def _extract_top_rows(xs, row_iota, n, rank_iota, extras=None):
    xs = list(xs)
    sentinel = float(xs[0].shape[0])
    zeros = lambda: [jnp.zeros((n, x.shape[1]), F32) for x in xs]
    vals, idxs, exts = zeros(), zeros(), zeros()
    for kk in range(n):
        here = rank_iota == kk
        for j, x in enumerate(xs):
            m = jnp.max(x, axis=0, keepdims=True)
            idx = jnp.min(jnp.where(x == m, row_iota, sentinel), axis=0, keepdims=True)
            hit = row_iota == idx
            if extras is not None:
                ext = jnp.max(jnp.where(hit, extras[j], -1.0), axis=0, keepdims=True)
                exts[j] = jnp.where(here, ext, exts[j])
            xs[j] = jnp.where(hit, -jnp.inf, x)
            vals[j] = jnp.where(here, m, vals[j])
            idxs[j] = jnp.where(here, idx, idxs[j])
    return list(zip(vals, idxs, exts))


def _route_body(q_ref, keys_ref, m1_ref, m2_ref, e1_ref, e2_ref, gw_ref):
    kt = PEER_TOPK
    tb = q_ref.shape[0]
    nh = ROUTE_HEADS_PER_STEP
    key_iota = lax.broadcasted_iota(I32, (N_KEYS, tb), 0).astype(F32)
    rank_iota = lax.broadcasted_iota(I32, (kt, tb), 0)
    scs = []
    for hp in range(2 * nh):
        qh = q_ref[:, hp * PEER_HALF:(hp + 1) * PEER_HALF].astype(BF16)
        scs.append(_dot_nt(keys_ref[hp], qh))
    tops = _extract_top_rows(scs, key_iota, kt, rank_iota)
    m1 = m1_ref[...]
    m2 = m2_ref[...]
    pair_iota = lax.broadcasted_iota(I32, (PAIR_ROWS, tb), 0).astype(F32)
    cands, eids = [], []
    for h in range(nh):
        (sv1, si1, _), (sv2, si2, _) = tops[2 * h], tops[2 * h + 1]
        cand = _exact_dot_left(m1, sv1) + _exact_dot_left(m2, sv2)
        cands.append(jnp.where(pair_iota < float(len(_rank_pairs())), cand, -jnp.inf))
        eids.append(_dot(m1, si1.astype(BF16)) * float(N_KEYS) + _dot(m2, si2.astype(BF16)))
    picks = _extract_top_rows(cands, pair_iota, kt, rank_iota, extras=eids)
    for h, (cv, _, ev) in enumerate(picks):
        rows = slice(h * kt, (h + 1) * kt)
        ex = jnp.exp(cv - cv[0:1, :])
        gw_ref[rows, :] = ex / jnp.sum(ex, axis=0, keepdims=True)
        ei = ev.astype(I32)
        e1_ref[rows, :] = (ei >> 7).astype(F32)
        e2_ref[rows, :] = (ei & (N_KEYS - 1)).astype(F32)


def _peer_route(q, keys):
    t = q.shape[0]
    tb = LANES
    m1, m2 = _pair_matrices()
    slots = PEER_HEADS * PEER_TOPK
    out = jax.ShapeDtypeStruct((slots, t), F32)
    nh = ROUTE_HEADS_PER_STEP
    pair_spec = pl.BlockSpec((PAIR_ROWS, PEER_TOPK), lambda i, h: (0, 0))
    out_spec = pl.BlockSpec((nh * PEER_TOPK, tb), lambda i, h: (h, i))
    return pl.pallas_call(
        _route_body,
        out_shape=[out, out, out],
        grid=(t // tb, PEER_HEADS // nh),
        in_specs=[pl.BlockSpec((tb, nh * 2 * PEER_HALF), lambda i, h: (i, h)),
                  pl.BlockSpec((nh * 2, N_KEYS, PEER_HALF), lambda i, h: (h, 0, 0)),
                  pair_spec, pair_spec],
        out_specs=[out_spec, out_spec, out_spec],
        name="peer_route",
        compiler_params=_cparams(("parallel", "parallel")),
    )(q, keys, jnp.asarray(m1, BF16), jnp.asarray(m2, BF16))


def _swap_token_and_row(xs):
    xs = list(xs)
    row = lax.broadcasted_iota(I32, xs[0].shape, 1)
    d = SUBLANES // 2
    while d >= 1:
        low = (row % (2 * d)) < d
        for s in range(SUBLANES):
            if (s // d) % 2 == 0:
                a, b = xs[s], xs[s + d]
                xs[s] = jnp.where(low, a, pltpu.roll(b, d, axis=1))
                xs[s + d] = jnp.where(low, pltpu.roll(a, SUBLANES - d, axis=1), b)
        d //= 2
    return xs


GATE_TOKENS = 2 * SUBLANES


def _gate_body(e1t_ref, e2t_ref, gwt_ref, uf_ref, o_ref, ub_ref, e1_s, e2_s, gw_s):
    ub_ref[...] = uf_ref[...].astype(ub_ref.dtype)
    tb = o_ref.shape[2]
    e1_s[...] = e1t_ref[...].T
    e2_s[...] = e2t_ref[...].T
    gw_s[...] = gwt_ref[...].T
    sub = lax.broadcasted_iota(I32, (N_KEYS, LANES), 0).astype(F32)
    nblk = N_KEYS // SUBLANES

    def eight_tokens(t0):
        lhs, rhs = [], []
        for s in range(SUBLANES):
            e1 = e1_s[pl.ds(t0 + s, 1), :]
            e2 = e2_s[pl.ds(t0 + s, 1), :]
            gw = gw_s[pl.ds(t0 + s, 1), :]
            lhs.append(jnp.where(e1 == sub, gw, 0.0).astype(BF16))
            rhs.append(jnp.where(e2 == sub, 1.0, 0.0).astype(BF16))
        per_token = [_dot_nt(a, b).reshape(nblk, SUBLANES, N_KEYS) for a, b in zip(lhs, rhs)]
        return _swap_token_and_row(per_token)

    def sixteen_tokens(g, carry):
        t0 = pl.multiple_of(g * GATE_TOKENS, GATE_TOKENS)
        first, second = eight_tokens(t0), eight_tokens(t0 + SUBLANES)
        for r, (y0, y1) in enumerate(zip(first, second)):
            o_ref[:, r, pl.ds(t0, GATE_TOKENS), :] = jnp.concatenate(
                [y0, y1], axis=1).astype(o_ref.dtype)
        return carry

    lax.fori_loop(0, tb // GATE_TOKENS, sixteen_tokens, 0, unroll=4)


def _peer_gates(e1t, e2t, gwt, u_all, layer):
    t = e1t.shape[1]
    tb = LANES
    steps = t // tb
    nblk = N_KEYS // SUBLANES
    n_exp, d = u_all.shape[1:]
    rows = n_exp // steps
    assert rows * steps == n_exp and rows % GATE_TOKENS == 0
    in_spec = pl.BlockSpec((LANES, tb), lambda i: (0, i))
    g, u_bf16 = pl.pallas_call(
        _gate_body,
        out_shape=[jax.ShapeDtypeStruct((nblk, SUBLANES, t, N_KEYS), BF16),
                   jax.ShapeDtypeStruct((n_exp, d), BF16)],
        grid=(steps,),
        in_specs=[in_spec, in_spec, in_spec,
                  pl.BlockSpec((None, rows, d), lambda i: (layer, i, 0))],
        out_specs=[pl.BlockSpec((nblk, SUBLANES, tb, N_KEYS), lambda i: (0, 0, i, 0)),
                   pl.BlockSpec((rows, d), lambda i: (i, 0))],
        scratch_shapes=[pltpu.VMEM((tb, LANES), F32)] * 3,
        name="peer_gates",
        compiler_params=_cparams(("parallel",)),
    )(e1t, e2t, gwt, u_all)
    return g.reshape(N_KEYS, t, N_KEYS), u_bf16


def _peer_up_body(h_ref, u_ref, g_ref, o_ref):
    s = _dot_nt(h_ref[...], u_ref[...])
    act = 0.5 * s * (1.0 + lax.erf(s * (2.0 ** -0.5)))
    for j in range(o_ref.shape[1] // N_KEYS):
        cols = slice(j * N_KEYS, (j + 1) * N_KEYS)
        o_ref[:, cols] = (act[:, cols] * g_ref[j]).astype(o_ref.dtype)


def _peer_up(h, u_tab, g3):
    t, d = h.shape
    n_exp = u_tab.shape[0]
    tm = _pick(t, (1024, 512, 256, 128))
    tn = SUBLANES * N_KEYS
    return pl.pallas_call(
        _peer_up_body,
        out_shape=jax.ShapeDtypeStruct((t, n_exp), BF16),
        grid=(t // tm, n_exp // tn),
        in_specs=[pl.BlockSpec((tm, d), lambda i, j: (i, 0), pipeline_mode=pl.Buffered(1)),
                  pl.BlockSpec((tn, d), lambda i, j: (j, 0)),
                  pl.BlockSpec((tn // N_KEYS, tm, N_KEYS), lambda i, j: (j, i, 0))],
        out_specs=pl.BlockSpec((tm, tn), lambda i, j: (i, j)),
        name="peer_up",
        compiler_params=_cparams(("parallel", "parallel")),
    )(h, u_tab, g3)


def _residual(acc, x, gate):
    return x + gate * acc


def _gated_residual_matmul(a, w_all, layer, x2, gate, s, *, tk, name):
    t, d = x2.shape
    bsz = gate.shape[0]
    tm = _pick(s, (1024, 512, 256, 128))
    tn = 1024
    return _matmul(
        a, w_all, tm=tm, tn=tn, tk=tk, out_dtype=F32, epilogue=_residual,
        extras=[(x2, (tm, tn), lambda i, j: (i, j)),
                (gate.reshape(bsz, 1, d), (None, 1, tn), lambda i, j: (i * tm // s, 0, j))],
        b_lead=layer, name=name)


def _peer_ffn(x, w_norm, shift, scale, gate, layer, w_q_all, keys, u_all, v_all):
    bsz, s, d = x.shape
    t = bsz * s
    h = _norm_mod(x, w_norm, shift, scale).reshape(t, d)
    tm = _pick(s, (1024, 512, 256, 128))
    q = _matmul(h, w_q_all, tm=tm, tn=1024, tk=d, out_dtype=F32, b_lead=layer,
                name="peer_query")
    keys_b = keys.reshape(PEER_HEADS * 2, N_KEYS, PEER_HALF).astype(BF16)
    e1t, e2t, gwt = _peer_route(q, keys_b)
    g3, u_tab = _peer_gates(e1t, e2t, gwt, u_all, layer)
    pmat = _peer_up(h, u_tab, g3)
    out = _gated_residual_matmul(pmat, v_all, layer, x.reshape(t, d), gate, s,
                                 tk=2048, name="peer_down")
    return out.reshape(bsz, s, d)


def _pad_mla_proj(w):
    o_kv = Q_LORA
    o_pe = Q_LORA + KV_LORA
    o_rw = o_pe + QK_ROPE
    half = QK_ROPE // 2
    z = lambda n: jnp.zeros((w.shape[0], n), w.dtype)
    cq = w[:, :Q_LORA]
    ckv = w[:, o_kv:o_pe]
    kpe = w[:, o_pe:o_rw]
    kpe_swap = jnp.concatenate([kpe[:, half:], kpe[:, :half]], axis=1)
    lora = w[:, o_rw + 3 * RWKV_WIDTH:]
    pad = LANES - QK_ROPE
    out = jnp.concatenate([ckv, kpe, z(pad), kpe_swap, z(pad), cq, lora,
                           z(P_WIDTH - P_LORA - lora.shape[1])], axis=1)
    assert out.shape[1] == P_WIDTH
    return out


def _hybrid_mixer(h, positions, w_in, q_norm_w, w_uq, kv_norm_w, w_ukv, qk_q_w, qk_k_w,
                  mu, w0, w2, a0, a2, g2, k_k, k_a, r_k, lnx_w, lnx_b):
    bsz, s, d = h.shape
    t = bsz * s
    tm = _pick(s, (1024, 512, 256, 128))
    o_rw = Q_LORA + KV_LORA + QK_ROPE
    n_rkv = 3 * RWKV_WIDTH
    h2 = h.reshape(t, d)
    p_mla = _matmul(h2, _pad_mla_proj(w_in).astype(BF16), tm=tm, tn=1024, tk=d, out_dtype=F32,
                    name="hyb_in_proj_mla")
    p_rkv = _matmul(h2, w_in[:, o_rw:o_rw + n_rkv].astype(BF16), tm=tm, tn=1024, tk=d,
                    out_dtype=F32, name="hyb_in_proj_rkv")
    q, k, v = _mla_qkv(p_mla, positions, q_norm_w, w_uq, kv_norm_w, w_ukv, qk_q_w, qk_k_w)
    y_mla = _attention(q.reshape(bsz, s, -1), k.reshape(bsz, s, -1), v.reshape(bsz, s, -1), bsz, s)

    zrow = jnp.zeros((RWKV_WIDTH,), F32)
    vecs = jnp.stack([w0, a0, k_k, k_a, r_k.reshape(-1), lnx_w, lnx_b, zrow])
    w2a2 = jnp.concatenate([w2, a2], axis=0).astype(BF16)
    y_rwkv = _rwkv(p_rkv.reshape(bsz, s, n_rkv), p_mla.reshape(bsz, s, P_WIDTH),
                   mu[:n_rkv].reshape(1, n_rkv), mu[n_rkv:].reshape(1, -1), vecs, w2a2,
                   g2.astype(BF16))
    return jnp.concatenate([y_mla, y_rwkv], axis=-1).reshape(t, -1)


def kernel(x, c, positions, ada_w, ada_b, norm_mix_w, norm_ffn_w, hyb_w_in, mla_q_norm_w, mla_w_uq, mla_kv_norm_w, mla_w_ukv, mla_qk_q_w, mla_qk_k_w, rwkv_mu, rwkv_w0, rwkv_w2, rwkv_a0, rwkv_a2, rwkv_g2, rwkv_k_k, rwkv_k_a, rwkv_r_k, rwkv_lnx_w, rwkv_lnx_b, hyb_w_out, conv_w_in, conv_w, conv_w_out, peer_w_q, peer_keys, peer_u, peer_v):
    bsz, s, d = x.shape
    t = bsz * s
    depth = ada_w.shape[0]
    mod = _ada_mod(c, ada_w, ada_b)
    w_q_all = peer_w_q.astype(BF16)
    u_all = peer_u
    v_all = peer_v
    for layer in range(depth):
        sh_m, sc_m, gt_m, sh_f, sc_f, gt_f = [mod[layer, :, j * d:(j + 1) * d] for j in range(6)]
        h = _norm_mod(x, norm_mix_w[layer], sh_m, sc_m)
        i = layer // 2
        if layer % 2 == 0:
            y = _hybrid_mixer(h, positions, hyb_w_in[i], mla_q_norm_w[i], mla_w_uq[i],
                              mla_kv_norm_w[i], mla_w_ukv[i], mla_qk_q_w[i], mla_qk_k_w[i],
                              rwkv_mu[i], rwkv_w0[i], rwkv_w2[i], rwkv_a0[i], rwkv_a2[i],
                              rwkv_g2[i], rwkv_k_k[i], rwkv_k_a[i], rwkv_r_k[i],
                              rwkv_lnx_w[i], rwkv_lnx_b[i])
            w_out = hyb_w_out.astype(BF16)
        else:
            y = _conv_mixer(h.reshape(t, d), conv_w_in[i], conv_w[i], s)
            w_out = conv_w_out.astype(BF16)
        x = _gated_residual_matmul(y, w_out, i, x.reshape(t, d), gt_m, s, tk=d,
                                   name="mixer_out_proj").reshape(bsz, s, d)
        x = _peer_ffn(x, norm_ffn_w[layer], sh_f, sc_f, gt_f, layer, w_q_all, peer_keys[layer],
                      u_all, v_all)
    return x
```

```python
import functools

import numpy as np
import jax
import jax.numpy as jnp
from jax import lax
from jax.experimental import pallas as pl
from jax.experimental.pallas import tpu as pltpu

F32 = jnp.float32
BF16 = jnp.bfloat16
I32 = jnp.int32

MLA_HEADS = 16
QK_NOPE = 128
QK_ROPE = 64
QK_HEAD = QK_NOPE + QK_ROPE
V_HEAD = 128
Q_LORA = 768
KV_LORA = 512
ROPE_THETA = 10000.0
RWKV_HEAD = 64
RWKV_HEADS = 32
RWKV_WIDTH = RWKV_HEADS * RWKV_HEAD
W_LORA = 64
A_LORA = 64
G_LORA = 128
LNX_EPS = 64e-5
CONV_WIDTH = 3
PEER_HEADS = 8
N_KEYS = 128
PEER_TOPK = 16
PEER_HALF = 128
NORM_EPS = 1e-6

LANES = 128
SUBLANES = 8
VMEM_LIMIT = 56 * 1024 * 1024

P_CKV = 0
P_KPE = 512
P_CQ = 768
P_LORA = 1536
P_WIDTH = 2048
RWKV_CHUNK = 64
RWKV_PAIRS_PER_STEP = 8
ATTN_HEADS_PER_STEP = 8
ROUTE_HEADS_PER_STEP = 8


def _cparams(sem):
    return pltpu.CompilerParams(dimension_semantics=sem, vmem_limit_bytes=VMEM_LIMIT)


def _dot(a, b):
    return lax.dot_general(a, b, (((1,), (0,)), ((), ())), preferred_element_type=F32)


def _dot_nt(a, b):
    return lax.dot_general(a, b, (((1,), (1,)), ((), ())), preferred_element_type=F32)


def _dot_tn(a, b):
    return lax.dot_general(a, b, (((0,), (0,)), ((), ())), preferred_element_type=F32)


def _mm_body(*refs, nk, n_extra, epilogue):
    a_ref, b_ref = refs[:2]
    extra = refs[2:2 + n_extra]
    o_ref = refs[2 + n_extra]
    part = _dot(a_ref[...].astype(BF16), b_ref[...].astype(BF16))

    def finish(acc):
        o_ref[...] = epilogue(acc, *[e[...] for e in extra]).astype(o_ref.dtype)

    if nk == 1:
        finish(part)
        return
    acc_ref = refs[3 + n_extra]
    k = pl.program_id(2)

    @pl.when(k == 0)
    def _():
        acc_ref[...] = part

    @pl.when(k > 0)
    def _():
        acc_ref[...] += part

    @pl.when(k == nk - 1)
    def _():
        finish(acc_ref[...])


def _matmul(a, b, *, tm, tn, tk, out_dtype, epilogue=None, extras=(), b_lead=None,
            name="matmul"):
    m, kdim = a.shape
    n = b.shape[-1]
    nk = kdim // tk
    assert m % tm == 0 and kdim % tk == 0 and n % tn == 0
    if epilogue is None:
        epilogue = lambda acc: acc
    if b_lead is None:
        b_spec = pl.BlockSpec((tk, tn), lambda i, j, k: (k, j))
    else:
        b_spec = pl.BlockSpec((None, tk, tn), lambda i, j, k: (b_lead, k, j))
    in_specs = [pl.BlockSpec((tm, tk), lambda i, j, k: (i, k)), b_spec]
    extra_arrays = []
    for arr, bshape, imap in extras:
        in_specs.append(pl.BlockSpec(bshape, lambda i, j, k, imap=imap: imap(i, j)))
        extra_arrays.append(arr)
    return pl.pallas_call(
        functools.partial(_mm_body, nk=nk, n_extra=len(extras), epilogue=epilogue),
        out_shape=jax.ShapeDtypeStruct((m, n), out_dtype),
        grid=(m // tm, n // tn, nk),
        in_specs=in_specs,
        out_specs=pl.BlockSpec((tm, tn), lambda i, j, k: (i, j)),
        scratch_shapes=[pltpu.VMEM((tm, tn), F32)] if nk > 1 else [],
        name=name,
        compiler_params=_cparams(("parallel", "parallel", "arbitrary")),
    )(a, b, *extra_arrays)


def _pick(n, prefs):
    for p in prefs:
        if n % p == 0:
            return p
    raise ValueError(f"no tile for {n}")


def _ada_body(c_ref, w_ref, b_ref, o_ref):
    c = c_ref[...]
    cond = (c * (1.0 / (1.0 + jnp.exp(-c)))).astype(BF16)
    o_ref[...] = _dot(cond, w_ref[...].astype(BF16)) + b_ref[...]


def _ada_mod(c, ada_w, ada_b):
    depth, d, n = ada_w.shape
    bsz = c.shape[0]
    tn = 512
    return pl.pallas_call(
        _ada_body,
        out_shape=jax.ShapeDtypeStruct((depth, bsz, n), F32),
        grid=(depth, n // tn),
        in_specs=[
            pl.BlockSpec((bsz, d), lambda l, j: (0, 0)),
            pl.BlockSpec((None, d, tn), lambda l, j: (l, 0, j)),
            pl.BlockSpec((None, 1, tn), lambda l, j: (l, 0, j)),
        ],
        out_specs=pl.BlockSpec((None, bsz, tn), lambda l, j: (l, 0, j)),
        name="ada_mod",
        compiler_params=_cparams(("parallel", "parallel")),
    )(c, ada_w, ada_b.reshape(depth, 1, n))


def _norm_mod_body(x_ref, w_ref, sh_ref, sc_ref, o_ref):
    x = x_ref[...]
    ms = jnp.mean(x * x, axis=-1, keepdims=True)
    y = x * lax.rsqrt(ms + NORM_EPS) * w_ref[...]
    o_ref[...] = (y * (1.0 + sc_ref[...]) + sh_ref[...]).astype(o_ref.dtype)


def _norm_mod(x, w, shift, scale):
    bsz, s, d = x.shape
    ts = _pick(s, (512, 256, 128))
    vec = lambda v: v.reshape(bsz, 1, d)
    return pl.pallas_call(
        _norm_mod_body,
        out_shape=jax.ShapeDtypeStruct((bsz, s, d), BF16),
        grid=(bsz, s // ts),
        in_specs=[
            pl.BlockSpec((None, ts, d), lambda b, i: (b, i, 0)),
            pl.BlockSpec((1, d), lambda b, i: (0, 0)),
            pl.BlockSpec((None, 1, d), lambda b, i: (b, 0, 0)),
            pl.BlockSpec((None, 1, d), lambda b, i: (b, 0, 0)),
        ],
        out_specs=pl.BlockSpec((None, ts, d), lambda b, i: (b, i, 0)),
        name="norm_mod",
        compiler_params=_cparams(("parallel", "parallel")),
    )(x, w.reshape(1, d), vec(shift), vec(scale))


def _rope_body(pos_ref, freq_ref, cos_ref, sin_ref):
    ang = pos_ref[...].astype(F32) * freq_ref[...]
    lane = lax.broadcasted_iota(I32, ang.shape, 1)
    cos_ref[...] = jnp.where(lane < QK_ROPE, jnp.cos(ang), 0.0)
    sin = jnp.sin(ang)
    sin_ref[...] = jnp.where(lane < QK_ROPE // 2, -sin, jnp.where(lane < QK_ROPE, sin, 0.0))


def _rope_tables(pos, freq):
    t = pos.shape[0]
    tm = _pick(t, (1024, 512, 256, 128))
    out = jax.ShapeDtypeStruct((t, LANES), F32)
    row = pl.BlockSpec((tm, LANES), lambda i: (i, 0))
    return pl.pallas_call(
        _rope_body, out_shape=[out, out], grid=(t // tm,),
        in_specs=[pl.BlockSpec((tm, 1), lambda i: (i, 0)), pl.BlockSpec((1, LANES), lambda i: (0, 0))],
        out_specs=[row, row], name="rope_tables", compiler_params=_cparams(("parallel",)),
    )(pos, freq)


def _mla_q_body(cq_ref, nw_ref, wuq_ref, hw_ref, cos_ref, sin_ref, o_ref, *, scale):
    cq = cq_ref[...]
    ms = jnp.mean(cq * cq, axis=-1, keepdims=True)
    cqn = (cq * lax.rsqrt(ms + NORM_EPS) * nw_ref[...]).astype(BF16)
    cos, sin_signed = cos_ref[...], sin_ref[...]
    w_nope = hw_ref[0:1, :]
    w_rope = hw_ref[1:2, :]
    w_swap = hw_ref[2:3, :]
    for h in range(MLA_HEADS):
        qh = _dot(cqn, wuq_ref[:, h * 3 * LANES:(h + 1) * 3 * LANES])
        nope = qh[:, :LANES]
        rope = qh[:, LANES:2 * LANES]
        swap = qh[:, 2 * LANES:]
        ss = jnp.sum(nope * nope + rope * rope, axis=-1, keepdims=True)
        fac = lax.rsqrt(ss * (1.0 / QK_HEAD) + NORM_EPS) * scale
        o_ref[:, h * 2 * LANES:h * 2 * LANES + LANES] = (nope * fac * w_nope).astype(o_ref.dtype)
        roped = (rope * w_rope * cos + swap * w_swap * sin_signed) * fac
        o_ref[:, h * 2 * LANES + LANES:(h + 1) * 2 * LANES] = roped.astype(o_ref.dtype)


def _mla_kv_body(ckv_ref, kpe_ref, nw_ref, wukv_ref, hw_ref, cos_ref, sin_ref, k_ref, v_ref):
    ckv = ckv_ref[...]
    ms = jnp.mean(ckv * ckv, axis=-1, keepdims=True)
    ckvn = (ckv * lax.rsqrt(ms + NORM_EPS) * nw_ref[...]).astype(BF16)
    cos, sin_signed = cos_ref[...], sin_ref[...]
    kpe = kpe_ref[:, :LANES]
    kpe_swap = kpe_ref[:, LANES:]
    pe_ss = jnp.sum(kpe * kpe, axis=-1, keepdims=True)
    pe_roped = kpe * hw_ref[1:2, :] * cos + kpe_swap * hw_ref[2:3, :] * sin_signed
    w_nope = hw_ref[0:1, :]
    for h in range(MLA_HEADS):
        kn = _dot(ckvn, wukv_ref[:, h * LANES:(h + 1) * LANES])
        ss = jnp.sum(kn * kn, axis=-1, keepdims=True) + pe_ss
        fac = lax.rsqrt(ss * (1.0 / QK_HEAD) + NORM_EPS)
        k_ref[:, h * 2 * LANES:h * 2 * LANES + LANES] = (kn * fac * w_nope).astype(k_ref.dtype)
        k_ref[:, h * 2 * LANES + LANES:(h + 1) * 2 * LANES] = (pe_roped * fac).astype(k_ref.dtype)
    nv = MLA_HEADS * V_HEAD
    v_ref[...] = _dot(ckvn, wukv_ref[:, nv:]).astype(v_ref.dtype)


def _head_norm_rows(w):
    half = QK_ROPE // 2
    zeros = jnp.zeros((LANES - QK_ROPE,), F32)
    rope = jnp.concatenate([w[QK_NOPE:], zeros])
    swap = jnp.concatenate([w[QK_NOPE + half:], w[QK_NOPE:QK_NOPE + half], zeros])
    rows = jnp.stack([w[:QK_NOPE], rope, swap])
    return jnp.concatenate([rows, jnp.zeros((SUBLANES - 3, LANES), F32)], axis=0)


def _rope_freq_row():
    half = QK_ROPE // 2
    f = ROPE_THETA ** (-jnp.arange(half, dtype=F32) / half)
    return jnp.concatenate([f, f, jnp.zeros((LANES - QK_ROPE,), F32)]).reshape(1, LANES)


def _wuq_layout():
    half = QK_ROPE // 2
    idx, mask = [], []
    for h in range(MLA_HEADS):
        base = h * QK_HEAD
        nope = list(range(base, base + QK_NOPE))
        rope = list(range(base + QK_NOPE, base + QK_HEAD))
        swap = rope[half:] + rope[:half]
        pad = [0] * (LANES - QK_ROPE)
        idx += nope + rope + pad + swap + pad
        mask += [1] * QK_NOPE + [1] * QK_ROPE + [0] * len(pad) + [1] * QK_ROPE + [0] * len(pad)
    return np.asarray(idx, np.int32), np.asarray(mask, np.float32)


def _wukv_layout():
    per = QK_NOPE + V_HEAD
    k_idx = [h * per + i for h in range(MLA_HEADS) for i in range(QK_NOPE)]
    v_idx = [h * per + QK_NOPE + i for h in range(MLA_HEADS) for i in range(V_HEAD)]
    return np.asarray(k_idx + v_idx, np.int32)


def _mla_qkv(p, positions, q_norm_w, w_uq, kv_norm_w, w_ukv, qk_q_w, qk_k_w):
    t = p.shape[0]
    tm = _pick(t, (256, 128))
    idx, mask = _wuq_layout()
    wuq = (w_uq[:, idx] * mask[None, :]).astype(BF16)
    wukv = w_ukv[:, _wukv_layout()].astype(BF16)
    cos, sin_signed = _rope_tables(positions.reshape(t, 1), _rope_freq_row())
    hq = _head_norm_rows(qk_q_w)
    hk = _head_norm_rows(qk_k_w)
    qw = MLA_HEADS * 2 * LANES
    row = lambda i: (i, 0)
    fixed = lambda i: (0, 0)
    q = pl.pallas_call(
        functools.partial(_mla_q_body, scale=QK_HEAD ** -0.5),
        out_shape=jax.ShapeDtypeStruct((t, qw), BF16),
        grid=(t // tm,),
        in_specs=[
            pl.BlockSpec((tm, Q_LORA), lambda i: (i, P_CQ // Q_LORA)),
            pl.BlockSpec((1, Q_LORA), fixed),
            pl.BlockSpec(wuq.shape, fixed),
            pl.BlockSpec((SUBLANES, LANES), fixed),
            pl.BlockSpec((tm, LANES), row),
            pl.BlockSpec((tm, LANES), row),
        ],
        out_specs=pl.BlockSpec((tm, qw), row),
        name="mla_q",
        compiler_params=_cparams(("parallel",)),
    )(p, q_norm_w.reshape(1, Q_LORA), wuq, hq, cos, sin_signed)
    k, v = pl.pallas_call(
        _mla_kv_body,
        out_shape=[jax.ShapeDtypeStruct((t, qw), BF16),
                   jax.ShapeDtypeStruct((t, MLA_HEADS * V_HEAD), BF16)],
        grid=(t // tm,),
        in_specs=[
            pl.BlockSpec((tm, KV_LORA), lambda i: (i, P_CKV // KV_LORA)),
            pl.BlockSpec((tm, 2 * LANES), lambda i: (i, P_KPE // (2 * LANES))),
            pl.BlockSpec((1, KV_LORA), fixed),
            pl.BlockSpec(wukv.shape, fixed),
            pl.BlockSpec((SUBLANES, LANES), fixed),
            pl.BlockSpec((tm, LANES), row),
            pl.BlockSpec((tm, LANES), row),
        ],
        out_specs=[pl.BlockSpec((tm, qw), row), pl.BlockSpec((tm, MLA_HEADS * V_HEAD), row)],
        name="mla_kv",
        compiler_params=_cparams(("parallel",)),
    )(p, p, kv_norm_w.reshape(1, KV_LORA), wukv, hk, cos, sin_signed)
    return q, k, v


def _attn_body(qi_ref, ki_ref, q_ref, k_ref, v_ref, o_ref, m_ref, l_ref, acc_ref, *, tq):
    step_id = pl.program_id(2)
    qi = qi_ref[step_id]
    ki = ki_ref[step_id]

    @pl.when(ki == 0)
    def _():
        m_ref[...] = jnp.full(m_ref.shape, -jnp.inf, F32)
        l_ref[...] = jnp.zeros(l_ref.shape, F32)
        acc_ref[...] = jnp.zeros(acc_ref.shape, F32)

    heads = range(ATTN_HEADS_PER_STEP)
    qk = lambda h: slice(h * 2 * LANES, (h + 1) * 2 * LANES)
    vo = lambda h: slice(h * V_HEAD, (h + 1) * V_HEAD)

    def step(masked):
        ss = [_dot_nt(q_ref[:, qk(h)], k_ref[:, qk(h)]) for h in heads]
        if masked:
            row = lax.broadcasted_iota(I32, ss[0].shape, 0)
            col = lax.broadcasted_iota(I32, ss[0].shape, 1)
            ss = [jnp.where(col <= row, s, -jnp.inf) for s in ss]
        reps = tq // LANES
        prs, alphas = [], []
        for h, s in zip(heads, ss):
            m_prev = m_ref[h]
            m_new = jnp.maximum(m_prev, jnp.max(s, axis=-1, keepdims=True))
            alpha = jnp.exp(m_prev - m_new)
            pr = jnp.exp(s - jnp.tile(m_new, (1, reps)))
            l_ref[h] = alpha * l_ref[h] + jnp.sum(pr, axis=-1, keepdims=True)
            m_ref[h] = m_new
            prs.append(pr.astype(BF16))
            alphas.append(alpha)
        pvs = [_dot(pr, v_ref[:, vo(h)]) for h, pr in zip(heads, prs)]
        for h, alpha, pv in zip(heads, alphas, pvs):
            acc_ref[h] = alpha * acc_ref[h] + pv

    @pl.when(ki < qi)
    def _():
        step(False)

    @pl.when(ki == qi)
    def _():
        step(True)
        for h in heads:
            o_ref[:, vo(h)] = (acc_ref[h] / l_ref[h]).astype(o_ref.dtype)


def _attention(q, k, v, bsz, s):
    tq = _pick(s, (512, 256, 128))
    n = s // tq
    pairs = [(i, j) for i in range(n) for j in range(i + 1)]
    qi_tab = jnp.asarray([i for i, _ in pairs], I32)
    ki_tab = jnp.asarray([j for _, j in pairs], I32)
    nh = ATTN_HEADS_PER_STEP
    grid_spec = pltpu.PrefetchScalarGridSpec(
        num_scalar_prefetch=2,
        grid=(bsz, MLA_HEADS // nh, len(pairs)),
        in_specs=[
            pl.BlockSpec((None, tq, nh * 2 * LANES), lambda b, h, p, qi, ki: (b, qi[p], h)),
            pl.BlockSpec((None, tq, nh * 2 * LANES), lambda b, h, p, qi, ki: (b, ki[p], h)),
            pl.BlockSpec((None, tq, nh * V_HEAD), lambda b, h, p, qi, ki: (b, ki[p], h)),
        ],
        out_specs=pl.BlockSpec((None, tq, nh * V_HEAD), lambda b, h, p, qi, ki: (b, qi[p], h)),
        scratch_shapes=[pltpu.VMEM((nh, tq, LANES), F32), pltpu.VMEM((nh, tq, LANES), F32),
                        pltpu.VMEM((nh, tq, V_HEAD), F32)],
    )
    return pl.pallas_call(
        functools.partial(_attn_body, tq=tq),
        out_shape=jax.ShapeDtypeStruct((bsz, s, MLA_HEADS * V_HEAD), BF16),
        grid_spec=grid_spec,
        name="mla_attention",
        compiler_params=_cparams(("parallel", "parallel", "arbitrary")),
    )(qi_tab, ki_tab, q, k, v)


def _rwkv_body(r_ref, k_ref, v_ref, xl_ref, mu_r_ref, mu_k_ref, mu_v_ref, mu_l_ref, vec_ref,
               w2a2_ref, g2_ref, o_ref, state_ref, pr_ref, pk_ref, pv_ref, pl_ref, *, tb, npairs):
    t_idx = pl.program_id(2)

    @pl.when(t_idx == 0)
    def _():
        state_ref[...] = jnp.zeros(state_ref.shape, F32)
        pr_ref[...] = jnp.zeros(pr_ref.shape, F32)
        pk_ref[...] = jnp.zeros(pk_ref.shape, F32)
        pv_ref[...] = jnp.zeros(pv_ref.shape, F32)
        pl_ref[...] = jnp.zeros(pl_ref.shape, F32)

    def mix(x_ref, prev_ref, mu_ref):
        x = x_ref[...]
        row = lax.broadcasted_iota(I32, x.shape, 0)
        shifted = jnp.where(row == 0, prev_ref[...], pltpu.roll(x, 1, axis=0))
        prev_ref[...] = x[tb - 1:tb, :]
        return x + (shifted - x) * mu_ref[...]

    r = mix(r_ref, pr_ref, mu_r_ref)
    k = mix(k_ref, pk_ref, mu_k_ref)
    v = mix(v_ref, pv_ref, mu_v_ref)
    xl = mix(xl_ref, pl_ref, mu_l_ref)

    lane = lax.broadcasted_iota(I32, (tb, LANES), 1)
    first = lane < W_LORA
    x0 = xl[:, :LANES]
    lhs_w = jnp.where(first, jnp.tanh(x0), 0.0).astype(BF16)
    lhs_a = jnp.where(first, 0.0, x0).astype(BF16)
    xg = xl[:, LANES:]
    lhs_g = (1.0 / (1.0 + jnp.exp(-xg))).astype(BF16)
    cols = [slice(p * LANES, (p + 1) * LANES) for p in range(npairs)]
    outs = _rwkv_pairs([r[:, c] for c in cols], [k[:, c] for c in cols], [v[:, c] for c in cols],
                       lhs_w, lhs_a, lhs_g, [vec_ref[:, c] for c in cols],
                       [w2a2_ref[:, c] for c in cols], [g2_ref[:, c] for c in cols],
                       state_ref, tb)
    for c, out in zip(cols, outs):
        o_ref[:, c] = out.astype(o_ref.dtype)


def _rwkv_pairs(rs, ks, vs_in, lhs_w, lhs_a, lhs_g, vecs, w2a2s, g2s, state_ref, tb):
    c = RWKV_CHUNK
    pairs = range(len(rs))
    chunks = range(tb // c)
    each = lambda f, *lists: [f(*args) for args in zip(*lists)]

    ri = lax.broadcasted_iota(I32, (LANES, LANES), 0)
    ci = lax.broadcasted_iota(I32, (LANES, LANES), 1)
    same = (ri // RWKV_HEAD) == (ci // RWKV_HEAD)
    ones_bd = jnp.where(same, 1.0, 0.0).astype(BF16)
    tri_r = lax.broadcasted_iota(I32, (c, c), 0)
    tri_c = lax.broadcasted_iota(I32, (c, c), 1)
    tri = jnp.where(tri_c <= tri_r, 1.0, 0.0).astype(BF16)
    eye = jnp.where(ri == ci, 1.0, 0.0)
    first_c = lax.broadcasted_iota(I32, (c, LANES), 1) < RWKV_HEAD

    def stack(z):
        return jnp.concatenate([jnp.where(first_c, z, 0.0), jnp.where(first_c, 0.0, z)],
                               axis=0).astype(BF16)

    def split(x):
        hi = x.astype(BF16)
        return hi, (x - hi.astype(F32)).astype(BF16)

    def group_sum(xs):
        parts = each(split, xs)
        his = [_dot(hi, ones_bd) for hi, _ in parts]
        los = [_dot(lo, ones_bd) for _, lo in parts]
        return each(lambda a, b: a + b, his, los)

    lws = [_dot(lhs_w, w) for w in w2a2s]
    las = [_dot(lhs_a, w) for w in w2a2s]
    gs = [_dot(lhs_g, g2) for g2 in g2s]
    kks = each(lambda k, vec: k * vec[2:3, :], ks, vecs)
    nrms = group_sum(each(lambda kk: kk * kk, kks))
    kks = each(lambda kk, n: kk / jnp.maximum(jnp.sqrt(n), 1e-12), kks, nrms)

    def decay_and_gate(lw, la, vec):
        zw = -(vec[0:1, :] + lw)
        softplus = jnp.maximum(zw, 0.0) + jnp.log(1.0 + jnp.exp(-jnp.abs(zw)))
        a = 1.0 / (1.0 + jnp.exp(-(vec[1:2, :] + la)))
        return -jnp.exp(-softplus - 0.5), a

    dg = each(decay_and_gate, lws, las, vecs)
    logds = [d for d, _ in dg]
    a_gates = [a for _, a in dg]
    kps = each(lambda k, a, vec: k * (1.0 + (a - 1.0) * vec[3:4, :]), ks, a_gates, vecs)
    bvecs = each(lambda kk, a: kk * a, kks, a_gates)
    bonus_sums = group_sum(each(lambda r, kp, vec: r * kp * vec[4:5, :], rs, kps, vecs))

    inst = [(p, ch) for p in pairs for ch in chunks]
    rows = lambda x, ch: x[ch * c:(ch + 1) * c]
    lds = [rows(logds[p], ch) for p, ch in inst]
    ld_parts = each(split, lds)
    lc_hi = [_dot(tri, hi) for hi, _ in ld_parts]
    lc_lo = [_dot(tri, lo) for _, lo in ld_parts]
    lcs = each(lambda a, b: a + b, lc_hi, lc_lo)
    ltots = [lc[c - 1:c, :] for lc in lcs]
    e_negs = [jnp.exp(-lc) for lc in lcs]
    e_rems = each(lambda lt, lc: jnp.exp(lt - lc), ltots, lcs)
    xas = [stack(-rows(kks[p], ch) * jnp.exp(lc - ld)) for (p, ch), lc, ld in zip(inst, lcs, lds)]
    xrs = [stack(rows(rs[p], ch) * jnp.exp(lc)) for (p, ch), lc in zip(inst, lcs)]
    ybs = [stack(rows(bvecs[p], ch) * e) for (p, ch), e in zip(inst, e_negs)]
    yks = [stack(rows(kps[p], ch) * e) for (p, ch), e in zip(inst, e_negs)]
    vss = [stack(rows(vs_in[p], ch)) for p, ch in inst]
    bhs = [stack(rows(bvecs[p], ch) * e) for (p, ch), e in zip(inst, e_rems)]
    khs = [stack(rows(kps[p], ch) * e) for (p, ch), e in zip(inst, e_rems)]

    n2 = 2 * c
    xars = each(lambda a, b: jnp.concatenate([a, b], axis=0), xas, xrs)
    rr = lax.broadcasted_iota(I32, (2 * n2, n2), 0)
    cc = lax.broadcasted_iota(I32, (2 * n2, n2), 1)
    rr_in = rr % n2
    tri_mask = ((rr_in // c) == (cc // c)) & ((cc % c) < (rr_in % c) + rr // n2)
    mbs = each(lambda a, b: jnp.where(tri_mask, _dot_nt(a, b), 0.0), xars, ybs)
    mks = each(lambda a, b: jnp.where(tri_mask, _dot_nt(a, b), 0.0).astype(BF16), xars, yks)
    labs = [mb[:n2] for mb in mbs]
    mrbs = [mb[n2:].astype(BF16) for mb in mbs]

    pbs = [lab.astype(BF16) for lab in labs]
    tinvs = [eye + lab for lab in labs]
    pbs = [_dot(pb, pb).astype(BF16) for pb in pbs]
    n_sq = 2
    while n_sq * 2 < c:
        both = each(lambda pb, ti: _dot(pb, jnp.concatenate([ti.astype(BF16), pb], axis=1)),
                    pbs, tinvs)
        tinvs = each(lambda ti, bo: ti + bo[:, :n2], tinvs, both)
        pbs = [bo[:, n2:].astype(BF16) for bo in both]
        n_sq *= 2
    tinvs = each(lambda ti, pb: (ti + _dot(pb, ti.astype(BF16))).astype(BF16), tinvs, pbs)
    mkvs = each(_dot, mks, vss)
    kvs = each(_dot_tn, vss, khs)

    states = [state_ref[p] for p in pairs]
    ys = [[] for _ in pairs]
    n_ch = len(chunks)
    for ch in chunks:
        ids = [p * n_ch + ch for p in pairs]
        sbs = [s.astype(BF16) for s in states]
        withs = [_dot_nt(xars[i], sb) + mkvs[i] for i, sb in zip(ids, sbs)]
        ubs = [_dot(tinvs[i], w[:n2].astype(BF16)).astype(BF16) for i, w in zip(ids, withs)]
        states = [s * jnp.exp(ltots[i]) + _dot_tn(ub, bhs[i]) + kvs[i]
                  for i, s, ub in zip(ids, states, ubs)]
        for p, i, w, ub in zip(pairs, ids, withs, ubs):
            ystack = w[n2:] + _dot(mrbs[i], ub)
            ys[p].append(ystack[:c] + ystack[c:])
    for p, s in zip(pairs, states):
        state_ref[p] = s

    ycat = [jnp.concatenate(y, axis=0) for y in ys]
    inv_n = 1.0 / RWKV_HEAD
    means = [m * inv_n for m in group_sum(ycat)]
    ds = each(lambda y, m: y - m, ycat, means)
    variances = [v * inv_n for v in group_sum(each(lambda d: d * d, ds))]
    outs = []
    for d, var, vec, bs, v, g in zip(ds, variances, vecs, bonus_sums, vs_in, gs):
        yn = d * lax.rsqrt(var + LNX_EPS) * vec[5:6, :] + vec[6:7, :]
        outs.append((yn + bs * v) * g)
    return outs


def _rwkv(p_rkv, p_mla, mu_rkv, mu_lora, vecs, w2a2, g2):
    assert W_LORA + A_LORA == LANES and G_LORA == LANES
    bsz, s, _ = p_rkv.shape
    tb = _pick(s, (512, 256, 128, 64))
    npairs = RWKV_PAIRS_PER_STEP
    wd = npairs * LANES
    groups = RWKV_HEADS // 2 // npairs
    colblk = lambda off: (lambda b, hp, t: (b, t, off // wd + hp))
    mublk = lambda off: (lambda b, hp, t: (0, off // wd + hp))
    o_r, o_k, o_v = 0, RWKV_WIDTH, 2 * RWKV_WIDTH
    return pl.pallas_call(
        functools.partial(_rwkv_body, tb=tb, npairs=npairs),
        out_shape=jax.ShapeDtypeStruct((bsz, s, RWKV_WIDTH), BF16),
        grid=(bsz, groups, s // tb),
        in_specs=[
            pl.BlockSpec((None, tb, wd), colblk(o_r)),
            pl.BlockSpec((None, tb, wd), colblk(o_k)),
            pl.BlockSpec((None, tb, wd), colblk(o_v)),
            pl.BlockSpec((None, tb, 2 * LANES), lambda b, hp, t: (b, t, P_LORA // (2 * LANES))),
            pl.BlockSpec((1, wd), mublk(o_r)),
            pl.BlockSpec((1, wd), mublk(o_k)),
            pl.BlockSpec((1, wd), mublk(o_v)),
            pl.BlockSpec((1, 2 * LANES), lambda b, hp, t: (0, 0)),
            pl.BlockSpec((SUBLANES, wd), lambda b, hp, t: (0, hp)),
            pl.BlockSpec((LANES, wd), lambda b, hp, t: (0, hp)),
            pl.BlockSpec((LANES, wd), lambda b, hp, t: (0, hp)),
        ],
        out_specs=pl.BlockSpec((None, tb, wd), lambda b, hp, t: (b, t, hp)),
        scratch_shapes=[pltpu.VMEM((npairs, LANES, LANES), F32),
                        pltpu.VMEM((1, wd), F32), pltpu.VMEM((1, wd), F32),
                        pltpu.VMEM((1, wd), F32), pltpu.VMEM((1, 2 * LANES), F32)],
        name="rwkv7",
        compiler_params=_cparams(("parallel", "parallel", "arbitrary")),
    )(p_rkv, p_rkv, p_rkv, p_mla, mu_rkv, mu_rkv, mu_rkv, mu_lora, vecs, w2a2, g2)


def _conv_body(h_ref, wb_ref, wc_ref, wu_ref, cw_ref, o_ref, tail_ref, *, tm, seq):
    i = pl.program_id(1)
    a = h_ref[...]
    gate_b = _dot(a, wb_ref[...])
    uu = _dot(a, wc_ref[...]) * _dot(a, wu_ref[...])

    @pl.when((i * tm) % seq == 0)
    def _():
        tail_ref[...] = jnp.zeros(tail_ref.shape, F32)

    tail = tail_ref[...]
    row = lax.broadcasted_iota(I32, uu.shape, 0)
    h1 = tail[SUBLANES - 1:SUBLANES, :]
    h2 = tail[SUBLANES - 2:SUBLANES - 1, :]
    s1 = jnp.where(row == 0, h1, pltpu.roll(uu, 1, axis=0))
    s2 = jnp.where(row == 0, h2, jnp.where(row == 1, h1, pltpu.roll(uu, 2, axis=0)))
    y = cw_ref[0:1, :] * s2 + cw_ref[1:2, :] * s1 + cw_ref[2:3, :] * uu
    o_ref[...] = (gate_b * y).astype(o_ref.dtype)
    tail_ref[...] = uu[tm - SUBLANES:tm, :]


def _conv_mixer(h, w_in, conv_w, s):
    t, d = h.shape
    tm = _pick(s, (512, 256, 128))
    tn = 512
    nblk = d // tn
    w = w_in.astype(BF16)
    cw = jnp.concatenate([conv_w, jnp.zeros((SUBLANES - CONV_WIDTH, d), F32)], axis=0)
    wspec = lambda off: pl.BlockSpec((d, tn), lambda j, i: (0, j + off))
    return pl.pallas_call(
        functools.partial(_conv_body, tm=tm, seq=s),
        out_shape=jax.ShapeDtypeStruct((t, d), BF16),
        grid=(nblk, t // tm),
        in_specs=[pl.BlockSpec((tm, d), lambda j, i: (i, 0)),
                  wspec(0), wspec(nblk), wspec(2 * nblk),
                  pl.BlockSpec((SUBLANES, tn), lambda j, i: (0, j))],
        out_specs=pl.BlockSpec((tm, tn), lambda j, i: (i, j)),
        scratch_shapes=[pltpu.VMEM((SUBLANES, tn), F32)],
        name="conv_mixer",
        compiler_params=_cparams(("parallel", "arbitrary")),
    )(h, w, w, w, cw)


def _rank_pairs():
    kt = PEER_TOPK
    return [(a, b) for a in range(kt) for b in range(kt) if (a + 1) * (b + 1) <= kt]


PAIR_ROWS = -(-len(_rank_pairs()) // SUBLANES) * SUBLANES


def _pair_matrices():
    m1 = np.zeros((PAIR_ROWS, PEER_TOPK), np.float32)
    m2 = np.zeros((PAIR_ROWS, PEER_TOPK), np.float32)
    for p, (a, b) in enumerate(_rank_pairs()):
        m1[p, a] = 1.0
        m2[p, b] = 1.0
    return m1, m2


def _exact_dot_left(m, x):
    hi = x.astype(BF16)
    r1 = x - hi.astype(F32)
    mid = r1.astype(BF16)
    lo = (r1 - mid.astype(F32)).astype(BF16)
    return (_dot(m, hi) + _dot(m, mid)) + _dot(m, lo)


def _extract_top_rows(xs, row_iota, n, rank_iota, extras=None):
    xs = list(xs)
    sentinel = float(xs[0].shape[0])
    zeros = lambda: [jnp.zeros((n, x.shape[1]), F32) for x in xs]
    vals, idxs, exts = zeros(), zeros(), zeros()
    for kk in range(n):
        here = rank_iota == kk
        for j, x in enumerate(xs):
            m = jnp.max(x, axis=0, keepdims=True)
            idx = jnp.min(jnp.where(x == m, row_iota, sentinel), axis=0, keepdims=True)
            hit = row_iota == idx
            if extras is not None:
                ext = jnp.max(jnp.where(hit, extras[j], -1.0), axis=0, keepdims=True)
                exts[j] = jnp.where(here, ext, exts[j])
            xs[j] = jnp.where(hit, -jnp.inf, x)
            vals[j] = jnp.where(here, m, vals[j])
            idxs[j] = jnp.where(here, idx, idxs[j])
    return list(zip(vals, idxs, exts))


def _route_body(q_ref, keys_ref, m1_ref, m2_ref, e1_ref, e2_ref, gw_ref):
    kt = PEER_TOPK
    tb = q_ref.shape[0]
    nh = ROUTE_HEADS_PER_STEP
    key_iota = lax.broadcasted_iota(I32, (N_KEYS, tb), 0).astype(F32)
    rank_iota = lax.broadcasted_iota(I32, (kt, tb), 0)
    scs = []
    for hp in range(2 * nh):
        qh = q_ref[:, hp * PEER_HALF:(hp + 1) * PEER_HALF].astype(BF16)
        scs.append(_dot_nt(keys_ref[hp], qh))
    tops = _extract_top_rows(scs, key_iota, kt, rank_iota)
    m1 = m1_ref[...]
    m2 = m2_ref[...]
    pair_iota = lax.broadcasted_iota(I32, (PAIR_ROWS, tb), 0).astype(F32)
    cands, eids = [], []
    for h in range(nh):
        (sv1, si1, _), (sv2, si2, _) = tops[2 * h], tops[2 * h + 1]
        cand = _exact_dot_left(m1, sv1) + _exact_dot_left(m2, sv2)
        cands.append(jnp.where(pair_iota < float(len(_rank_pairs())), cand, -jnp.inf))
        eids.append(_dot(m1, si1.astype(BF16)) * float(N_KEYS) + _dot(m2, si2.astype(BF16)))
    picks = _extract_top_rows(cands, pair_iota, kt, rank_iota, extras=eids)
    for h, (cv, _, ev) in enumerate(picks):
        rows = slice(h * kt, (h + 1) * kt)
        ex = jnp.exp(cv - cv[0:1, :])
        gw_ref[rows, :] = ex / jnp.sum(ex, axis=0, keepdims=True)
        ei = ev.astype(I32)
        e1_ref[rows, :] = (ei >> 7).astype(F32)
        e2_ref[rows, :] = (ei & (N_KEYS - 1)).astype(F32)


def _peer_route(q, keys):
    t = q.shape[0]
    tb = LANES
    m1, m2 = _pair_matrices()
    slots = PEER_HEADS * PEER_TOPK
    out = jax.ShapeDtypeStruct((slots, t), F32)
    nh = ROUTE_HEADS_PER_STEP
    pair_spec = pl.BlockSpec((PAIR_ROWS, PEER_TOPK), lambda i, h: (0, 0))
    out_spec = pl.BlockSpec((nh * PEER_TOPK, tb), lambda i, h: (h, i))
    return pl.pallas_call(
        _route_body,
        out_shape=[out, out, out],
        grid=(t // tb, PEER_HEADS // nh),
        in_specs=[pl.BlockSpec((tb, nh * 2 * PEER_HALF), lambda i, h: (i, h)),
                  pl.BlockSpec((nh * 2, N_KEYS, PEER_HALF), lambda i, h: (h, 0, 0)),
                  pair_spec, pair_spec],
        out_specs=[out_spec, out_spec, out_spec],
        name="peer_route",
        compiler_params=_cparams(("parallel", "parallel")),
    )(q, keys, jnp.asarray(m1, BF16), jnp.asarray(m2, BF16))


def _swap_token_and_row(xs):
    xs = list(xs)
    row = lax.broadcasted_iota(I32, xs[0].shape, 1)
    d = SUBLANES // 2
    while d >= 1:
        low = (row % (2 * d)) < d
        for s in range(SUBLANES):
            if (s // d) % 2 == 0:
                a, b = xs[s], xs[s + d]
                xs[s] = jnp.where(low, a, pltpu.roll(b, d, axis=1))
                xs[s + d] = jnp.where(low, pltpu.roll(a, SUBLANES - d, axis=1), b)
        d //= 2
    return xs


GATE_TOKENS = 2 * SUBLANES


def _gate_body(e1t_ref, e2t_ref, gwt_ref, uf_ref, o_ref, ub_ref, e1_s, e2_s, gw_s):
    ub_ref[...] = uf_ref[...].astype(ub_ref.dtype)
    tb = o_ref.shape[2]
    e1_s[...] = e1t_ref[...].T
    e2_s[...] = e2t_ref[...].T
    gw_s[...] = gwt_ref[...].T
    sub = lax.broadcasted_iota(I32, (N_KEYS, LANES), 0).astype(F32)
    nblk = N_KEYS // SUBLANES

    def eight_tokens(t0):
        lhs, rhs = [], []
        for s in range(SUBLANES):
            e1 = e1_s[pl.ds(t0 + s, 1), :]
            e2 = e2_s[pl.ds(t0 + s, 1), :]
            gw = gw_s[pl.ds(t0 + s, 1), :]
            lhs.append(jnp.where(e1 == sub, gw, 0.0).astype(BF16))
            rhs.append(jnp.where(e2 == sub, 1.0, 0.0).astype(BF16))
        per_token = [_dot_nt(a, b).reshape(nblk, SUBLANES, N_KEYS) for a, b in zip(lhs, rhs)]
        return _swap_token_and_row(per_token)

    def sixteen_tokens(g, carry):
        t0 = pl.multiple_of(g * GATE_TOKENS, GATE_TOKENS)
        first, second = eight_tokens(t0), eight_tokens(t0 + SUBLANES)
        for r, (y0, y1) in enumerate(zip(first, second)):
            o_ref[:, r, pl.ds(t0, GATE_TOKENS), :] = jnp.concatenate(
                [y0, y1], axis=1).astype(o_ref.dtype)
        return carry

    lax.fori_loop(0, tb // GATE_TOKENS, sixteen_tokens, 0, unroll=4)


def _peer_gates(e1t, e2t, gwt, u_all, layer):
    t = e1t.shape[1]
    tb = LANES
    steps = t // tb
    nblk = N_KEYS // SUBLANES
    n_exp, d = u_all.shape[1:]
    rows = n_exp // steps
    assert rows * steps == n_exp and rows % GATE_TOKENS == 0
    in_spec = pl.BlockSpec((LANES, tb), lambda i: (0, i))
    g, u_bf16 = pl.pallas_call(
        _gate_body,
        out_shape=[jax.ShapeDtypeStruct((nblk, SUBLANES, t, N_KEYS), BF16),
                   jax.ShapeDtypeStruct((n_exp, d), BF16)],
        grid=(steps,),
        in_specs=[in_spec, in_spec, in_spec,
                  pl.BlockSpec((None, rows, d), lambda i: (layer, i, 0))],
        out_specs=[pl.BlockSpec((nblk, SUBLANES, tb, N_KEYS), lambda i: (0, 0, i, 0)),
                   pl.BlockSpec((rows, d), lambda i: (i, 0))],
        scratch_shapes=[pltpu.VMEM((tb, LANES), F32)] * 3,
        name="peer_gates",
        compiler_params=_cparams(("parallel",)),
    )(e1t, e2t, gwt, u_all)
    return g.reshape(N_KEYS, t, N_KEYS), u_bf16


def _peer_up_body(h_ref, u_ref, g_ref, o_ref):
    s = _dot_nt(h_ref[...], u_ref[...])
    act = 0.5 * s * (1.0 + lax.erf(s * (2.0 ** -0.5)))
    for j in range(o_ref.shape[1] // N_KEYS):
        cols = slice(j * N_KEYS, (j + 1) * N_KEYS)
        o_ref[:, cols] = (act[:, cols] * g_ref[j]).astype(o_ref.dtype)


def _peer_up(h, u_tab, g3):
    t, d = h.shape
    n_exp = u_tab.shape[0]
    tm = _pick(t, (1024, 512, 256, 128))
    tn = SUBLANES * N_KEYS
    return pl.pallas_call(
        _peer_up_body,
        out_shape=jax.ShapeDtypeStruct((t, n_exp), BF16),
        grid=(t // tm, n_exp // tn),
        in_specs=[pl.BlockSpec((tm, d), lambda i, j: (i, 0), pipeline_mode=pl.Buffered(1)),
                  pl.BlockSpec((tn, d), lambda i, j: (j, 0)),
                  pl.BlockSpec((tn // N_KEYS, tm, N_KEYS), lambda i, j: (j, i, 0))],
        out_specs=pl.BlockSpec((tm, tn), lambda i, j: (i, j)),
        name="peer_up",
        compiler_params=_cparams(("parallel", "parallel")),
    )(h, u_tab, g3)


def _residual(acc, x, gate):
    return x + gate * acc


def _gated_residual_matmul(a, w_all, layer, x2, gate, s, *, tk, name):
    t, d = x2.shape
    bsz = gate.shape[0]
    tm = _pick(s, (1024, 512, 256, 128))
    tn = 1024
    return _matmul(
        a, w_all, tm=tm, tn=tn, tk=tk, out_dtype=F32, epilogue=_residual,
        extras=[(x2, (tm, tn), lambda i, j: (i, j)),
                (gate.reshape(bsz, 1, d), (None, 1, tn), lambda i, j: (i * tm // s, 0, j))],
        b_lead=layer, name=name)


def _peer_ffn(x, w_norm, shift, scale, gate, layer, w_q_all, keys, u_all, v_all):
    bsz, s, d = x.shape
    t = bsz * s
    h = _norm_mod(x, w_norm, shift, scale).reshape(t, d)
    tm = _pick(s, (1024, 512, 256, 128))
    q = _matmul(h, w_q_all, tm=tm, tn=1024, tk=d, out_dtype=F32, b_lead=layer,
                name="peer_query")
    keys_b = keys.reshape(PEER_HEADS * 2, N_KEYS, PEER_HALF).astype(BF16)
    e1t, e2t, gwt = _peer_route(q, keys_b)
    g3, u_tab = _peer_gates(e1t, e2t, gwt, u_all, layer)
    pmat = _peer_up(h, u_tab, g3)
    out = _gated_residual_matmul(pmat, v_all, layer, x.reshape(t, d), gate, s,
                                 tk=2048, name="peer_down")
    return out.reshape(bsz, s, d)


def _pad_mla_proj(w):
    o_kv = Q_LORA
    o_pe = Q_LORA + KV_LORA
    o_rw = o_pe + QK_ROPE
    half = QK_ROPE // 2
    z = lambda n: jnp.zeros((w.shape[0], n), w.dtype)
    cq = w[:, :Q_LORA]
    ckv = w[:, o_kv:o_pe]
    kpe = w[:, o_pe:o_rw]
    kpe_swap = jnp.concatenate([kpe[:, half:], kpe[:, :half]], axis=1)
    lora = w[:, o_rw + 3 * RWKV_WIDTH:]
    pad = LANES - QK_ROPE
    out = jnp.concatenate([ckv, kpe, z(pad), kpe_swap, z(pad), cq, lora,
                           z(P_WIDTH - P_LORA - lora.shape[1])], axis=1)
    assert out.shape[1] == P_WIDTH
    return out


def _hybrid_mixer(h, positions, w_in, q_norm_w, w_uq, kv_norm_w, w_ukv, qk_q_w, qk_k_w,
                  mu, w0, w2, a0, a2, g2, k_k, k_a, r_k, lnx_w, lnx_b):
    bsz, s, d = h.shape
    t = bsz * s
    tm = _pick(s, (1024, 512, 256, 128))
    o_rw = Q_LORA + KV_LORA + QK_ROPE
    n_rkv = 3 * RWKV_WIDTH
    h2 = h.reshape(t, d)
    p_mla = _matmul(h2, _pad_mla_proj(w_in).astype(BF16), tm=tm, tn=1024, tk=d, out_dtype=F32,
                    name="hyb_in_proj_mla")
    p_rkv = _matmul(h2, w_in[:, o_rw:o_rw + n_rkv].astype(BF16), tm=tm, tn=1024, tk=d,
                    out_dtype=F32, name="hyb_in_proj_rkv")
    q, k, v = _mla_qkv(p_mla, positions, q_norm_w, w_uq, kv_norm_w, w_ukv, qk_q_w, qk_k_w)
    y_mla = _attention(q.reshape(bsz, s, -1), k.reshape(bsz, s, -1), v.reshape(bsz, s, -1), bsz, s)

    zrow = jnp.zeros((RWKV_WIDTH,), F32)
    vecs = jnp.stack([w0, a0, k_k, k_a, r_k.reshape(-1), lnx_w, lnx_b, zrow])
    w2a2 = jnp.concatenate([w2, a2], axis=0).astype(BF16)
    y_rwkv = _rwkv(p_rkv.reshape(bsz, s, n_rkv), p_mla.reshape(bsz, s, P_WIDTH),
                   mu[:n_rkv].reshape(1, n_rkv), mu[n_rkv:].reshape(1, -1), vecs, w2a2,
                   g2.astype(BF16))
    return jnp.concatenate([y_mla, y_rwkv], axis=-1).reshape(t, -1)


def kernel(x, c, positions, ada_w, ada_b, norm_mix_w, norm_ffn_w, hyb_w_in, mla_q_norm_w, mla_w_uq, mla_kv_norm_w, mla_w_ukv, mla_qk_q_w, mla_qk_k_w, rwkv_mu, rwkv_w0, rwkv_w2, rwkv_a0, rwkv_a2, rwkv_g2, rwkv_k_k, rwkv_k_a, rwkv_r_k, rwkv_lnx_w, rwkv_lnx_b, hyb_w_out, conv_w_in, conv_w, conv_w_out, peer_w_q, peer_keys, peer_u, peer_v):
    bsz, s, d = x.shape
    t = bsz * s
    depth = ada_w.shape[0]
    mod = _ada_mod(c, ada_w, ada_b)
    w_q_all = peer_w_q.astype(BF16)
    u_all = peer_u
    v_all = peer_v
    for layer in range(depth):
        sh_m, sc_m, gt_m, sh_f, sc_f, gt_f = [mod[layer, :, j * d:(j + 1) * d] for j in range(6)]
        h = _norm_mod(x, norm_mix_w[layer], sh_m, sc_m)
        i = layer // 2
        if layer % 2 == 0:
            y = _hybrid_mixer(h, positions, hyb_w_in[i], mla_q_norm_w[i], mla_w_uq[i],
                              mla_kv_norm_w[i], mla_w_ukv[i], mla_qk_q_w[i], mla_qk_k_w[i],
                              rwkv_mu[i], rwkv_w0[i], rwkv_w2[i], rwkv_a0[i], rwkv_a2[i],
                              rwkv_g2[i], rwkv_k_k[i], rwkv_k_a[i], rwkv_r_k[i],
                              rwkv_lnx_w[i], rwkv_lnx_b[i])
            w_out = hyb_w_out.astype(BF16)
        else:
            y = _conv_mixer(h.reshape(t, d), conv_w_in[i], conv_w[i], s)
            w_out = conv_w_out.astype(BF16)
        x = _gated_residual_matmul(y, w_out, i, x.reshape(t, d), gt_m, s, tk=d,
                                   name="mixer_out_proj").reshape(bsz, s, d)
        x = _peer_ffn(x, norm_ffn_w[layer], sh_f, sc_f, gt_f, layer, w_q_all, peer_keys[layer],
                      u_all, v_all)
    return x
```

```python
import functools

import numpy as np
import jax
import jax.numpy as jnp
from jax import lax
from jax.experimental import pallas as pl
from jax.experimental.pallas import tpu as pltpu

F32 = jnp.float32
BF16 = jnp.bfloat16
I32 = jnp.int32

MLA_HEADS = 16
QK_NOPE = 128
QK_ROPE = 64
QK_HEAD = QK_NOPE + QK_ROPE
V_HEAD = 128
Q_LORA = 768
KV_LORA = 512
ROPE_THETA = 10000.0
RWKV_HEAD = 64
RWKV_HEADS = 32
RWKV_WIDTH = RWKV_HEADS * RWKV_HEAD
W_LORA = 64
A_LORA = 64
G_LORA = 128
LNX_EPS = 64e-5
CONV_WIDTH = 3
PEER_HEADS = 8
N_KEYS = 128
PEER_TOPK = 16
PEER_HALF = 128
NORM_EPS = 1e-6

LANES = 128
SUBLANES = 8
VMEM_LIMIT = 56 * 1024 * 1024

P_CKV = 0
P_KPE = 512
P_CQ = 768
P_LORA = 1536
P_WIDTH = 2048
MM_ACC_CHUNK = 256
RWKV_CHUNK = 64
RWKV_PAIRS_PER_STEP = 8
ATTN_HEADS_PER_STEP = 8
ROUTE_HEADS_PER_STEP = 8


def _cparams(sem):
    return pltpu.CompilerParams(dimension_semantics=sem, vmem_limit_bytes=VMEM_LIMIT)


def _dot(a, b):
    return lax.dot_general(a, b, (((1,), (0,)), ((), ())), preferred_element_type=F32)


def _dot_nt(a, b):
    return lax.dot_general(a, b, (((1,), (1,)), ((), ())), preferred_element_type=F32)


def _dot_tn(a, b):
    return lax.dot_general(a, b, (((0,), (0,)), ((), ())), preferred_element_type=F32)


def _mm_body(*refs, nk, n_extra, epilogue):
    a_ref, b_ref = refs[:2]
    extra = refs[2:2 + n_extra]
    o_ref = refs[2 + n_extra]
    a = a_ref[...].astype(BF16)

    def finish(acc):
        o_ref[...] = epilogue(acc, *[e[...] for e in extra]).astype(o_ref.dtype)

    if nk == 1:
        finish(_dot(a, b_ref[...].astype(BF16)))
        return
    acc_ref = refs[3 + n_extra]
    k = pl.program_id(2)

    @pl.when(k == 0)
    def _():
        acc_ref[...] = jnp.zeros(acc_ref.shape, F32)

    tn = o_ref.shape[1]
    for c0 in range(0, tn, MM_ACC_CHUNK):
        cols = slice(c0, min(c0 + MM_ACC_CHUNK, tn))
        acc_ref[:, cols] += _dot(a, b_ref[:, cols].astype(BF16))

    @pl.when(k == nk - 1)
    def _():
        finish(acc_ref[...])


def _matmul(a, b, *, tm, tn, tk, out_dtype, epilogue=None, extras=(), b_lead=None,
            name="matmul"):
    m, kdim = a.shape
    n = b.shape[-1]
    nk = kdim // tk
    assert m % tm == 0 and kdim % tk == 0 and n % tn == 0
    if epilogue is None:
        epilogue = lambda acc: acc
    if b_lead is None:
        b_spec = pl.BlockSpec((tk, tn), lambda i, j, k: (k, j))
    else:
        b_spec = pl.BlockSpec((None, tk, tn), lambda i, j, k: (b_lead, k, j))
    in_specs = [pl.BlockSpec((tm, tk), lambda i, j, k: (i, k)), b_spec]
    extra_arrays = []
    for arr, bshape, imap in extras:
        in_specs.append(pl.BlockSpec(bshape, lambda i, j, k, imap=imap: imap(i, j)))
        extra_arrays.append(arr)
    return pl.pallas_call(
        functools.partial(_mm_body, nk=nk, n_extra=len(extras), epilogue=epilogue),
        out_shape=jax.ShapeDtypeStruct((m, n), out_dtype),
        grid=(m // tm, n // tn, nk),
        in_specs=in_specs,
        out_specs=pl.BlockSpec((tm, tn), lambda i, j, k: (i, j)),
        scratch_shapes=[pltpu.VMEM((tm, tn), F32)] if nk > 1 else [],
        name=name,
        compiler_params=_cparams(("parallel", "parallel", "arbitrary")),
    )(a, b, *extra_arrays)


def _pick(n, prefs):
    for p in prefs:
        if n % p == 0:
            return p
    raise ValueError(f"no tile for {n}")


def _ada_body(c_ref, w_ref, b_ref, o_ref):
    c = c_ref[...]
    cond = (c * (1.0 / (1.0 + jnp.exp(-c)))).astype(BF16)
    o_ref[...] = _dot(cond, w_ref[...].astype(BF16)) + b_ref[...]


def _ada_mod(c, ada_w, ada_b):
    depth, d, n = ada_w.shape
    bsz = c.shape[0]
    tn = 512
    return pl.pallas_call(
        _ada_body,
        out_shape=jax.ShapeDtypeStruct((depth, bsz, n), F32),
        grid=(depth, n // tn),
        in_specs=[
            pl.BlockSpec((bsz, d), lambda l, j: (0, 0)),
            pl.BlockSpec((None, d, tn), lambda l, j: (l, 0, j)),
            pl.BlockSpec((None, 1, tn), lambda l, j: (l, 0, j)),
        ],
        out_specs=pl.BlockSpec((None, bsz, tn), lambda l, j: (l, 0, j)),
        name="ada_mod",
        compiler_params=_cparams(("parallel", "parallel")),
    )(c, ada_w, ada_b.reshape(depth, 1, n))


def _norm_mod_body(x_ref, w_ref, sh_ref, sc_ref, o_ref):
    x = x_ref[...]
    ms = jnp.mean(x * x, axis=-1, keepdims=True)
    y = x * lax.rsqrt(ms + NORM_EPS) * w_ref[...]
    o_ref[...] = (y * (1.0 + sc_ref[...]) + sh_ref[...]).astype(o_ref.dtype)


def _norm_mod(x, w, shift, scale):
    bsz, s, d = x.shape
    ts = _pick(s, (512, 256, 128))
    vec = lambda v: v.reshape(bsz, 1, d)
    return pl.pallas_call(
        _norm_mod_body,
        out_shape=jax.ShapeDtypeStruct((bsz, s, d), BF16),
        grid=(bsz, s // ts),
        in_specs=[
            pl.BlockSpec((None, ts, d), lambda b, i: (b, i, 0)),
            pl.BlockSpec((1, d), lambda b, i: (0, 0)),
            pl.BlockSpec((None, 1, d), lambda b, i: (b, 0, 0)),
            pl.BlockSpec((None, 1, d), lambda b, i: (b, 0, 0)),
        ],
        out_specs=pl.BlockSpec((None, ts, d), lambda b, i: (b, i, 0)),
        name="norm_mod",
        compiler_params=_cparams(("parallel", "parallel")),
    )(x, w.reshape(1, d), vec(shift), vec(scale))


def _rope_body(pos_ref, freq_ref, cos_ref, sin_ref):
    ang = pos_ref[...].astype(F32) * freq_ref[...]
    lane = lax.broadcasted_iota(I32, ang.shape, 1)
    cos_ref[...] = jnp.where(lane < QK_ROPE, jnp.cos(ang), 0.0)
    sin = jnp.sin(ang)
    sin_ref[...] = jnp.where(lane < QK_ROPE // 2, -sin, jnp.where(lane < QK_ROPE, sin, 0.0))


def _rope_tables(pos, freq):
    t = pos.shape[0]
    tm = _pick(t, (1024, 512, 256, 128))
    out = jax.ShapeDtypeStruct((t, LANES), F32)
    row = pl.BlockSpec((tm, LANES), lambda i: (i, 0))
    return pl.pallas_call(
        _rope_body, out_shape=[out, out], grid=(t // tm,),
        in_specs=[pl.BlockSpec((tm, 1), lambda i: (i, 0)), pl.BlockSpec((1, LANES), lambda i: (0, 0))],
        out_specs=[row, row], name="rope_tables", compiler_params=_cparams(("parallel",)),
    )(pos, freq)


def _mla_q_body(cq_ref, nw_ref, wuq_ref, hw_ref, cos_ref, sin_ref, o_ref, *, scale):
    cq = cq_ref[...]
    ms = jnp.mean(cq * cq, axis=-1, keepdims=True)
    cqn = (cq * lax.rsqrt(ms + NORM_EPS) * nw_ref[...]).astype(BF16)
    cos, sin_signed = cos_ref[...], sin_ref[...]
    w_nope = hw_ref[0:1, :]
    w_rope = hw_ref[1:2, :]
    w_swap = hw_ref[2:3, :]
    for h in range(MLA_HEADS):
        qh = _dot(cqn, wuq_ref[:, h * 3 * LANES:(h + 1) * 3 * LANES])
        nope = qh[:, :LANES]
        rope = qh[:, LANES:2 * LANES]
        swap = qh[:, 2 * LANES:]
        ss = jnp.sum(nope * nope + rope * rope, axis=-1, keepdims=True)
        fac = lax.rsqrt(ss * (1.0 / QK_HEAD) + NORM_EPS) * scale
        o_ref[:, h * 2 * LANES:h * 2 * LANES + LANES] = (nope * fac * w_nope).astype(o_ref.dtype)
        roped = (rope * w_rope * cos + swap * w_swap * sin_signed) * fac
        o_ref[:, h * 2 * LANES + LANES:(h + 1) * 2 * LANES] = roped.astype(o_ref.dtype)


def _mla_kv_body(ckv_ref, kpe_ref, nw_ref, wukv_ref, hw_ref, cos_ref, sin_ref, k_ref, v_ref):
    ckv = ckv_ref[...]
    ms = jnp.mean(ckv * ckv, axis=-1, keepdims=True)
    ckvn = (ckv * lax.rsqrt(ms + NORM_EPS) * nw_ref[...]).astype(BF16)
    cos, sin_signed = cos_ref[...], sin_ref[...]
    kpe = kpe_ref[:, :LANES]
    kpe_swap = kpe_ref[:, LANES:]
    pe_ss = jnp.sum(kpe * kpe, axis=-1, keepdims=True)
    pe_roped = kpe * hw_ref[1:2, :] * cos + kpe_swap * hw_ref[2:3, :] * sin_signed
    w_nope = hw_ref[0:1, :]
    for h in range(MLA_HEADS):
        kn = _dot(ckvn, wukv_ref[:, h * LANES:(h + 1) * LANES])
        ss = jnp.sum(kn * kn, axis=-1, keepdims=True) + pe_ss
        fac = lax.rsqrt(ss * (1.0 / QK_HEAD) + NORM_EPS)
        k_ref[:, h * 2 * LANES:h * 2 * LANES + LANES] = (kn * fac * w_nope).astype(k_ref.dtype)
        k_ref[:, h * 2 * LANES + LANES:(h + 1) * 2 * LANES] = (pe_roped * fac).astype(k_ref.dtype)
    nv = MLA_HEADS * V_HEAD
    v_ref[...] = _dot(ckvn, wukv_ref[:, nv:]).astype(v_ref.dtype)


def _head_norm_rows(w):
    half = QK_ROPE // 2
    zeros = jnp.zeros((LANES - QK_ROPE,), F32)
    rope = jnp.concatenate([w[QK_NOPE:], zeros])
    swap = jnp.concatenate([w[QK_NOPE + half:], w[QK_NOPE:QK_NOPE + half], zeros])
    rows = jnp.stack([w[:QK_NOPE], rope, swap])
    return jnp.concatenate([rows, jnp.zeros((SUBLANES - 3, LANES), F32)], axis=0)


def _rope_freq_row():
    half = QK_ROPE // 2
    f = ROPE_THETA ** (-jnp.arange(half, dtype=F32) / half)
    return jnp.concatenate([f, f, jnp.zeros((LANES - QK_ROPE,), F32)]).reshape(1, LANES)


def _wuq_layout():
    half = QK_ROPE // 2
    idx, mask = [], []
    for h in range(MLA_HEADS):
        base = h * QK_HEAD
        nope = list(range(base, base + QK_NOPE))
        rope = list(range(base + QK_NOPE, base + QK_HEAD))
        swap = rope[half:] + rope[:half]
        pad = [0] * (LANES - QK_ROPE)
        idx += nope + rope + pad + swap + pad
        mask += [1] * QK_NOPE + [1] * QK_ROPE + [0] * len(pad) + [1] * QK_ROPE + [0] * len(pad)
    return np.asarray(idx, np.int32), np.asarray(mask, np.float32)


def _wukv_layout():
    per = QK_NOPE + V_HEAD
    k_idx = [h * per + i for h in range(MLA_HEADS) for i in range(QK_NOPE)]
    v_idx = [h * per + QK_NOPE + i for h in range(MLA_HEADS) for i in range(V_HEAD)]
    return np.asarray(k_idx + v_idx, np.int32)


def _mla_qkv(p, positions, q_norm_w, w_uq, kv_norm_w, w_ukv, qk_q_w, qk_k_w):
    t = p.shape[0]
    tm = _pick(t, (256, 128))
    idx, mask = _wuq_layout()
    wuq = (w_uq[:, idx] * mask[None, :]).astype(BF16)
    wukv = w_ukv[:, _wukv_layout()].astype(BF16)
    cos, sin_signed = _rope_tables(positions.reshape(t, 1), _rope_freq_row())
    hq = _head_norm_rows(qk_q_w)
    hk = _head_norm_rows(qk_k_w)
    qw = MLA_HEADS * 2 * LANES
    row = lambda i: (i, 0)
    fixed = lambda i: (0, 0)
    q = pl.pallas_call(
        functools.partial(_mla_q_body, scale=QK_HEAD ** -0.5),
        out_shape=jax.ShapeDtypeStruct((t, qw), BF16),
        grid=(t // tm,),
        in_specs=[
            pl.BlockSpec((tm, Q_LORA), lambda i: (i, P_CQ // Q_LORA)),
            pl.BlockSpec((1, Q_LORA), fixed),
            pl.BlockSpec(wuq.shape, fixed),
            pl.BlockSpec((SUBLANES, LANES), fixed),
            pl.BlockSpec((tm, LANES), row),
            pl.BlockSpec((tm, LANES), row),
        ],
        out_specs=pl.BlockSpec((tm, qw), row),
        name="mla_q",
        compiler_params=_cparams(("parallel",)),
    )(p, q_norm_w.reshape(1, Q_LORA), wuq, hq, cos, sin_signed)
    k, v = pl.pallas_call(
        _mla_kv_body,
        out_shape=[jax.ShapeDtypeStruct((t, qw), BF16),
                   jax.ShapeDtypeStruct((t, MLA_HEADS * V_HEAD), BF16)],
        grid=(t // tm,),
        in_specs=[
            pl.BlockSpec((tm, KV_LORA), lambda i: (i, P_CKV // KV_LORA)),
            pl.BlockSpec((tm, 2 * LANES), lambda i: (i, P_KPE // (2 * LANES))),
            pl.BlockSpec((1, KV_LORA), fixed),
            pl.BlockSpec(wukv.shape, fixed),
            pl.BlockSpec((SUBLANES, LANES), fixed),
            pl.BlockSpec((tm, LANES), row),
            pl.BlockSpec((tm, LANES), row),
        ],
        out_specs=[pl.BlockSpec((tm, qw), row), pl.BlockSpec((tm, MLA_HEADS * V_HEAD), row)],
        name="mla_kv",
        compiler_params=_cparams(("parallel",)),
    )(p, p, kv_norm_w.reshape(1, KV_LORA), wukv, hk, cos, sin_signed)
    return q, k, v


def _attn_body(qi_ref, ki_ref, q_ref, k_ref, v_ref, o_ref, m_ref, l_ref, acc_ref, *, tq):
    step_id = pl.program_id(2)
    qi = qi_ref[step_id]
    ki = ki_ref[step_id]

    @pl.when(ki == 0)
    def _():
        m_ref[...] = jnp.full(m_ref.shape, -jnp.inf, F32)
        l_ref[...] = jnp.zeros(l_ref.shape, F32)
        acc_ref[...] = jnp.zeros(acc_ref.shape, F32)

    heads = range(ATTN_HEADS_PER_STEP)
    qk = lambda h: slice(h * 2 * LANES, (h + 1) * 2 * LANES)
    vo = lambda h: slice(h * V_HEAD, (h + 1) * V_HEAD)

    def step(masked):
        ss = [_dot_nt(q_ref[:, qk(h)], k_ref[:, qk(h)]) for h in heads]
        if masked:
            row = lax.broadcasted_iota(I32, ss[0].shape, 0)
            col = lax.broadcasted_iota(I32, ss[0].shape, 1)
            ss = [jnp.where(col <= row, s, -jnp.inf) for s in ss]
        reps = tq // LANES
        prs, alphas = [], []
        for h, s in zip(heads, ss):
            m_prev = m_ref[h]
            m_new = jnp.maximum(m_prev, jnp.max(s, axis=-1, keepdims=True))
            alpha = jnp.exp(m_prev - m_new)
            pr = jnp.exp(s - jnp.tile(m_new, (1, reps)))
            l_ref[h] = alpha * l_ref[h] + jnp.sum(pr, axis=-1, keepdims=True)
            m_ref[h] = m_new
            prs.append(pr.astype(BF16))
            alphas.append(alpha)
        pvs = [_dot(pr, v_ref[:, vo(h)]) for h, pr in zip(heads, prs)]
        for h, alpha, pv in zip(heads, alphas, pvs):
            acc_ref[h] = alpha * acc_ref[h] + pv

    @pl.when(ki < qi)
    def _():
        step(False)

    @pl.when(ki == qi)
    def _():
        step(True)
        for h in heads:
            o_ref[:, vo(h)] = (acc_ref[h] / l_ref[h]).astype(o_ref.dtype)


def _attention(q, k, v, bsz, s):
    tq = _pick(s, (512, 256, 128))
    n = s // tq
    pairs = [(i, j) for i in range(n) for j in range(i + 1)]
    qi_tab = jnp.asarray([i for i, _ in pairs], I32)
    ki_tab = jnp.asarray([j for _, j in pairs], I32)
    nh = ATTN_HEADS_PER_STEP
    grid_spec = pltpu.PrefetchScalarGridSpec(
        num_scalar_prefetch=2,
        grid=(bsz, MLA_HEADS // nh, len(pairs)),
        in_specs=[
            pl.BlockSpec((None, tq, nh * 2 * LANES), lambda b, h, p, qi, ki: (b, qi[p], h)),
            pl.BlockSpec((None, tq, nh * 2 * LANES), lambda b, h, p, qi, ki: (b, ki[p], h)),
            pl.BlockSpec((None, tq, nh * V_HEAD), lambda b, h, p, qi, ki: (b, ki[p], h)),
        ],
        out_specs=pl.BlockSpec((None, tq, nh * V_HEAD), lambda b, h, p, qi, ki: (b, qi[p], h)),
        scratch_shapes=[pltpu.VMEM((nh, tq, LANES), F32), pltpu.VMEM((nh, tq, LANES), F32),
                        pltpu.VMEM((nh, tq, V_HEAD), F32)],
    )
    return pl.pallas_call(
        functools.partial(_attn_body, tq=tq),
        out_shape=jax.ShapeDtypeStruct((bsz, s, MLA_HEADS * V_HEAD), BF16),
        grid_spec=grid_spec,
        name="mla_attention",
        compiler_params=_cparams(("parallel", "parallel", "arbitrary")),
    )(qi_tab, ki_tab, q, k, v)


def _rwkv_body(r_ref, k_ref, v_ref, xl_ref, mu_r_ref, mu_k_ref, mu_v_ref, mu_l_ref, vec_ref,
               w2a2_ref, g2_ref, o_ref, state_ref, pr_ref, pk_ref, pv_ref, pl_ref, *, tb, npairs):
    t_idx = pl.program_id(2)

    @pl.when(t_idx == 0)
    def _():
        state_ref[...] = jnp.zeros(state_ref.shape, F32)
        pr_ref[...] = jnp.zeros(pr_ref.shape, F32)
        pk_ref[...] = jnp.zeros(pk_ref.shape, F32)
        pv_ref[...] = jnp.zeros(pv_ref.shape, F32)
        pl_ref[...] = jnp.zeros(pl_ref.shape, F32)

    def mix(x_ref, prev_ref, mu_ref):
        x = x_ref[...]
        row = lax.broadcasted_iota(I32, x.shape, 0)
        shifted = jnp.where(row == 0, prev_ref[...], pltpu.roll(x, 1, axis=0))
        prev_ref[...] = x[tb - 1:tb, :]
        return x + (shifted - x) * mu_ref[...]

    r = mix(r_ref, pr_ref, mu_r_ref)
    k = mix(k_ref, pk_ref, mu_k_ref)
    v = mix(v_ref, pv_ref, mu_v_ref)
    xl = mix(xl_ref, pl_ref, mu_l_ref)

    lane = lax.broadcasted_iota(I32, (tb, LANES), 1)
    first = lane < W_LORA
    x0 = xl[:, :LANES]
    lhs_w = jnp.where(first, jnp.tanh(x0), 0.0).astype(BF16)
    lhs_a = jnp.where(first, 0.0, x0).astype(BF16)
    xg = xl[:, LANES:]
    lhs_g = (1.0 / (1.0 + jnp.exp(-xg))).astype(BF16)
    cols = [slice(p * LANES, (p + 1) * LANES) for p in range(npairs)]
    outs = _rwkv_pairs([r[:, c] for c in cols], [k[:, c] for c in cols], [v[:, c] for c in cols],
                       lhs_w, lhs_a, lhs_g, [vec_ref[:, c] for c in cols],
                       [w2a2_ref[:, c] for c in cols], [g2_ref[:, c] for c in cols],
                       state_ref, tb)
    for c, out in zip(cols, outs):
        o_ref[:, c] = out.astype(o_ref.dtype)


def _rwkv_pairs(rs, ks, vs_in, lhs_w, lhs_a, lhs_g, vecs, w2a2s, g2s, state_ref, tb):
    c = RWKV_CHUNK
    pairs = range(len(rs))
    chunks = range(tb // c)
    each = lambda f, *lists: [f(*args) for args in zip(*lists)]

    ri = lax.broadcasted_iota(I32, (LANES, LANES), 0)
    ci = lax.broadcasted_iota(I32, (LANES, LANES), 1)
    same = (ri // RWKV_HEAD) == (ci // RWKV_HEAD)
    ones_bd = jnp.where(same, 1.0, 0.0).astype(BF16)
    tri_r = lax.broadcasted_iota(I32, (c, c), 0)
    tri_c = lax.broadcasted_iota(I32, (c, c), 1)
    tri = jnp.where(tri_c <= tri_r, 1.0, 0.0).astype(BF16)
    eye = jnp.where(ri == ci, 1.0, 0.0)
    first_c = lax.broadcasted_iota(I32, (c, LANES), 1) < RWKV_HEAD

    def stack(z):
        return jnp.concatenate([jnp.where(first_c, z, 0.0), jnp.where(first_c, 0.0, z)],
                               axis=0).astype(BF16)

    def split(x):
        hi = x.astype(BF16)
        return hi, (x - hi.astype(F32)).astype(BF16)

    def group_sum(xs):
        parts = each(split, xs)
        his = [_dot(hi, ones_bd) for hi, _ in parts]
        los = [_dot(lo, ones_bd) for _, lo in parts]
        return each(lambda a, b: a + b, his, los)

    lws = [_dot(lhs_w, w) for w in w2a2s]
    las = [_dot(lhs_a, w) for w in w2a2s]
    gs = [_dot(lhs_g, g2) for g2 in g2s]
    kks = each(lambda k, vec: k * vec[2:3, :], ks, vecs)
    nrms = group_sum(each(lambda kk: kk * kk, kks))
    kks = each(lambda kk, n: kk / jnp.maximum(jnp.sqrt(n), 1e-12), kks, nrms)

    def decay_and_gate(lw, la, vec):
        zw = -(vec[0:1, :] + lw)
        softplus = jnp.maximum(zw, 0.0) + jnp.log(1.0 + jnp.exp(-jnp.abs(zw)))
        a = 1.0 / (1.0 + jnp.exp(-(vec[1:2, :] + la)))
        return -jnp.exp(-softplus - 0.5), a

    dg = each(decay_and_gate, lws, las, vecs)
    logds = [d for d, _ in dg]
    a_gates = [a for _, a in dg]
    kps = each(lambda k, a, vec: k * (1.0 + (a - 1.0) * vec[3:4, :]), ks, a_gates, vecs)
    bvecs = each(lambda kk, a: kk * a, kks, a_gates)
    bonus_sums = group_sum(each(lambda r, kp, vec: r * kp * vec[4:5, :], rs, kps, vecs))

    inst = [(p, ch) for p in pairs for ch in chunks]
    rows = lambda x, ch: x[ch * c:(ch + 1) * c]
    lds = [rows(logds[p], ch) for p, ch in inst]
    ld_parts = each(split, lds)
    lc_hi = [_dot(tri, hi) for hi, _ in ld_parts]
    lc_lo = [_dot(tri, lo) for _, lo in ld_parts]
    lcs = each(lambda a, b: a + b, lc_hi, lc_lo)
    ltots = [lc[c - 1:c, :] for lc in lcs]
    e_negs = [jnp.exp(-lc) for lc in lcs]
    e_rems = each(lambda lt, lc: jnp.exp(lt - lc), ltots, lcs)
    xas = [stack(-rows(kks[p], ch) * jnp.exp(lc - ld)) for (p, ch), lc, ld in zip(inst, lcs, lds)]
    xrs = [stack(rows(rs[p], ch) * jnp.exp(lc)) for (p, ch), lc in zip(inst, lcs)]
    ybs = [stack(rows(bvecs[p], ch) * e) for (p, ch), e in zip(inst, e_negs)]
    yks = [stack(rows(kps[p], ch) * e) for (p, ch), e in zip(inst, e_negs)]
    vss = [stack(rows(vs_in[p], ch)) for p, ch in inst]
    bhs = [stack(rows(bvecs[p], ch) * e) for (p, ch), e in zip(inst, e_rems)]
    khs = [stack(rows(kps[p], ch) * e) for (p, ch), e in zip(inst, e_rems)]

    n2 = 2 * c
    xars = each(lambda a, b: jnp.concatenate([a, b], axis=0), xas, xrs)
    rr = lax.broadcasted_iota(I32, (2 * n2, n2), 0)
    cc = lax.broadcasted_iota(I32, (2 * n2, n2), 1)
    rr_in = rr % n2
    tri_mask = ((rr_in // c) == (cc // c)) & ((cc % c) < (rr_in % c) + rr // n2)
    mbs = each(lambda a, b: jnp.where(tri_mask, _dot_nt(a, b), 0.0), xars, ybs)
    mks = each(lambda a, b: jnp.where(tri_mask, _dot_nt(a, b), 0.0).astype(BF16), xars, yks)
    labs = [mb[:n2] for mb in mbs]
    mrbs = [mb[n2:].astype(BF16) for mb in mbs]

    pbs = [lab.astype(BF16) for lab in labs]
    tinvs = [eye + lab for lab in labs]
    pbs = [_dot(pb, pb).astype(BF16) for pb in pbs]
    n_sq = 2
    while n_sq * 2 < c:
        both = each(lambda pb, ti: _dot(pb, jnp.concatenate([ti.astype(BF16), pb], axis=1)),
                    pbs, tinvs)
        tinvs = each(lambda ti, bo: ti + bo[:, :n2], tinvs, both)
        pbs = [bo[:, n2:].astype(BF16) for bo in both]
        n_sq *= 2
    tinvs = each(lambda ti, pb: (ti + _dot(pb, ti.astype(BF16))).astype(BF16), tinvs, pbs)
    mkvs = each(_dot, mks, vss)
    kvs = each(_dot_tn, vss, khs)

    states = [state_ref[p] for p in pairs]
    ys = [[] for _ in pairs]
    n_ch = len(chunks)
    for ch in chunks:
        ids = [p * n_ch + ch for p in pairs]
        sbs = [s.astype(BF16) for s in states]
        withs = [_dot_nt(xars[i], sb) + mkvs[i] for i, sb in zip(ids, sbs)]
        ubs = [_dot(tinvs[i], w[:n2].astype(BF16)).astype(BF16) for i, w in zip(ids, withs)]
        states = [s * jnp.exp(ltots[i]) + _dot_tn(ub, bhs[i]) + kvs[i]
                  for i, s, ub in zip(ids, states, ubs)]
        for p, i, w, ub in zip(pairs, ids, withs, ubs):
            ystack = w[n2:] + _dot(mrbs[i], ub)
            ys[p].append(ystack[:c] + ystack[c:])
    for p, s in zip(pairs, states):
        state_ref[p] = s

    ycat = [jnp.concatenate(y, axis=0) for y in ys]
    inv_n = 1.0 / RWKV_HEAD
    means = [m * inv_n for m in group_sum(ycat)]
    ds = each(lambda y, m: y - m, ycat, means)
    variances = [v * inv_n for v in group_sum(each(lambda d: d * d, ds))]
    outs = []
    for d, var, vec, bs, v, g in zip(ds, variances, vecs, bonus_sums, vs_in, gs):
        yn = d * lax.rsqrt(var + LNX_EPS) * vec[5:6, :] + vec[6:7, :]
        outs.append((yn + bs * v) * g)
    return outs


def _rwkv(p_rkv, p_mla, mu_rkv, mu_lora, vecs, w2a2, g2):
    assert W_LORA + A_LORA == LANES and G_LORA == LANES
    bsz, s, _ = p_rkv.shape
    tb = _pick(s, (512, 256, 128, 64))
    npairs = RWKV_PAIRS_PER_STEP
    wd = npairs * LANES
    groups = RWKV_HEADS // 2 // npairs
    colblk = lambda off: (lambda b, hp, t: (b, t, off // wd + hp))
    mublk = lambda off: (lambda b, hp, t: (0, off // wd + hp))
    o_r, o_k, o_v = 0, RWKV_WIDTH, 2 * RWKV_WIDTH
    return pl.pallas_call(
        functools.partial(_rwkv_body, tb=tb, npairs=npairs),
        out_shape=jax.ShapeDtypeStruct((bsz, s, RWKV_WIDTH), BF16),
        grid=(bsz, groups, s // tb),
        in_specs=[
            pl.BlockSpec((None, tb, wd), colblk(o_r)),
            pl.BlockSpec((None, tb, wd), colblk(o_k)),
            pl.BlockSpec((None, tb, wd), colblk(o_v)),
            pl.BlockSpec((None, tb, 2 * LANES), lambda b, hp, t: (b, t, P_LORA // (2 * LANES))),
            pl.BlockSpec((1, wd), mublk(o_r)),
            pl.BlockSpec((1, wd), mublk(o_k)),
            pl.BlockSpec((1, wd), mublk(o_v)),
            pl.BlockSpec((1, 2 * LANES), lambda b, hp, t: (0, 0)),
            pl.BlockSpec((SUBLANES, wd), lambda b, hp, t: (0, hp)),
            pl.BlockSpec((LANES, wd), lambda b, hp, t: (0, hp)),
            pl.BlockSpec((LANES, wd), lambda b, hp, t: (0, hp)),
        ],
        out_specs=pl.BlockSpec((None, tb, wd), lambda b, hp, t: (b, t, hp)),
        scratch_shapes=[pltpu.VMEM((npairs, LANES, LANES), F32),
                        pltpu.VMEM((1, wd), F32), pltpu.VMEM((1, wd), F32),
                        pltpu.VMEM((1, wd), F32), pltpu.VMEM((1, 2 * LANES), F32)],
        name="rwkv7",
        compiler_params=_cparams(("parallel", "parallel", "arbitrary")),
    )(p_rkv, p_rkv, p_rkv, p_mla, mu_rkv, mu_rkv, mu_rkv, mu_lora, vecs, w2a2, g2)


def _conv_body(h_ref, wb_ref, wc_ref, wu_ref, cw_ref, o_ref, tail_ref, *, tm, seq):
    i = pl.program_id(1)
    a = h_ref[...]
    gate_b = _dot(a, wb_ref[...])
    uu = _dot(a, wc_ref[...]) * _dot(a, wu_ref[...])

    @pl.when((i * tm) % seq == 0)
    def _():
        tail_ref[...] = jnp.zeros(tail_ref.shape, F32)

    tail = tail_ref[...]
    row = lax.broadcasted_iota(I32, uu.shape, 0)
    h1 = tail[SUBLANES - 1:SUBLANES, :]
    h2 = tail[SUBLANES - 2:SUBLANES - 1, :]
    s1 = jnp.where(row == 0, h1, pltpu.roll(uu, 1, axis=0))
    s2 = jnp.where(row == 0, h2, jnp.where(row == 1, h1, pltpu.roll(uu, 2, axis=0)))
    y = cw_ref[0:1, :] * s2 + cw_ref[1:2, :] * s1 + cw_ref[2:3, :] * uu
    o_ref[...] = (gate_b * y).astype(o_ref.dtype)
    tail_ref[...] = uu[tm - SUBLANES:tm, :]


def _conv_mixer(h, w_in, conv_w, s):
    t, d = h.shape
    tm = _pick(s, (512, 256, 128))
    tn = 512
    nblk = d // tn
    w = w_in.astype(BF16)
    cw = jnp.concatenate([conv_w, jnp.zeros((SUBLANES - CONV_WIDTH, d), F32)], axis=0)
    wspec = lambda off: pl.BlockSpec((d, tn), lambda j, i: (0, j + off))
    return pl.pallas_call(
        functools.partial(_conv_body, tm=tm, seq=s),
        out_shape=jax.ShapeDtypeStruct((t, d), BF16),
        grid=(nblk, t // tm),
        in_specs=[pl.BlockSpec((tm, d), lambda j, i: (i, 0)),
                  wspec(0), wspec(nblk), wspec(2 * nblk),
                  pl.BlockSpec((SUBLANES, tn), lambda j, i: (0, j))],
        out_specs=pl.BlockSpec((tm, tn), lambda j, i: (i, j)),
        scratch_shapes=[pltpu.VMEM((SUBLANES, tn), F32)],
        name="conv_mixer",
        compiler_params=_cparams(("parallel", "arbitrary")),
    )(h, w, w, w, cw)


def _rank_pairs():
    kt = PEER_TOPK
    return [(a, b) for a in range(kt) for b in range(kt) if (a + 1) * (b + 1) <= kt]


PAIR_ROWS = -(-len(_rank_pairs()) // SUBLANES) * SUBLANES


def _pair_matrices():
    m1 = np.zeros((PAIR_ROWS, PEER_TOPK), np.float32)
    m2 = np.zeros((PAIR_ROWS, PEER_TOPK), np.float32)
    for p, (a, b) in enumerate(_rank_pairs()):
        m1[p, a] = 1.0
        m2[p, b] = 1.0
    return m1, m2


def _exact_dot_left(m, x):
    hi = x.astype(BF16)
    r1 = x - hi.astype(F32)
    mid = r1.astype(BF16)
    lo = (r1 - mid.astype(F32)).astype(BF16)
    return (_dot(m, hi) + _dot(m, mid)) + _dot(m, lo)


def _extract_top_rows(xs, row_iota, n, rank_iota, extras=None):
    xs = list(xs)
    sentinel = float(xs[0].shape[0])
    zeros = lambda: [jnp.zeros((n, x.shape[1]), F32) for x in xs]
    vals, idxs, exts = zeros(), zeros(), zeros()
    for kk in range(n):
        here = rank_iota == kk
        for j, x in enumerate(xs):
            m = jnp.max(x, axis=0, keepdims=True)
            idx = jnp.min(jnp.where(x == m, row_iota, sentinel), axis=0, keepdims=True)
            hit = row_iota == idx
            if extras is not None:
                ext = jnp.max(jnp.where(hit, extras[j], -1.0), axis=0, keepdims=True)
                exts[j] = jnp.where(here, ext, exts[j])
            xs[j] = jnp.where(hit, -jnp.inf, x)
            vals[j] = jnp.where(here, m, vals[j])
            idxs[j] = jnp.where(here, idx, idxs[j])
    return list(zip(vals, idxs, exts))


def _route_body(q_ref, keys_ref, m1_ref, m2_ref, e1_ref, e2_ref, gw_ref):
    kt = PEER_TOPK
    tb = q_ref.shape[0]
    nh = ROUTE_HEADS_PER_STEP
    key_iota = lax.broadcasted_iota(I32, (N_KEYS, tb), 0).astype(F32)
    rank_iota = lax.broadcasted_iota(I32, (kt, tb), 0)
    scs = []
    for hp in range(2 * nh):
        qh = q_ref[:, hp * PEER_HALF:(hp + 1) * PEER_HALF].astype(BF16)
        scs.append(_dot_nt(keys_ref[hp], qh))
    tops = _extract_top_rows(scs, key_iota, kt, rank_iota)
    m1 = m1_ref[...]
    m2 = m2_ref[...]
    pair_iota = lax.broadcasted_iota(I32, (PAIR_ROWS, tb), 0).astype(F32)
    cands, eids = [], []
    for h in range(nh):
        (sv1, si1, _), (sv2, si2, _) = tops[2 * h], tops[2 * h + 1]
        cand = _exact_dot_left(m1, sv1) + _exact_dot_left(m2, sv2)
        cands.append(jnp.where(pair_iota < float(len(_rank_pairs())), cand, -jnp.inf))
        eids.append(_dot(m1, si1.astype(BF16)) * float(N_KEYS) + _dot(m2, si2.astype(BF16)))
    picks = _extract_top_rows(cands, pair_iota, kt, rank_iota, extras=eids)
    for h, (cv, _, ev) in enumerate(picks):
        rows = slice(h * kt, (h + 1) * kt)
        ex = jnp.exp(cv - cv[0:1, :])
        gw_ref[rows, :] = ex / jnp.sum(ex, axis=0, keepdims=True)
        ei = ev.astype(I32)
        e1_ref[rows, :] = (ei >> 7).astype(F32)
        e2_ref[rows, :] = (ei & (N_KEYS - 1)).astype(F32)


def _peer_route(q, keys):
    t = q.shape[0]
    tb = LANES
    m1, m2 = _pair_matrices()
    slots = PEER_HEADS * PEER_TOPK
    out = jax.ShapeDtypeStruct((slots, t), F32)
    nh = ROUTE_HEADS_PER_STEP
    pair_spec = pl.BlockSpec((PAIR_ROWS, PEER_TOPK), lambda i, h: (0, 0))
    out_spec = pl.BlockSpec((nh * PEER_TOPK, tb), lambda i, h: (h, i))
    return pl.pallas_call(
        _route_body,
        out_shape=[out, out, out],
        grid=(t // tb, PEER_HEADS // nh),
        in_specs=[pl.BlockSpec((tb, nh * 2 * PEER_HALF), lambda i, h: (i, h)),
                  pl.BlockSpec((nh * 2, N_KEYS, PEER_HALF), lambda i, h: (h, 0, 0)),
                  pair_spec, pair_spec],
        out_specs=[out_spec, out_spec, out_spec],
        name="peer_route",
        compiler_params=_cparams(("parallel", "parallel")),
    )(q, keys, jnp.asarray(m1, BF16), jnp.asarray(m2, BF16))


def _swap_token_and_row(xs):
    xs = list(xs)
    row = lax.broadcasted_iota(I32, xs[0].shape, 1)
    d = SUBLANES // 2
    while d >= 1:
        low = (row % (2 * d)) < d
        for s in range(SUBLANES):
            if (s // d) % 2 == 0:
                a, b = xs[s], xs[s + d]
                xs[s] = jnp.where(low, a, pltpu.roll(b, d, axis=1))
                xs[s + d] = jnp.where(low, pltpu.roll(a, SUBLANES - d, axis=1), b)
        d //= 2
    return xs


GATE_TOKENS = 2 * SUBLANES


def _gate_body(e1t_ref, e2t_ref, gwt_ref, uf_ref, o_ref, ub_ref, e1_s, e2_s, gw_s):
    ub_ref[...] = uf_ref[...].astype(ub_ref.dtype)
    tb = o_ref.shape[2]
    e1_s[...] = e1t_ref[...].T
    e2_s[...] = e2t_ref[...].T
    gw_s[...] = gwt_ref[...].T
    sub = lax.broadcasted_iota(I32, (N_KEYS, LANES), 0).astype(F32)
    nblk = N_KEYS // SUBLANES

    def eight_tokens(t0):
        lhs, rhs = [], []
        for s in range(SUBLANES):
            e1 = e1_s[pl.ds(t0 + s, 1), :]
            e2 = e2_s[pl.ds(t0 + s, 1), :]
            gw = gw_s[pl.ds(t0 + s, 1), :]
            lhs.append(jnp.where(e1 == sub, gw, 0.0).astype(BF16))
            rhs.append(jnp.where(e2 == sub, 1.0, 0.0).astype(BF16))
        per_token = [_dot_nt(a, b).reshape(nblk, SUBLANES, N_KEYS) for a, b in zip(lhs, rhs)]
        return _swap_token_and_row(per_token)

    def sixteen_tokens(g, carry):
        t0 = pl.multiple_of(g * GATE_TOKENS, GATE_TOKENS)
        first, second = eight_tokens(t0), eight_tokens(t0 + SUBLANES)
        for r, (y0, y1) in enumerate(zip(first, second)):
            o_ref[:, r, pl.ds(t0, GATE_TOKENS), :] = jnp.concatenate(
                [y0, y1], axis=1).astype(o_ref.dtype)
        return carry

    lax.fori_loop(0, tb // GATE_TOKENS, sixteen_tokens, 0, unroll=4)


def _peer_gates(e1t, e2t, gwt, u_all, layer):
    t = e1t.shape[1]
    tb = LANES
    steps = t // tb
    nblk = N_KEYS // SUBLANES
    n_exp, d = u_all.shape[1:]
    rows = n_exp // steps
    assert rows * steps == n_exp and rows % GATE_TOKENS == 0
    in_spec = pl.BlockSpec((LANES, tb), lambda i: (0, i))
    g, u_bf16 = pl.pallas_call(
        _gate_body,
        out_shape=[jax.ShapeDtypeStruct((nblk, SUBLANES, t, N_KEYS), BF16),
                   jax.ShapeDtypeStruct((n_exp, d), BF16)],
        grid=(steps,),
        in_specs=[in_spec, in_spec, in_spec,
                  pl.BlockSpec((None, rows, d), lambda i: (layer, i, 0))],
        out_specs=[pl.BlockSpec((nblk, SUBLANES, tb, N_KEYS), lambda i: (0, 0, i, 0)),
                   pl.BlockSpec((rows, d), lambda i: (i, 0))],
        scratch_shapes=[pltpu.VMEM((tb, LANES), F32)] * 3,
        name="peer_gates",
        compiler_params=_cparams(("parallel",)),
    )(e1t, e2t, gwt, u_all)
    return g.reshape(N_KEYS, t, N_KEYS), u_bf16


def _peer_up_body(h_ref, u_ref, g_ref, o_ref):
    s = _dot_nt(h_ref[...], u_ref[...])
    act = 0.5 * s * (1.0 + lax.erf(s * (2.0 ** -0.5)))
    for j in range(o_ref.shape[1] // N_KEYS):
        cols = slice(j * N_KEYS, (j + 1) * N_KEYS)
        o_ref[:, cols] = (act[:, cols] * g_ref[j]).astype(o_ref.dtype)


def _peer_up(h, u_tab, g3):
    t, d = h.shape
    n_exp = u_tab.shape[0]
    tm = _pick(t, (1024, 512, 256, 128))
    tn = SUBLANES * N_KEYS
    return pl.pallas_call(
        _peer_up_body,
        out_shape=jax.ShapeDtypeStruct((t, n_exp), BF16),
        grid=(t // tm, n_exp // tn),
        in_specs=[pl.BlockSpec((tm, d), lambda i, j: (i, 0), pipeline_mode=pl.Buffered(1)),
                  pl.BlockSpec((tn, d), lambda i, j: (j, 0)),
                  pl.BlockSpec((tn // N_KEYS, tm, N_KEYS), lambda i, j: (j, i, 0))],
        out_specs=pl.BlockSpec((tm, tn), lambda i, j: (i, j)),
        name="peer_up",
        compiler_params=_cparams(("parallel", "parallel")),
    )(h, u_tab, g3)


def _residual(acc, x, gate):
    return x + gate * acc


def _gated_residual_matmul(a, w_all, layer, x2, gate, s, *, tk, name):
    t, d = x2.shape
    bsz = gate.shape[0]
    tm = _pick(s, (1024, 512, 256, 128))
    tn = 1024
    return _matmul(
        a, w_all, tm=tm, tn=tn, tk=tk, out_dtype=F32, epilogue=_residual,
        extras=[(x2, (tm, tn), lambda i, j: (i, j)),
                (gate.reshape(bsz, 1, d), (None, 1, tn), lambda i, j: (i * tm // s, 0, j))],
        b_lead=layer, name=name)


def _peer_ffn(x, w_norm, shift, scale, gate, layer, w_q_all, keys, u_all, v_all):
    bsz, s, d = x.shape
    t = bsz * s
    h = _norm_mod(x, w_norm, shift, scale).reshape(t, d)
    tm = _pick(s, (1024, 512, 256, 128))
    q = _matmul(h, w_q_all, tm=tm, tn=1024, tk=d, out_dtype=F32, b_lead=layer,
                name="peer_query")
    keys_b = keys.reshape(PEER_HEADS * 2, N_KEYS, PEER_HALF).astype(BF16)
    e1t, e2t, gwt = _peer_route(q, keys_b)
    g3, u_tab = _peer_gates(e1t, e2t, gwt, u_all, layer)
    pmat = _peer_up(h, u_tab, g3)
    out = _gated_residual_matmul(pmat, v_all, layer, x.reshape(t, d), gate, s,
                                 tk=2048, name="peer_down")
    return out.reshape(bsz, s, d)


def _pad_mla_proj(w):
    o_kv = Q_LORA
    o_pe = Q_LORA + KV_LORA
    o_rw = o_pe + QK_ROPE
    half = QK_ROPE // 2
    z = lambda n: jnp.zeros((w.shape[0], n), w.dtype)
    cq = w[:, :Q_LORA]
    ckv = w[:, o_kv:o_pe]
    kpe = w[:, o_pe:o_rw]
    kpe_swap = jnp.concatenate([kpe[:, half:], kpe[:, :half]], axis=1)
    lora = w[:, o_rw + 3 * RWKV_WIDTH:]
    pad = LANES - QK_ROPE
    out = jnp.concatenate([ckv, kpe, z(pad), kpe_swap, z(pad), cq, lora,
                           z(P_WIDTH - P_LORA - lora.shape[1])], axis=1)
    assert out.shape[1] == P_WIDTH
    return out


def _hybrid_mixer(h, positions, w_in, q_norm_w, w_uq, kv_norm_w, w_ukv, qk_q_w, qk_k_w,
                  mu, w0, w2, a0, a2, g2, k_k, k_a, r_k, lnx_w, lnx_b):
    bsz, s, d = h.shape
    t = bsz * s
    tm = _pick(s, (1024, 512, 256, 128))
    o_rw = Q_LORA + KV_LORA + QK_ROPE
    n_rkv = 3 * RWKV_WIDTH
    h2 = h.reshape(t, d)
    p_mla = _matmul(h2, _pad_mla_proj(w_in).astype(BF16), tm=tm, tn=1024, tk=d, out_dtype=F32,
                    name="hyb_in_proj_mla")
    p_rkv = _matmul(h2, w_in[:, o_rw:o_rw + n_rkv].astype(BF16), tm=tm, tn=1024, tk=d,
                    out_dtype=F32, name="hyb_in_proj_rkv")
    q, k, v = _mla_qkv(p_mla, positions, q_norm_w, w_uq, kv_norm_w, w_ukv, qk_q_w, qk_k_w)
    y_mla = _attention(q.reshape(bsz, s, -1), k.reshape(bsz, s, -1), v.reshape(bsz, s, -1), bsz, s)

    zrow = jnp.zeros((RWKV_WIDTH,), F32)
    vecs = jnp.stack([w0, a0, k_k, k_a, r_k.reshape(-1), lnx_w, lnx_b, zrow])
    w2a2 = jnp.concatenate([w2, a2], axis=0).astype(BF16)
    y_rwkv = _rwkv(p_rkv.reshape(bsz, s, n_rkv), p_mla.reshape(bsz, s, P_WIDTH),
                   mu[:n_rkv].reshape(1, n_rkv), mu[n_rkv:].reshape(1, -1), vecs, w2a2,
                   g2.astype(BF16))
    return jnp.concatenate([y_mla, y_rwkv], axis=-1).reshape(t, -1)


def kernel(x, c, positions, ada_w, ada_b, norm_mix_w, norm_ffn_w, hyb_w_in, mla_q_norm_w, mla_w_uq, mla_kv_norm_w, mla_w_ukv, mla_qk_q_w, mla_qk_k_w, rwkv_mu, rwkv_w0, rwkv_w2, rwkv_a0, rwkv_a2, rwkv_g2, rwkv_k_k, rwkv_k_a, rwkv_r_k, rwkv_lnx_w, rwkv_lnx_b, hyb_w_out, conv_w_in, conv_w, conv_w_out, peer_w_q, peer_keys, peer_u, peer_v):
    bsz, s, d = x.shape
    t = bsz * s
    depth = ada_w.shape[0]
    mod = _ada_mod(c, ada_w, ada_b)
    w_q_all = peer_w_q.astype(BF16)
    u_all = peer_u
    v_all = peer_v
    for layer in range(depth):
        sh_m, sc_m, gt_m, sh_f, sc_f, gt_f = [mod[layer, :, j * d:(j + 1) * d] for j in range(6)]
        h = _norm_mod(x, norm_mix_w[layer], sh_m, sc_m)
        i = layer // 2
        if layer % 2 == 0:
            y = _hybrid_mixer(h, positions, hyb_w_in[i], mla_q_norm_w[i], mla_w_uq[i],
                              mla_kv_norm_w[i], mla_w_ukv[i], mla_qk_q_w[i], mla_qk_k_w[i],
                              rwkv_mu[i], rwkv_w0[i], rwkv_w2[i], rwkv_a0[i], rwkv_a2[i],
                              rwkv_g2[i], rwkv_k_k[i], rwkv_k_a[i], rwkv_r_k[i],
                              rwkv_lnx_w[i], rwkv_lnx_b[i])
            w_out = hyb_w_out.astype(BF16)
        else:
            y = _conv_mixer(h.reshape(t, d), conv_w_in[i], conv_w[i], s)
            w_out = conv_w_out.astype(BF16)
        x = _gated_residual_matmul(y, w_out, i, x.reshape(t, d), gt_m, s, tk=d,
                                   name="mixer_out_proj").reshape(bsz, s, d)
        x = _peer_ffn(x, norm_ffn_w[layer], sh_f, sc_f, gt_f, layer, w_q_all, peer_keys[layer],
                      u_all, v_all)
    return x
```
